```python
import jax, jax.numpy as jnp
from jax import lax
import numpy as np

D_MODEL = 1024
BATCH = 8
SEQ = 8192
DEPTH = 2

CHUNK = 128
A_GROUPS = 8
A_WIDTH = 512
A_HEAD = A_WIDTH // A_GROUPS
B_WIDTH = 512
CONV_WIDTH = 3
C_WIDTH = 512
POOL_WINDOWS = (2, 4, 8, 16)
C_GROUP = C_WIDTH // len(POOL_WINDOWS)
IN_TOTAL = 3 * A_WIDTH + 4 * B_WIDTH + 2 * C_WIDTH + 3 * D_MODEL
RMS_EPS = 1e-6
LN_EPS = 1e-5

kernel_name = "hybrid_gmlp_shortconv_pool_gated_merge"


def _rmsnorm(x, g):
    xf = x.astype(jnp.float32)
    y = xf * lax.rsqrt(jnp.mean(xf * xf, axis=-1, keepdims=True) + RMS_EPS)
    return (y * g.astype(jnp.float32)).astype(x.dtype)


def _layernorm(x, g, b):
    xf = x.astype(jnp.float32)
    mu = jnp.mean(xf, axis=-1, keepdims=True)
    xc = xf - mu
    var = jnp.mean(xc * xc, axis=-1, keepdims=True)
    y = xc * lax.rsqrt(var + LN_EPS)
    return (y * g.astype(jnp.float32) + b.astype(jnp.float32)).astype(x.dtype)


def _split_points():
    widths = [A_WIDTH] * 3 + [B_WIDTH] * 4 + [C_WIDTH] * 2 + [D_MODEL] * 3
    return [int(s) for s in np.cumsum(widths)[:-1]]


def _gmlp_branch(u, v, ln_g, ln_b, w_s, b_s):
    u = jax.nn.gelu(u)
    v = _layernorm(jax.nn.gelu(v), ln_g, ln_b)
    bsz, s, _ = v.shape
    vc = v.reshape(bsz, s // CHUNK, CHUNK, A_GROUPS, A_HEAD)
    causal = jnp.tril(jnp.ones((CHUNK, CHUNK), dtype=bool))
    w_m = jnp.where(causal, w_s, 0.0)
    sg = jnp.einsum('gts,bnsgc->bntgc', w_m, vc) + b_s.T[:, :, None]
    return u * sg.reshape(bsz, s, A_WIDTH)


def _shortconv_branch(xb, bg, cg, conv_w, conv_b):
    y = cg * xb
    y = lax.conv_general_dilated(
        y, conv_w[:, None, :].astype(y.dtype), window_strides=(1,),
        padding=[(CONV_WIDTH - 1, 0)], dimension_numbers=('NWC', 'WIO', 'NWC'),
        feature_group_count=B_WIDTH) + conv_b
    return bg * y


def _pool_branch(xc, w_pool, pool_scale):
    bsz, s, _ = xc.shape
    xf = xc.astype(jnp.float32).reshape(bsz, s, len(POOL_WINDOWS), C_GROUP)
    cs = jnp.cumsum(xf, axis=1)
    t_count = jnp.arange(1, s + 1, dtype=jnp.float32)
    pooled = []
    for gi, w in enumerate(POOL_WINDOWS):
        c = cs[:, :, gi]
        lag = jnp.pad(c[:, :s - w], ((0, 0), (w, 0), (0, 0)))
        cnt = jnp.minimum(t_count, float(w))[None, :, None]
        pooled.append((c - lag) / cnt)
    pooled = (jnp.stack(pooled, axis=2) - xf).astype(xc.dtype)
    y = jnp.einsum('bsgc,gcd->bsgd', pooled, w_pool).reshape(bsz, s, C_WIDTH)
    return y * pool_scale


def _hybrid_layer(x, norm_g, w_in, ln_g, ln_b, w_s, b_s, conv_w, conv_b,
                  w_pool, pool_scale, w_pa, w_pb, w_pc, w_o):
    h = _rmsnorm(x, norm_g)
    p = h @ w_in
    (u, v, z_a, x_b, b_g, c_g, z_b, x_c, z_c,
     g_a, g_b, g_c) = jnp.split(p, _split_points(), axis=-1)
    y_a = (_gmlp_branch(u, v, ln_g, ln_b, w_s, b_s) * jax.nn.silu(z_a)) @ w_pa
    y_b = (_shortconv_branch(x_b, b_g, c_g, conv_w, conv_b) * jax.nn.silu(z_b)) @ w_pb
    y_c = (_pool_branch(x_c, w_pool, pool_scale) * jax.nn.silu(z_c)) @ w_pc
    merged = (jax.nn.sigmoid(g_a) * y_a + jax.nn.sigmoid(g_b) * y_b
              + jax.nn.sigmoid(g_c) * y_c)
    return x + merged @ w_o


def _fwd_setup_inputs(seed: int = 0) -> dict:
    key = jax.random.key(seed)
    ks = jax.random.split(key, 20)
    f32 = jnp.float32
    n = lambda k, shape: jax.random.normal(k, shape, dtype=f32)
    return {
        "x": n(ks[0], (BATCH, SEQ, D_MODEL)),
        "norm_g": 1.0 + 0.02 * n(ks[1], (DEPTH, D_MODEL)),
        "w_in": n(ks[2], (DEPTH, D_MODEL, IN_TOTAL)) * D_MODEL ** -0.5,
        "ln_g": 1.0 + 0.02 * n(ks[3], (DEPTH, A_WIDTH)),
        "ln_b": 0.02 * n(ks[4], (DEPTH, A_WIDTH)),
        "w_s": n(ks[5], (DEPTH, A_GROUPS, CHUNK, CHUNK)) * CHUNK ** -0.5,
        "b_s": 1.0 + 0.1 * n(ks[6], (DEPTH, A_GROUPS, CHUNK)),
        "conv_w": n(ks[7], (DEPTH, CONV_WIDTH, B_WIDTH)) * CONV_WIDTH ** -0.5,
        "conv_b": 0.02 * n(ks[8], (DEPTH, B_WIDTH)),
        "w_pool": n(ks[9], (DEPTH, len(POOL_WINDOWS), C_GROUP, C_GROUP)) * C_GROUP ** -0.5,
        "pool_scale": 1.0 + 0.02 * n(ks[10], (DEPTH, C_WIDTH)),
        "w_pa": n(ks[11], (DEPTH, A_WIDTH, D_MODEL)) * A_WIDTH ** -0.5,
        "w_pb": n(ks[12], (DEPTH, B_WIDTH, D_MODEL)) * B_WIDTH ** -0.5,
        "w_pc": n(ks[13], (DEPTH, C_WIDTH, D_MODEL)) * C_WIDTH ** -0.5,
        "w_o": n(ks[14], (DEPTH, D_MODEL, D_MODEL)) * D_MODEL ** -0.5,
        "final_g": 1.0 + 0.02 * n(ks[15], (D_MODEL,)),
    }


def _fwd_reference(x, norm_g, w_in, ln_g, ln_b, w_s, b_s, conv_w, conv_b, w_pool,
              pool_scale, w_pa, w_pb, w_pc, w_o, final_g):
    for l in range(DEPTH):
        x = _hybrid_layer(x, norm_g[l], w_in[l], ln_g[l], ln_b[l], w_s[l], b_s[l],
                          conv_w[l], conv_b[l], w_pool[l], pool_scale[l],
                          w_pa[l], w_pb[l], w_pc[l], w_o[l])
    return _rmsnorm(x, final_g)


import jax as _jax
import jax.numpy as _jnp

TWIN_FORMAT = 'train_step'
FWD_PARAMS = ['x', 'norm_g', 'w_in', 'ln_g', 'ln_b', 'w_s', 'b_s', 'conv_w', 'conv_b', 'w_pool', 'pool_scale', 'w_pa', 'w_pb', 'w_pc', 'w_o', 'final_g']
TWIN_WEIGHTS = ['norm_g', 'w_in', 'ln_g', 'ln_b', 'w_s', 'b_s', 'conv_w', 'conv_b', 'w_pool', 'pool_scale', 'w_pa', 'w_pb', 'w_pc', 'w_o', 'final_g']
TWIN_DIFF_INPUT = 'x'
TWIN_INPUTS = ['x', 'norm_g', 'w_in', 'ln_g', 'ln_b', 'w_s', 'b_s', 'conv_w', 'conv_b', 'w_pool', 'pool_scale', 'w_pa', 'w_pb', 'w_pc', 'w_o', 'final_g', 'loss_target', 'm_norm_g', 'm_w_in', 'm_ln_g', 'm_ln_b', 'm_w_s', 'm_b_s', 'm_conv_w', 'm_conv_b', 'm_w_pool', 'm_pool_scale', 'm_w_pa', 'm_w_pb', 'm_w_pc', 'm_w_o', 'm_final_g', 'v_norm_g', 'v_w_in', 'v_ln_g', 'v_ln_b', 'v_w_s', 'v_b_s', 'v_conv_w', 'v_conv_b', 'v_w_pool', 'v_pool_scale', 'v_w_pa', 'v_w_pb', 'v_w_pc', 'v_w_o', 'v_final_g']
TWIN_OUTPUTS = ['loss', 'grad_x', 'grad_norm_g', 'grad_w_in', 'grad_ln_g', 'grad_ln_b', 'grad_w_s', 'grad_b_s', 'grad_conv_w', 'grad_conv_b', 'grad_w_pool', 'grad_pool_scale', 'grad_w_pa', 'grad_w_pb', 'grad_w_pc', 'grad_w_o', 'grad_final_g', 'delta_norm_g', 'delta_w_in', 'delta_ln_g', 'delta_ln_b', 'delta_w_s', 'delta_b_s', 'delta_conv_w', 'delta_conv_b', 'delta_w_pool', 'delta_pool_scale', 'delta_w_pa', 'delta_w_pb', 'delta_w_pc', 'delta_w_o', 'delta_final_g', 'new_m_norm_g', 'new_m_w_in', 'new_m_ln_g', 'new_m_ln_b', 'new_m_w_s', 'new_m_b_s', 'new_m_conv_w', 'new_m_conv_b', 'new_m_w_pool', 'new_m_pool_scale', 'new_m_w_pa', 'new_m_w_pb', 'new_m_w_pc', 'new_m_w_o', 'new_m_final_g', 'new_v_norm_g', 'new_v_w_in', 'new_v_ln_g', 'new_v_ln_b', 'new_v_w_s', 'new_v_b_s', 'new_v_conv_w', 'new_v_conv_b', 'new_v_w_pool', 'new_v_pool_scale', 'new_v_w_pa', 'new_v_w_pb', 'new_v_w_pc', 'new_v_w_o', 'new_v_final_g']
TWIN_LEAF_KINDS = {'loss': 'loss', 'grad_x': 'grad_x', 'grad_norm_g': 'grad_w', 'grad_w_in': 'grad_w', 'grad_ln_g': 'grad_w', 'grad_ln_b': 'grad_w', 'grad_w_s': 'grad_w', 'grad_b_s': 'grad_w', 'grad_conv_w': 'grad_w', 'grad_conv_b': 'grad_w', 'grad_w_pool': 'grad_w', 'grad_pool_scale': 'grad_w', 'grad_w_pa': 'grad_w', 'grad_w_pb': 'grad_w', 'grad_w_pc': 'grad_w', 'grad_w_o': 'grad_w', 'grad_final_g': 'grad_w', 'delta_norm_g': 'delta_w', 'delta_w_in': 'delta_w', 'delta_ln_g': 'delta_w', 'delta_ln_b': 'delta_w', 'delta_w_s': 'delta_w', 'delta_b_s': 'delta_w', 'delta_conv_w': 'delta_w', 'delta_conv_b': 'delta_w', 'delta_w_pool': 'delta_w', 'delta_pool_scale': 'delta_w', 'delta_w_pa': 'delta_w', 'delta_w_pb': 'delta_w', 'delta_w_pc': 'delta_w', 'delta_w_o': 'delta_w', 'delta_final_g': 'delta_w', 'new_m_norm_g': 'new_m', 'new_m_w_in': 'new_m', 'new_m_ln_g': 'new_m', 'new_m_ln_b': 'new_m', 'new_m_w_s': 'new_m', 'new_m_b_s': 'new_m', 'new_m_conv_w': 'new_m', 'new_m_conv_b': 'new_m', 'new_m_w_pool': 'new_m', 'new_m_pool_scale': 'new_m', 'new_m_w_pa': 'new_m', 'new_m_w_pb': 'new_m', 'new_m_w_pc': 'new_m', 'new_m_w_o': 'new_m', 'new_m_final_g': 'new_m', 'new_v_norm_g': 'new_v', 'new_v_w_in': 'new_v', 'new_v_ln_g': 'new_v', 'new_v_ln_b': 'new_v', 'new_v_w_s': 'new_v', 'new_v_b_s': 'new_v', 'new_v_conv_w': 'new_v', 'new_v_conv_b': 'new_v', 'new_v_w_pool': 'new_v', 'new_v_pool_scale': 'new_v', 'new_v_w_pa': 'new_v', 'new_v_w_pb': 'new_v', 'new_v_w_pc': 'new_v', 'new_v_w_o': 'new_v', 'new_v_final_g': 'new_v'}


def _forward(args):
    return _fwd_reference(*[args[k] for k in FWD_PARAMS])


def _output_shape():
    def fwd():
        inp = _fwd_setup_inputs(0)
        return _fwd_reference(*[inp[k] for k in FWD_PARAMS])
    out = _jax.eval_shape(fwd)
    return out.shape, out.dtype

N_MICROBATCH = 1
ADAM_LR = 0.001
ADAM_B1 = 0.9
ADAM_B2 = 0.999
ADAM_EPS = 1e-08
ADAM_WD = 0.01
ADAM_STEP = 10
PER_EXAMPLE_BATCH_AXIS = {'x': 0, 'loss_target': 0}
SHARED_INPUTS = []
_WEIGHT_DTYPES = {'norm_g': _jnp.float32, 'w_in': _jnp.float32, 'ln_g': _jnp.float32, 'ln_b': _jnp.float32, 'w_s': _jnp.float32, 'b_s': _jnp.float32, 'conv_w': _jnp.float32, 'conv_b': _jnp.float32, 'w_pool': _jnp.float32, 'pool_scale': _jnp.float32, 'w_pa': _jnp.float32, 'w_pb': _jnp.float32, 'w_pc': _jnp.float32, 'w_o': _jnp.float32, 'final_g': _jnp.float32}
MOMENT_SCALE = {'norm_g': 2.147848e-01, 'w_in': 7.678575e-02, 'ln_g': 4.935064e-02, 'ln_b': 4.939699e-02, 'w_s': 3.503085e-02, 'b_s': 4.896858e-02, 'conv_w': 1.065629e-01, 'conv_b': 1.082497e-01, 'w_pool': 9.431115e-02, 'pool_scale': 9.508934e-02, 'w_pa': 6.068306e-02, 'w_pb': 7.557470e-02, 'w_pc': 6.674864e-02, 'w_o': 1.176590e-01, 'final_g': 6.399650e+01}


def _to_microbatches(a, axis):
    t = _jnp.moveaxis(a, axis, 0)
    t = t.reshape((N_MICROBATCH, t.shape[0] // N_MICROBATCH) + t.shape[1:])
    return _jnp.moveaxis(t, 1, axis + 1)


def setup_inputs(seed: int = 0) -> dict:
    inp = _fwd_setup_inputs(seed)
    key = _jax.random.fold_in(_jax.random.key(seed), 7919)
    shape, _ = _output_shape()
    out = dict(inp)
    out["loss_target"] = _jax.random.normal(_jax.random.fold_in(key, 0), shape, _jnp.float32)
    for i, name in enumerate(TWIN_WEIGHTS):
        w = inp[name].astype(_jnp.float32)
        if MOMENT_SCALE is None:
            s = _jnp.sqrt(_jnp.mean(_jnp.square(w)) + 1e-30)
        else:
            s = MOMENT_SCALE[name]
        km, kv = _jax.random.split(_jax.random.fold_in(key, i + 1))
        out[name] = w
        out["m_" + name] = s * _jax.random.normal(km, w.shape, _jnp.float32)
        out["v_" + name] = (s * s) * _jax.random.uniform(kv, w.shape, _jnp.float32, 0.5, 1.5)
    if N_MICROBATCH > 1:
        for name, axis in PER_EXAMPLE_BATCH_AXIS.items():
            out[name] = _to_microbatches(out[name], axis)
    return {'x': out['x'], 'norm_g': out['norm_g'], 'w_in': out['w_in'], 'ln_g': out['ln_g'], 'ln_b': out['ln_b'], 'w_s': out['w_s'], 'b_s': out['b_s'], 'conv_w': out['conv_w'], 'conv_b': out['conv_b'], 'w_pool': out['w_pool'], 'pool_scale': out['pool_scale'], 'w_pa': out['w_pa'], 'w_pb': out['w_pb'], 'w_pc': out['w_pc'], 'w_o': out['w_o'], 'final_g': out['final_g'], 'loss_target': out['loss_target'], 'm_norm_g': out['m_norm_g'], 'm_w_in': out['m_w_in'], 'm_ln_g': out['m_ln_g'], 'm_ln_b': out['m_ln_b'], 'm_w_s': out['m_w_s'], 'm_b_s': out['m_b_s'], 'm_conv_w': out['m_conv_w'], 'm_conv_b': out['m_conv_b'], 'm_w_pool': out['m_w_pool'], 'm_pool_scale': out['m_pool_scale'], 'm_w_pa': out['m_w_pa'], 'm_w_pb': out['m_w_pb'], 'm_w_pc': out['m_w_pc'], 'm_w_o': out['m_w_o'], 'm_final_g': out['m_final_g'], 'v_norm_g': out['v_norm_g'], 'v_w_in': out['v_w_in'], 'v_ln_g': out['v_ln_g'], 'v_ln_b': out['v_ln_b'], 'v_w_s': out['v_w_s'], 'v_b_s': out['v_b_s'], 'v_conv_w': out['v_conv_w'], 'v_conv_b': out['v_conv_b'], 'v_w_pool': out['v_w_pool'], 'v_pool_scale': out['v_pool_scale'], 'v_w_pa': out['v_w_pa'], 'v_w_pb': out['v_w_pb'], 'v_w_pc': out['v_w_pc'], 'v_w_o': out['v_w_o'], 'v_final_g': out['v_final_g']}


def _loss(weights, diff, rest, loss_target):
    with _jax.named_scope("forward"):
        args = {**rest, TWIN_DIFF_INPUT: diff, **{k: w.astype(_WEIGHT_DTYPES[k]) for k, w in weights.items()}}
        y = _forward(args)
    with _jax.named_scope("loss_head"):
        err = _jnp.square(y.astype(_jnp.float32) - loss_target)
        return 0.5 * _jnp.sum(_jnp.mean(err, axis=-1)) if err.ndim else 0.5 * err


def _adamw(w, g, m, v):
    m = ADAM_B1 * m + (1.0 - ADAM_B1) * g
    v = ADAM_B2 * v + (1.0 - ADAM_B2) * _jnp.square(g)
    m_hat = m / (1.0 - ADAM_B1 ** ADAM_STEP)
    v_hat = v / (1.0 - ADAM_B2 ** ADAM_STEP)
    delta = -ADAM_LR * (m_hat / (_jnp.sqrt(v_hat) + ADAM_EPS) + ADAM_WD * w)
    return delta, m, v


def reference(x, norm_g, w_in, ln_g, ln_b, w_s, b_s, conv_w, conv_b, w_pool, pool_scale, w_pa, w_pb, w_pc, w_o, final_g, loss_target, m_norm_g, m_w_in, m_ln_g, m_ln_b, m_w_s, m_b_s, m_conv_w, m_conv_b, m_w_pool, m_pool_scale, m_w_pa, m_w_pb, m_w_pc, m_w_o, m_final_g, v_norm_g, v_w_in, v_ln_g, v_ln_b, v_w_s, v_b_s, v_conv_w, v_conv_b, v_w_pool, v_pool_scale, v_w_pa, v_w_pb, v_w_pc, v_w_o, v_final_g):
    given = dict(x=x, norm_g=norm_g, w_in=w_in, ln_g=ln_g, ln_b=ln_b, w_s=w_s, b_s=b_s, conv_w=conv_w, conv_b=conv_b, w_pool=w_pool, pool_scale=pool_scale, w_pa=w_pa, w_pb=w_pb, w_pc=w_pc, w_o=w_o, final_g=final_g, loss_target=loss_target, m_norm_g=m_norm_g, m_w_in=m_w_in, m_ln_g=m_ln_g, m_ln_b=m_ln_b, m_w_s=m_w_s, m_b_s=m_b_s, m_conv_w=m_conv_w, m_conv_b=m_conv_b, m_w_pool=m_w_pool, m_pool_scale=m_pool_scale, m_w_pa=m_w_pa, m_w_pb=m_w_pb, m_w_pc=m_w_pc, m_w_o=m_w_o, m_final_g=m_final_g, v_norm_g=v_norm_g, v_w_in=v_w_in, v_ln_g=v_ln_g, v_ln_b=v_ln_b, v_w_s=v_w_s, v_b_s=v_b_s, v_conv_w=v_conv_w, v_conv_b=v_conv_b, v_w_pool=v_w_pool, v_pool_scale=v_pool_scale, v_w_pa=v_w_pa, v_w_pb=v_w_pb, v_w_pc=v_w_pc, v_w_o=v_w_o, v_final_g=v_final_g)
    weights = {n: given[n] for n in TWIN_WEIGHTS}
    shared = {n: given[n] for n in SHARED_INPUTS}
    per_example = {n: given[n] for n in ['x']}
    grad_fn = _jax.value_and_grad(_loss, argnums=(0, 1))

    def one_microbatch(ex, loss_target):
        ex = dict(ex)
        diff = ex.pop(TWIN_DIFF_INPUT)
        return grad_fn(weights, diff, {**shared, **ex}, loss_target)

    if N_MICROBATCH == 1:
        loss, (grad_w, grad_x) = one_microbatch(per_example, given["loss_target"])
    else:
        def body(carry, xs):
            loss_sum, grad_sum = carry
            l_k, (gw_k, gx_k) = one_microbatch(xs[0], xs[1])
            with _jax.named_scope("update"):
                return (loss_sum + l_k, _jax.tree.map(_jnp.add, grad_sum, gw_k)), gx_k

        init = (_jnp.zeros((), _jnp.float32), _jax.tree.map(_jnp.zeros_like, weights))
        (loss, grad_w), grad_x = _jax.lax.scan(body, init, (per_example, given["loss_target"]))
    with _jax.named_scope("update"):
        delta_w, new_m, new_v = {}, {}, {}
        for n in TWIN_WEIGHTS:
            delta_w[n], new_m[n], new_v[n] = _adamw(weights[n], grad_w[n], given["m_" + n], given["v_" + n])
    return (loss, grad_x, *[grad_w[n] for n in TWIN_WEIGHTS], *[delta_w[n] for n in TWIN_WEIGHTS],
            *[new_m[n] for n in TWIN_WEIGHTS], *[new_v[n] for n in TWIN_WEIGHTS])
```

```python
import functools

import jax
import jax.numpy as jnp
from jax import lax
from jax.experimental import pallas as pl
from jax.experimental.pallas import tpu as pltpu

F32 = jnp.float32
BF16 = jnp.bfloat16

D_MODEL = 1024
DEPTH = 2
CHUNK = 128
WIDTH = 512
POOL_WINDOWS = (2, 4, 8, 16)
IN_TOTAL = 7680
N_CHIPS = 4
SHARD_W = IN_TOTAL // N_CHIPS
RMS_EPS = 1e-6
LN_EPS = 1e-5
HALO = 16

U, V, ZA, XB, BG, CG, ZB, XC, ZC, GA, GB, GC = (0, 512, 1024, 1536, 2048, 2560, 3072, 3584, 4096, 4608, 5632, 6656)

ADAM_LR = 0.001
ADAM_B1 = 0.9
ADAM_B2 = 0.999
ADAM_EPS = 1e-08
ADAM_WD = 0.01
ADAM_STEP = 10

VMEM_LIMIT = 56 * 1024 * 1024
MESH = pl.DeviceIdType.MESH
ANY = pl.BlockSpec(memory_space=pl.ANY)
NT = (((1,), (1,)), ((), ()))
TN = (((0,), (0,)), ((), ()))


def _params(sem=None):
    kw = dict(vmem_limit_bytes=VMEM_LIMIT)
    if sem is not None:
        kw["dimension_semantics"] = sem
    return pltpu.CompilerParams(**kw)


def _dot(a, b):
    return jnp.dot(a, b, preferred_element_type=F32)


def _dotg(a, b, dims):
    return lax.dot_general(a, b, dims, preferred_element_type=F32)


def _sigmoid(x):
    return 1.0 / (1.0 + jnp.exp(-x))


_GELU_K = 0.7978845608028654


def _gelu(x):
    th = jnp.tanh(_GELU_K * (x + 0.044715 * (x * x * x)))
    return 0.5 * x * (1.0 + th), th


def _gelu_grad(x, th):
    return 0.5 * (1.0 + th) + 0.5 * x * (1.0 - th * th) * (_GELU_K * (1.0 + 3.0 * 0.044715 * (x * x)))


def _colsum8(x):
    t, c = x.shape
    return jnp.sum(x.reshape(t // 8, 8, c), axis=0)


def _branches_fwd(p_ref, hxb_ref, hcg_ref, hxc_ref, first, tstart, w, sg_scr):
    t = p_ref.shape[0]
    nch = t // CHUNK

    def seg(o, width=WIDTH):
        return p_ref[:, o:o + width].astype(F32)

    lo = lax.broadcasted_iota(jnp.int32, (CHUNK, CHUNK), 1) < 64
    r = {}
    pu = seg(U)
    u_act, th_u = _gelu(pu)
    pv = seg(V)
    vg, th_v = _gelu(pv)
    mu = jnp.mean(vg, axis=-1, keepdims=True)
    xc = vg - mu
    var = jnp.mean(xc * xc, axis=-1, keepdims=True)
    rs = lax.rsqrt(var + LN_EPS)
    vhat = xc * rs
    vn = vhat * w["ln_g"][...] + w["ln_b"][...]
    vnb = vn.astype(BF16)
    for n in range(nch):
        for j in range(4):
            vb = vnb[n * CHUNK:(n + 1) * CHUNK, j * 128:(j + 1) * 128]
            z = _dot(w["w2"][j], vb)
            sg_scr[n * CHUNK:(n + 1) * CHUNK, j * 128:(j + 1) * 128] = (
                jnp.where(lo, z[:CHUNK], z[CHUNK:]) + w["bst"][:, j * 128:(j + 1) * 128])
    sg = sg_scr[...]
    a_br = u_act * sg
    za = seg(ZA)
    sa = _sigmoid(za)
    r.update(pu=pu, th_u=th_u, pv=pv, th_v=th_v, rs=rs, vhat=vhat, vnb=vnb, u_act=u_act, sg=sg,
             a_br=a_br, za=za, sa=sa, a_in=a_br * (za * sa))

    xb = seg(XB)
    cg = seg(CG)
    yb0 = cg * xb
    hal = hcg_ref[...].astype(F32) * hxb_ref[...].astype(F32)
    hal = jnp.where(first, 0.0, hal)
    ext = jnp.concatenate([hal, yb0], axis=0)
    y1 = pltpu.roll(ext, 1, 0)[HALO:]
    y2 = pltpu.roll(ext, 2, 0)[HALO:]
    cw = w["cw"]
    conv = cw[0:1, :] * y2 + cw[1:2, :] * y1 + cw[2:3, :] * yb0 + w["cb"][...]
    bg = seg(BG)
    b_br = bg * conv
    zb = seg(ZB)
    sb = _sigmoid(zb)
    r.update(xb=xb, cg=cg, yb0=yb0, y1=y1, y2=y2, conv=conv, bg=bg, b_br=b_br, zb=zb, sb=sb,
             b_in=b_br * (zb * sb))

    xcv = seg(XC)
    hxc = jnp.where(first, 0.0, hxc_ref[...].astype(F32))
    extc = jnp.concatenate([hxc, xcv], axis=0)
    tpos = tstart + lax.broadcasted_iota(jnp.int32, (t, 1), 0) + 1
    pooled, inv, q = [], [], []
    for gi, win in enumerate(POOL_WINDOWS):
        s = extc[:, gi * 128:(gi + 1) * 128]
        sh = 1
        while sh < win:
            s = s + pltpu.roll(s, sh, 0)
            sh *= 2
        inv_g = jnp.where(tpos >= win, 1.0 / win, 1.0 / jnp.minimum(tpos, win).astype(F32))
        pg = s[HALO:] * inv_g - xcv[:, gi * 128:(gi + 1) * 128]
        pooled.append(pg)
        inv.append(inv_g)
        q.append(_dot(pg.astype(BF16), w["wpool"][gi]))
    qv = jnp.concatenate(q, axis=1)
    c_br = qv * w["ps"][...]
    zc = seg(ZC)
    sc = _sigmoid(zc)
    r.update(pooled=pooled, inv=inv, q=qv, c_br=c_br, zc=zc, sc=sc, c_in=c_br * (zc * sc))
    return r


def _halo_specs(t, rev_n=None):
    def imap(col):
        def f(i):
            ti = i if rev_n is None else rev_n - 1 - i
            return (jnp.maximum(ti * (t // HALO) - 1, 0), col)
        return f
    return [pl.BlockSpec((HALO, WIDTH), imap(XB // WIDTH)),
            pl.BlockSpec((HALO, WIDTH), imap(CG // WIDTH)),
            pl.BlockSpec((HALO, WIDTH), imap(XC // WIDTH))]


def _const_spec(shape):
    nd = len(shape)
    return pl.BlockSpec(shape, lambda *_: (0,) * nd)


def _f1(x, norm_g3, wfull, l, tm=512):
    s = x.shape[0]

    def body(x_ref, g_ref, w_ref, p_ref, h_ref):
        @pl.when(pl.program_id(1) == 0)
        def _():
            xv = x_ref[...]
            r = lax.rsqrt(jnp.mean(xv * xv, axis=-1, keepdims=True) + RMS_EPS)
            h_ref[...] = ((xv * r) * g_ref[...]).astype(BF16)
        p_ref[...] = _dot(h_ref[...], w_ref[...]).astype(BF16)

    return pl.pallas_call(
        body, name=f"f1_l{l}", grid=(s // tm, N_CHIPS),
        in_specs=[pl.BlockSpec((tm, D_MODEL), lambda i, k: (i, 0)),
                  pl.BlockSpec((None, 1, D_MODEL), lambda i, k: (l, 0, 0)),
                  pl.BlockSpec((None, None, D_MODEL, SHARD_W), lambda i, k: (k, l, 0, 0))],
        out_specs=[pl.BlockSpec((tm, SHARD_W), lambda i, k: (i, k)),
                   pl.BlockSpec((tm, D_MODEL), lambda i, k: (i, 0))],
        out_shape=[jax.ShapeDtypeStruct((s, IN_TOTAL), BF16), jax.ShapeDtypeStruct((s, D_MODEL), BF16)],
        compiler_params=_params(("arbitrary", "arbitrary")),
    )(x, norm_g3, wfull)


def _f2(x, p, lw, l, t=256):
    s = x.shape[0]
    n = s // t

    def body(p_ref, hxb_ref, hcg_ref, hxc_ref, x_ref, lng, lnb, w2, bst, cw, cb, wpool, ps,
             wpa, wpb, wpc, wo, ya_ref, yb_ref, yc_ref, m_ref, xo_ref, sg_scr):
        i = pl.program_id(0)
        w = dict(ln_g=lng, ln_b=lnb, w2=w2, bst=bst, cw=cw, cb=cb, wpool=wpool, ps=ps)
        r = _branches_fwd(p_ref, hxb_ref, hcg_ref, hxc_ref, i == 0, i * t, w, sg_scr)
        ya = _dot(r["a_in"].astype(BF16), wpa[...])
        yb = _dot(r["b_in"].astype(BF16), wpb[...])
        yc = _dot(r["c_in"].astype(BF16), wpc[...])
        ya_ref[...] = ya.astype(BF16)
        yb_ref[...] = yb.astype(BF16)
        yc_ref[...] = yc.astype(BF16)
        m = (_sigmoid(p_ref[:, GA:GA + D_MODEL].astype(F32)) * ya
             + _sigmoid(p_ref[:, GB:GB + D_MODEL].astype(F32)) * yb
             + _sigmoid(p_ref[:, GC:GC + D_MODEL].astype(F32)) * yc)
        mb = m.astype(BF16)
        m_ref[...] = mb
        xo_ref[...] = x_ref[...] + _dot(mb, wo[...])

    tile = lambda c: pl.BlockSpec((t, c), lambda i: (i, 0))
    lsel = lambda *blk: pl.BlockSpec((None,) + blk, lambda i: (l,) + (0,) * len(blk))
    act = jax.ShapeDtypeStruct((s, D_MODEL), BF16)
    return pl.pallas_call(
        body, name=f"f2_l{l}", grid=(n,),
        in_specs=[tile(IN_TOTAL)] + _halo_specs(t) + [
            tile(D_MODEL), lsel(1, WIDTH), lsel(1, WIDTH), lsel(4, 256, 128), lsel(CHUNK, WIDTH),
            lsel(8, WIDTH), lsel(1, WIDTH), lsel(4, 128, 128), lsel(1, WIDTH),
            lsel(WIDTH, D_MODEL), lsel(WIDTH, D_MODEL), lsel(WIDTH, D_MODEL), lsel(D_MODEL, D_MODEL)],
        out_specs=[tile(D_MODEL)] * 5,
        out_shape=[act, act, act, act, jax.ShapeDtypeStruct((s, D_MODEL), F32)],
        scratch_shapes=[pltpu.VMEM((t, WIDTH), F32)],
        compiler_params=_params(("arbitrary",)),
    )(p, p, p, p, x, lw["ln_g"], lw["ln_b"], lw["w2"], lw["bst"], lw["cw"], lw["cb"], lw["wpool"], lw["ps"],
      lw["wpa"], lw["wpb"], lw["wpc"], lw["wo"])


def _b1(p, dout, ya, yb, yc, lw, l, t=256):
    s = p.shape[0]
    n = s // t
    nch = t // CHUNK

    def body(p_ref, hxb_ref, hcg_ref, hxc_ref, do_ref, ya_ref, yb_ref, yc_ref,
             lng, lnb, w2, wt2, bst, cw, cb, wpool, ps, wpa_h, wpb_h, wpc_h, wo_h, sel_ref,
             dp_ref, gwpa_h, gwpb_h, gwpc_h, gws_ref, gbs_ref, glng_ref, glnb_ref, gcw_ref, gcb_ref,
             gwpool_ref, gps_ref,
             wpa, wpb, wpc, wo, gwpa, gwpb, gwpc, gbs_acc, vec_acc, sg_scr, dvn_scr, car_dc, car_e):
        i = pl.program_id(0)
        ti = n - 1 - i

        @pl.when(i == 0)
        def _():
            pltpu.sync_copy(wpa_h.at[l], wpa)
            pltpu.sync_copy(wpb_h.at[l], wpb)
            pltpu.sync_copy(wpc_h.at[l], wpc)
            pltpu.sync_copy(wo_h.at[l], wo)
            for acc in (gwpa, gwpb, gwpc, gbs_acc, vec_acc, car_dc, car_e):
                acc[...] = jnp.zeros(acc.shape, acc.dtype)
            gws_ref[...] = jnp.zeros(gws_ref.shape, F32)
            gwpool_ref[...] = jnp.zeros(gwpool_ref.shape, F32)

        w = dict(ln_g=lng, ln_b=lnb, w2=w2, bst=bst, cw=cw, cb=cb, wpool=wpool, ps=ps)
        r = _branches_fwd(p_ref, hxb_ref, hcg_ref, hxc_ref, ti == 0, ti * t, w, sg_scr)

        def seg(o, width=WIDTH):
            return p_ref[:, o:o + width].astype(F32)

        def put(o, val):
            dp_ref[:, o:o + val.shape[1]] = val.astype(BF16)

        dob = do_ref[...].astype(BF16)
        dm = _dotg(dob, wo[...], NT)

        def merge_bwd(goff, y_ref, xin, wp, gwp):
            sx = _sigmoid(seg(goff, D_MODEL))
            dmy = dm * sx
            put(goff, dmy * y_ref[...].astype(F32) * (1.0 - sx))
            dyb = dmy.astype(BF16)
            gwp[...] += _dotg(xin.astype(BF16), dyb, TN)
            return _dotg(dyb, wp[...], NT)

        d_ain = merge_bwd(GA, ya_ref, r["a_in"], wpa, gwpa)
        d_bin = merge_bwd(GB, yb_ref, r["b_in"], wpb, gwpb)
        d_cin = merge_bwd(GC, yc_ref, r["c_in"], wpc, gwpc)

        def dsilu(z, sz):
            return sz * (1.0 + z * (1.0 - sz))

        za, sa = r["za"], r["sa"]
        d_abr = d_ain * (za * sa)
        put(ZA, d_ain * r["a_br"] * dsilu(za, sa))
        put(U, d_abr * r["sg"] * _gelu_grad(r["pu"], r["th_u"]))
        d_sg = d_abr * r["u_act"]
        dsgb = d_sg.astype(BF16)
        lo = lax.broadcasted_iota(jnp.int32, (CHUNK, CHUNK), 1) < 64
        zero = jnp.zeros((CHUNK, CHUNK), BF16)
        for c in range(nch):
            rows = slice(c * CHUNK, (c + 1) * CHUNK)
            gbs_acc[...] += d_sg[rows]
            for j in range(4):
                cols = slice(j * 128, (j + 1) * 128)
                dj = dsgb[rows, cols]
                zt = _dot(wt2[j], dj)
                dvn_scr[rows, cols] = jnp.where(lo, zt[:CHUNK], zt[CHUNK:])
                stacked = jnp.concatenate([jnp.where(lo, dj, zero), jnp.where(lo, zero, dj)], axis=0)
                gws_ref[j] += _dotg(stacked, r["vnb"][rows, cols], NT)
        d_vn = dvn_scr[...]
        vhat = r["vhat"]
        vec_acc[0] += _colsum8(d_vn * vhat)
        vec_acc[1] += _colsum8(d_vn)
        d_vhat = d_vn * lng[...]
        d_vg = r["rs"] * (d_vhat - jnp.mean(d_vhat, axis=-1, keepdims=True)
                          - vhat * jnp.mean(d_vhat * vhat, axis=-1, keepdims=True))
        put(V, d_vg * _gelu_grad(r["pv"], r["th_v"]))

        zb, sb = r["zb"], r["sb"]
        d_bbr = d_bin * (zb * sb)
        put(ZB, d_bin * r["b_br"] * dsilu(zb, sb))
        put(BG, d_bbr * r["conv"])
        dc = d_bbr * r["bg"]
        vec_acc[2] += _colsum8(dc)
        vec_acc[3] += _colsum8(dc * r["y2"])
        vec_acc[4] += _colsum8(dc * r["y1"])
        vec_acc[5] += _colsum8(dc * r["yb0"])
        ext = jnp.concatenate([dc, car_dc[...]], axis=0)
        ne = t + HALO
        d1 = pltpu.roll(ext, ne - 1, 0)[:t]
        d2 = pltpu.roll(ext, ne - 2, 0)[:t]
        d_yb0 = cw[2:3, :] * dc + cw[1:2, :] * d1 + cw[0:1, :] * d2
        put(CG, d_yb0 * r["xb"])
        put(XB, d_yb0 * r["cg"])
        car_dc[...] = dc[:HALO]

        zc, sc = r["zc"], r["sc"]
        d_cbr = d_cin * (zc * sc)
        put(ZC, d_cin * r["c_br"] * dsilu(zc, sc))
        vec_acc[6] += _colsum8(d_cbr * r["q"])
        d_q = d_cbr * ps[...]
        for gi, win in enumerate(POOL_WINDOWS):
            cols = slice(gi * 128, (gi + 1) * 128)
            dqb = d_q[:, cols].astype(BF16)
            d_pool = _dotg(dqb, wpool[gi], NT)
            gwpool_ref[gi] += _dotg(r["pooled"][gi].astype(BF16), dqb, TN)
            e = d_pool * r["inv"][gi]
            sx = jnp.concatenate([e, car_e[:, cols]], axis=0)
            sh = 1
            while sh < win:
                sx = sx + pltpu.roll(sx, ne - sh, 0)
                sh *= 2
            put(XC + gi * 128, sx[:t] - d_pool)
            car_e[:, cols] = e[:HALO]

        @pl.when(i == n - 1)
        def _():
            pltpu.sync_copy(gwpa, gwpa_h)
            pltpu.sync_copy(gwpb, gwpb_h)
            pltpu.sync_copy(gwpc, gwpc_h)
            gbs_ref[...] = jnp.dot(gbs_acc[...], sel_ref[...], preferred_element_type=F32,
                                   precision=lax.Precision.HIGHEST)
            red = lambda k: jnp.sum(vec_acc[k], axis=0, keepdims=True)
            glng_ref[...] = red(0)
            glnb_ref[...] = red(1)
            gcb_ref[...] = red(2)
            gcw_ref[...] = jnp.zeros(gcw_ref.shape, F32)
            for k in range(3):
                gcw_ref[k:k + 1, :] = red(3 + k)
            gps_ref[...] = red(6)
            tt = lax.broadcasted_iota(jnp.int32, (2 * CHUNK, CHUNK), 0) % CHUNK
            ss = lax.broadcasted_iota(jnp.int32, (2 * CHUNK, CHUNK), 1)
            for j in range(4):
                gws_ref[j] = jnp.where(tt >= ss, gws_ref[j], 0.0)

    rtile = lambda c: pl.BlockSpec((t, c), lambda i: (n - 1 - i, 0))
    lsel = lambda *blk: pl.BlockSpec((None,) + blk, lambda i: (l,) + (0,) * len(blk))
    f32s = lambda *shape: jax.ShapeDtypeStruct(shape, F32)
    return pl.pallas_call(
        body, name=f"b1_l{l}", grid=(n,),
        in_specs=[rtile(IN_TOTAL)] + _halo_specs(t, rev_n=n) + [rtile(D_MODEL)] * 4 + [
            lsel(1, WIDTH), lsel(1, WIDTH), lsel(4, 256, 128), lsel(4, 256, 128), lsel(CHUNK, WIDTH),
            lsel(8, WIDTH), lsel(1, WIDTH), lsel(4, 128, 128), lsel(1, WIDTH),
            ANY, ANY, ANY, ANY, _const_spec((WIDTH, 128))],
        out_specs=[rtile(IN_TOTAL), ANY, ANY, ANY,
                   _const_spec((4, 256, 128)), _const_spec((CHUNK, 128)), _const_spec((1, WIDTH)),
                   _const_spec((1, WIDTH)), _const_spec((8, WIDTH)), _const_spec((1, WIDTH)),
                   _const_spec((4, 128, 128)), _const_spec((1, WIDTH))],
        out_shape=[jax.ShapeDtypeStruct((s, IN_TOTAL), BF16), f32s(WIDTH, D_MODEL),
                   f32s(WIDTH, D_MODEL), f32s(WIDTH, D_MODEL), f32s(4, 256, 128), f32s(CHUNK, 128),
                   f32s(1, WIDTH), f32s(1, WIDTH), f32s(8, WIDTH), f32s(1, WIDTH), f32s(4, 128, 128),
                   f32s(1, WIDTH)],
        scratch_shapes=[pltpu.VMEM((WIDTH, D_MODEL), BF16), pltpu.VMEM((WIDTH, D_MODEL), BF16),
                        pltpu.VMEM((WIDTH, D_MODEL), BF16), pltpu.VMEM((D_MODEL, D_MODEL), BF16),
                        pltpu.VMEM((WIDTH, D_MODEL), F32),
                        pltpu.VMEM((WIDTH, D_MODEL), F32), pltpu.VMEM((WIDTH, D_MODEL), F32),
                        pltpu.VMEM((CHUNK, WIDTH), F32), pltpu.VMEM((8, 8, WIDTH), F32),
                        pltpu.VMEM((t, WIDTH), F32), pltpu.VMEM((t, WIDTH), F32),
                        pltpu.VMEM((HALO, WIDTH), F32), pltpu.VMEM((HALO, WIDTH), F32)],
        compiler_params=_params(("arbitrary",)),
    )(p, p, p, p, dout, ya, yb, yc, lw["ln_g"], lw["ln_b"], lw["w2"], lw["wt2"], lw["bst"], lw["cw"],
      lw["cb"], lw["wpool"], lw["ps"], lw["wpa"], lw["wpb"], lw["wpc"], lw["wo"], lw["sel"])


def _rms_bwd(xv, g, dh):
    r = lax.rsqrt(jnp.mean(xv * xv, axis=-1, keepdims=True) + RMS_EPS)
    xhat = xv * r
    dxh = dh * g
    dx = r * (dxh - xhat * jnp.mean(dxh * xhat, axis=-1, keepdims=True))
    return dx, dh * xhat


def _b2a(dp, wfull, x, dout, norm_g3, l, tm=512):
    s = x.shape[0]
    nm = s // tm

    def body(dp_ref, w_ref, x_ref, do_ref, g_ref, dx_ref, gg_ref, acc, gacc):
        i, k = pl.program_id(0), pl.program_id(1)

        @pl.when(jnp.logical_and(i == 0, k == 0))
        def _():
            gacc[...] = jnp.zeros(gacc.shape, F32)

        prod = _dotg(dp_ref[...], w_ref[...], NT)

        @pl.when(k == 0)
        def _():
            acc[...] = prod

        @pl.when(k > 0)
        def _():
            acc[...] += prod

        @pl.when(k == N_CHIPS - 1)
        def _():
            dx, gx = _rms_bwd(x_ref[...], g_ref[...], acc[...])
            dx_ref[...] = do_ref[...] + dx
            gacc[...] += _colsum8(gx)

        @pl.when(jnp.logical_and(i == nm - 1, k == N_CHIPS - 1))
        def _():
            gg_ref[...] = jnp.sum(gacc[...], axis=0, keepdims=True)

    return pl.pallas_call(
        body, name=f"b2a_l{l}", grid=(nm, N_CHIPS),
        in_specs=[pl.BlockSpec((tm, SHARD_W), lambda i, k: (i, k)),
                  pl.BlockSpec((None, None, D_MODEL, SHARD_W), lambda i, k: (k, l, 0, 0)),
                  pl.BlockSpec((tm, D_MODEL), lambda i, k: (i, 0)),
                  pl.BlockSpec((tm, D_MODEL), lambda i, k: (i, 0)),
                  pl.BlockSpec((None, 1, D_MODEL), lambda i, k: (l, 0, 0))],
        out_specs=[pl.BlockSpec((tm, D_MODEL), lambda i, k: (i, 0)),
                   pl.BlockSpec((1, D_MODEL), lambda i, k: (0, 0))],
        out_shape=[jax.ShapeDtypeStruct((s, D_MODEL), F32), jax.ShapeDtypeStruct((1, D_MODEL), F32)],
        scratch_shapes=[pltpu.VMEM((tm, D_MODEL), F32), pltpu.VMEM((8, D_MODEL), F32)],
        compiler_params=_params(("arbitrary", "arbitrary")),
    )(dp, wfull, x, dout, norm_g3)


def _b2b(h, dp, l, tk=512):
    s = h.shape[0]
    nk = s // tk

    def body(h_ref, dp_ref, g_ref, acc):
        kk = pl.program_id(1)
        prod = _dotg(h_ref[...], dp_ref[...], TN)

        @pl.when(kk == 0)
        def _():
            acc[...] = prod

        @pl.when(kk > 0)
        def _():
            acc[...] += prod

        @pl.when(kk == nk - 1)
        def _():
            g_ref[...] = acc[...].astype(BF16)

    return pl.pallas_call(
        body, name=f"b2b_l{l}", grid=(N_CHIPS, nk),
        in_specs=[pl.BlockSpec((tk, D_MODEL), lambda k, kk: (kk, 0)),
                  pl.BlockSpec((tk, SHARD_W), lambda k, kk: (kk, k))],
        out_specs=pl.BlockSpec((None, D_MODEL, SHARD_W), lambda k, kk: (k, 0, 0)),
        out_shape=jax.ShapeDtypeStruct((N_CHIPS, D_MODEL, SHARD_W), BF16),
        scratch_shapes=[pltpu.VMEM((D_MODEL, SHARD_W), F32)],
        compiler_params=_params(("arbitrary", "arbitrary")),
    )(h, dp)


def _gwo(m, dout, l, tk=512):
    s = m.shape[0]
    nk = s // tk

    def body(m_ref, do_ref, g_ref, acc):
        kk = pl.program_id(0)
        prod = _dotg(m_ref[...], do_ref[...].astype(BF16), TN)

        @pl.when(kk == 0)
        def _():
            acc[...] = prod

        @pl.when(kk > 0)
        def _():
            acc[...] += prod

        @pl.when(kk == nk - 1)
        def _():
            g_ref[...] = acc[...].astype(BF16)

    return pl.pallas_call(
        body, name=f"gwo_l{l}", grid=(nk,),
        in_specs=[pl.BlockSpec((tk, D_MODEL), lambda kk: (kk, 0)), pl.BlockSpec((tk, D_MODEL), lambda kk: (kk, 0))],
        out_specs=_const_spec((D_MODEL, D_MODEL)),
        out_shape=jax.ShapeDtypeStruct((D_MODEL, D_MODEL), BF16),
        scratch_shapes=[pltpu.VMEM((D_MODEL, D_MODEL), F32)],
        compiler_params=_params(("arbitrary",)),
    )(m, dout)


def _loss_head(x, tgt, final_g2, tm=512):
    s = x.shape[0]
    nm = s // tm

    def body(x_ref, t_ref, g_ref, dx_ref, loss_ref, gg_ref, lacc, gacc):
        i = pl.program_id(0)

        @pl.when(i == 0)
        def _():
            lacc[...] = jnp.zeros(lacc.shape, F32)
            gacc[...] = jnp.zeros(gacc.shape, F32)

        xv = x_ref[...]
        g = g_ref[...]
        r = lax.rsqrt(jnp.mean(xv * xv, axis=-1, keepdims=True) + RMS_EPS)
        err = (xv * r) * g - t_ref[...]
        lacc[...] += _colsum8(err * err)
        dx, gx = _rms_bwd(xv, g, err * (1.0 / D_MODEL))
        dx_ref[...] = dx
        gacc[...] += _colsum8(gx)

        @pl.when(i == nm - 1)
        def _():
            tot = jnp.sum(jnp.sum(lacc[...], axis=0, keepdims=True), axis=1, keepdims=True)
            loss_ref[...] = jnp.broadcast_to(tot * (0.5 / D_MODEL), loss_ref.shape)
            gg_ref[...] = jnp.sum(gacc[...], axis=0, keepdims=True)

    return pl.pallas_call(
        body, name="loss_head", grid=(nm,),
        in_specs=[pl.BlockSpec((tm, D_MODEL), lambda i: (i, 0)), pl.BlockSpec((tm, D_MODEL), lambda i: (i, 0)),
                  _const_spec((1, D_MODEL))],
        out_specs=[pl.BlockSpec((tm, D_MODEL), lambda i: (i, 0)), _const_spec((8, 128)),
                   _const_spec((1, D_MODEL))],
        out_shape=[jax.ShapeDtypeStruct((s, D_MODEL), F32), jax.ShapeDtypeStruct((8, 128), F32),
                   jax.ShapeDtypeStruct((1, D_MODEL), F32)],
        scratch_shapes=[pltpu.VMEM((8, D_MODEL), F32), pltpu.VMEM((8, D_MODEL), F32)],
        compiler_params=_params(("arbitrary",)),
    )(x, tgt, final_g2)


def _row_block(rows, cols, n_arrays):
    budget = VMEM_LIMIT // 3 // (2 * 4 * n_arrays * cols)
    rb = rows
    while rb > budget and rb % 16 == 0:
        rb //= 2
    return rb


def _cast_bf16(name, a):
    rows, cols = a.shape
    rb = _row_block(rows, cols, 2)

    def body(a_ref, o_ref):
        o_ref[...] = a_ref[...].astype(BF16)

    spec = pl.BlockSpec((rb, cols), lambda i: (i, 0))
    return pl.pallas_call(body, name=name, grid=(rows // rb,), in_specs=[spec], out_specs=spec,
                          out_shape=jax.ShapeDtypeStruct(a.shape, BF16), compiler_params=_params(("arbitrary",)))(a)


def _sum4(name, own, own_index, recv):
    _, rows, cols = own.shape
    rb = _row_block(rows, cols, 5)

    def body(idx_ref, own_ref, r_ref, o_ref):
        o_ref[...] = ((own_ref[...].astype(F32) + r_ref[0].astype(F32)) + r_ref[1].astype(F32)) + r_ref[2].astype(F32)

    gs = pltpu.PrefetchScalarGridSpec(
        num_scalar_prefetch=1, grid=(rows // rb,),
        in_specs=[pl.BlockSpec((None, rb, cols), lambda i, idx: (idx[0], i, 0)),
                  pl.BlockSpec((3, rb, cols), lambda i, idx: (0, i, 0))],
        out_specs=pl.BlockSpec((rb, cols), lambda i, idx: (i, 0)))
    return pl.pallas_call(body, name=name, grid_spec=gs, out_shape=jax.ShapeDtypeStruct((rows, cols), F32),
                          compiler_params=_params(("arbitrary",)))(own_index, own, recv)


def _adamw_math(w, g, m, v):
    m = ADAM_B1 * m + (1.0 - ADAM_B1) * g
    v = ADAM_B2 * v + (1.0 - ADAM_B2) * (g * g)
    m_hat = m / (1.0 - ADAM_B1 ** ADAM_STEP)
    v_hat = v / (1.0 - ADAM_B2 ** ADAM_STEP)
    delta = -ADAM_LR * (m_hat / (jnp.sqrt(v_hat) + ADAM_EPS) + ADAM_WD * w)
    return delta, m, v


def _adamw(name, w, m, v, g_parts):
    rows, cols = w.shape
    np_ = len(g_parts)
    rb = _row_block(rows, cols, 7 + np_)

    def body(*refs):
        w_ref, m_ref, v_ref = refs[:3]
        g_refs = refs[3:3 + np_]
        go_ref, d_ref, mo_ref, vo_ref = refs[3 + np_:]
        g = g_refs[0][...]
        for gr in g_refs[1:]:
            g = g + gr[...]
        d, mn, vn = _adamw_math(w_ref[...], g, m_ref[...], v_ref[...])
        go_ref[...] = g
        d_ref[...] = d
        mo_ref[...] = mn
        vo_ref[...] = vn

    spec = pl.BlockSpec((rb, cols), lambda i: (i, 0))
    shp = jax.ShapeDtypeStruct((rows, cols), F32)
    return pl.pallas_call(body, name=name, grid=(rows // rb,), in_specs=[spec] * (3 + np_),
                          out_specs=[spec] * 4, out_shape=[shp] * 4,
                          compiler_params=_params(("arbitrary",)))(w, m, v, *g_parts)


def _place():
    return lax.axis_index("x"), lax.axis_index("y"), lax.axis_index("c")


def _chip_peer(x, y, jm):
    px = (1 - x) if (jm & 2) else x
    py = (1 - y) if (jm & 1) else y
    return px, py


def _all_gather_chips(name, shards):
    na = len(shards)

    def body(*refs):
        src = refs[:na]
        out = refs[na:2 * na]
        send_sems, recv_sems, local_sems = refs[2 * na:]
        x, y, c = _place()
        k_me = 2 * x + y
        local = [pltpu.make_async_copy(src[a], out[a].at[k_me], local_sems.at[a]) for a in range(na)]
        for cp in local:
            cp.start()
        sends, recvs = [], []
        for jj, jm in enumerate((1, 2, 3)):
            px, py = _chip_peer(x, y, jm)
            k_peer = 2 * px + py
            for a in range(na):
                sem = a * 3 + jj
                sends.append(pltpu.make_async_remote_copy(
                    src_ref=src[a], dst_ref=out[a].at[k_me], send_sem=send_sems.at[sem],
                    recv_sem=recv_sems.at[sem], device_id=(px, py, c), device_id_type=MESH))
                recvs.append(pltpu.make_async_remote_copy(
                    src_ref=src[a], dst_ref=out[a].at[k_peer], send_sem=send_sems.at[sem],
                    recv_sem=recv_sems.at[sem], device_id=(px, py, c), device_id_type=MESH))
        for cp in sends:
            cp.start()
        for cp in recvs:
            cp.wait_recv()
        for cp in sends:
            cp.wait_send()
        for cp in local:
            cp.wait()

    return pl.pallas_call(
        body, name=name, in_specs=[ANY] * na, out_specs=[ANY] * na,
        out_shape=[jax.ShapeDtypeStruct((N_CHIPS,) + s.shape, s.dtype) for s in shards],
        scratch_shapes=[pltpu.SemaphoreType.DMA((3 * na,)), pltpu.SemaphoreType.DMA((3 * na,)),
                        pltpu.SemaphoreType.DMA((na,))],
    )(*shards)


def _scatter_chips(name, parts):
    na = len(parts)

    def body(*refs):
        src = refs[:na]
        out = refs[na:2 * na]
        send_sems, recv_sems = refs[2 * na:]
        x, y, c = _place()
        sends, recvs = [], []
        for jj, jm in enumerate((1, 2, 3)):
            px, py = _chip_peer(x, y, jm)
            k_peer = 2 * px + py
            for a in range(na):
                sem = a * 3 + jj
                cp = pltpu.make_async_remote_copy(
                    src_ref=src[a].at[k_peer], dst_ref=out[a].at[jj], send_sem=send_sems.at[sem],
                    recv_sem=recv_sems.at[sem], device_id=(px, py, c), device_id_type=MESH)
                sends.append(cp)
                recvs.append(cp)
        for cp in sends:
            cp.start()
        for cp in recvs:
            cp.wait_recv()
        for cp in sends:
            cp.wait_send()

    return pl.pallas_call(
        body, name=name, in_specs=[ANY] * na, out_specs=[ANY] * na,
        out_shape=[jax.ShapeDtypeStruct((3,) + p.shape[1:], p.dtype) for p in parts],
        scratch_shapes=[pltpu.SemaphoreType.DMA((3 * na,)), pltpu.SemaphoreType.DMA((3 * na,))],
    )(*parts)


def _swap_sibling(name, arrs):
    na = len(arrs)

    def body(*refs):
        src = refs[:na]
        out = refs[na:2 * na]
        send_sems, recv_sems = refs[2 * na:]
        x, y, c = _place()
        cps = [pltpu.make_async_remote_copy(
            src_ref=src[a], dst_ref=out[a], send_sem=send_sems.at[a], recv_sem=recv_sems.at[a],
            device_id=(x, y, 1 - c), device_id_type=MESH) for a in range(na)]
        for cp in cps:
            cp.start()
        for cp in cps:
            cp.wait_recv()
        for cp in cps:
            cp.wait_send()

    return pl.pallas_call(
        body, name=name, in_specs=[ANY] * na, out_specs=[ANY] * na,
        out_shape=[jax.ShapeDtypeStruct(a.shape, a.dtype) for a in arrs],
        scratch_shapes=[pltpu.SemaphoreType.DMA((na,)), pltpu.SemaphoreType.DMA((na,))],
    )(*arrs)


def _all_reduce_small(name, a):
    def body(a_ref, o_ref, buf, send_sems, recv_sems):
        x, y, c = _place()
        o_ref[...] = a_ref[...]
        for rnd, peer in enumerate(((x, y, 1 - c), (x, 1 - y, c), (1 - x, y, c))):
            cp = pltpu.make_async_remote_copy(
                src_ref=o_ref, dst_ref=buf.at[rnd], send_sem=send_sems.at[rnd], recv_sem=recv_sems.at[rnd],
                device_id=peer, device_id_type=MESH)
            cp.start()
            cp.wait_recv()
            cp.wait_send()
            o_ref[...] = o_ref[...] + buf[rnd]

    vm = pl.BlockSpec(memory_space=pltpu.VMEM)
    return pl.pallas_call(
        body, name=name, in_specs=[vm], out_specs=vm, out_shape=jax.ShapeDtypeStruct(a.shape, F32),
        scratch_shapes=[pltpu.VMEM((3,) + a.shape, F32), pltpu.SemaphoreType.DMA((3,)),
                        pltpu.SemaphoreType.DMA((3,))],
        compiler_params=_params(),
    )(a)


_SMALL = ("norm_g", "ln_g", "ln_b", "w_s", "b_s", "conv_b", "w_pool", "pool_scale", "final_g")


def _rows(a):
    return a.reshape(-1, 128)


def _pad_rows(a, mult=8):
    pad = (-a.shape[0]) % mult
    return jnp.pad(a, ((0, pad), (0, 0))) if pad else a


def kernel(x, norm_g, w_in, ln_g, ln_b, w_s, b_s, conv_w, conv_b, w_pool, pool_scale, w_pa, w_pb, w_pc, w_o, final_g, loss_target, m_norm_g, m_w_in, m_ln_g, m_ln_b, m_w_s, m_b_s, m_conv_w, m_conv_b, m_w_pool, m_pool_scale, m_w_pa, m_w_pb, m_w_pc, m_w_o, m_final_g, v_norm_g, v_w_in, v_ln_g, v_ln_b, v_w_s, v_b_s, v_conv_w, v_conv_b, v_w_pool, v_pool_scale, v_w_pa, v_w_pb, v_w_pc, v_w_o, v_final_g):
    W = dict(norm_g=norm_g, w_in=w_in, ln_g=ln_g, ln_b=ln_b, w_s=w_s, b_s=b_s, conv_w=conv_w, conv_b=conv_b,
             w_pool=w_pool, pool_scale=pool_scale, w_pa=w_pa, w_pb=w_pb, w_pc=w_pc, w_o=w_o, final_g=final_g)
    M = dict(norm_g=m_norm_g, w_in=m_w_in, ln_g=m_ln_g, ln_b=m_ln_b, w_s=m_w_s, b_s=m_b_s, conv_w=m_conv_w,
             conv_b=m_conv_b, w_pool=m_w_pool, pool_scale=m_pool_scale, w_pa=m_w_pa, w_pb=m_w_pb, w_pc=m_w_pc,
             w_o=m_w_o, final_g=m_final_g)
    Vv = dict(norm_g=v_norm_g, w_in=v_w_in, ln_g=v_ln_g, ln_b=v_ln_b, w_s=v_w_s, b_s=v_b_s, conv_w=v_conv_w,
              conv_b=v_conv_b, w_pool=v_w_pool, pool_scale=v_pool_scale, w_pa=v_w_pa, w_pb=v_w_pb, w_pc=v_w_pc,
              w_o=v_w_o, final_g=v_final_g)
    L = DEPTH
    s = x.shape[1]
    xs = x.reshape(s, D_MODEL)
    tgt = loss_target.reshape(s, D_MODEL)
    k_me = (2 * lax.axis_index("x") + lax.axis_index("y")).astype(jnp.int32)

    w_in_b = _cast_bf16("cast_w_in", w_in.reshape(L * D_MODEL, SHARD_W)).reshape(L, D_MODEL, SHARD_W)
    proj = jnp.concatenate([w_pa.reshape(L * WIDTH, 256), w_pb.reshape(L * WIDTH, 256),
                            w_pc.reshape(L * WIDTH, 256)], axis=0)
    proj_b = _cast_bf16("cast_proj", proj).reshape(3, L, WIDTH, 256)
    w_o_b = _cast_bf16("cast_w_o", w_o.reshape(L * 256, D_MODEL)).reshape(L, 256, D_MODEL)
    cw_sh = jnp.pad(conv_w, ((0, 0), (0, 5), (0, 0)))
    g_w_in, g_proj, g_w_o, g_cw = _all_gather_chips("gather_weights", [w_in_b, proj_b, w_o_b, cw_sh])
    wfull = g_w_in
    projf = g_proj.transpose(1, 2, 3, 0, 4).reshape(3, L, WIDTH, D_MODEL)
    wof = g_w_o.transpose(1, 0, 2, 3).reshape(L, D_MODEL, D_MODEL)
    cwf = g_cw.transpose(1, 2, 0, 3).reshape(L, 8, WIDTH)

    causal = jnp.tril(jnp.ones((CHUNK, CHUNK), dtype=bool))
    w_m = jnp.where(causal, w_s, 0.0)
    lw = dict(
        ln_g=ln_g.reshape(L, 1, WIDTH), ln_b=ln_b.reshape(L, 1, WIDTH),
        w2=w_m.reshape(L, 4, 256, CHUNK).astype(BF16),
        wt2=jnp.swapaxes(w_m, -1, -2).reshape(L, 4, 256, CHUNK).astype(BF16),
        bst=jnp.repeat(jnp.swapaxes(b_s, -1, -2), 64, axis=-1),
        cw=cwf, cb=conv_b.reshape(L, 1, WIDTH), wpool=w_pool.astype(BF16), ps=pool_scale.reshape(L, 1, WIDTH),
        wpa=projf[0], wpb=projf[1], wpc=projf[2], wo=wof,
        sel=(jnp.arange(WIDTH)[:, None] // 64 == jnp.arange(128)[None, :]).astype(F32))
    norm_g3 = norm_g.reshape(L, 1, D_MODEL)

    acts = []
    xl = xs
    for l in range(L):
        p, h = _f1(xl, norm_g3, wfull, l)
        ya, yb, yc, mm, xn = _f2(xl, p, lw, l)
        acts.append((xl, p, h, ya, yb, yc, mm))
        xl = xn
    dxl, loss_blk, g_final = _loss_head(xl, tgt, final_g.reshape(1, D_MODEL))
    loss = lax.psum(loss_blk[0, 0], ("x", "y", "c"))

    gl = [None] * L
    for l in reversed(range(L)):
        xin, p, h, ya, yb, yc, mm = acts[l]
        gwo = _gwo(mm, dxl, l)
        (dp, gwpa, gwpb, gwpc, gws, gbs, glng, glnb, gcw, gcb, gwpool, gps) = _b1(p, dxl, ya, yb, yc, lw, l)
        dxl, gng = _b2a(dp, wfull, xin, dxl, norm_g3, l)
        gwin = _b2b(h, dp, l)
        gl[l] = dict(w_in=gwin, w_o=gwo, w_pa=gwpa, w_pb=gwpb, w_pc=gwpc, w_s=gws.reshape(8, CHUNK, CHUNK),
                     b_s=gbs[:, :8].T, ln_g=glng[0], ln_b=glnb[0], conv_w=gcw[:3], conv_b=gcb[0], w_pool=gwpool,
                     pool_scale=gps[0], norm_g=gng[0])
    grad_x = dxl.reshape(1, s, D_MODEL)
    st = lambda name: jnp.stack([gl[l][name] for l in range(L)])

    def to_chips_cols(g, cols):
        lr = g.shape[0] * g.shape[1]
        return _cast_bf16("cast_g", g.reshape(lr, N_CHIPS * cols)).reshape(lr, N_CHIPS, cols).transpose(1, 0, 2)

    part_w_in = jnp.stack([gl[l]["w_in"] for l in range(L)], axis=1).reshape(N_CHIPS, L * D_MODEL, SHARD_W)
    part_proj = to_chips_cols(jnp.concatenate([st("w_pa"), st("w_pb"), st("w_pc")], axis=0), 256)
    part_wo = st("w_o").reshape(L, N_CHIPS, 256, D_MODEL).transpose(1, 0, 2, 3).reshape(
        N_CHIPS, L * 256, D_MODEL)
    parts = [part_w_in, part_proj, part_wo]
    recv = _scatter_chips("scatter_grads", parts)
    idx = k_me.reshape(1)
    half = [_sum4(f"sum_parts{a}", parts[a], idx, recv[a]) for a in range(3)]
    other = _swap_sibling("swap_halves", half)

    outs = {}
    shard2d = dict(w_in=(L * D_MODEL, SHARD_W), w_o=(L * 256, D_MODEL))
    for a, name in ((0, "w_in"), (2, "w_o")):
        r2 = shard2d[name]
        res = _adamw(f"adamw_{name}", W[name].reshape(r2), M[name].reshape(r2), Vv[name].reshape(r2),
                     [half[a], other[a]])
        outs[name] = [o.reshape(W[name].shape) for o in res]
    pw = lambda d: jnp.concatenate([d["w_pa"].reshape(L * WIDTH, 256), d["w_pb"].reshape(L * WIDTH, 256),
                                    d["w_pc"].reshape(L * WIDTH, 256)], axis=0)
    res = _adamw("adamw_proj", pw(W), pw(M), pw(Vv), [half[1], other[1]])
    for i, name in enumerate(("w_pa", "w_pb", "w_pc")):
        outs[name] = [o.reshape(3, L, WIDTH, 256)[i] for o in res]

    gsm = dict(norm_g=st("norm_g"), ln_g=st("ln_g"), ln_b=st("ln_b"), w_s=st("w_s"), b_s=st("b_s"),
               conv_b=st("conv_b"), w_pool=st("w_pool"), pool_scale=st("pool_scale"), final_g=g_final[0])
    sizes = [W[nm].size // 128 for nm in _SMALL]
    n_small = sum(sizes)
    gpack = _pad_rows(jnp.concatenate([_rows(gsm[nm]) for nm in _SMALL] + [_rows(st("conv_w"))], axis=0))
    gred = _all_reduce_small("all_reduce_small", gpack)
    g_cw_full = gred[n_small:n_small + L * 3 * 4].reshape(L, 3, N_CHIPS, 128)
    g_cw_mine = lax.dynamic_index_in_dim(g_cw_full, k_me, axis=2, keepdims=False)
    pack = lambda d: _pad_rows(jnp.concatenate([_rows(d[nm]) for nm in _SMALL] + [_rows(d["conv_w"])], axis=0))
    gp2 = _pad_rows(jnp.concatenate([gred[:n_small], _rows(g_cw_mine)], axis=0))
    res = _adamw("adamw_small", pack(W), pack(M), pack(Vv), [gp2])
    off = 0
    for nm, sz in zip(_SMALL, sizes):
        outs[nm] = [o[off:off + sz].reshape(W[nm].shape) for o in res]
        off += sz
    outs["conv_w"] = [o[off:off + L * 3].reshape(W["conv_w"].shape) for o in res]

    order = ("norm_g", "w_in", "ln_g", "ln_b", "w_s", "b_s", "conv_w", "conv_b", "w_pool", "pool_scale",
             "w_pa", "w_pb", "w_pc", "w_o", "final_g")
    return (loss, grad_x, *[outs[nm][0] for nm in order], *[outs[nm][1] for nm in order],
            *[outs[nm][2] for nm in order], *[outs[nm][3] for nm in order])
```

```python
import functools

import jax
import jax.numpy as jnp
from jax import lax
from jax.experimental import pallas as pl
from jax.experimental.pallas import tpu as pltpu

F32 = jnp.float32
BF16 = jnp.bfloat16

D_MODEL = 1024
DEPTH = 2
CHUNK = 128
WIDTH = 512
POOL_WINDOWS = (2, 4, 8, 16)
IN_TOTAL = 7680
N_CHIPS = 4
SHARD_W = IN_TOTAL // N_CHIPS
RMS_EPS = 1e-6
LN_EPS = 1e-5
HALO = 16

U, V, ZA, XB, BG, CG, ZB, XC, ZC, GA, GB, GC = (0, 512, 1024, 1536, 2048, 2560, 3072, 3584, 4096, 4608, 5632, 6656)

ADAM_LR = 0.001
ADAM_B1 = 0.9
ADAM_B2 = 0.999
ADAM_EPS = 1e-08
ADAM_WD = 0.01
ADAM_STEP = 10

VMEM_LIMIT = 56 * 1024 * 1024
MESH = pl.DeviceIdType.MESH
ANY = pl.BlockSpec(memory_space=pl.ANY)
NT = (((1,), (1,)), ((), ()))
TN = (((0,), (0,)), ((), ()))


def _params(sem=None):
    kw = dict(vmem_limit_bytes=VMEM_LIMIT)
    if sem is not None:
        kw["dimension_semantics"] = sem
    return pltpu.CompilerParams(**kw)


def _dot(a, b):
    return jnp.dot(a, b, preferred_element_type=F32)


def _dotg(a, b, dims):
    return lax.dot_general(a, b, dims, preferred_element_type=F32)


def _sigmoid(x):
    return 1.0 / (1.0 + jnp.exp(-x))


_GELU_K = 0.7978845608028654


def _gelu(x):
    th = jnp.tanh(_GELU_K * (x + 0.044715 * (x * x * x)))
    return 0.5 * x * (1.0 + th), th


def _gelu_grad(x, th):
    return 0.5 * (1.0 + th) + 0.5 * x * (1.0 - th * th) * (_GELU_K * (1.0 + 3.0 * 0.044715 * (x * x)))


def _colsum8(x):
    t, c = x.shape
    return jnp.sum(x.reshape(t // 8, 8, c), axis=0)


def _branches_fwd(p_ref, hxb_ref, hcg_ref, hxc_ref, first, tstart, w, sg_scr):
    t = p_ref.shape[0]
    nch = t // CHUNK

    def seg(o, width=WIDTH):
        return p_ref[:, o:o + width].astype(F32)

    lo = lax.broadcasted_iota(jnp.int32, (CHUNK, CHUNK), 1) < 64
    r = {}
    pu = seg(U)
    u_act, th_u = _gelu(pu)
    pv = seg(V)
    vg, th_v = _gelu(pv)
    mu = jnp.mean(vg, axis=-1, keepdims=True)
    xc = vg - mu
    var = jnp.mean(xc * xc, axis=-1, keepdims=True)
    rs = lax.rsqrt(var + LN_EPS)
    vhat = xc * rs
    vn = vhat * w["ln_g"][...] + w["ln_b"][...]
    vnb = vn.astype(BF16)
    for n in range(nch):
        for j in range(4):
            vb = vnb[n * CHUNK:(n + 1) * CHUNK, j * 128:(j + 1) * 128]
            z = _dot(w["w2"][j], vb)
            sg_scr[n * CHUNK:(n + 1) * CHUNK, j * 128:(j + 1) * 128] = (
                jnp.where(lo, z[:CHUNK], z[CHUNK:]) + w["bst"][:, j * 128:(j + 1) * 128])
    sg = sg_scr[...]
    a_br = u_act * sg
    za = seg(ZA)
    sa = _sigmoid(za)
    r.update(pu=pu, th_u=th_u, pv=pv, th_v=th_v, rs=rs, vhat=vhat, vnb=vnb, u_act=u_act, sg=sg,
             a_br=a_br, za=za, sa=sa, a_in=a_br * (za * sa))

    xb = seg(XB)
    cg = seg(CG)
    yb0 = cg * xb
    hal = hcg_ref[...].astype(F32) * hxb_ref[...].astype(F32)
    hal = jnp.where(first, 0.0, hal)
    ext = jnp.concatenate([hal, yb0], axis=0)
    y1 = pltpu.roll(ext, 1, 0)[HALO:]
    y2 = pltpu.roll(ext, 2, 0)[HALO:]
    cw = w["cw"]
    conv = cw[0:1, :] * y2 + cw[1:2, :] * y1 + cw[2:3, :] * yb0 + w["cb"][...]
    bg = seg(BG)
    b_br = bg * conv
    zb = seg(ZB)
    sb = _sigmoid(zb)
    r.update(xb=xb, cg=cg, yb0=yb0, y1=y1, y2=y2, conv=conv, bg=bg, b_br=b_br, zb=zb, sb=sb,
             b_in=b_br * (zb * sb))

    xcv = seg(XC)
    hxc = jnp.where(first, 0.0, hxc_ref[...].astype(F32))
    extc = jnp.concatenate([hxc, xcv], axis=0)
    tpos = tstart + lax.broadcasted_iota(jnp.int32, (t, 1), 0) + 1
    pooled, inv, q = [], [], []
    for gi, win in enumerate(POOL_WINDOWS):
        s = extc[:, gi * 128:(gi + 1) * 128]
        sh = 1
        while sh < win:
            s = s + pltpu.roll(s, sh, 0)
            sh *= 2
        inv_g = jnp.where(tpos >= win, 1.0 / win, 1.0 / jnp.minimum(tpos, win).astype(F32))
        pg = s[HALO:] * inv_g - xcv[:, gi * 128:(gi + 1) * 128]
        pooled.append(pg)
        inv.append(inv_g)
        q.append(_dot(pg.astype(BF16), w["wpool"][gi]))
    qv = jnp.concatenate(q, axis=1)
    c_br = qv * w["ps"][...]
    zc = seg(ZC)
    sc = _sigmoid(zc)
    r.update(pooled=pooled, inv=inv, q=qv, c_br=c_br, zc=zc, sc=sc, c_in=c_br * (zc * sc))
    return r


def _halo_specs(t, rev_n=None):
    def imap(col):
        def f(i):
            ti = i if rev_n is None else rev_n - 1 - i
            return (jnp.maximum(ti * (t // HALO) - 1, 0), col)
        return f
    return [pl.BlockSpec((HALO, WIDTH), imap(XB // WIDTH)),
            pl.BlockSpec((HALO, WIDTH), imap(CG // WIDTH)),
            pl.BlockSpec((HALO, WIDTH), imap(XC // WIDTH))]


def _const_spec(shape):
    nd = len(shape)
    return pl.BlockSpec(shape, lambda *_: (0,) * nd)


def _place():
    return lax.axis_index("x"), lax.axis_index("y"), lax.axis_index("c")


def _chip_peer(x, y, jm):
    px = (1 - x) if (jm & 2) else x
    py = (1 - y) if (jm & 1) else y
    return px, py


def _rows_of(buf, k, r0, nr):
    return buf.at[k] if nr is None else buf.at[k, pl.ds(r0, nr)]


def _gather_plan(items):
    def plan(ins, thru, fresh, x, y, c):
        k_me = 2 * x + y
        cps = []
        for jm in (1, 2, 3):
            px, py = _chip_peer(x, y, jm)
            k_peer = 2 * px + py
            for ti, r0, nr in items:
                mine = _rows_of(thru[ti], k_me, r0, nr)
                cps.append((mine, mine, _rows_of(thru[ti], k_peer, r0, nr), (px, py, c)))
        return cps
    return plan


def _scatter_plan(items):
    def plan(ins, thru, fresh, x, y, c):
        cps = []
        for jj, jm in enumerate((1, 2, 3)):
            px, py = _chip_peer(x, y, jm)
            k_peer = 2 * px + py
            for ii, fi in items:
                cps.append((ins[ii].at[k_peer], fresh[fi].at[jj], fresh[fi].at[jj], (px, py, c)))
        return cps
    return plan


def _comm(plan, n_copies, ins=(), thru=(), fresh=()):
    return dict(plan=plan, n=n_copies, ins=list(ins), thru=list(thru), fresh=list(fresh))


def _pcall(body, *, name, grid, in_specs, out_specs, out_shape, scratch, sem, args, comm=None):
    if comm is None:
        outs = pl.pallas_call(body, name=name, grid=grid, in_specs=in_specs, out_specs=out_specs,
                              out_shape=out_shape, scratch_shapes=list(scratch), compiler_params=_params(sem))(*args)
        return list(outs), [], []
    n_in, n_out, n_scr = len(in_specs), len(out_specs), len(scratch)
    n_ci, n_ct, n_cf, n_cp = len(comm["ins"]), len(comm["thru"]), len(comm["fresh"]), comm["n"]

    def wrapped(*refs):
        pos = 0
        def take(n):
            nonlocal pos
            got = refs[pos:pos + n]
            pos += n
            return got
        a, ci, _ = take(n_in), take(n_ci), take(n_ct)
        o, ct, cf = take(n_out), take(n_ct), take(n_cf)
        scr = take(n_scr)
        send_sems, recv_sems = take(2)
        first = functools.reduce(jnp.logical_and, [pl.program_id(d) == 0 for d in range(len(grid))])
        last = functools.reduce(jnp.logical_and, [pl.program_id(d) == grid[d] - 1 for d in range(len(grid))])
        x, y, c = _place()
        cps = comm["plan"](ci, ct, cf, x, y, c)
        assert len(cps) == n_cp

        def copy(i, src, dst, dev):
            return pltpu.make_async_remote_copy(src_ref=src, dst_ref=dst, send_sem=send_sems.at[i],
                                                recv_sem=recv_sems.at[i], device_id=dev, device_id_type=MESH)

        @pl.when(first)
        def _():
            for i, (src, dst, _, dev) in enumerate(cps):
                copy(i, src, dst, dev).start()

        body(*a, *o, *scr)

        @pl.when(last)
        def _():
            for i, (src, _, land, dev) in enumerate(cps):
                copy(i, src, land, dev).wait_recv()
            for i, (src, dst, _, dev) in enumerate(cps):
                copy(i, src, dst, dev).wait_send()

    thru_shapes = [jax.ShapeDtypeStruct(t.shape, t.dtype) for t in comm["thru"]]
    outs = pl.pallas_call(
        wrapped, name=name, grid=grid,
        in_specs=list(in_specs) + [ANY] * (n_ci + n_ct),
        out_specs=list(out_specs) + [ANY] * (n_ct + n_cf),
        out_shape=list(out_shape) + thru_shapes + comm["fresh"],
        input_output_aliases={n_in + n_ci + t: n_out + t for t in range(n_ct)},
        scratch_shapes=list(scratch) + [pltpu.SemaphoreType.DMA((n_cp,)), pltpu.SemaphoreType.DMA((n_cp,))],
        compiler_params=_params(sem),
    )(*args, *comm["ins"], *comm["thru"])
    outs = list(outs)
    return outs[:n_out], outs[n_out:n_out + n_ct], outs[n_out + n_ct:]


def _comm_only(name, comm):
    def body():
        pass
    _, thru, fresh = _pcall(body, name=name, grid=(1,), in_specs=[], out_specs=[], out_shape=[], scratch=[],
                            sem=("arbitrary",), args=[], comm=comm)
    return thru, fresh


def _f1(x, norm_g3, w_all, l, comm=None, tm=512):
    s = x.shape[0]

    def body(x_ref, g_ref, w_ref, p_ref, h_ref):
        @pl.when(pl.program_id(1) == 0)
        def _():
            xv = x_ref[...]
            r = lax.rsqrt(jnp.mean(xv * xv, axis=-1, keepdims=True) + RMS_EPS)
            h_ref[...] = ((xv * r) * g_ref[...]).astype(BF16)
        p_ref[...] = _dot(h_ref[...], w_ref[...]).astype(BF16)

    return _pcall(
        body, name=f"f1_l{l}", grid=(s // tm, N_CHIPS),
        in_specs=[pl.BlockSpec((tm, D_MODEL), lambda i, k: (i, 0)),
                  pl.BlockSpec((None, 1, D_MODEL), lambda i, k: (l, 0, 0)),
                  pl.BlockSpec((None, D_MODEL, SHARD_W), lambda i, k: (k, 0, 0))],
        out_specs=[pl.BlockSpec((tm, SHARD_W), lambda i, k: (i, k)),
                   pl.BlockSpec((tm, D_MODEL), lambda i, k: (i, 0))],
        out_shape=[jax.ShapeDtypeStruct((s, IN_TOTAL), BF16), jax.ShapeDtypeStruct((s, D_MODEL), BF16)],
        scratch=[], sem=("arbitrary", "arbitrary"), args=[x, norm_g3, w_all], comm=comm)


def _f2(x, p, lw, l, comm=None, t=256):
    s = x.shape[0]
    n = s // t

    def body(p_ref, hxb_ref, hcg_ref, hxc_ref, x_ref, lng, lnb, w2, bst, cw, cb, wpool, ps,
             wpa, wpb, wpc, wo, ya_ref, yb_ref, yc_ref, m_ref, xo_ref, sg_scr):
        i = pl.program_id(0)
        w = dict(ln_g=lng, ln_b=lnb, w2=w2, bst=bst, cw=cw, cb=cb, wpool=wpool, ps=ps)
        r = _branches_fwd(p_ref, hxb_ref, hcg_ref, hxc_ref, i == 0, i * t, w, sg_scr)
        ya = _dot(r["a_in"].astype(BF16), wpa[...])
        yb = _dot(r["b_in"].astype(BF16), wpb[...])
        yc = _dot(r["c_in"].astype(BF16), wpc[...])
        ya_ref[...] = ya.astype(BF16)
        yb_ref[...] = yb.astype(BF16)
        yc_ref[...] = yc.astype(BF16)
        m = (_sigmoid(p_ref[:, GA:GA + D_MODEL].astype(F32)) * ya
             + _sigmoid(p_ref[:, GB:GB + D_MODEL].astype(F32)) * yb
             + _sigmoid(p_ref[:, GC:GC + D_MODEL].astype(F32)) * yc)
        mb = m.astype(BF16)
        m_ref[...] = mb
        xo_ref[...] = x_ref[...] + _dot(mb, wo[...])

    tile = lambda c: pl.BlockSpec((t, c), lambda i: (i, 0))
    lsel = lambda *blk: pl.BlockSpec((None,) + blk, lambda i: (l,) + (0,) * len(blk))
    proj = lambda b: pl.BlockSpec((None, None, WIDTH, D_MODEL), lambda i: (l, b, 0, 0))
    act = jax.ShapeDtypeStruct((s, D_MODEL), BF16)
    return _pcall(
        body, name=f"f2_l{l}", grid=(n,),
        in_specs=[tile(IN_TOTAL)] + _halo_specs(t) + [
            tile(D_MODEL), lsel(1, WIDTH), lsel(1, WIDTH), lsel(4, 256, 128), lsel(CHUNK, WIDTH),
            lsel(8, WIDTH), lsel(1, WIDTH), lsel(4, 128, 128), lsel(1, WIDTH),
            proj(0), proj(1), proj(2), lsel(D_MODEL, D_MODEL)],
        out_specs=[tile(D_MODEL)] * 5,
        out_shape=[act, act, act, act, jax.ShapeDtypeStruct((s, D_MODEL), F32)],
        scratch=[pltpu.VMEM((t, WIDTH), F32)], sem=("arbitrary",),
        args=[p, p, p, p, x, lw["ln_g"], lw["ln_b"], lw["w2"], lw["bst"], lw["cw"], lw["cb"], lw["wpool"], lw["ps"],
              lw["proj"], lw["proj"], lw["proj"], lw["wo"]], comm=comm)


def _b1(p, dout, ya, yb, yc, lw, l, comm=None, t=256):
    s = p.shape[0]
    n = s // t
    nch = t // CHUNK

    def body(p_ref, hxb_ref, hcg_ref, hxc_ref, do_ref, ya_ref, yb_ref, yc_ref,
             lng, lnb, w2, wt2, bst, cw, cb, wpool, ps, proj_h, wo_h, sel_ref,
             dp_ref, gwp_h, gws_ref, gbs_ref, glng_ref, glnb_ref, gcw_ref, gcb_ref,
             gwpool_ref, gps_ref,
             wpa, wpb, wpc, wo, gwpa, gwpb, gwpc, gbs_acc, vec_acc, sg_scr, dvn_scr, car_dc, car_e):
        i = pl.program_id(0)
        ti = n - 1 - i

        @pl.when(i == 0)
        def _():
            pltpu.sync_copy(proj_h.at[l, 0], wpa)
            pltpu.sync_copy(proj_h.at[l, 1], wpb)
            pltpu.sync_copy(proj_h.at[l, 2], wpc)
            pltpu.sync_copy(wo_h.at[l], wo)
            for acc in (gwpa, gwpb, gwpc, gbs_acc, vec_acc, car_dc, car_e):
                acc[...] = jnp.zeros(acc.shape, acc.dtype)
            gws_ref[...] = jnp.zeros(gws_ref.shape, F32)
            gwpool_ref[...] = jnp.zeros(gwpool_ref.shape, F32)

        w = dict(ln_g=lng, ln_b=lnb, w2=w2, bst=bst, cw=cw, cb=cb, wpool=wpool, ps=ps)
        r = _branches_fwd(p_ref, hxb_ref, hcg_ref, hxc_ref, ti == 0, ti * t, w, sg_scr)

        def seg(o, width=WIDTH):
            return p_ref[:, o:o + width].astype(F32)

        def put(o, val):
            dp_ref[:, o:o + val.shape[1]] = val.astype(BF16)

        dob = do_ref[...].astype(BF16)
        dm = _dotg(dob, wo[...], NT)

        def merge_bwd(goff, y_ref, xin, wp, gwp):
            sx = _sigmoid(seg(goff, D_MODEL))
            dmy = dm * sx
            put(goff, dmy * y_ref[...].astype(F32) * (1.0 - sx))
            dyb = dmy.astype(BF16)
            gwp[...] += _dotg(xin.astype(BF16), dyb, TN)
            return _dotg(dyb, wp[...], NT)

        d_ain = merge_bwd(GA, ya_ref, r["a_in"], wpa, gwpa)
        d_bin = merge_bwd(GB, yb_ref, r["b_in"], wpb, gwpb)
        d_cin = merge_bwd(GC, yc_ref, r["c_in"], wpc, gwpc)

        def dsilu(z, sz):
            return sz * (1.0 + z * (1.0 - sz))

        za, sa = r["za"], r["sa"]
        d_abr = d_ain * (za * sa)
        put(ZA, d_ain * r["a_br"] * dsilu(za, sa))
        put(U, d_abr * r["sg"] * _gelu_grad(r["pu"], r["th_u"]))
        d_sg = d_abr * r["u_act"]
        dsgb = d_sg.astype(BF16)
        lo = lax.broadcasted_iota(jnp.int32, (CHUNK, CHUNK), 1) < 64
        zero = jnp.zeros((CHUNK, CHUNK), BF16)
        for c in range(nch):
            rows = slice(c * CHUNK, (c + 1) * CHUNK)
            gbs_acc[...] += d_sg[rows]
            for j in range(4):
                cols = slice(j * 128, (j + 1) * 128)
                dj = dsgb[rows, cols]
                zt = _dot(wt2[j], dj)
                dvn_scr[rows, cols] = jnp.where(lo, zt[:CHUNK], zt[CHUNK:])
                stacked = jnp.concatenate([jnp.where(lo, dj, zero), jnp.where(lo, zero, dj)], axis=0)
                gws_ref[j] += _dotg(stacked, r["vnb"][rows, cols], NT)
        d_vn = dvn_scr[...]
        vhat = r["vhat"]
        vec_acc[0] += _colsum8(d_vn * vhat)
        vec_acc[1] += _colsum8(d_vn)
        d_vhat = d_vn * lng[...]
        d_vg = r["rs"] * (d_vhat - jnp.mean(d_vhat, axis=-1, keepdims=True)
                          - vhat * jnp.mean(d_vhat * vhat, axis=-1, keepdims=True))
        put(V, d_vg * _gelu_grad(r["pv"], r["th_v"]))

        zb, sb = r["zb"], r["sb"]
        d_bbr = d_bin * (zb * sb)
        put(ZB, d_bin * r["b_br"] * dsilu(zb, sb))
        put(BG, d_bbr * r["conv"])
        dc = d_bbr * r["bg"]
        vec_acc[2] += _colsum8(dc)
        vec_acc[3] += _colsum8(dc * r["y2"])
        vec_acc[4] += _colsum8(dc * r["y1"])
        vec_acc[5] += _colsum8(dc * r["yb0"])
        ext = jnp.concatenate([dc, car_dc[...]], axis=0)
        ne = t + HALO
        d1 = pltpu.roll(ext, ne - 1, 0)[:t]
        d2 = pltpu.roll(ext, ne - 2, 0)[:t]
        d_yb0 = cw[2:3, :] * dc + cw[1:2, :] * d1 + cw[0:1, :] * d2
        put(CG, d_yb0 * r["xb"])
        put(XB, d_yb0 * r["cg"])
        car_dc[...] = dc[:HALO]

        zc, sc = r["zc"], r["sc"]
        d_cbr = d_cin * (zc * sc)
        put(ZC, d_cin * r["c_br"] * dsilu(zc, sc))
        vec_acc[6] += _colsum8(d_cbr * r["q"])
        d_q = d_cbr * ps[...]
        for gi, win in enumerate(POOL_WINDOWS):
            cols = slice(gi * 128, (gi + 1) * 128)
            dqb = d_q[:, cols].astype(BF16)
            d_pool = _dotg(dqb, wpool[gi], NT)
            gwpool_ref[gi] += _dotg(r["pooled"][gi].astype(BF16), dqb, TN)
            e = d_pool * r["inv"][gi]
            sx = jnp.concatenate([e, car_e[:, cols]], axis=0)
            sh = 1
            while sh < win:
                sx = sx + pltpu.roll(sx, ne - sh, 0)
                sh *= 2
            put(XC + gi * 128, sx[:t] - d_pool)
            car_e[:, cols] = e[:HALO]

        @pl.when(i == n - 1)
        def _():
            for b, (acc, stage) in enumerate(((gwpa, wpa), (gwpb, wpb), (gwpc, wpc))):
                stage[...] = acc[...].astype(BF16)
                pltpu.sync_copy(stage, gwp_h.at[b])
            gbs_ref[...] = jnp.dot(gbs_acc[...], sel_ref[...], preferred_element_type=F32,
                                   precision=lax.Precision.HIGHEST)
            red = lambda k: jnp.sum(vec_acc[k], axis=0, keepdims=True)
            glng_ref[...] = red(0)
            glnb_ref[...] = red(1)
            gcb_ref[...] = red(2)
            gcw_ref[...] = jnp.zeros(gcw_ref.shape, F32)
            for k in range(3):
                gcw_ref[k:k + 1, :] = red(3 + k)
            gps_ref[...] = red(6)
            tt = lax.broadcasted_iota(jnp.int32, (2 * CHUNK, CHUNK), 0) % CHUNK
            ss = lax.broadcasted_iota(jnp.int32, (2 * CHUNK, CHUNK), 1)
            for j in range(4):
                gws_ref[j] = jnp.where(tt >= ss, gws_ref[j], 0.0)

    rtile = lambda c: pl.BlockSpec((t, c), lambda i: (n - 1 - i, 0))
    lsel = lambda *blk: pl.BlockSpec((None,) + blk, lambda i: (l,) + (0,) * len(blk))
    f32s = lambda *shape: jax.ShapeDtypeStruct(shape, F32)
    return _pcall(
        body, name=f"b1_l{l}", grid=(n,),
        in_specs=[rtile(IN_TOTAL)] + _halo_specs(t, rev_n=n) + [rtile(D_MODEL)] * 4 + [
            lsel(1, WIDTH), lsel(1, WIDTH), lsel(4, 256, 128), lsel(4, 256, 128), lsel(CHUNK, WIDTH),
            lsel(8, WIDTH), lsel(1, WIDTH), lsel(4, 128, 128), lsel(1, WIDTH),
            ANY, ANY, _const_spec((WIDTH, 128))],
        out_specs=[rtile(IN_TOTAL), ANY,
                   _const_spec((4, 256, 128)), _const_spec((CHUNK, 128)), _const_spec((1, WIDTH)),
                   _const_spec((1, WIDTH)), _const_spec((8, WIDTH)), _const_spec((1, WIDTH)),
                   _const_spec((4, 128, 128)), _const_spec((1, WIDTH))],
        out_shape=[jax.ShapeDtypeStruct((s, IN_TOTAL), BF16), jax.ShapeDtypeStruct((3, WIDTH, D_MODEL), BF16),
                   f32s(4, 256, 128), f32s(CHUNK, 128),
                   f32s(1, WIDTH), f32s(1, WIDTH), f32s(8, WIDTH), f32s(1, WIDTH), f32s(4, 128, 128),
                   f32s(1, WIDTH)],
        scratch=[pltpu.VMEM((WIDTH, D_MODEL), BF16), pltpu.VMEM((WIDTH, D_MODEL), BF16),
                 pltpu.VMEM((WIDTH, D_MODEL), BF16), pltpu.VMEM((D_MODEL, D_MODEL), BF16),
                 pltpu.VMEM((WIDTH, D_MODEL), F32),
                 pltpu.VMEM((WIDTH, D_MODEL), F32), pltpu.VMEM((WIDTH, D_MODEL), F32),
                 pltpu.VMEM((CHUNK, WIDTH), F32), pltpu.VMEM((8, 8, WIDTH), F32),
                 pltpu.VMEM((t, WIDTH), F32), pltpu.VMEM((t, WIDTH), F32),
                 pltpu.VMEM((HALO, WIDTH), F32), pltpu.VMEM((HALO, WIDTH), F32)],
        sem=("arbitrary",),
        args=[p, p, p, p, dout, ya, yb, yc, lw["ln_g"], lw["ln_b"], lw["w2"], lw["wt2"], lw["bst"], lw["cw"],
              lw["cb"], lw["wpool"], lw["ps"], lw["proj"], lw["wo"], lw["sel"]], comm=comm)


def _rms_bwd(xv, g, dh):
    r = lax.rsqrt(jnp.mean(xv * xv, axis=-1, keepdims=True) + RMS_EPS)
    xhat = xv * r
    dxh = dh * g
    dx = r * (dxh - xhat * jnp.mean(dxh * xhat, axis=-1, keepdims=True))
    return dx, dh * xhat


def _b2a(dp, w_all, x, dout, norm_g3, l, comm=None, tm=512):
    s = x.shape[0]
    nm = s // tm

    def body(dp_ref, w_ref, x_ref, do_ref, g_ref, dx_ref, gg_ref, acc, gacc):
        i, k = pl.program_id(0), pl.program_id(1)

        @pl.when(jnp.logical_and(i == 0, k == 0))
        def _():
            gacc[...] = jnp.zeros(gacc.shape, F32)

        prod = _dotg(dp_ref[...], w_ref[...], NT)

        @pl.when(k == 0)
        def _():
            acc[...] = prod

        @pl.when(k > 0)
        def _():
            acc[...] += prod

        @pl.when(k == N_CHIPS - 1)
        def _():
            dx, gx = _rms_bwd(x_ref[...], g_ref[...], acc[...])
            dx_ref[...] = do_ref[...] + dx
            gacc[...] += _colsum8(gx)

        @pl.when(jnp.logical_and(i == nm - 1, k == N_CHIPS - 1))
        def _():
            gg_ref[...] = jnp.sum(gacc[...], axis=0, keepdims=True)

    return _pcall(
        body, name=f"b2a_l{l}", grid=(nm, N_CHIPS),
        in_specs=[pl.BlockSpec((tm, SHARD_W), lambda i, k: (i, k)),
                  pl.BlockSpec((None, D_MODEL, SHARD_W), lambda i, k: (k, 0, 0)),
                  pl.BlockSpec((tm, D_MODEL), lambda i, k: (i, 0)),
                  pl.BlockSpec((tm, D_MODEL), lambda i, k: (i, 0)),
                  pl.BlockSpec((None, 1, D_MODEL), lambda i, k: (l, 0, 0))],
        out_specs=[pl.BlockSpec((tm, D_MODEL), lambda i, k: (i, 0)),
                   pl.BlockSpec((1, D_MODEL), lambda i, k: (0, 0))],
        out_shape=[jax.ShapeDtypeStruct((s, D_MODEL), F32), jax.ShapeDtypeStruct((1, D_MODEL), F32)],
        scratch=[pltpu.VMEM((tm, D_MODEL), F32), pltpu.VMEM((8, D_MODEL), F32)],
        sem=("arbitrary", "arbitrary"), args=[dp, w_all, x, dout, norm_g3], comm=comm)


def _b2b(h, dp, l, comm=None, tk=512):
    s = h.shape[0]
    nk = s // tk

    def body(h_ref, dp_ref, g_ref, acc):
        kk = pl.program_id(1)
        prod = _dotg(h_ref[...], dp_ref[...], TN)

        @pl.when(kk == 0)
        def _():
            acc[...] = prod

        @pl.when(kk > 0)
        def _():
            acc[...] += prod

        @pl.when(kk == nk - 1)
        def _():
            g_ref[...] = acc[...].astype(BF16)

    return _pcall(
        body, name=f"b2b_l{l}", grid=(N_CHIPS, nk),
        in_specs=[pl.BlockSpec((tk, D_MODEL), lambda k, kk: (kk, 0)),
                  pl.BlockSpec((tk, SHARD_W), lambda k, kk: (kk, k))],
        out_specs=[pl.BlockSpec((None, D_MODEL, SHARD_W), lambda k, kk: (k, 0, 0))],
        out_shape=[jax.ShapeDtypeStruct((N_CHIPS, D_MODEL, SHARD_W), BF16)],
        scratch=[pltpu.VMEM((D_MODEL, SHARD_W), F32)],
        sem=("arbitrary", "arbitrary"), args=[h, dp], comm=comm)


def _gwo(m, dout, l, tk=512):
    s = m.shape[0]
    nk = s // tk

    def body(m_ref, do_ref, g_ref, acc):
        kk = pl.program_id(0)
        prod = _dotg(m_ref[...], do_ref[...].astype(BF16), TN)

        @pl.when(kk == 0)
        def _():
            acc[...] = prod

        @pl.when(kk > 0)
        def _():
            acc[...] += prod

        @pl.when(kk == nk - 1)
        def _():
            g_ref[...] = acc[...].astype(BF16)

    return pl.pallas_call(
        body, name=f"gwo_l{l}", grid=(nk,),
        in_specs=[pl.BlockSpec((tk, D_MODEL), lambda kk: (kk, 0)), pl.BlockSpec((tk, D_MODEL), lambda kk: (kk, 0))],
        out_specs=_const_spec((D_MODEL, D_MODEL)),
        out_shape=jax.ShapeDtypeStruct((D_MODEL, D_MODEL), BF16),
        scratch_shapes=[pltpu.VMEM((D_MODEL, D_MODEL), F32)],
        compiler_params=_params(("arbitrary",)),
    )(m, dout)


def _loss_head(x, tgt, final_g2, tm=512):
    s = x.shape[0]
    nm = s // tm

    def body(x_ref, t_ref, g_ref, dx_ref, loss_ref, gg_ref, lacc, gacc):
        i = pl.program_id(0)

        @pl.when(i == 0)
        def _():
            lacc[...] = jnp.zeros(lacc.shape, F32)
            gacc[...] = jnp.zeros(gacc.shape, F32)

        xv = x_ref[...]
        g = g_ref[...]
        r = lax.rsqrt(jnp.mean(xv * xv, axis=-1, keepdims=True) + RMS_EPS)
        err = (xv * r) * g - t_ref[...]
        lacc[...] += _colsum8(err * err)
        dx, gx = _rms_bwd(xv, g, err * (1.0 / D_MODEL))
        dx_ref[...] = dx
        gacc[...] += _colsum8(gx)

        @pl.when(i == nm - 1)
        def _():
            tot = jnp.sum(jnp.sum(lacc[...], axis=0, keepdims=True), axis=1, keepdims=True)
            loss_ref[...] = jnp.broadcast_to(tot * (0.5 / D_MODEL), loss_ref.shape)
            gg_ref[...] = jnp.sum(gacc[...], axis=0, keepdims=True)

    return pl.pallas_call(
        body, name="loss_head", grid=(nm,),
        in_specs=[pl.BlockSpec((tm, D_MODEL), lambda i: (i, 0)), pl.BlockSpec((tm, D_MODEL), lambda i: (i, 0)),
                  _const_spec((1, D_MODEL))],
        out_specs=[pl.BlockSpec((tm, D_MODEL), lambda i: (i, 0)), _const_spec((8, 128)),
                   _const_spec((1, D_MODEL))],
        out_shape=[jax.ShapeDtypeStruct((s, D_MODEL), F32), jax.ShapeDtypeStruct((8, 128), F32),
                   jax.ShapeDtypeStruct((1, D_MODEL), F32)],
        scratch_shapes=[pltpu.VMEM((8, D_MODEL), F32), pltpu.VMEM((8, D_MODEL), F32)],
        compiler_params=_params(("arbitrary",)),
    )(x, tgt, final_g2)


def _row_block(rows, cols, n_arrays):
    budget = VMEM_LIMIT // 3 // (2 * 4 * n_arrays * cols)
    rb = rows
    while rb > budget and rb % 16 == 0:
        rb //= 2
    return rb


def _cast_slot(name, a, l, chip):
    _, rows, cols = a.shape
    rb = _row_block(rows, cols, 2)

    def body(idx_ref, a_ref, o_ref):
        o_ref[...] = a_ref[...].astype(BF16)

    gs = pltpu.PrefetchScalarGridSpec(
        num_scalar_prefetch=1, grid=(rows // rb,),
        in_specs=[pl.BlockSpec((None, rb, cols), lambda i, idx: (l, i, 0))],
        out_specs=pl.BlockSpec((None, rb, cols), lambda i, idx: (idx[0], i, 0)))
    return pl.pallas_call(body, name=name, grid_spec=gs, out_shape=jax.ShapeDtypeStruct((N_CHIPS, rows, cols), BF16),
                          compiler_params=_params(("arbitrary",)))(chip, a)


def _sum4(name, own, chip, recv, l, n_layers, prev=None):
    _, rows, cols = own.shape
    rb = _row_block(rows, cols, 5)
    nb = rows // rb

    def body(idx_ref, own_ref, r_ref, *rest):
        o_ref = rest[-1]
        o_ref[...] = ((own_ref[...].astype(F32) + r_ref[0].astype(F32)) + r_ref[1].astype(F32)) + r_ref[2].astype(F32)

    gs = pltpu.PrefetchScalarGridSpec(
        num_scalar_prefetch=1, grid=(nb,),
        in_specs=[pl.BlockSpec((None, rb, cols), lambda i, idx: (idx[0], i, 0)),
                  pl.BlockSpec((3, rb, cols), lambda i, idx: (0, i, 0))] + ([ANY] if prev is not None else []),
        out_specs=pl.BlockSpec((rb, cols), lambda i, idx: (l * nb + i, 0)))
    args = (chip, own, recv) + ((prev,) if prev is not None else ())
    return pl.pallas_call(body, name=name, grid_spec=gs,
                          out_shape=jax.ShapeDtypeStruct((n_layers * rows, cols), F32),
                          input_output_aliases=({3: 0} if prev is not None else {}),
                          compiler_params=_params(("arbitrary",)))(*args)


def _adamw_math(w, g, m, v):
    m = ADAM_B1 * m + (1.0 - ADAM_B1) * g
    v = ADAM_B2 * v + (1.0 - ADAM_B2) * (g * g)
    m_hat = m / (1.0 - ADAM_B1 ** ADAM_STEP)
    v_hat = v / (1.0 - ADAM_B2 ** ADAM_STEP)
    delta = -ADAM_LR * (m_hat / (jnp.sqrt(v_hat) + ADAM_EPS) + ADAM_WD * w)
    return delta, m, v


def _adamw(name, w, m, v, g_parts):
    rows, cols = w.shape
    np_ = len(g_parts)
    rb = _row_block(rows, cols, 7 + np_)

    def body(*refs):
        w_ref, m_ref, v_ref = refs[:3]
        g_refs = refs[3:3 + np_]
        go_ref, d_ref, mo_ref, vo_ref = refs[3 + np_:]
        g = g_refs[0][...]
        for gr in g_refs[1:]:
            g = g + gr[...]
        d, mn, vn = _adamw_math(w_ref[...], g, m_ref[...], v_ref[...])
        go_ref[...] = g
        d_ref[...] = d
        mo_ref[...] = mn
        vo_ref[...] = vn

    spec = pl.BlockSpec((rb, cols), lambda i: (i, 0))
    shp = jax.ShapeDtypeStruct((rows, cols), F32)
    return pl.pallas_call(body, name=name, grid=(rows // rb,), in_specs=[spec] * (3 + np_),
                          out_specs=[spec] * 4, out_shape=[shp] * 4,
                          compiler_params=_params(("arbitrary",)))(w, m, v, *g_parts)


def _swap_sibling(name, arrs):
    na = len(arrs)

    def body(*refs):
        src = refs[:na]
        out = refs[na:2 * na]
        send_sems, recv_sems = refs[2 * na:]
        x, y, c = _place()
        cps = [pltpu.make_async_remote_copy(
            src_ref=src[a], dst_ref=out[a], send_sem=send_sems.at[a], recv_sem=recv_sems.at[a],
            device_id=(x, y, 1 - c), device_id_type=MESH) for a in range(na)]
        for cp in cps:
            cp.start()
        for cp in cps:
            cp.wait_recv()
        for cp in cps:
            cp.wait_send()

    return pl.pallas_call(
        body, name=name, in_specs=[ANY] * na, out_specs=[ANY] * na,
        out_shape=[jax.ShapeDtypeStruct(a.shape, a.dtype) for a in arrs],
        scratch_shapes=[pltpu.SemaphoreType.DMA((na,)), pltpu.SemaphoreType.DMA((na,))],
    )(*arrs)


def _all_reduce_small(name, a):
    def body(a_ref, o_ref, buf, send_sems, recv_sems):
        x, y, c = _place()
        o_ref[...] = a_ref[...]
        for rnd, peer in enumerate(((x, y, 1 - c), (x, 1 - y, c), (1 - x, y, c))):
            cp = pltpu.make_async_remote_copy(
                src_ref=o_ref, dst_ref=buf.at[rnd], send_sem=send_sems.at[rnd], recv_sem=recv_sems.at[rnd],
                device_id=peer, device_id_type=MESH)
            cp.start()
            cp.wait_recv()
            cp.wait_send()
            o_ref[...] = o_ref[...] + buf[rnd]

    vm = pl.BlockSpec(memory_space=pltpu.VMEM)
    return pl.pallas_call(
        body, name=name, in_specs=[vm], out_specs=vm, out_shape=jax.ShapeDtypeStruct(a.shape, F32),
        scratch_shapes=[pltpu.VMEM((3,) + a.shape, F32), pltpu.SemaphoreType.DMA((3,)),
                        pltpu.SemaphoreType.DMA((3,))],
        compiler_params=_params(),
    )(a)


_SMALL = ("norm_g", "ln_g", "ln_b", "w_s", "b_s", "conv_b", "w_pool", "pool_scale", "final_g")


def _rows(a):
    return a.reshape(-1, 128)


def _pad_rows(a, mult=8):
    pad = (-a.shape[0]) % mult
    return jnp.pad(a, ((0, pad), (0, 0))) if pad else a


def kernel(x, norm_g, w_in, ln_g, ln_b, w_s, b_s, conv_w, conv_b, w_pool, pool_scale, w_pa, w_pb, w_pc, w_o, final_g, loss_target, m_norm_g, m_w_in, m_ln_g, m_ln_b, m_w_s, m_b_s, m_conv_w, m_conv_b, m_w_pool, m_pool_scale, m_w_pa, m_w_pb, m_w_pc, m_w_o, m_final_g, v_norm_g, v_w_in, v_ln_g, v_ln_b, v_w_s, v_b_s, v_conv_w, v_conv_b, v_w_pool, v_pool_scale, v_w_pa, v_w_pb, v_w_pc, v_w_o, v_final_g):
    W = dict(norm_g=norm_g, w_in=w_in, ln_g=ln_g, ln_b=ln_b, w_s=w_s, b_s=b_s, conv_w=conv_w, conv_b=conv_b,
             w_pool=w_pool, pool_scale=pool_scale, w_pa=w_pa, w_pb=w_pb, w_pc=w_pc, w_o=w_o, final_g=final_g)
    M = dict(norm_g=m_norm_g, w_in=m_w_in, ln_g=m_ln_g, ln_b=m_ln_b, w_s=m_w_s, b_s=m_b_s, conv_w=m_conv_w,
             conv_b=m_conv_b, w_pool=m_w_pool, pool_scale=m_pool_scale, w_pa=m_w_pa, w_pb=m_w_pb, w_pc=m_w_pc,
             w_o=m_w_o, final_g=m_final_g)
    Vv = dict(norm_g=v_norm_g, w_in=v_w_in, ln_g=v_ln_g, ln_b=v_ln_b, w_s=v_w_s, b_s=v_b_s, conv_w=v_conv_w,
              conv_b=v_conv_b, w_pool=v_w_pool, pool_scale=v_pool_scale, w_pa=v_w_pa, w_pb=v_w_pb, w_pc=v_w_pc,
              w_o=v_w_o, final_g=v_final_g)
    L = DEPTH
    s = x.shape[1]
    xs = x.reshape(s, D_MODEL)
    tgt = loss_target.reshape(s, D_MODEL)
    k_me = (2 * lax.axis_index("x") + lax.axis_index("y")).astype(jnp.int32)

    chip = k_me.reshape(1)
    assert L == 2
    half_rows = D_MODEL // 2

    land_win = [_cast_slot(f"cast_w_in{l}", w_in, l, chip) for l in range(L)]
    pcat = lambda d: jnp.concatenate([d[b][l] for l in range(L) for b in ("w_pa", "w_pb", "w_pc")], axis=0)
    land_proj = _cast_slot("cast_proj", pcat(W)[None], 0, chip)
    land_wo = _cast_slot("cast_w_o", w_o.reshape(1, L * 256, D_MODEL), 0, chip)
    cw_sh = jnp.pad(conv_w, ((0, 0), (0, 5), (0, 0))).reshape(1, L * 8, 128)
    land_cw = lax.dynamic_update_slice(jnp.zeros((N_CHIPS, L * 8, 128), F32), cw_sh, (k_me, 0, 0))
    (w_all0,), _ = _comm_only("gather_w_in0", _comm(_gather_plan([(0, 0, None)]), 3, thru=[land_win[0]]))

    causal = jnp.tril(jnp.ones((CHUNK, CHUNK), dtype=bool))
    w_m = jnp.where(causal, w_s, 0.0)
    lw = dict(
        ln_g=ln_g.reshape(L, 1, WIDTH), ln_b=ln_b.reshape(L, 1, WIDTH),
        w2=w_m.reshape(L, 4, 256, CHUNK).astype(BF16),
        wt2=jnp.swapaxes(w_m, -1, -2).reshape(L, 4, 256, CHUNK).astype(BF16),
        bst=jnp.repeat(jnp.swapaxes(b_s, -1, -2), 64, axis=-1),
        cb=conv_b.reshape(L, 1, WIDTH), wpool=w_pool.astype(BF16), ps=pool_scale.reshape(L, 1, WIDTH),
        sel=(jnp.arange(WIDTH)[:, None] // 64 == jnp.arange(128)[None, :]).astype(F32))
    norm_g3 = norm_g.reshape(L, 1, D_MODEL)

    (p0, h0), (land_proj, land_wo, land_cw, w_half1), _ = _f1(
        xs, norm_g3, w_all0, 0,
        comm=_comm(_gather_plan([(0, 0, None), (1, 0, None), (2, 0, None), (3, 0, half_rows)]), 12,
                   thru=[land_proj, land_wo, land_cw, land_win[1]]))
    lw["proj"] = land_proj.transpose(1, 0, 2).reshape(L, 3, WIDTH, D_MODEL)
    lw["wo"] = land_wo.reshape(N_CHIPS, L, 256, D_MODEL).transpose(1, 0, 2, 3).reshape(L, D_MODEL, D_MODEL)
    lw["cw"] = land_cw.reshape(N_CHIPS, L, 8, 128).transpose(1, 2, 0, 3).reshape(L, 8, WIDTH)
    (ya0, yb0, yc0, mm0, x1), (w_all1,), _ = _f2(
        xs, p0, lw, 0, comm=_comm(_gather_plan([(0, half_rows, half_rows)]), 3, thru=[w_half1]))
    (p1, h1), _, _ = _f1(x1, norm_g3, w_all1, 1)
    (ya1, yb1, yc1, mm1, x2), _, _ = _f2(x1, p1, lw, 1)
    dxl, loss_blk, g_final = _loss_head(x2, tgt, final_g.reshape(1, D_MODEL))
    loss = lax.psum(loss_blk[0, 0], ("x", "y", "c"))

    def scatter(parts):
        fresh = [jax.ShapeDtypeStruct((3,) + a.shape[1:], a.dtype) for a in parts]
        return _comm(_scatter_plan([(i, i) for i in range(len(parts))]), 3 * len(parts), ins=parts, fresh=fresh)

    def small_grads(b1_outs, gng):
        gws, gbs, glng, glnb, gcw, gcb, gwpool, gps = b1_outs
        return dict(w_s=gws.reshape(8, CHUNK, CHUNK), b_s=gbs[:, :8].T, ln_g=glng[0], ln_b=glnb[0], conv_w=gcw[:3],
                    conv_b=gcb[0], w_pool=gwpool, pool_scale=gps[0], norm_g=gng[0])

    by_chip = lambda gwp, gwo: (gwp.reshape(3 * WIDTH, N_CHIPS, 256).transpose(1, 0, 2),
                                gwo.reshape(N_CHIPS, 256, D_MODEL))
    gl, gwin, recv_win, part_proj, part_wo, recv_proj, recv_wo = ([None] * L for _ in range(7))
    gwo1 = _gwo(mm1, dxl, 1)
    (dp1, gwp1, *sm1), _, _ = _b1(p1, dxl, ya1, yb1, yc1, lw, 1)
    (gwin[1],), _, _ = _b2b(h1, dp1, 1)
    (dxl, gng1), _, (recv_win[1],) = _b2a(dp1, w_all1, x1, dxl, norm_g3, 1, comm=scatter([gwin[1]]))
    gl[1] = small_grads(sm1, gng1)
    part_proj[1], part_wo[1] = by_chip(gwp1, gwo1)
    gwo0 = _gwo(mm0, dxl, 0)
    (dp0, gwp0, *sm0), _, (recv_proj[1], recv_wo[1]) = _b1(p0, dxl, ya0, yb0, yc0, lw, 0,
                                                            comm=scatter([part_proj[1], part_wo[1]]))
    part_proj[0], part_wo[0] = by_chip(gwp0, gwo0)
    (gwin[0],), _, (recv_proj[0], recv_wo[0]) = _b2b(h0, dp0, 0, comm=scatter([part_proj[0], part_wo[0]]))
    (dxl, gng0), _, (recv_win[0],) = _b2a(dp0, w_all0, xs, dxl, norm_g3, 0, comm=scatter([gwin[0]]))
    gl[0] = small_grads(sm0, gng0)
    grad_x = dxl.reshape(1, s, D_MODEL)
    st = lambda name: jnp.stack([gl[l][name] for l in range(L)])

    half = [None, None, None]
    for l in range(L):
        half[0] = _sum4(f"sum_w_in{l}", gwin[l], chip, recv_win[l], l, L, half[0])
        half[1] = _sum4(f"sum_proj{l}", part_proj[l], chip, recv_proj[l], l, L, half[1])
        half[2] = _sum4(f"sum_w_o{l}", part_wo[l], chip, recv_wo[l], l, L, half[2])
    other = _swap_sibling("swap_halves", half)

    outs = {}
    shard2d = dict(w_in=(L * D_MODEL, SHARD_W), w_o=(L * 256, D_MODEL))
    for a, name in ((0, "w_in"), (2, "w_o")):
        r2 = shard2d[name]
        res = _adamw(f"adamw_{name}", W[name].reshape(r2), M[name].reshape(r2), Vv[name].reshape(r2),
                     [half[a], other[a]])
        outs[name] = [o.reshape(W[name].shape) for o in res]
    res = _adamw("adamw_proj", pcat(W), pcat(M), pcat(Vv), [half[1], other[1]])
    for i, name in enumerate(("w_pa", "w_pb", "w_pc")):
        outs[name] = [o.reshape(L, 3, WIDTH, 256)[:, i] for o in res]

    gsm = dict(norm_g=st("norm_g"), ln_g=st("ln_g"), ln_b=st("ln_b"), w_s=st("w_s"), b_s=st("b_s"),
               conv_b=st("conv_b"), w_pool=st("w_pool"), pool_scale=st("pool_scale"), final_g=g_final[0])
    sizes = [W[nm].size // 128 for nm in _SMALL]
    n_small = sum(sizes)
    gpack = _pad_rows(jnp.concatenate([_rows(gsm[nm]) for nm in _SMALL] + [_rows(st("conv_w"))], axis=0))
    gred = _all_reduce_small("all_reduce_small", gpack)
    g_cw_full = gred[n_small:n_small + L * 3 * 4].reshape(L, 3, N_CHIPS, 128)
    g_cw_mine = lax.dynamic_index_in_dim(g_cw_full, k_me, axis=2, keepdims=False)
    pack = lambda d: _pad_rows(jnp.concatenate([_rows(d[nm]) for nm in _SMALL] + [_rows(d["conv_w"])], axis=0))
    gp2 = _pad_rows(jnp.concatenate([gred[:n_small], _rows(g_cw_mine)], axis=0))
    res = _adamw("adamw_small", pack(W), pack(M), pack(Vv), [gp2])
    off = 0
    for nm, sz in zip(_SMALL, sizes):
        outs[nm] = [o[off:off + sz].reshape(W[nm].shape) for o in res]
        off += sz
    outs["conv_w"] = [o[off:off + L * 3].reshape(W["conv_w"].shape) for o in res]

    order = ("norm_g", "w_in", "ln_g", "ln_b", "w_s", "b_s", "conv_w", "conv_b", "w_pool", "pool_scale",
             "w_pa", "w_pb", "w_pc", "w_o", "final_g")
    return (loss, grad_x, *[outs[nm][0] for nm in order], *[outs[nm][1] for nm in order],
            *[outs[nm][2] for nm in order], *[outs[nm][3] for nm in order])
```

```python
import functools

import jax
import jax.numpy as jnp
from jax import lax
from jax.experimental import pallas as pl
from jax.experimental.pallas import tpu as pltpu

F32 = jnp.float32
BF16 = jnp.bfloat16

D_MODEL = 1024
DEPTH = 2
CHUNK = 128
WIDTH = 512
POOL_WINDOWS = (2, 4, 8, 16)
IN_TOTAL = 7680
N_CHIPS = 4
SHARD_W = IN_TOTAL // N_CHIPS
RMS_EPS = 1e-6
LN_EPS = 1e-5
HALO = 16

U, V, ZA, XB, BG, CG, ZB, XC, ZC, GA, GB, GC = (0, 512, 1024, 1536, 2048, 2560, 3072, 3584, 4096, 4608, 5632, 6656)

ADAM_LR = 0.001
ADAM_B1 = 0.9
ADAM_B2 = 0.999
ADAM_EPS = 1e-08
ADAM_WD = 0.01
ADAM_STEP = 10

VMEM_LIMIT = 56 * 1024 * 1024
MESH = pl.DeviceIdType.MESH
ANY = pl.BlockSpec(memory_space=pl.ANY)
NT = (((1,), (1,)), ((), ()))
TN = (((0,), (0,)), ((), ()))


def _params(sem=None):
    kw = dict(vmem_limit_bytes=VMEM_LIMIT)
    if sem is not None:
        kw["dimension_semantics"] = sem
    return pltpu.CompilerParams(**kw)


def _dot(a, b):
    return jnp.dot(a, b, preferred_element_type=F32)


def _dotg(a, b, dims):
    return lax.dot_general(a, b, dims, preferred_element_type=F32)


def _sigmoid(x):
    return 1.0 / (1.0 + jnp.exp(-x))


_GELU_K = 0.7978845608028654


def _gelu(x):
    th = jnp.tanh(_GELU_K * (x + 0.044715 * (x * x * x)))
    return 0.5 * x * (1.0 + th), th


def _gelu_grad(x, th):
    return 0.5 * (1.0 + th) + 0.5 * x * (1.0 - th * th) * (_GELU_K * (1.0 + 3.0 * 0.044715 * (x * x)))


def _colsum8(x):
    t, c = x.shape
    return jnp.sum(x.reshape(t // 8, 8, c), axis=0)


def _branches_fwd(p_ref, hxb_ref, hcg_ref, hxc_ref, first, tstart, w, sg_scr):
    t = p_ref.shape[0]
    nch = t // CHUNK

    def seg(o, width=WIDTH):
        return p_ref[:, o:o + width].astype(F32)

    lo = lax.broadcasted_iota(jnp.int32, (CHUNK, CHUNK), 1) < 64
    r = {}
    pu = seg(U)
    u_act, th_u = _gelu(pu)
    pv = seg(V)
    vg, th_v = _gelu(pv)
    mu = jnp.mean(vg, axis=-1, keepdims=True)
    xc = vg - mu
    var = jnp.mean(xc * xc, axis=-1, keepdims=True)
    rs = lax.rsqrt(var + LN_EPS)
    vhat = xc * rs
    vn = vhat * w["ln_g"][...] + w["ln_b"][...]
    vnb = vn.astype(BF16)
    for n in range(nch):
        for j in range(4):
            vb = vnb[n * CHUNK:(n + 1) * CHUNK, j * 128:(j + 1) * 128]
            z = _dot(w["w2"][j], vb)
            sg_scr[n * CHUNK:(n + 1) * CHUNK, j * 128:(j + 1) * 128] = (
                jnp.where(lo, z[:CHUNK], z[CHUNK:]) + w["bst"][:, j * 128:(j + 1) * 128])
    sg = sg_scr[...]
    a_br = u_act * sg
    za = seg(ZA)
    sa = _sigmoid(za)
    r.update(pu=pu, th_u=th_u, pv=pv, th_v=th_v, rs=rs, vhat=vhat, vnb=vnb, u_act=u_act, sg=sg,
             a_br=a_br, za=za, sa=sa, a_in=a_br * (za * sa))

    xb = seg(XB)
    cg = seg(CG)
    yb0 = cg * xb
    hal = hcg_ref[...].astype(F32) * hxb_ref[...].astype(F32)
    hal = jnp.where(first, 0.0, hal)
    ext = jnp.concatenate([hal, yb0], axis=0)
    y1 = pltpu.roll(ext, 1, 0)[HALO:]
    y2 = pltpu.roll(ext, 2, 0)[HALO:]
    cw = w["cw"]
    conv = cw[0:1, :] * y2 + cw[1:2, :] * y1 + cw[2:3, :] * yb0 + w["cb"][...]
    bg = seg(BG)
    b_br = bg * conv
    zb = seg(ZB)
    sb = _sigmoid(zb)
    r.update(xb=xb, cg=cg, yb0=yb0, y1=y1, y2=y2, conv=conv, bg=bg, b_br=b_br, zb=zb, sb=sb,
             b_in=b_br * (zb * sb))

    xcv = seg(XC)
    hxc = jnp.where(first, 0.0, hxc_ref[...].astype(F32))
    extc = jnp.concatenate([hxc, xcv], axis=0)
    tpos = tstart + lax.broadcasted_iota(jnp.int32, (t, 1), 0) + 1
    pooled, inv, q = [], [], []
    for gi, win in enumerate(POOL_WINDOWS):
        s = extc[:, gi * 128:(gi + 1) * 128]
        sh = 1
        while sh < win:
            s = s + pltpu.roll(s, sh, 0)
            sh *= 2
        inv_g = jnp.where(tpos >= win, 1.0 / win, 1.0 / jnp.minimum(tpos, win).astype(F32))
        pg = s[HALO:] * inv_g - xcv[:, gi * 128:(gi + 1) * 128]
        pooled.append(pg)
        inv.append(inv_g)
        q.append(_dot(pg.astype(BF16), w["wpool"][gi]))
    qv = jnp.concatenate(q, axis=1)
    c_br = qv * w["ps"][...]
    zc = seg(ZC)
    sc = _sigmoid(zc)
    r.update(pooled=pooled, inv=inv, q=qv, c_br=c_br, zc=zc, sc=sc, c_in=c_br * (zc * sc))
    return r


def _halo_specs(t, rev_n=None):
    def imap(col):
        def f(i):
            ti = i if rev_n is None else rev_n - 1 - i
            return (jnp.maximum(ti * (t // HALO) - 1, 0), col)
        return f
    return [pl.BlockSpec((HALO, WIDTH), imap(XB // WIDTH)),
            pl.BlockSpec((HALO, WIDTH), imap(CG // WIDTH)),
            pl.BlockSpec((HALO, WIDTH), imap(XC // WIDTH))]


def _const_spec(shape):
    nd = len(shape)
    return pl.BlockSpec(shape, lambda *_: (0,) * nd)


def _place():
    return lax.axis_index("x"), lax.axis_index("y"), lax.axis_index("c")


def _chip_peer(x, y, jm):
    px = (1 - x) if (jm & 2) else x
    py = (1 - y) if (jm & 1) else y
    return px, py


def _chip_part(buf, k, r0=0, nr=None):
    if len(buf.shape) == 3:
        return buf.at[k] if nr is None else buf.at[k, pl.ds(r0, nr)]
    cols = buf.shape[1] // N_CHIPS
    rows = pl.ds(0, buf.shape[0]) if nr is None else pl.ds(r0, nr)
    return buf.at[rows, pl.ds(pl.multiple_of(k * cols, 128), cols)]


def _gather_plan(items):
    def plan(ins, thru, fresh, x, y, c):
        k_me = 2 * x + y
        cps = []
        for jm in (1, 2, 3):
            px, py = _chip_peer(x, y, jm)
            k_peer = 2 * px + py
            for ti, r0, nr in items:
                mine = _chip_part(thru[ti], k_me, r0, nr)
                cps.append((mine, mine, _chip_part(thru[ti], k_peer, r0, nr), (px, py, c)))
        return cps
    return plan


def _scatter_plan(items):
    def plan(ins, thru, fresh, x, y, c):
        cps = []
        for jj, jm in enumerate((1, 2, 3)):
            px, py = _chip_peer(x, y, jm)
            k_peer = 2 * px + py
            for ii, fi in items:
                cps.append((_chip_part(ins[ii], k_peer), fresh[fi].at[jj], fresh[fi].at[jj], (px, py, c)))
        return cps
    return plan


def _comm(plan, n_copies, ins=(), thru=(), fresh=()):
    return dict(plan=plan, n=n_copies, ins=list(ins), thru=list(thru), fresh=list(fresh))


def _pcall(body, *, name, grid, in_specs, out_specs, out_shape, scratch, sem, args, comm=None):
    if comm is None:
        outs = pl.pallas_call(body, name=name, grid=grid, in_specs=in_specs, out_specs=out_specs,
                              out_shape=out_shape, scratch_shapes=list(scratch), compiler_params=_params(sem))(*args)
        return list(outs), [], []
    n_in, n_out, n_scr = len(in_specs), len(out_specs), len(scratch)
    n_ci, n_ct, n_cf, n_cp = len(comm["ins"]), len(comm["thru"]), len(comm["fresh"]), comm["n"]

    def wrapped(*refs):
        pos = 0
        def take(n):
            nonlocal pos
            got = refs[pos:pos + n]
            pos += n
            return got
        a, ci, _ = take(n_in), take(n_ci), take(n_ct)
        o, ct, cf = take(n_out), take(n_ct), take(n_cf)
        scr = take(n_scr)
        send_sems, recv_sems = take(2)
        first = functools.reduce(jnp.logical_and, [pl.program_id(d) == 0 for d in range(len(grid))])
        last = functools.reduce(jnp.logical_and, [pl.program_id(d) == grid[d] - 1 for d in range(len(grid))])
        x, y, c = _place()
        cps = comm["plan"](ci, ct, cf, x, y, c)
        assert len(cps) == n_cp

        def copy(i, src, dst, dev):
            return pltpu.make_async_remote_copy(src_ref=src, dst_ref=dst, send_sem=send_sems.at[i],
                                                recv_sem=recv_sems.at[i], device_id=dev, device_id_type=MESH)

        @pl.when(first)
        def _():
            for i, (src, dst, _, dev) in enumerate(cps):
                copy(i, src, dst, dev).start()

        body(*a, *o, *scr)

        @pl.when(last)
        def _():
            for i, (src, _, land, dev) in enumerate(cps):
                copy(i, src, land, dev).wait_recv()
            for i, (src, dst, _, dev) in enumerate(cps):
                copy(i, src, dst, dev).wait_send()

    thru_shapes = [jax.ShapeDtypeStruct(t.shape, t.dtype) for t in comm["thru"]]
    outs = pl.pallas_call(
        wrapped, name=name, grid=grid,
        in_specs=list(in_specs) + [ANY] * (n_ci + n_ct),
        out_specs=list(out_specs) + [ANY] * (n_ct + n_cf),
        out_shape=list(out_shape) + thru_shapes + comm["fresh"],
        input_output_aliases={n_in + n_ci + t: n_out + t for t in range(n_ct)},
        scratch_shapes=list(scratch) + [pltpu.SemaphoreType.DMA((n_cp,)), pltpu.SemaphoreType.DMA((n_cp,))],
        compiler_params=_params(sem),
    )(*args, *comm["ins"], *comm["thru"])
    outs = list(outs)
    return outs[:n_out], outs[n_out:n_out + n_ct], outs[n_out + n_ct:]


def _gather_first(name, w_all, pieces=4):
    hr = w_all.shape[0] // 2
    pr = hr // pieces
    n = 3 * pieces

    def body(_, w_ref, send1, recv1, send2, recv2):
        x, y, c = _place()
        k_me = 2 * x + y

        def part(k, core, q):
            return _chip_part(w_ref, k, pl.multiple_of(core * hr + q * pr, 16), pr)

        def copy(src, dst, send, recv, i, dev):
            return pltpu.make_async_remote_copy(src_ref=src, dst_ref=dst, send_sem=send.at[i], recv_sem=recv.at[i],
                                                device_id=dev, device_id_type=MESH)

        over_ici, landing, onward, from_sibling = [], [], [], []
        for jj, jm in enumerate((1, 2, 3)):
            px, py = _chip_peer(x, y, jm)
            k_peer = 2 * px + py
            for q in range(pieces):
                i = jj * pieces + q
                over_ici.append(copy(part(k_me, c, q), part(k_me, c, q), send1, recv1, i, (px, py, c)))
                landing.append(copy(part(k_me, c, q), part(k_peer, c, q), send1, recv1, i, (px, py, c)))
                onward.append(copy(part(k_peer, c, q), part(k_peer, c, q), send2, recv2, i, (x, y, 1 - c)))
                from_sibling.append(copy(part(k_peer, c, q), part(k_peer, 1 - c, q), send2, recv2, i, (x, y, 1 - c)))
        for cp in over_ici:
            cp.start()
        for i in range(n):
            landing[i].wait_recv()
            onward[i].start()
        for i in range(n):
            from_sibling[i].wait_recv()
        for i in range(n):
            over_ici[i].wait_send()
            onward[i].wait_send()

    return pl.pallas_call(
        body, name=name, in_specs=[ANY], out_specs=ANY, out_shape=jax.ShapeDtypeStruct(w_all.shape, w_all.dtype),
        input_output_aliases={0: 0}, scratch_shapes=[pltpu.SemaphoreType.DMA((n,))] * 4,
    )(w_all)


def _f1(x, norm_g3, w_all, l, comm=None, tm=256):
    s = x.shape[0]

    def body(x_ref, g_ref, w_hbm, p_ref, h_ref, w_vmem):
        @pl.when(pl.program_id(0) == 0)
        def _():
            pltpu.sync_copy(w_hbm, w_vmem)
        xv = x_ref[...]
        r = lax.rsqrt(jnp.mean(xv * xv, axis=-1, keepdims=True) + RMS_EPS)
        hb = ((xv * r) * g_ref[...]).astype(BF16)
        h_ref[...] = hb
        p_ref[...] = _dot(hb, w_vmem[...]).astype(BF16)

    return _pcall(
        body, name=f"f1_l{l}", grid=(s // tm,),
        in_specs=[pl.BlockSpec((tm, D_MODEL), lambda i: (i, 0)),
                  pl.BlockSpec((None, 1, D_MODEL), lambda i: (l, 0, 0)), ANY],
        out_specs=[pl.BlockSpec((tm, IN_TOTAL), lambda i: (i, 0)),
                   pl.BlockSpec((tm, D_MODEL), lambda i: (i, 0))],
        out_shape=[jax.ShapeDtypeStruct((s, IN_TOTAL), BF16), jax.ShapeDtypeStruct((s, D_MODEL), BF16)],
        scratch=[pltpu.VMEM((D_MODEL, IN_TOTAL), BF16)], sem=("arbitrary",), args=[x, norm_g3, w_all], comm=comm)


def _f2(x, p, lw, l, comm=None, t=256):
    s = x.shape[0]
    n = s // t

    def body(p_ref, hxb_ref, hcg_ref, hxc_ref, x_ref, lng, lnb, w2, bst, cw, cb, wpool, ps,
             wpa, wpb, wpc, wo, ya_ref, yb_ref, yc_ref, m_ref, xo_ref, sg_scr):
        i = pl.program_id(0)
        w = dict(ln_g=lng, ln_b=lnb, w2=w2, bst=bst, cw=cw, cb=cb, wpool=wpool, ps=ps)
        r = _branches_fwd(p_ref, hxb_ref, hcg_ref, hxc_ref, i == 0, i * t, w, sg_scr)
        ya = _dot(r["a_in"].astype(BF16), wpa[...])
        yb = _dot(r["b_in"].astype(BF16), wpb[...])
        yc = _dot(r["c_in"].astype(BF16), wpc[...])
        ya_ref[...] = ya.astype(BF16)
        yb_ref[...] = yb.astype(BF16)
        yc_ref[...] = yc.astype(BF16)
        m = (_sigmoid(p_ref[:, GA:GA + D_MODEL].astype(F32)) * ya
             + _sigmoid(p_ref[:, GB:GB + D_MODEL].astype(F32)) * yb
             + _sigmoid(p_ref[:, GC:GC + D_MODEL].astype(F32)) * yc)
        mb = m.astype(BF16)
        m_ref[...] = mb
        xo_ref[...] = x_ref[...] + _dot(mb, wo[...])

    tile = lambda c: pl.BlockSpec((t, c), lambda i: (i, 0))
    lsel = lambda *blk: pl.BlockSpec((None,) + blk, lambda i: (l,) + (0,) * len(blk))
    proj = lambda b: pl.BlockSpec((None, None, WIDTH, D_MODEL), lambda i: (l, b, 0, 0))
    act = jax.ShapeDtypeStruct((s, D_MODEL), BF16)
    return _pcall(
        body, name=f"f2_l{l}", grid=(n,),
        in_specs=[tile(IN_TOTAL)] + _halo_specs(t) + [
            tile(D_MODEL), lsel(1, WIDTH), lsel(1, WIDTH), lsel(4, 256, 128), lsel(CHUNK, WIDTH),
            lsel(8, WIDTH), lsel(1, WIDTH), lsel(4, 128, 128), lsel(1, WIDTH),
            proj(0), proj(1), proj(2), lsel(D_MODEL, D_MODEL)],
        out_specs=[tile(D_MODEL)] * 5,
        out_shape=[act, act, act, act, jax.ShapeDtypeStruct((s, D_MODEL), F32)],
        scratch=[pltpu.VMEM((t, WIDTH), F32)], sem=("arbitrary",),
        args=[p, p, p, p, x, lw["ln_g"], lw["ln_b"], lw["w2"], lw["bst"], lw["cw"], lw["cb"], lw["wpool"], lw["ps"],
              lw["proj"], lw["proj"], lw["proj"], lw["wo"]], comm=comm)


def _b1(p, dout, ya, yb, yc, lw, l, comm=None, t=256):
    s = p.shape[0]
    n = s // t
    nch = t // CHUNK

    def body(p_ref, hxb_ref, hcg_ref, hxc_ref, do_ref, ya_ref, yb_ref, yc_ref,
             lng, lnb, w2, wt2, bst, cw, cb, wpool, ps, proj_h, wo_h, sel_ref,
             dp_ref, gwp_h, gws_ref, gbs_ref, glng_ref, glnb_ref, gcw_ref, gcb_ref,
             gwpool_ref, gps_ref,
             wpa, wpb, wpc, wo, gwpa, gwpb, gwpc, gbs_acc, vec_acc, sg_scr, dvn_scr, car_dc, car_e):
        i = pl.program_id(0)
        ti = n - 1 - i

        @pl.when(i == 0)
        def _():
            pltpu.sync_copy(proj_h.at[l, 0], wpa)
            pltpu.sync_copy(proj_h.at[l, 1], wpb)
            pltpu.sync_copy(proj_h.at[l, 2], wpc)
            pltpu.sync_copy(wo_h.at[l], wo)
            for acc in (gwpa, gwpb, gwpc, gbs_acc, vec_acc, car_dc, car_e):
                acc[...] = jnp.zeros(acc.shape, acc.dtype)
            gws_ref[...] = jnp.zeros(gws_ref.shape, F32)
            gwpool_ref[...] = jnp.zeros(gwpool_ref.shape, F32)

        w = dict(ln_g=lng, ln_b=lnb, w2=w2, bst=bst, cw=cw, cb=cb, wpool=wpool, ps=ps)
        r = _branches_fwd(p_ref, hxb_ref, hcg_ref, hxc_ref, ti == 0, ti * t, w, sg_scr)

        def seg(o, width=WIDTH):
            return p_ref[:, o:o + width].astype(F32)

        def put(o, val):
            dp_ref[:, o:o + val.shape[1]] = val.astype(BF16)

        dob = do_ref[...].astype(BF16)
        dm = _dotg(dob, wo[...], NT)

        def merge_bwd(goff, y_ref, xin, wp, gwp):
            sx = _sigmoid(seg(goff, D_MODEL))
            dmy = dm * sx
            put(goff, dmy * y_ref[...].astype(F32) * (1.0 - sx))
            dyb = dmy.astype(BF16)
            gwp[...] += _dotg(xin.astype(BF16), dyb, TN)
            return _dotg(dyb, wp[...], NT)

        d_ain = merge_bwd(GA, ya_ref, r["a_in"], wpa, gwpa)
        d_bin = merge_bwd(GB, yb_ref, r["b_in"], wpb, gwpb)
        d_cin = merge_bwd(GC, yc_ref, r["c_in"], wpc, gwpc)

        def dsilu(z, sz):
            return sz * (1.0 + z * (1.0 - sz))

        za, sa = r["za"], r["sa"]
        d_abr = d_ain * (za * sa)
        put(ZA, d_ain * r["a_br"] * dsilu(za, sa))
        put(U, d_abr * r["sg"] * _gelu_grad(r["pu"], r["th_u"]))
        d_sg = d_abr * r["u_act"]
        dsgb = d_sg.astype(BF16)
        lo = lax.broadcasted_iota(jnp.int32, (CHUNK, CHUNK), 1) < 64
        zero = jnp.zeros((CHUNK, CHUNK), BF16)
        for c in range(nch):
            rows = slice(c * CHUNK, (c + 1) * CHUNK)
            gbs_acc[...] += d_sg[rows]
            for j in range(4):
                cols = slice(j * 128, (j + 1) * 128)
                dj = dsgb[rows, cols]
                zt = _dot(wt2[j], dj)
                dvn_scr[rows, cols] = jnp.where(lo, zt[:CHUNK], zt[CHUNK:])
                stacked = jnp.concatenate([jnp.where(lo, dj, zero), jnp.where(lo, zero, dj)], axis=0)
                gws_ref[j] += _dotg(stacked, r["vnb"][rows, cols], NT)
        d_vn = dvn_scr[...]
        vhat = r["vhat"]
        vec_acc[0] += _colsum8(d_vn * vhat)
        vec_acc[1] += _colsum8(d_vn)
        d_vhat = d_vn * lng[...]
        d_vg = r["rs"] * (d_vhat - jnp.mean(d_vhat, axis=-1, keepdims=True)
                          - vhat * jnp.mean(d_vhat * vhat, axis=-1, keepdims=True))
        put(V, d_vg * _gelu_grad(r["pv"], r["th_v"]))

        zb, sb = r["zb"], r["sb"]
        d_bbr = d_bin * (zb * sb)
        put(ZB, d_bin * r["b_br"] * dsilu(zb, sb))
        put(BG, d_bbr * r["conv"])
        dc = d_bbr * r["bg"]
        vec_acc[2] += _colsum8(dc)
        vec_acc[3] += _colsum8(dc * r["y2"])
        vec_acc[4] += _colsum8(dc * r["y1"])
        vec_acc[5] += _colsum8(dc * r["yb0"])
        ext = jnp.concatenate([dc, car_dc[...]], axis=0)
        ne = t + HALO
        d1 = pltpu.roll(ext, ne - 1, 0)[:t]
        d2 = pltpu.roll(ext, ne - 2, 0)[:t]
        d_yb0 = cw[2:3, :] * dc + cw[1:2, :] * d1 + cw[0:1, :] * d2
        put(CG, d_yb0 * r["xb"])
        put(XB, d_yb0 * r["cg"])
        car_dc[...] = dc[:HALO]

        zc, sc = r["zc"], r["sc"]
        d_cbr = d_cin * (zc * sc)
        put(ZC, d_cin * r["c_br"] * dsilu(zc, sc))
        vec_acc[6] += _colsum8(d_cbr * r["q"])
        d_q = d_cbr * ps[...]
        for gi, win in enumerate(POOL_WINDOWS):
            cols = slice(gi * 128, (gi + 1) * 128)
            dqb = d_q[:, cols].astype(BF16)
            d_pool = _dotg(dqb, wpool[gi], NT)
            gwpool_ref[gi] += _dotg(r["pooled"][gi].astype(BF16), dqb, TN)
            e = d_pool * r["inv"][gi]
            sx = jnp.concatenate([e, car_e[:, cols]], axis=0)
            sh = 1
            while sh < win:
                sx = sx + pltpu.roll(sx, ne - sh, 0)
                sh *= 2
            put(XC + gi * 128, sx[:t] - d_pool)
            car_e[:, cols] = e[:HALO]

        @pl.when(i == n - 1)
        def _():
            for b, (acc, stage) in enumerate(((gwpa, wpa), (gwpb, wpb), (gwpc, wpc))):
                stage[...] = acc[...].astype(BF16)
                pltpu.sync_copy(stage, gwp_h.at[b])
            gbs_ref[...] = jnp.dot(gbs_acc[...], sel_ref[...], preferred_element_type=F32,
                                   precision=lax.Precision.HIGHEST)
            red = lambda k: jnp.sum(vec_acc[k], axis=0, keepdims=True)
            glng_ref[...] = red(0)
            glnb_ref[...] = red(1)
            gcb_ref[...] = red(2)
            gcw_ref[...] = jnp.zeros(gcw_ref.shape, F32)
            for k in range(3):
                gcw_ref[k:k + 1, :] = red(3 + k)
            gps_ref[...] = red(6)
            tt = lax.broadcasted_iota(jnp.int32, (2 * CHUNK, CHUNK), 0) % CHUNK
            ss = lax.broadcasted_iota(jnp.int32, (2 * CHUNK, CHUNK), 1)
            for j in range(4):
                gws_ref[j] = jnp.where(tt >= ss, gws_ref[j], 0.0)

    rtile = lambda c: pl.BlockSpec((t, c), lambda i: (n - 1 - i, 0))
    lsel = lambda *blk: pl.BlockSpec((None,) + blk, lambda i: (l,) + (0,) * len(blk))
    f32s = lambda *shape: jax.ShapeDtypeStruct(shape, F32)
    return _pcall(
        body, name=f"b1_l{l}", grid=(n,),
        in_specs=[rtile(IN_TOTAL)] + _halo_specs(t, rev_n=n) + [rtile(D_MODEL)] * 4 + [
            lsel(1, WIDTH), lsel(1, WIDTH), lsel(4, 256, 128), lsel(4, 256, 128), lsel(CHUNK, WIDTH),
            lsel(8, WIDTH), lsel(1, WIDTH), lsel(4, 128, 128), lsel(1, WIDTH),
            ANY, ANY, _const_spec((WIDTH, 128))],
        out_specs=[rtile(IN_TOTAL), ANY,
                   _const_spec((4, 256, 128)), _const_spec((CHUNK, 128)), _const_spec((1, WIDTH)),
                   _const_spec((1, WIDTH)), _const_spec((8, WIDTH)), _const_spec((1, WIDTH)),
                   _const_spec((4, 128, 128)), _const_spec((1, WIDTH))],
        out_shape=[jax.ShapeDtypeStruct((s, IN_TOTAL), BF16), jax.ShapeDtypeStruct((3, WIDTH, D_MODEL), BF16),
                   f32s(4, 256, 128), f32s(CHUNK, 128),
                   f32s(1, WIDTH), f32s(1, WIDTH), f32s(8, WIDTH), f32s(1, WIDTH), f32s(4, 128, 128),
                   f32s(1, WIDTH)],
        scratch=[pltpu.VMEM((WIDTH, D_MODEL), BF16), pltpu.VMEM((WIDTH, D_MODEL), BF16),
                 pltpu.VMEM((WIDTH, D_MODEL), BF16), pltpu.VMEM((D_MODEL, D_MODEL), BF16),
                 pltpu.VMEM((WIDTH, D_MODEL), F32),
                 pltpu.VMEM((WIDTH, D_MODEL), F32), pltpu.VMEM((WIDTH, D_MODEL), F32),
                 pltpu.VMEM((CHUNK, WIDTH), F32), pltpu.VMEM((8, 8, WIDTH), F32),
                 pltpu.VMEM((t, WIDTH), F32), pltpu.VMEM((t, WIDTH), F32),
                 pltpu.VMEM((HALO, WIDTH), F32), pltpu.VMEM((HALO, WIDTH), F32)],
        sem=("arbitrary",),
        args=[p, p, p, p, dout, ya, yb, yc, lw["ln_g"], lw["ln_b"], lw["w2"], lw["wt2"], lw["bst"], lw["cw"],
              lw["cb"], lw["wpool"], lw["ps"], lw["proj"], lw["wo"], lw["sel"]], comm=comm)


def _rms_bwd(xv, g, dh):
    r = lax.rsqrt(jnp.mean(xv * xv, axis=-1, keepdims=True) + RMS_EPS)
    xhat = xv * r
    dxh = dh * g
    dx = r * (dxh - xhat * jnp.mean(dxh * xhat, axis=-1, keepdims=True))
    return dx, dh * xhat


def _b2a(dp, w_all, x, dout, norm_g3, l, comm=None, tm=256):
    s = x.shape[0]
    nm = s // tm

    def body(dp_ref, w_hbm, x_ref, do_ref, g_ref, dx_ref, gg_ref, w_vmem, gacc):
        i = pl.program_id(0)

        @pl.when(i == 0)
        def _():
            pltpu.sync_copy(w_hbm, w_vmem)
            gacc[...] = jnp.zeros(gacc.shape, F32)

        dh = _dotg(dp_ref[...], w_vmem[...], NT)
        dx, gx = _rms_bwd(x_ref[...], g_ref[...], dh)
        dx_ref[...] = do_ref[...] + dx
        gacc[...] += _colsum8(gx)

        @pl.when(i == nm - 1)
        def _():
            gg_ref[...] = jnp.sum(gacc[...], axis=0, keepdims=True)

    return _pcall(
        body, name=f"b2a_l{l}", grid=(nm,),
        in_specs=[pl.BlockSpec((tm, IN_TOTAL), lambda i: (i, 0)), ANY,
                  pl.BlockSpec((tm, D_MODEL), lambda i: (i, 0)),
                  pl.BlockSpec((tm, D_MODEL), lambda i: (i, 0)),
                  pl.BlockSpec((None, 1, D_MODEL), lambda i: (l, 0, 0))],
        out_specs=[pl.BlockSpec((tm, D_MODEL), lambda i: (i, 0)),
                   pl.BlockSpec((1, D_MODEL), lambda i: (0, 0))],
        out_shape=[jax.ShapeDtypeStruct((s, D_MODEL), F32), jax.ShapeDtypeStruct((1, D_MODEL), F32)],
        scratch=[pltpu.VMEM((D_MODEL, IN_TOTAL), BF16), pltpu.VMEM((8, D_MODEL), F32)],
        sem=("arbitrary",), args=[dp, w_all, x, dout, norm_g3], comm=comm)


def _b2b(h, dp, l, comm=None, tk=1024):
    s = h.shape[0]
    nk = s // tk

    def body(h_ref, dp_ref, g_ref, acc):
        kk = pl.program_id(1)

        @pl.when(kk == 0)
        def _():
            acc[...] = jnp.zeros(acc.shape, F32)

        acc[...] += _dotg(h_ref[...], dp_ref[...], TN)

        @pl.when(kk == nk - 1)
        def _():
            g_ref[...] = acc[...].astype(BF16)

    return _pcall(
        body, name=f"b2b_l{l}", grid=(N_CHIPS, nk),
        in_specs=[pl.BlockSpec((tk, D_MODEL), lambda k, kk: (kk, 0)),
                  pl.BlockSpec((tk, SHARD_W), lambda k, kk: (kk, k))],
        out_specs=[pl.BlockSpec((D_MODEL, SHARD_W), lambda k, kk: (0, k))],
        out_shape=[jax.ShapeDtypeStruct((D_MODEL, IN_TOTAL), BF16)],
        scratch=[pltpu.VMEM((D_MODEL, SHARD_W), F32)],
        sem=("arbitrary", "arbitrary"), args=[h, dp], comm=comm)


def _gwo(m, dout, l, tk=512):
    s = m.shape[0]
    nk = s // tk

    def body(m_ref, do_ref, g_ref, acc):
        kk = pl.program_id(0)
        prod = _dotg(m_ref[...], do_ref[...].astype(BF16), TN)

        @pl.when(kk == 0)
        def _():
            acc[...] = prod

        @pl.when(kk > 0)
        def _():
            acc[...] += prod

        @pl.when(kk == nk - 1)
        def _():
            g_ref[...] = acc[...].astype(BF16)

    return pl.pallas_call(
        body, name=f"gwo_l{l}", grid=(nk,),
        in_specs=[pl.BlockSpec((tk, D_MODEL), lambda kk: (kk, 0)), pl.BlockSpec((tk, D_MODEL), lambda kk: (kk, 0))],
        out_specs=_const_spec((D_MODEL, D_MODEL)),
        out_shape=jax.ShapeDtypeStruct((D_MODEL, D_MODEL), BF16),
        scratch_shapes=[pltpu.VMEM((D_MODEL, D_MODEL), F32)],
        compiler_params=_params(("arbitrary",)),
    )(m, dout)


def _loss_head(x, tgt, final_g2, tm=512):
    s = x.shape[0]
    nm = s // tm

    def body(x_ref, t_ref, g_ref, dx_ref, loss_ref, gg_ref, lacc, gacc):
        i = pl.program_id(0)

        @pl.when(i == 0)
        def _():
            lacc[...] = jnp.zeros(lacc.shape, F32)
            gacc[...] = jnp.zeros(gacc.shape, F32)

        xv = x_ref[...]
        g = g_ref[...]
        r = lax.rsqrt(jnp.mean(xv * xv, axis=-1, keepdims=True) + RMS_EPS)
        err = (xv * r) * g - t_ref[...]
        lacc[...] += _colsum8(err * err)
        dx, gx = _rms_bwd(xv, g, err * (1.0 / D_MODEL))
        dx_ref[...] = dx
        gacc[...] += _colsum8(gx)

        @pl.when(i == nm - 1)
        def _():
            tot = jnp.sum(jnp.sum(lacc[...], axis=0, keepdims=True), axis=1, keepdims=True)
            loss_ref[...] = jnp.broadcast_to(tot * (0.5 / D_MODEL), loss_ref.shape)
            gg_ref[...] = jnp.sum(gacc[...], axis=0, keepdims=True)

    return pl.pallas_call(
        body, name="loss_head", grid=(nm,),
        in_specs=[pl.BlockSpec((tm, D_MODEL), lambda i: (i, 0)), pl.BlockSpec((tm, D_MODEL), lambda i: (i, 0)),
                  _const_spec((1, D_MODEL))],
        out_specs=[pl.BlockSpec((tm, D_MODEL), lambda i: (i, 0)), _const_spec((8, 128)),
                   _const_spec((1, D_MODEL))],
        out_shape=[jax.ShapeDtypeStruct((s, D_MODEL), F32), jax.ShapeDtypeStruct((8, 128), F32),
                   jax.ShapeDtypeStruct((1, D_MODEL), F32)],
        scratch_shapes=[pltpu.VMEM((8, D_MODEL), F32), pltpu.VMEM((8, D_MODEL), F32)],
        compiler_params=_params(("arbitrary",)),
    )(x, tgt, final_g2)


def _row_block(rows, cols, n_arrays):
    budget = VMEM_LIMIT // 3 // (2 * 4 * n_arrays * cols)
    rb = rows
    while rb > budget and rb % 16 == 0:
        rb //= 2
    return rb


def _cast_slot(name, a, l, chip):
    _, rows, cols = a.shape
    rb = _row_block(rows, cols, 2)

    def body(idx_ref, a_ref, o_ref):
        o_ref[...] = a_ref[...].astype(BF16)

    gs = pltpu.PrefetchScalarGridSpec(
        num_scalar_prefetch=1, grid=(rows // rb,),
        in_specs=[pl.BlockSpec((None, rb, cols), lambda i, idx: (l, i, 0))],
        out_specs=pl.BlockSpec((None, rb, cols), lambda i, idx: (idx[0], i, 0)))
    return pl.pallas_call(body, name=name, grid_spec=gs, out_shape=jax.ShapeDtypeStruct((N_CHIPS, rows, cols), BF16),
                          compiler_params=_params(("arbitrary",)))(chip, a)


def _cast_cols(name, a, l, chip):
    _, rows, cols = a.shape
    rb = _row_block(rows, cols, 2)

    def body(idx_ref, a_ref, o_ref):
        o_ref[...] = a_ref[...].astype(BF16)

    gs = pltpu.PrefetchScalarGridSpec(
        num_scalar_prefetch=1, grid=(rows // rb,),
        in_specs=[pl.BlockSpec((None, rb, cols), lambda i, idx: (l, i, 0))],
        out_specs=pl.BlockSpec((rb, cols), lambda i, idx: (i, idx[0])))
    return pl.pallas_call(body, name=name, grid_spec=gs, out_shape=jax.ShapeDtypeStruct((rows, N_CHIPS * cols), BF16),
                          compiler_params=_params(("arbitrary",)))(chip, a)


def _sum4(name, own, chip, recv, l, n_layers, prev=None):
    _, rows, cols = recv.shape
    rb = _row_block(rows, cols, 5)
    nb = rows // rb
    if len(own.shape) == 3:
        own_spec = pl.BlockSpec((None, rb, cols), lambda i, idx: (idx[0], i, 0))
    else:
        own_spec = pl.BlockSpec((rb, cols), lambda i, idx: (i, idx[0]))

    def body(idx_ref, own_ref, r_ref, *rest):
        o_ref = rest[-1]
        o_ref[...] = ((own_ref[...].astype(F32) + r_ref[0].astype(F32)) + r_ref[1].astype(F32)) + r_ref[2].astype(F32)

    gs = pltpu.PrefetchScalarGridSpec(
        num_scalar_prefetch=1, grid=(nb,),
        in_specs=[own_spec,
                  pl.BlockSpec((3, rb, cols), lambda i, idx: (0, i, 0))] + ([ANY] if prev is not None else []),
        out_specs=pl.BlockSpec((rb, cols), lambda i, idx: (l * nb + i, 0)))
    args = (chip, own, recv) + ((prev,) if prev is not None else ())
    return pl.pallas_call(body, name=name, grid_spec=gs,
                          out_shape=jax.ShapeDtypeStruct((n_layers * rows, cols), F32),
                          input_output_aliases=({3: 0} if prev is not None else {}),
                          compiler_params=_params(("arbitrary",)))(*args)


def _adamw_math(w, g, m, v):
    m = ADAM_B1 * m + (1.0 - ADAM_B1) * g
    v = ADAM_B2 * v + (1.0 - ADAM_B2) * (g * g)
    m_hat = m / (1.0 - ADAM_B1 ** ADAM_STEP)
    v_hat = v / (1.0 - ADAM_B2 ** ADAM_STEP)
    delta = -ADAM_LR * (m_hat / (jnp.sqrt(v_hat) + ADAM_EPS) + ADAM_WD * w)
    return delta, m, v


def _adamw(name, w, m, v, g_parts):
    rows, cols = w.shape
    np_ = len(g_parts)
    rb = _row_block(rows, cols, 7 + np_)

    def body(*refs):
        w_ref, m_ref, v_ref = refs[:3]
        g_refs = refs[3:3 + np_]
        go_ref, d_ref, mo_ref, vo_ref = refs[3 + np_:]
        g = g_refs[0][...]
        for gr in g_refs[1:]:
            g = g + gr[...]
        d, mn, vn = _adamw_math(w_ref[...], g, m_ref[...], v_ref[...])
        go_ref[...] = g
        d_ref[...] = d
        mo_ref[...] = mn
        vo_ref[...] = vn

    spec = pl.BlockSpec((rb, cols), lambda i: (i, 0))
    shp = jax.ShapeDtypeStruct((rows, cols), F32)
    return pl.pallas_call(body, name=name, grid=(rows // rb,), in_specs=[spec] * (3 + np_),
                          out_specs=[spec] * 4, out_shape=[shp] * 4,
                          compiler_params=_params(("arbitrary",)))(w, m, v, *g_parts)


def _swap_sibling(name, arrs):
    na = len(arrs)

    def body(*refs):
        src = refs[:na]
        out = refs[na:2 * na]
        send_sems, recv_sems = refs[2 * na:]
        x, y, c = _place()
        cps = [pltpu.make_async_remote_copy(
            src_ref=src[a], dst_ref=out[a], send_sem=send_sems.at[a], recv_sem=recv_sems.at[a],
            device_id=(x, y, 1 - c), device_id_type=MESH) for a in range(na)]
        for cp in cps:
            cp.start()
        for cp in cps:
            cp.wait_recv()
        for cp in cps:
            cp.wait_send()

    return pl.pallas_call(
        body, name=name, in_specs=[ANY] * na, out_specs=[ANY] * na,
        out_shape=[jax.ShapeDtypeStruct(a.shape, a.dtype) for a in arrs],
        scratch_shapes=[pltpu.SemaphoreType.DMA((na,)), pltpu.SemaphoreType.DMA((na,))],
    )(*arrs)


def _all_reduce_small(name, a):
    def body(a_ref, o_ref, buf, send_sems, recv_sems):
        x, y, c = _place()
        o_ref[...] = a_ref[...]
        for rnd, peer in enumerate(((x, y, 1 - c), (x, 1 - y, c), (1 - x, y, c))):
            cp = pltpu.make_async_remote_copy(
                src_ref=o_ref, dst_ref=buf.at[rnd], send_sem=send_sems.at[rnd], recv_sem=recv_sems.at[rnd],
                device_id=peer, device_id_type=MESH)
            cp.start()
            cp.wait_recv()
            cp.wait_send()
            o_ref[...] = o_ref[...] + buf[rnd]

    vm = pl.BlockSpec(memory_space=pltpu.VMEM)
    return pl.pallas_call(
        body, name=name, in_specs=[vm], out_specs=vm, out_shape=jax.ShapeDtypeStruct(a.shape, F32),
        scratch_shapes=[pltpu.VMEM((3,) + a.shape, F32), pltpu.SemaphoreType.DMA((3,)),
                        pltpu.SemaphoreType.DMA((3,))],
        compiler_params=_params(),
    )(a)


_SMALL = ("norm_g", "ln_g", "ln_b", "w_s", "b_s", "conv_b", "w_pool", "pool_scale", "final_g")


def _rows(a):
    return a.reshape(-1, 128)


def _pad_rows(a, mult=8):
    pad = (-a.shape[0]) % mult
    return jnp.pad(a, ((0, pad), (0, 0))) if pad else a


def kernel(x, norm_g, w_in, ln_g, ln_b, w_s, b_s, conv_w, conv_b, w_pool, pool_scale, w_pa, w_pb, w_pc, w_o, final_g, loss_target, m_norm_g, m_w_in, m_ln_g, m_ln_b, m_w_s, m_b_s, m_conv_w, m_conv_b, m_w_pool, m_pool_scale, m_w_pa, m_w_pb, m_w_pc, m_w_o, m_final_g, v_norm_g, v_w_in, v_ln_g, v_ln_b, v_w_s, v_b_s, v_conv_w, v_conv_b, v_w_pool, v_pool_scale, v_w_pa, v_w_pb, v_w_pc, v_w_o, v_final_g):
    W = dict(norm_g=norm_g, w_in=w_in, ln_g=ln_g, ln_b=ln_b, w_s=w_s, b_s=b_s, conv_w=conv_w, conv_b=conv_b,
             w_pool=w_pool, pool_scale=pool_scale, w_pa=w_pa, w_pb=w_pb, w_pc=w_pc, w_o=w_o, final_g=final_g)
    M = dict(norm_g=m_norm_g, w_in=m_w_in, ln_g=m_ln_g, ln_b=m_ln_b, w_s=m_w_s, b_s=m_b_s, conv_w=m_conv_w,
             conv_b=m_conv_b, w_pool=m_w_pool, pool_scale=m_pool_scale, w_pa=m_w_pa, w_pb=m_w_pb, w_pc=m_w_pc,
             w_o=m_w_o, final_g=m_final_g)
    Vv = dict(norm_g=v_norm_g, w_in=v_w_in, ln_g=v_ln_g, ln_b=v_ln_b, w_s=v_w_s, b_s=v_b_s, conv_w=v_conv_w,
              conv_b=v_conv_b, w_pool=v_w_pool, pool_scale=v_pool_scale, w_pa=v_w_pa, w_pb=v_w_pb, w_pc=v_w_pc,
              w_o=v_w_o, final_g=v_final_g)
    L = DEPTH
    s = x.shape[1]
    xs = x.reshape(s, D_MODEL)
    tgt = loss_target.reshape(s, D_MODEL)
    k_me = (2 * lax.axis_index("x") + lax.axis_index("y")).astype(jnp.int32)

    chip = k_me.reshape(1)
    assert L == 2
    half_rows = D_MODEL // 2

    land_win = [_cast_cols(f"cast_w_in{l}", w_in, l, chip) for l in range(L)]
    pcat = lambda d: jnp.concatenate([d[b][l] for l in range(L) for b in ("w_pa", "w_pb", "w_pc")], axis=0)
    land_proj = _cast_slot("cast_proj", pcat(W)[None], 0, chip)
    land_wo = _cast_slot("cast_w_o", w_o.reshape(1, L * 256, D_MODEL), 0, chip)
    cw_sh = jnp.pad(conv_w, ((0, 0), (0, 5), (0, 0))).reshape(1, L * 8, 128)
    land_cw = lax.dynamic_update_slice(jnp.zeros((N_CHIPS, L * 8, 128), F32), cw_sh, (k_me, 0, 0))
    w_all0 = _gather_first("gather_w_in0", land_win[0])

    causal = jnp.tril(jnp.ones((CHUNK, CHUNK), dtype=bool))
    w_m = jnp.where(causal, w_s, 0.0)
    lw = dict(
        ln_g=ln_g.reshape(L, 1, WIDTH), ln_b=ln_b.reshape(L, 1, WIDTH),
        w2=w_m.reshape(L, 4, 256, CHUNK).astype(BF16),
        wt2=jnp.swapaxes(w_m, -1, -2).reshape(L, 4, 256, CHUNK).astype(BF16),
        bst=jnp.repeat(jnp.swapaxes(b_s, -1, -2), 64, axis=-1),
        cb=conv_b.reshape(L, 1, WIDTH), wpool=w_pool.astype(BF16), ps=pool_scale.reshape(L, 1, WIDTH),
        sel=(jnp.arange(WIDTH)[:, None] // 64 == jnp.arange(128)[None, :]).astype(F32))
    norm_g3 = norm_g.reshape(L, 1, D_MODEL)

    (p0, h0), (land_proj, land_wo, land_cw, w_half1), _ = _f1(
        xs, norm_g3, w_all0, 0,
        comm=_comm(_gather_plan([(0, 0, None), (1, 0, None), (2, 0, None), (3, 0, half_rows)]), 12,
                   thru=[land_proj, land_wo, land_cw, land_win[1]]))
    lw["proj"] = land_proj.transpose(1, 0, 2).reshape(L, 3, WIDTH, D_MODEL)
    lw["wo"] = land_wo.reshape(N_CHIPS, L, 256, D_MODEL).transpose(1, 0, 2, 3).reshape(L, D_MODEL, D_MODEL)
    lw["cw"] = land_cw.reshape(N_CHIPS, L, 8, 128).transpose(1, 2, 0, 3).reshape(L, 8, WIDTH)
    (ya0, yb0, yc0, mm0, x1), (w_all1,), _ = _f2(
        xs, p0, lw, 0, comm=_comm(_gather_plan([(0, half_rows, half_rows)]), 3, thru=[w_half1]))
    (p1, h1), _, _ = _f1(x1, norm_g3, w_all1, 1)
    (ya1, yb1, yc1, mm1, x2), _, _ = _f2(x1, p1, lw, 1)
    dxl, loss_blk, g_final = _loss_head(x2, tgt, final_g.reshape(1, D_MODEL))
    loss = lax.psum(loss_blk[0, 0], ("x", "y", "c"))

    def scatter(parts):
        part_shape = lambda a: a.shape[1:] if a.ndim == 3 else (a.shape[0], a.shape[1] // N_CHIPS)
        fresh = [jax.ShapeDtypeStruct((3,) + part_shape(a), a.dtype) for a in parts]
        return _comm(_scatter_plan([(i, i) for i in range(len(parts))]), 3 * len(parts), ins=parts, fresh=fresh)

    def small_grads(b1_outs, gng):
        gws, gbs, glng, glnb, gcw, gcb, gwpool, gps = b1_outs
        return dict(w_s=gws.reshape(8, CHUNK, CHUNK), b_s=gbs[:, :8].T, ln_g=glng[0], ln_b=glnb[0], conv_w=gcw[:3],
                    conv_b=gcb[0], w_pool=gwpool, pool_scale=gps[0], norm_g=gng[0])

    by_chip = lambda gwp, gwo: (gwp.reshape(3 * WIDTH, N_CHIPS, 256).transpose(1, 0, 2),
                                gwo.reshape(N_CHIPS, 256, D_MODEL))
    gl, gwin, recv_win, part_proj, part_wo, recv_proj, recv_wo = ([None] * L for _ in range(7))
    gwo1 = _gwo(mm1, dxl, 1)
    (dp1, gwp1, *sm1), _, _ = _b1(p1, dxl, ya1, yb1, yc1, lw, 1)
    (gwin[1],), _, _ = _b2b(h1, dp1, 1)
    (dxl, gng1), _, (recv_win[1],) = _b2a(dp1, w_all1, x1, dxl, norm_g3, 1, comm=scatter([gwin[1]]))
    gl[1] = small_grads(sm1, gng1)
    part_proj[1], part_wo[1] = by_chip(gwp1, gwo1)
    gwo0 = _gwo(mm0, dxl, 0)
    (dp0, gwp0, *sm0), _, (recv_proj[1], recv_wo[1]) = _b1(p0, dxl, ya0, yb0, yc0, lw, 0,
                                                            comm=scatter([part_proj[1], part_wo[1]]))
    part_proj[0], part_wo[0] = by_chip(gwp0, gwo0)
    (gwin[0],), _, (recv_proj[0], recv_wo[0]) = _b2b(h0, dp0, 0, comm=scatter([part_proj[0], part_wo[0]]))
    (dxl, gng0), _, (recv_win[0],) = _b2a(dp0, w_all0, xs, dxl, norm_g3, 0, comm=scatter([gwin[0]]))
    gl[0] = small_grads(sm0, gng0)
    grad_x = dxl.reshape(1, s, D_MODEL)
    st = lambda name: jnp.stack([gl[l][name] for l in range(L)])

    half = [None, None, None]
    for l in range(L):
        half[0] = _sum4(f"sum_w_in{l}", gwin[l], chip, recv_win[l], l, L, half[0])
        half[1] = _sum4(f"sum_proj{l}", part_proj[l], chip, recv_proj[l], l, L, half[1])
        half[2] = _sum4(f"sum_w_o{l}", part_wo[l], chip, recv_wo[l], l, L, half[2])
    other = _swap_sibling("swap_halves", half)

    outs = {}
    shard2d = dict(w_in=(L * D_MODEL, SHARD_W), w_o=(L * 256, D_MODEL))
    for a, name in ((0, "w_in"), (2, "w_o")):
        r2 = shard2d[name]
        res = _adamw(f"adamw_{name}", W[name].reshape(r2), M[name].reshape(r2), Vv[name].reshape(r2),
                     [half[a], other[a]])
        outs[name] = [o.reshape(W[name].shape) for o in res]
    res = _adamw("adamw_proj", pcat(W), pcat(M), pcat(Vv), [half[1], other[1]])
    for i, name in enumerate(("w_pa", "w_pb", "w_pc")):
        outs[name] = [o.reshape(L, 3, WIDTH, 256)[:, i] for o in res]

    gsm = dict(norm_g=st("norm_g"), ln_g=st("ln_g"), ln_b=st("ln_b"), w_s=st("w_s"), b_s=st("b_s"),
               conv_b=st("conv_b"), w_pool=st("w_pool"), pool_scale=st("pool_scale"), final_g=g_final[0])
    sizes = [W[nm].size // 128 for nm in _SMALL]
    n_small = sum(sizes)
    gpack = _pad_rows(jnp.concatenate([_rows(gsm[nm]) for nm in _SMALL] + [_rows(st("conv_w"))], axis=0))
    gred = _all_reduce_small("all_reduce_small", gpack)
    g_cw_full = gred[n_small:n_small + L * 3 * 4].reshape(L, 3, N_CHIPS, 128)
    g_cw_mine = lax.dynamic_index_in_dim(g_cw_full, k_me, axis=2, keepdims=False)
    pack = lambda d: _pad_rows(jnp.concatenate([_rows(d[nm]) for nm in _SMALL] + [_rows(d["conv_w"])], axis=0))
    gp2 = _pad_rows(jnp.concatenate([gred[:n_small], _rows(g_cw_mine)], axis=0))
    res = _adamw("adamw_small", pack(W), pack(M), pack(Vv), [gp2])
    off = 0
    for nm, sz in zip(_SMALL, sizes):
        outs[nm] = [o[off:off + sz].reshape(W[nm].shape) for o in res]
        off += sz
    outs["conv_w"] = [o[off:off + L * 3].reshape(W["conv_w"].shape) for o in res]

    order = ("norm_g", "w_in", "ln_g", "ln_b", "w_s", "b_s", "conv_w", "conv_b", "w_pool", "pool_scale",
             "w_pa", "w_pb", "w_pc", "w_o", "final_g")
    return (loss, grad_x, *[outs[nm][0] for nm in order], *[outs[nm][1] for nm in order],
            *[outs[nm][2] for nm in order], *[outs[nm][3] for nm in order])
```

```python
import functools

import jax
import jax.numpy as jnp
from jax import lax
from jax.experimental import pallas as pl
from jax.experimental.pallas import tpu as pltpu

F32 = jnp.float32
BF16 = jnp.bfloat16

D_MODEL = 1024
DEPTH = 2
CHUNK = 128
WIDTH = 512
POOL_WINDOWS = (2, 4, 8, 16)
IN_TOTAL = 7680
N_CHIPS = 4
SHARD_W = IN_TOTAL // N_CHIPS
RMS_EPS = 1e-6
LN_EPS = 1e-5
HALO = 16

U, V, ZA, XB, BG, CG, ZB, XC, ZC, GA, GB, GC = (0, 512, 1024, 1536, 2048, 2560, 3072, 3584, 4096, 4608, 5632, 6656)

ADAM_LR = 0.001
ADAM_B1 = 0.9
ADAM_B2 = 0.999
ADAM_EPS = 1e-08
ADAM_WD = 0.01
ADAM_STEP = 10

VMEM_LIMIT = 56 * 1024 * 1024
MESH = pl.DeviceIdType.MESH
ANY = pl.BlockSpec(memory_space=pl.ANY)
NT = (((1,), (1,)), ((), ()))
TN = (((0,), (0,)), ((), ()))


def _params(sem=None):
    kw = dict(vmem_limit_bytes=VMEM_LIMIT)
    if sem is not None:
        kw["dimension_semantics"] = sem
    return pltpu.CompilerParams(**kw)


def _dot(a, b):
    return jnp.dot(a, b, preferred_element_type=F32)


def _dotg(a, b, dims):
    return lax.dot_general(a, b, dims, preferred_element_type=F32)


def _sigmoid(x):
    return 1.0 / (1.0 + jnp.exp(-x))


_GELU_K = 0.7978845608028654


def _gelu(x):
    th = jnp.tanh(_GELU_K * (x + 0.044715 * (x * x * x)))
    return 0.5 * x * (1.0 + th), th


def _gelu_grad(x, th):
    return 0.5 * (1.0 + th) + 0.5 * x * (1.0 - th * th) * (_GELU_K * (1.0 + 3.0 * 0.044715 * (x * x)))


def _colsum8(x):
    t, c = x.shape
    return jnp.sum(x.reshape(t // 8, 8, c), axis=0)


def _branches_fwd(p_ref, hxb_ref, hcg_ref, hxc_ref, first, tstart, w, sg_scr):
    t = p_ref.shape[0]
    nch = t // CHUNK

    def seg(o, width=WIDTH):
        return p_ref[:, o:o + width].astype(F32)

    lo = lax.broadcasted_iota(jnp.int32, (CHUNK, CHUNK), 1) < 64
    r = {}
    pu = seg(U)
    u_act, th_u = _gelu(pu)
    pv = seg(V)
    vg, th_v = _gelu(pv)
    mu = jnp.mean(vg, axis=-1, keepdims=True)
    xc = vg - mu
    var = jnp.mean(xc * xc, axis=-1, keepdims=True)
    rs = lax.rsqrt(var + LN_EPS)
    vhat = xc * rs
    vn = vhat * w["ln_g"][...] + w["ln_b"][...]
    vnb = vn.astype(BF16)
    for n in range(nch):
        for j in range(4):
            vb = vnb[n * CHUNK:(n + 1) * CHUNK, j * 128:(j + 1) * 128]
            z = _dot(w["w2"][j], vb)
            sg_scr[n * CHUNK:(n + 1) * CHUNK, j * 128:(j + 1) * 128] = (
                jnp.where(lo, z[:CHUNK], z[CHUNK:]) + w["bst"][:, j * 128:(j + 1) * 128])
    sg = sg_scr[...]
    a_br = u_act * sg
    za = seg(ZA)
    sa = _sigmoid(za)
    r.update(pu=pu, th_u=th_u, pv=pv, th_v=th_v, rs=rs, vhat=vhat, vnb=vnb, u_act=u_act, sg=sg,
             a_br=a_br, za=za, sa=sa, a_in=a_br * (za * sa))

    xb = seg(XB)
    cg = seg(CG)
    yb0 = cg * xb
    hal = hcg_ref[...].astype(F32) * hxb_ref[...].astype(F32)
    hal = jnp.where(first, 0.0, hal)
    ext = jnp.concatenate([hal, yb0], axis=0)
    y1 = pltpu.roll(ext, 1, 0)[HALO:]
    y2 = pltpu.roll(ext, 2, 0)[HALO:]
    cw = w["cw"]
    conv = cw[0:1, :] * y2 + cw[1:2, :] * y1 + cw[2:3, :] * yb0 + w["cb"][...]
    bg = seg(BG)
    b_br = bg * conv
    zb = seg(ZB)
    sb = _sigmoid(zb)
    r.update(xb=xb, cg=cg, yb0=yb0, y1=y1, y2=y2, conv=conv, bg=bg, b_br=b_br, zb=zb, sb=sb,
             b_in=b_br * (zb * sb))

    xcv = seg(XC)
    hxc = jnp.where(first, 0.0, hxc_ref[...].astype(F32))
    extc = jnp.concatenate([hxc, xcv], axis=0)
    tpos = tstart + lax.broadcasted_iota(jnp.int32, (t, 1), 0) + 1
    pooled, inv, q = [], [], []
    for gi, win in enumerate(POOL_WINDOWS):
        s = extc[:, gi * 128:(gi + 1) * 128]
        sh = 1
        while sh < win:
            s = s + pltpu.roll(s, sh, 0)
            sh *= 2
        inv_g = jnp.where(tpos >= win, 1.0 / win, 1.0 / jnp.minimum(tpos, win).astype(F32))
        pg = s[HALO:] * inv_g - xcv[:, gi * 128:(gi + 1) * 128]
        pooled.append(pg)
        inv.append(inv_g)
        q.append(_dot(pg.astype(BF16), w["wpool"][gi]))
    qv = jnp.concatenate(q, axis=1)
    c_br = qv * w["ps"][...]
    zc = seg(ZC)
    sc = _sigmoid(zc)
    r.update(pooled=pooled, inv=inv, q=qv, c_br=c_br, zc=zc, sc=sc, c_in=c_br * (zc * sc))
    return r


def _halo_specs(t, rev_n=None):
    def imap(col):
        def f(i):
            ti = i if rev_n is None else rev_n - 1 - i
            return (jnp.maximum(ti * (t // HALO) - 1, 0), col)
        return f
    return [pl.BlockSpec((HALO, WIDTH), imap(XB // WIDTH)),
            pl.BlockSpec((HALO, WIDTH), imap(CG // WIDTH)),
            pl.BlockSpec((HALO, WIDTH), imap(XC // WIDTH))]


def _const_spec(shape):
    nd = len(shape)
    return pl.BlockSpec(shape, lambda *_: (0,) * nd)


def _place():
    return lax.axis_index("x"), lax.axis_index("y"), lax.axis_index("c")


def _chip_peer(x, y, jm):
    px = (1 - x) if (jm & 2) else x
    py = (1 - y) if (jm & 1) else y
    return px, py


def _chip_part(buf, k, r0=0, nr=None):
    if len(buf.shape) == 3:
        return buf.at[k] if nr is None else buf.at[k, pl.ds(r0, nr)]
    cols = buf.shape[1] // N_CHIPS
    rows = pl.ds(0, buf.shape[0]) if nr is None else pl.ds(r0, nr)
    return buf.at[rows, pl.ds(pl.multiple_of(k * cols, 128), cols)]


def _row_range(ref, r0, nr):
    return ref if nr is None else ref.at[pl.ds(r0, nr)]


class _Comm:
    def __init__(self):
        self.ins, self.thru, self.fresh, self.moves, self.n = [], [], [], [], 0

    def read(self, arr):
        self.ins.append(arr)
        return ("ins", len(self.ins) - 1)

    def through(self, arr):
        self.thru.append(arr)
        return ("thru", len(self.thru) - 1)

    def new(self, shape, dtype):
        self.fresh.append(jax.ShapeDtypeStruct(tuple(shape), dtype))
        return ("fresh", len(self.fresh) - 1)

    def _add(self, n, *move):
        self.moves.append(move)
        self.n += n

    def gather(self, buf, r0=0, nr=None):
        self._add(3, "gather", buf, r0, nr)

    def scatter(self, src, dst, r0=0, nr=None):
        self._add(3, "scatter", src, dst, r0, nr)

    def sibling(self, src, dst, r0=0, nr=None):
        self._add(1, "sibling", src, dst, r0, nr)

    def slices_out(self, src, dst, rows):
        self._add(7, "slices_out", src, dst, rows)

    def slices_back(self, buf, rows):
        self._add(7, "slices_back", buf, rows)

    def copies(self, bufs, x, y, c):
        ref = lambda h: bufs[h[0]][h[1]]
        k_me = 2 * x + y
        me = 4 * x + 2 * y + c
        cps = []
        for move in self.moves:
            kind = move[0]
            if kind in ("gather", "scatter"):
                for jj, jm in enumerate((1, 2, 3)):
                    px, py = _chip_peer(x, y, jm)
                    k_peer = 2 * px + py
                    if kind == "gather":
                        _, buf, r0, nr = move
                        mine = _chip_part(ref(buf), k_me, r0, nr)
                        cps.append((mine, mine, _chip_part(ref(buf), k_peer, r0, nr), (px, py, c)))
                    else:
                        _, src, dst, r0, nr = move
                        slot = ref(dst).at[jj] if nr is None else ref(dst).at[jj, pl.ds(r0, nr)]
                        cps.append((_chip_part(ref(src), k_peer, r0, nr), slot, slot, (px, py, c)))
            elif kind == "sibling":
                _, src, dst, r0, nr = move
                land = _row_range(ref(dst), r0, nr)
                cps.append((_row_range(ref(src), r0, nr), land, land, (x, y, 1 - c)))
            else:
                for j in range(1, 8):
                    px = (1 - x) if (j & 4) else x
                    py = (1 - y) if (j & 2) else y
                    pc = (1 - c) if (j & 1) else c
                    peer = 4 * px + 2 * py + pc
                    if kind == "slices_out":
                        _, src, dst, rows = move
                        slot = ref(dst).at[j - 1]
                        cps.append((ref(src).at[pl.ds(pl.multiple_of(peer * rows, 8), rows)], slot, slot,
                                    (px, py, pc)))
                    else:
                        _, buf, rows = move
                        mine = ref(buf).at[pl.ds(pl.multiple_of(me * rows, 8), rows)]
                        cps.append((mine, mine, ref(buf).at[pl.ds(pl.multiple_of(peer * rows, 8), rows)],
                                    (px, py, pc)))
        assert len(cps) == self.n
        return cps


def _pcall(body, *, name, grid, in_specs, out_specs, out_shape, scratch, sem, args, comm=None):
    if comm is None or comm.n == 0:
        outs = pl.pallas_call(body, name=name, grid=grid, in_specs=in_specs, out_specs=out_specs,
                              out_shape=out_shape, scratch_shapes=list(scratch), compiler_params=_params(sem))(*args)
        return list(outs), [], []
    n_in, n_out, n_scr = len(in_specs), len(out_specs), len(scratch)
    n_ci, n_ct, n_cf, n_cp = len(comm.ins), len(comm.thru), len(comm.fresh), comm.n

    def wrapped(*refs):
        pos = 0
        def take(n):
            nonlocal pos
            got = refs[pos:pos + n]
            pos += n
            return got
        a, ci, _ = take(n_in), take(n_ci), take(n_ct)
        o, ct, cf = take(n_out), take(n_ct), take(n_cf)
        scr = take(n_scr)
        send_sems, recv_sems = take(2)
        first = functools.reduce(jnp.logical_and, [pl.program_id(d) == 0 for d in range(len(grid))])
        last = functools.reduce(jnp.logical_and, [pl.program_id(d) == grid[d] - 1 for d in range(len(grid))])
        x, y, c = _place()
        cps = comm.copies(dict(ins=ci, thru=ct, fresh=cf), x, y, c)

        def copy(i, src, dst, dev):
            return pltpu.make_async_remote_copy(src_ref=src, dst_ref=dst, send_sem=send_sems.at[i],
                                                recv_sem=recv_sems.at[i], device_id=dev, device_id_type=MESH)

        @pl.when(first)
        def _():
            for i, (src, dst, _, dev) in enumerate(cps):
                copy(i, src, dst, dev).start()

        body(*a, *o, *scr)

        @pl.when(last)
        def _():
            for i, (src, _, land, dev) in enumerate(cps):
                copy(i, src, land, dev).wait_recv()
            for i, (src, dst, _, dev) in enumerate(cps):
                copy(i, src, dst, dev).wait_send()

    thru_shapes = [jax.ShapeDtypeStruct(t.shape, t.dtype) for t in comm.thru]
    outs = pl.pallas_call(
        wrapped, name=name, grid=grid,
        in_specs=list(in_specs) + [ANY] * (n_ci + n_ct),
        out_specs=list(out_specs) + [ANY] * (n_ct + n_cf),
        out_shape=list(out_shape) + thru_shapes + comm.fresh,
        input_output_aliases={n_in + n_ci + t: n_out + t for t in range(n_ct)},
        scratch_shapes=list(scratch) + [pltpu.SemaphoreType.DMA((n_cp,)), pltpu.SemaphoreType.DMA((n_cp,))],
        compiler_params=_params(sem),
    )(*args, *comm.ins, *comm.thru)
    outs = list(outs)
    return outs[:n_out], outs[n_out:n_out + n_ct], outs[n_out + n_ct:]


def _comm_only(name, comm):
    def body():
        pass
    _, thru, _ = _pcall(body, name=name, grid=(1,), in_specs=[], out_specs=[], out_shape=[], scratch=[],
                        sem=("arbitrary",), args=[], comm=comm)
    return thru


def _gather_first(name, w_all, pieces=4):
    hr = w_all.shape[0] // 2
    pr = hr // pieces
    n = 3 * pieces

    def body(_, w_ref, send1, recv1, send2, recv2):
        x, y, c = _place()
        k_me = 2 * x + y

        def part(k, core, q):
            return _chip_part(w_ref, k, pl.multiple_of(core * hr + q * pr, 16), pr)

        def copy(src, dst, send, recv, i, dev):
            return pltpu.make_async_remote_copy(src_ref=src, dst_ref=dst, send_sem=send.at[i], recv_sem=recv.at[i],
                                                device_id=dev, device_id_type=MESH)

        over_ici, landing, onward, from_sibling = [], [], [], []
        for jj, jm in enumerate((1, 2, 3)):
            px, py = _chip_peer(x, y, jm)
            k_peer = 2 * px + py
            for q in range(pieces):
                i = jj * pieces + q
                over_ici.append(copy(part(k_me, c, q), part(k_me, c, q), send1, recv1, i, (px, py, c)))
                landing.append(copy(part(k_me, c, q), part(k_peer, c, q), send1, recv1, i, (px, py, c)))
                onward.append(copy(part(k_peer, c, q), part(k_peer, c, q), send2, recv2, i, (x, y, 1 - c)))
                from_sibling.append(copy(part(k_peer, c, q), part(k_peer, 1 - c, q), send2, recv2, i, (x, y, 1 - c)))
        for cp in over_ici:
            cp.start()
        for i in range(n):
            landing[i].wait_recv()
            onward[i].start()
        for i in range(n):
            from_sibling[i].wait_recv()
        for i in range(n):
            over_ici[i].wait_send()
            onward[i].wait_send()

    return pl.pallas_call(
        body, name=name, in_specs=[ANY], out_specs=ANY, out_shape=jax.ShapeDtypeStruct(w_all.shape, w_all.dtype),
        input_output_aliases={0: 0}, scratch_shapes=[pltpu.SemaphoreType.DMA((n,))] * 4,
    )(w_all)


def _f1(x, norm_g3, w_all, l, comm=None, tm=256):
    s = x.shape[0]

    def body(x_ref, g_ref, w_hbm, p_ref, h_ref, w_vmem):
        @pl.when(pl.program_id(0) == 0)
        def _():
            pltpu.sync_copy(w_hbm, w_vmem)
        xv = x_ref[...]
        r = lax.rsqrt(jnp.mean(xv * xv, axis=-1, keepdims=True) + RMS_EPS)
        hb = ((xv * r) * g_ref[...]).astype(BF16)
        h_ref[...] = hb
        p_ref[...] = _dot(hb, w_vmem[...]).astype(BF16)

    return _pcall(
        body, name=f"f1_l{l}", grid=(s // tm,),
        in_specs=[pl.BlockSpec((tm, D_MODEL), lambda i: (i, 0)),
                  pl.BlockSpec((None, 1, D_MODEL), lambda i: (l, 0, 0)), ANY],
        out_specs=[pl.BlockSpec((tm, IN_TOTAL), lambda i: (i, 0)),
                   pl.BlockSpec((tm, D_MODEL), lambda i: (i, 0))],
        out_shape=[jax.ShapeDtypeStruct((s, IN_TOTAL), BF16), jax.ShapeDtypeStruct((s, D_MODEL), BF16)],
        scratch=[pltpu.VMEM((D_MODEL, IN_TOTAL), BF16)], sem=("arbitrary",), args=[x, norm_g3, w_all], comm=comm)


def _f2(x, p, lw, l, comm=None, t=256):
    s = x.shape[0]
    n = s // t

    def body(p_ref, hxb_ref, hcg_ref, hxc_ref, x_ref, lng, lnb, w2, bst, cw, cb, wpool, ps,
             wpa, wpb, wpc, wo, ya_ref, yb_ref, yc_ref, m_ref, xo_ref, sg_scr):
        i = pl.program_id(0)
        w = dict(ln_g=lng, ln_b=lnb, w2=w2, bst=bst, cw=cw, cb=cb, wpool=wpool, ps=ps)
        r = _branches_fwd(p_ref, hxb_ref, hcg_ref, hxc_ref, i == 0, i * t, w, sg_scr)
        ya = _dot(r["a_in"].astype(BF16), wpa[...])
        yb = _dot(r["b_in"].astype(BF16), wpb[...])
        yc = _dot(r["c_in"].astype(BF16), wpc[...])
        ya_ref[...] = ya.astype(BF16)
        yb_ref[...] = yb.astype(BF16)
        yc_ref[...] = yc.astype(BF16)
        m = (_sigmoid(p_ref[:, GA:GA + D_MODEL].astype(F32)) * ya
             + _sigmoid(p_ref[:, GB:GB + D_MODEL].astype(F32)) * yb
             + _sigmoid(p_ref[:, GC:GC + D_MODEL].astype(F32)) * yc)
        mb = m.astype(BF16)
        m_ref[...] = mb
        xo_ref[...] = x_ref[...] + _dot(mb, wo[...])

    tile = lambda c: pl.BlockSpec((t, c), lambda i: (i, 0))
    lsel = lambda *blk: pl.BlockSpec((None,) + blk, lambda i: (l,) + (0,) * len(blk))
    proj = lambda b: pl.BlockSpec((None, WIDTH, D_MODEL), lambda i: (b, 0, 0))
    act = jax.ShapeDtypeStruct((s, D_MODEL), BF16)
    return _pcall(
        body, name=f"f2_l{l}", grid=(n,),
        in_specs=[tile(IN_TOTAL)] + _halo_specs(t) + [
            tile(D_MODEL), lsel(1, WIDTH), lsel(1, WIDTH), lsel(4, 256, 128), lsel(CHUNK, WIDTH),
            lsel(8, WIDTH), lsel(1, WIDTH), lsel(4, 128, 128), lsel(1, WIDTH),
            proj(0), proj(1), proj(2), _const_spec((D_MODEL, D_MODEL))],
        out_specs=[tile(D_MODEL)] * 5,
        out_shape=[act, act, act, act, jax.ShapeDtypeStruct((s, D_MODEL), F32)],
        scratch=[pltpu.VMEM((t, WIDTH), F32)], sem=("arbitrary",),
        args=[p, p, p, p, x, lw["ln_g"], lw["ln_b"], lw["w2"], lw["bst"], lw["cw"], lw["cb"], lw["wpool"], lw["ps"],
              lw["proj"][l], lw["proj"][l], lw["proj"][l], lw["wo"][l]], comm=comm)


def _b1(p, dout, ya, yb, yc, lw, l, comm=None, t=256):
    s = p.shape[0]
    n = s // t
    nch = t // CHUNK

    def body(p_ref, hxb_ref, hcg_ref, hxc_ref, do_ref, ya_ref, yb_ref, yc_ref,
             lng, lnb, w2, wt2, bst, cw, cb, wpool, ps, proj_h, wo_h, sel_ref,
             dp_ref, gwp_h, gws_ref, gbs_ref, glng_ref, glnb_ref, gcw_ref, gcb_ref,
             gwpool_ref, gps_ref,
             wpa, wpb, wpc, wo, gwpa, gwpb, gwpc, gbs_acc, vec_acc, sg_scr, dvn_scr, car_dc, car_e):
        i = pl.program_id(0)
        ti = n - 1 - i

        @pl.when(i == 0)
        def _():
            pltpu.sync_copy(proj_h.at[0], wpa)
            pltpu.sync_copy(proj_h.at[1], wpb)
            pltpu.sync_copy(proj_h.at[2], wpc)
            pltpu.sync_copy(wo_h, wo)
            for acc in (gwpa, gwpb, gwpc, gbs_acc, vec_acc, car_dc, car_e):
                acc[...] = jnp.zeros(acc.shape, acc.dtype)
            gws_ref[...] = jnp.zeros(gws_ref.shape, F32)
            gwpool_ref[...] = jnp.zeros(gwpool_ref.shape, F32)

        w = dict(ln_g=lng, ln_b=lnb, w2=w2, bst=bst, cw=cw, cb=cb, wpool=wpool, ps=ps)
        r = _branches_fwd(p_ref, hxb_ref, hcg_ref, hxc_ref, ti == 0, ti * t, w, sg_scr)

        def seg(o, width=WIDTH):
            return p_ref[:, o:o + width].astype(F32)

        def put(o, val):
            dp_ref[:, o:o + val.shape[1]] = val.astype(BF16)

        dob = do_ref[...].astype(BF16)
        dm = _dotg(dob, wo[...], NT)

        def merge_bwd(goff, y_ref, xin, wp, gwp):
            sx = _sigmoid(seg(goff, D_MODEL))
            dmy = dm * sx
            put(goff, dmy * y_ref[...].astype(F32) * (1.0 - sx))
            dyb = dmy.astype(BF16)
            gwp[...] += _dotg(xin.astype(BF16), dyb, TN)
            return _dotg(dyb, wp[...], NT)

        d_ain = merge_bwd(GA, ya_ref, r["a_in"], wpa, gwpa)
        d_bin = merge_bwd(GB, yb_ref, r["b_in"], wpb, gwpb)
        d_cin = merge_bwd(GC, yc_ref, r["c_in"], wpc, gwpc)

        def dsilu(z, sz):
            return sz * (1.0 + z * (1.0 - sz))

        za, sa = r["za"], r["sa"]
        d_abr = d_ain * (za * sa)
        put(ZA, d_ain * r["a_br"] * dsilu(za, sa))
        put(U, d_abr * r["sg"] * _gelu_grad(r["pu"], r["th_u"]))
        d_sg = d_abr * r["u_act"]
        dsgb = d_sg.astype(BF16)
        lo = lax.broadcasted_iota(jnp.int32, (CHUNK, CHUNK), 1) < 64
        zero = jnp.zeros((CHUNK, CHUNK), BF16)
        for c in range(nch):
            rows = slice(c * CHUNK, (c + 1) * CHUNK)
            gbs_acc[...] += d_sg[rows]
            for j in range(4):
                cols = slice(j * 128, (j + 1) * 128)
                dj = dsgb[rows, cols]
                zt = _dot(wt2[j], dj)
                dvn_scr[rows, cols] = jnp.where(lo, zt[:CHUNK], zt[CHUNK:])
                stacked = jnp.concatenate([jnp.where(lo, dj, zero), jnp.where(lo, zero, dj)], axis=0)
                gws_ref[j] += _dotg(stacked, r["vnb"][rows, cols], NT)
        d_vn = dvn_scr[...]
        vhat = r["vhat"]
        vec_acc[0] += _colsum8(d_vn * vhat)
        vec_acc[1] += _colsum8(d_vn)
        d_vhat = d_vn * lng[...]
        d_vg = r["rs"] * (d_vhat - jnp.mean(d_vhat, axis=-1, keepdims=True)
                          - vhat * jnp.mean(d_vhat * vhat, axis=-1, keepdims=True))
        put(V, d_vg * _gelu_grad(r["pv"], r["th_v"]))

        zb, sb = r["zb"], r["sb"]
        d_bbr = d_bin * (zb * sb)
        put(ZB, d_bin * r["b_br"] * dsilu(zb, sb))
        put(BG, d_bbr * r["conv"])
        dc = d_bbr * r["bg"]
        vec_acc[2] += _colsum8(dc)
        vec_acc[3] += _colsum8(dc * r["y2"])
        vec_acc[4] += _colsum8(dc * r["y1"])
        vec_acc[5] += _colsum8(dc * r["yb0"])
        ext = jnp.concatenate([dc, car_dc[...]], axis=0)
        ne = t + HALO
        d1 = pltpu.roll(ext, ne - 1, 0)[:t]
        d2 = pltpu.roll(ext, ne - 2, 0)[:t]
        d_yb0 = cw[2:3, :] * dc + cw[1:2, :] * d1 + cw[0:1, :] * d2
        put(CG, d_yb0 * r["xb"])
        put(XB, d_yb0 * r["cg"])
        car_dc[...] = dc[:HALO]

        zc, sc = r["zc"], r["sc"]
        d_cbr = d_cin * (zc * sc)
        put(ZC, d_cin * r["c_br"] * dsilu(zc, sc))
        vec_acc[6] += _colsum8(d_cbr * r["q"])
        d_q = d_cbr * ps[...]
        for gi, win in enumerate(POOL_WINDOWS):
            cols = slice(gi * 128, (gi + 1) * 128)
            dqb = d_q[:, cols].astype(BF16)
            d_pool = _dotg(dqb, wpool[gi], NT)
            gwpool_ref[gi] += _dotg(r["pooled"][gi].astype(BF16), dqb, TN)
            e = d_pool * r["inv"][gi]
            sx = jnp.concatenate([e, car_e[:, cols]], axis=0)
            sh = 1
            while sh < win:
                sx = sx + pltpu.roll(sx, ne - sh, 0)
                sh *= 2
            put(XC + gi * 128, sx[:t] - d_pool)
            car_e[:, cols] = e[:HALO]

        @pl.when(i == n - 1)
        def _():
            for b, (acc, stage) in enumerate(((gwpa, wpa), (gwpb, wpb), (gwpc, wpc))):
                stage[...] = acc[...].astype(BF16)
                pltpu.sync_copy(stage, gwp_h.at[b])
            gbs_ref[...] = jnp.dot(gbs_acc[...], sel_ref[...], preferred_element_type=F32,
                                   precision=lax.Precision.HIGHEST)
            red = lambda k: jnp.sum(vec_acc[k], axis=0, keepdims=True)
            glng_ref[...] = red(0)
            glnb_ref[...] = red(1)
            gcb_ref[...] = red(2)
            gcw_ref[...] = jnp.zeros(gcw_ref.shape, F32)
            for k in range(3):
                gcw_ref[k:k + 1, :] = red(3 + k)
            gps_ref[...] = red(6)
            tt = lax.broadcasted_iota(jnp.int32, (2 * CHUNK, CHUNK), 0) % CHUNK
            ss = lax.broadcasted_iota(jnp.int32, (2 * CHUNK, CHUNK), 1)
            for j in range(4):
                gws_ref[j] = jnp.where(tt >= ss, gws_ref[j], 0.0)

    rtile = lambda c: pl.BlockSpec((t, c), lambda i: (n - 1 - i, 0))
    lsel = lambda *blk: pl.BlockSpec((None,) + blk, lambda i: (l,) + (0,) * len(blk))
    f32s = lambda *shape: jax.ShapeDtypeStruct(shape, F32)
    return _pcall(
        body, name=f"b1_l{l}", grid=(n,),
        in_specs=[rtile(IN_TOTAL)] + _halo_specs(t, rev_n=n) + [rtile(D_MODEL)] * 4 + [
            lsel(1, WIDTH), lsel(1, WIDTH), lsel(4, 256, 128), lsel(4, 256, 128), lsel(CHUNK, WIDTH),
            lsel(8, WIDTH), lsel(1, WIDTH), lsel(4, 128, 128), lsel(1, WIDTH),
            ANY, ANY, _const_spec((WIDTH, 128))],
        out_specs=[rtile(IN_TOTAL), ANY,
                   _const_spec((4, 256, 128)), _const_spec((CHUNK, 128)), _const_spec((1, WIDTH)),
                   _const_spec((1, WIDTH)), _const_spec((8, WIDTH)), _const_spec((1, WIDTH)),
                   _const_spec((4, 128, 128)), _const_spec((1, WIDTH))],
        out_shape=[jax.ShapeDtypeStruct((s, IN_TOTAL), BF16), jax.ShapeDtypeStruct((3, WIDTH, D_MODEL), BF16),
                   f32s(4, 256, 128), f32s(CHUNK, 128),
                   f32s(1, WIDTH), f32s(1, WIDTH), f32s(8, WIDTH), f32s(1, WIDTH), f32s(4, 128, 128),
                   f32s(1, WIDTH)],
        scratch=[pltpu.VMEM((WIDTH, D_MODEL), BF16), pltpu.VMEM((WIDTH, D_MODEL), BF16),
                 pltpu.VMEM((WIDTH, D_MODEL), BF16), pltpu.VMEM((D_MODEL, D_MODEL), BF16),
                 pltpu.VMEM((WIDTH, D_MODEL), F32),
                 pltpu.VMEM((WIDTH, D_MODEL), F32), pltpu.VMEM((WIDTH, D_MODEL), F32),
                 pltpu.VMEM((CHUNK, WIDTH), F32), pltpu.VMEM((8, 8, WIDTH), F32),
                 pltpu.VMEM((t, WIDTH), F32), pltpu.VMEM((t, WIDTH), F32),
                 pltpu.VMEM((HALO, WIDTH), F32), pltpu.VMEM((HALO, WIDTH), F32)],
        sem=("arbitrary",),
        args=[p, p, p, p, dout, ya, yb, yc, lw["ln_g"], lw["ln_b"], lw["w2"], lw["wt2"], lw["bst"], lw["cw"],
              lw["cb"], lw["wpool"], lw["ps"], lw["proj"][l], lw["wo"][l], lw["sel"]], comm=comm)


def _rms_bwd(xv, g, dh):
    r = lax.rsqrt(jnp.mean(xv * xv, axis=-1, keepdims=True) + RMS_EPS)
    xhat = xv * r
    dxh = dh * g
    dx = r * (dxh - xhat * jnp.mean(dxh * xhat, axis=-1, keepdims=True))
    return dx, dh * xhat


def _b2a(dp, w_all, x, dout, norm_g3, l, comm=None, tm=256):
    s = x.shape[0]
    nm = s // tm

    def body(dp_ref, w_hbm, x_ref, do_ref, g_ref, dx_ref, gg_ref, w_vmem, gacc):
        i = pl.program_id(0)

        @pl.when(i == 0)
        def _():
            pltpu.sync_copy(w_hbm, w_vmem)
            gacc[...] = jnp.zeros(gacc.shape, F32)

        dh = _dotg(dp_ref[...], w_vmem[...], NT)
        dx, gx = _rms_bwd(x_ref[...], g_ref[...], dh)
        dx_ref[...] = do_ref[...] + dx
        gacc[...] += _colsum8(gx)

        @pl.when(i == nm - 1)
        def _():
            gg_ref[...] = jnp.sum(gacc[...], axis=0, keepdims=True)

    return _pcall(
        body, name=f"b2a_l{l}", grid=(nm,),
        in_specs=[pl.BlockSpec((tm, IN_TOTAL), lambda i: (i, 0)), ANY,
                  pl.BlockSpec((tm, D_MODEL), lambda i: (i, 0)),
                  pl.BlockSpec((tm, D_MODEL), lambda i: (i, 0)),
                  pl.BlockSpec((None, 1, D_MODEL), lambda i: (l, 0, 0))],
        out_specs=[pl.BlockSpec((tm, D_MODEL), lambda i: (i, 0)),
                   pl.BlockSpec((1, D_MODEL), lambda i: (0, 0))],
        out_shape=[jax.ShapeDtypeStruct((s, D_MODEL), F32), jax.ShapeDtypeStruct((1, D_MODEL), F32)],
        scratch=[pltpu.VMEM((D_MODEL, IN_TOTAL), BF16), pltpu.VMEM((8, D_MODEL), F32)],
        sem=("arbitrary",), args=[dp, w_all, x, dout, norm_g3], comm=comm)


def _b2b(h, dp, l, comm=None, tk=1024):
    s = h.shape[0]
    nk = s // tk

    def body(h_ref, dp_ref, g_ref, acc):
        kk = pl.program_id(1)

        @pl.when(kk == 0)
        def _():
            acc[...] = jnp.zeros(acc.shape, F32)

        acc[...] += _dotg(h_ref[...], dp_ref[...], TN)

        @pl.when(kk == nk - 1)
        def _():
            g_ref[...] = acc[...].astype(BF16)

    return _pcall(
        body, name=f"b2b_l{l}", grid=(N_CHIPS, nk),
        in_specs=[pl.BlockSpec((tk, D_MODEL), lambda k, kk: (kk, 0)),
                  pl.BlockSpec((tk, SHARD_W), lambda k, kk: (kk, k))],
        out_specs=[pl.BlockSpec((D_MODEL, SHARD_W), lambda k, kk: (0, k))],
        out_shape=[jax.ShapeDtypeStruct((D_MODEL, IN_TOTAL), BF16)],
        scratch=[pltpu.VMEM((D_MODEL, SHARD_W), F32)],
        sem=("arbitrary", "arbitrary"), args=[h, dp], comm=comm)


def _gwo(m, dout, l, tk=512):
    s = m.shape[0]
    nk = s // tk

    def body(m_ref, do_ref, g_ref, acc):
        kk = pl.program_id(0)
        prod = _dotg(m_ref[...], do_ref[...].astype(BF16), TN)

        @pl.when(kk == 0)
        def _():
            acc[...] = prod

        @pl.when(kk > 0)
        def _():
            acc[...] += prod

        @pl.when(kk == nk - 1)
        def _():
            g_ref[...] = acc[...].astype(BF16)

    return pl.pallas_call(
        body, name=f"gwo_l{l}", grid=(nk,),
        in_specs=[pl.BlockSpec((tk, D_MODEL), lambda kk: (kk, 0)), pl.BlockSpec((tk, D_MODEL), lambda kk: (kk, 0))],
        out_specs=_const_spec((D_MODEL, D_MODEL)),
        out_shape=jax.ShapeDtypeStruct((D_MODEL, D_MODEL), BF16),
        scratch_shapes=[pltpu.VMEM((D_MODEL, D_MODEL), F32)],
        compiler_params=_params(("arbitrary",)),
    )(m, dout)


def _loss_head(x, tgt, final_g2, tm=512):
    s = x.shape[0]
    nm = s // tm

    def body(x_ref, t_ref, g_ref, dx_ref, loss_ref, gg_ref, lacc, gacc):
        i = pl.program_id(0)

        @pl.when(i == 0)
        def _():
            lacc[...] = jnp.zeros(lacc.shape, F32)
            gacc[...] = jnp.zeros(gacc.shape, F32)

        xv = x_ref[...]
        g = g_ref[...]
        r = lax.rsqrt(jnp.mean(xv * xv, axis=-1, keepdims=True) + RMS_EPS)
        err = (xv * r) * g - t_ref[...]
        lacc[...] += _colsum8(err * err)
        dx, gx = _rms_bwd(xv, g, err * (1.0 / D_MODEL))
        dx_ref[...] = dx
        gacc[...] += _colsum8(gx)

        @pl.when(i == nm - 1)
        def _():
            tot = jnp.sum(jnp.sum(lacc[...], axis=0, keepdims=True), axis=1, keepdims=True)
            loss_ref[...] = jnp.broadcast_to(tot * (0.5 / D_MODEL), loss_ref.shape)
            gg_ref[...] = jnp.sum(gacc[...], axis=0, keepdims=True)

    return pl.pallas_call(
        body, name="loss_head", grid=(nm,),
        in_specs=[pl.BlockSpec((tm, D_MODEL), lambda i: (i, 0)), pl.BlockSpec((tm, D_MODEL), lambda i: (i, 0)),
                  _const_spec((1, D_MODEL))],
        out_specs=[pl.BlockSpec((tm, D_MODEL), lambda i: (i, 0)), _const_spec((8, 128)),
                   _const_spec((1, D_MODEL))],
        out_shape=[jax.ShapeDtypeStruct((s, D_MODEL), F32), jax.ShapeDtypeStruct((8, 128), F32),
                   jax.ShapeDtypeStruct((1, D_MODEL), F32)],
        scratch_shapes=[pltpu.VMEM((8, D_MODEL), F32), pltpu.VMEM((8, D_MODEL), F32)],
        compiler_params=_params(("arbitrary",)),
    )(x, tgt, final_g2)


def _row_block(rows, cols, n_arrays):
    budget = VMEM_LIMIT // 3 // (2 * 4 * n_arrays * cols)
    rb = rows
    while rb > budget and rb % 16 == 0:
        rb //= 2
    return rb


def _cast_slot(name, a, l, chip):
    _, rows, cols = a.shape
    rb = _row_block(rows, cols, 2)

    def body(idx_ref, a_ref, o_ref):
        o_ref[...] = a_ref[...].astype(BF16)

    gs = pltpu.PrefetchScalarGridSpec(
        num_scalar_prefetch=1, grid=(rows // rb,),
        in_specs=[pl.BlockSpec((None, rb, cols), lambda i, idx: (l, i, 0))],
        out_specs=pl.BlockSpec((None, rb, cols), lambda i, idx: (idx[0], i, 0)))
    return pl.pallas_call(body, name=name, grid_spec=gs, out_shape=jax.ShapeDtypeStruct((N_CHIPS, rows, cols), BF16),
                          compiler_params=_params(("arbitrary",)))(chip, a)


def _cast_cols(name, a, l, chip):
    _, rows, cols = a.shape
    rb = _row_block(rows, cols, 2)

    def body(idx_ref, a_ref, o_ref):
        o_ref[...] = a_ref[...].astype(BF16)

    gs = pltpu.PrefetchScalarGridSpec(
        num_scalar_prefetch=1, grid=(rows // rb,),
        in_specs=[pl.BlockSpec((None, rb, cols), lambda i, idx: (l, i, 0))],
        out_specs=pl.BlockSpec((rb, cols), lambda i, idx: (i, idx[0])))
    return pl.pallas_call(body, name=name, grid_spec=gs, out_shape=jax.ShapeDtypeStruct((rows, N_CHIPS * cols), BF16),
                          compiler_params=_params(("arbitrary",)))(chip, a)


def _sum4(name, own, chip, recv, l, n_layers, prev=None):
    _, rows, cols = recv.shape
    rb = _row_block(rows, cols, 5)
    nb = rows // rb
    if len(own.shape) == 3:
        own_spec = pl.BlockSpec((None, rb, cols), lambda i, idx: (idx[0], i, 0))
    else:
        own_spec = pl.BlockSpec((rb, cols), lambda i, idx: (i, idx[0]))

    def body(idx_ref, own_ref, r_ref, *rest):
        o_ref = rest[-1]
        o_ref[...] = ((own_ref[...].astype(F32) + r_ref[0].astype(F32)) + r_ref[1].astype(F32)) + r_ref[2].astype(F32)

    gs = pltpu.PrefetchScalarGridSpec(
        num_scalar_prefetch=1, grid=(nb,),
        in_specs=[own_spec,
                  pl.BlockSpec((3, rb, cols), lambda i, idx: (0, i, 0))] + ([ANY] if prev is not None else []),
        out_specs=pl.BlockSpec((rb, cols), lambda i, idx: (l * nb + i, 0)))
    args = (chip, own, recv) + ((prev,) if prev is not None else ())
    return pl.pallas_call(body, name=name, grid_spec=gs,
                          out_shape=jax.ShapeDtypeStruct((n_layers * rows, cols), F32),
                          input_output_aliases=({3: 0} if prev is not None else {}),
                          compiler_params=_params(("arbitrary",)))(*args)


def _sum8(name, pack, me, recv):
    rows = recv.shape[1]

    def body(idx_ref, own_ref, r_ref, o_ref):
        acc = own_ref[...]
        for j in range(7):
            acc = acc + r_ref[j]
        o_ref[...] = acc

    gs = pltpu.PrefetchScalarGridSpec(
        num_scalar_prefetch=1, grid=(1,),
        in_specs=[pl.BlockSpec((rows, 128), lambda i, idx: (idx[0], 0)),
                  pl.BlockSpec((7, rows, 128), lambda i, idx: (0, 0, 0))],
        out_specs=pl.BlockSpec((rows, 128), lambda i, idx: (idx[0], 0)))
    return pl.pallas_call(body, name=name, grid_spec=gs, out_shape=jax.ShapeDtypeStruct(pack.shape, F32),
                          compiler_params=_params(("arbitrary",)))(me, pack, recv)


def _adamw_math(w, g, m, v):
    m = ADAM_B1 * m + (1.0 - ADAM_B1) * g
    v = ADAM_B2 * v + (1.0 - ADAM_B2) * (g * g)
    m_hat = m / (1.0 - ADAM_B1 ** ADAM_STEP)
    v_hat = v / (1.0 - ADAM_B2 ** ADAM_STEP)
    delta = -ADAM_LR * (m_hat / (jnp.sqrt(v_hat) + ADAM_EPS) + ADAM_WD * w)
    return delta, m, v


def _adamw(name, w, m, v, g_parts):
    rows, cols = w.shape
    np_ = len(g_parts)
    rb = _row_block(rows, cols, 7 + np_)

    def body(*refs):
        w_ref, m_ref, v_ref = refs[:3]
        g_refs = refs[3:3 + np_]
        go_ref, d_ref, mo_ref, vo_ref = refs[3 + np_:]
        g = g_refs[0][...]
        for gr in g_refs[1:]:
            g = g + gr[...]
        d, mn, vn = _adamw_math(w_ref[...], g, m_ref[...], v_ref[...])
        go_ref[...] = g
        d_ref[...] = d
        mo_ref[...] = mn
        vo_ref[...] = vn

    spec = pl.BlockSpec((rb, cols), lambda i: (i, 0))
    shp = jax.ShapeDtypeStruct((rows, cols), F32)
    return pl.pallas_call(body, name=name, grid=(rows // rb,), in_specs=[spec] * (3 + np_),
                          out_specs=[spec] * 4, out_shape=[shp] * 4,
                          compiler_params=_params(("arbitrary",)))(w, m, v, *g_parts)


def _all_reduce_small(name, a):
    def body(a_ref, o_ref, buf, send_sems, recv_sems):
        x, y, c = _place()
        o_ref[...] = a_ref[...]
        for rnd, peer in enumerate(((x, y, 1 - c), (x, 1 - y, c), (1 - x, y, c))):
            cp = pltpu.make_async_remote_copy(
                src_ref=o_ref, dst_ref=buf.at[rnd], send_sem=send_sems.at[rnd], recv_sem=recv_sems.at[rnd],
                device_id=peer, device_id_type=MESH)
            cp.start()
            cp.wait_recv()
            cp.wait_send()
            o_ref[...] = o_ref[...] + buf[rnd]

    vm = pl.BlockSpec(memory_space=pltpu.VMEM)
    return pl.pallas_call(
        body, name=name, in_specs=[vm], out_specs=vm, out_shape=jax.ShapeDtypeStruct(a.shape, F32),
        scratch_shapes=[pltpu.VMEM((3,) + a.shape, F32), pltpu.SemaphoreType.DMA((3,)),
                        pltpu.SemaphoreType.DMA((3,))],
        compiler_params=_params(),
    )(a)


_SMALL = ("norm_g", "ln_g", "ln_b", "w_s", "b_s", "conv_b", "w_pool", "pool_scale", "final_g")


def _rows(a):
    return a.reshape(-1, 128)


def _pad_rows(a, mult=8):
    pad = (-a.shape[0]) % mult
    return jnp.pad(a, ((0, pad), (0, 0))) if pad else a


def kernel(x, norm_g, w_in, ln_g, ln_b, w_s, b_s, conv_w, conv_b, w_pool, pool_scale, w_pa, w_pb, w_pc, w_o, final_g, loss_target, m_norm_g, m_w_in, m_ln_g, m_ln_b, m_w_s, m_b_s, m_conv_w, m_conv_b, m_w_pool, m_pool_scale, m_w_pa, m_w_pb, m_w_pc, m_w_o, m_final_g, v_norm_g, v_w_in, v_ln_g, v_ln_b, v_w_s, v_b_s, v_conv_w, v_conv_b, v_w_pool, v_pool_scale, v_w_pa, v_w_pb, v_w_pc, v_w_o, v_final_g):
    W = dict(norm_g=norm_g, w_in=w_in, ln_g=ln_g, ln_b=ln_b, w_s=w_s, b_s=b_s, conv_w=conv_w, conv_b=conv_b,
             w_pool=w_pool, pool_scale=pool_scale, w_pa=w_pa, w_pb=w_pb, w_pc=w_pc, w_o=w_o, final_g=final_g)
    M = dict(norm_g=m_norm_g, w_in=m_w_in, ln_g=m_ln_g, ln_b=m_ln_b, w_s=m_w_s, b_s=m_b_s, conv_w=m_conv_w,
             conv_b=m_conv_b, w_pool=m_w_pool, pool_scale=m_pool_scale, w_pa=m_w_pa, w_pb=m_w_pb, w_pc=m_w_pc,
             w_o=m_w_o, final_g=m_final_g)
    Vv = dict(norm_g=v_norm_g, w_in=v_w_in, ln_g=v_ln_g, ln_b=v_ln_b, w_s=v_w_s, b_s=v_b_s, conv_w=v_conv_w,
              conv_b=v_conv_b, w_pool=v_w_pool, pool_scale=v_pool_scale, w_pa=v_w_pa, w_pb=v_w_pb, w_pc=v_w_pc,
              w_o=v_w_o, final_g=v_final_g)
    L = DEPTH
    s = x.shape[1]
    xs = x.reshape(s, D_MODEL)
    tgt = loss_target.reshape(s, D_MODEL)
    k_me = (2 * lax.axis_index("x") + lax.axis_index("y")).astype(jnp.int32)

    chip = k_me.reshape(1)
    assert L == 2
    half_rows = D_MODEL // 2

    land_win = [_cast_cols(f"cast_w_in{l}", w_in, l, chip) for l in range(L)]
    pcat = lambda d: jnp.concatenate([d[b][l] for l in range(L) for b in ("w_pa", "w_pb", "w_pc")], axis=0)
    pcat_w = pcat(W).reshape(L, 3 * WIDTH, 256)
    land_proj = [_cast_slot(f"cast_proj{l}", pcat_w, l, chip) for l in range(L)]
    land_wo = [_cast_slot(f"cast_w_o{l}", w_o, l, chip) for l in range(L)]
    cw_sh = jnp.pad(conv_w, ((0, 0), (0, 5), (0, 0))).reshape(1, L * 8, 128)
    land_cw = lax.dynamic_update_slice(jnp.zeros((N_CHIPS, L * 8, 128), F32), cw_sh, (k_me, 0, 0))
    w_all0 = _gather_first("gather_w_in0", land_win[0])

    causal = jnp.tril(jnp.ones((CHUNK, CHUNK), dtype=bool))
    w_m = jnp.where(causal, w_s, 0.0)
    lw = dict(
        ln_g=ln_g.reshape(L, 1, WIDTH), ln_b=ln_b.reshape(L, 1, WIDTH),
        w2=w_m.reshape(L, 4, 256, CHUNK).astype(BF16),
        wt2=jnp.swapaxes(w_m, -1, -2).reshape(L, 4, 256, CHUNK).astype(BF16),
        bst=jnp.repeat(jnp.swapaxes(b_s, -1, -2), 64, axis=-1),
        cb=conv_b.reshape(L, 1, WIDTH), wpool=w_pool.astype(BF16), ps=pool_scale.reshape(L, 1, WIDTH),
        sel=(jnp.arange(WIDTH)[:, None] // 64 == jnp.arange(128)[None, :]).astype(F32))
    norm_g3 = norm_g.reshape(L, 1, D_MODEL)

    by_cols = lambda g: g.transpose(1, 0, 2).reshape(3, WIDTH, D_MODEL)
    cm = _Comm()
    for buf in (land_proj[0], land_wo[0], land_cw):
        cm.gather(cm.through(buf))
    cm.gather(cm.through(land_win[1]), 0, half_rows)
    (p0, h0), (g_proj0, g_wo0, g_cw, w_half1), _ = _f1(xs, norm_g3, w_all0, 0, comm=cm)
    lw["proj"] = [by_cols(g_proj0), None]
    lw["wo"] = [g_wo0.reshape(D_MODEL, D_MODEL), None]
    lw["cw"] = g_cw.reshape(N_CHIPS, L, 8, 128).transpose(1, 2, 0, 3).reshape(L, 8, WIDTH)
    cm = _Comm()
    cm.gather(cm.through(w_half1), half_rows, half_rows)
    (ya0, yb0, yc0, mm0, x1), (w_all1,), _ = _f2(xs, p0, lw, 0, comm=cm)
    cm = _Comm()
    for buf in (land_proj[1], land_wo[1]):
        cm.gather(cm.through(buf))
    (p1, h1), (g_proj1, g_wo1), _ = _f1(x1, norm_g3, w_all1, 1, comm=cm)
    lw["proj"][1] = by_cols(g_proj1)
    lw["wo"][1] = g_wo1.reshape(D_MODEL, D_MODEL)
    (ya1, yb1, yc1, mm1, x2), _, _ = _f2(x1, p1, lw, 1)
    dxl, loss_blk, g_final = _loss_head(x2, tgt, final_g.reshape(1, D_MODEL))
    loss = lax.psum(loss_blk[0, 0], ("x", "y", "c"))

    def small_grads(b1_outs, gng):
        gws, gbs, glng, glnb, gcw, gcb, gwpool, gps = b1_outs
        return dict(w_s=gws.reshape(8, CHUNK, CHUNK), b_s=gbs[:, :8].T, ln_g=glng[0], ln_b=glnb[0], conv_w=gcw[:3],
                    conv_b=gcb[0], w_pool=gwpool, pool_scale=gps[0], norm_g=gng[0])

    by_chip = lambda gwp, gwo: (gwp.reshape(3 * WIDTH, N_CHIPS, 256).transpose(1, 0, 2),
                                gwo.reshape(N_CHIPS, 256, D_MODEL))
    recv_like = lambda a: ((3,) + a.shape[1:], a.dtype)
    part_rows = (D_MODEL, 3 * WIDTH, 256)
    gl, gwin, recv_win, part_proj, part_wo, recv_proj, recv_wo = ([None] * L for _ in range(7))

    def sums(l, half):
        return [_sum4(f"sum_w_in{l}", gwin[l], chip, recv_win[l], l, L, half[0]),
                _sum4(f"sum_proj{l}", part_proj[l], chip, recv_proj[l], l, L, half[1]),
                _sum4(f"sum_w_o{l}", part_wo[l], chip, recv_wo[l], l, L, half[2])]

    gwo1 = _gwo(mm1, dxl, 1)
    (dp1, gwp1, *sm1), _, _ = _b1(p1, dxl, ya1, yb1, yc1, lw, 1)
    (gwin[1],), _, _ = _b2b(h1, dp1, 1)
    cm = _Comm()
    cm.scatter(cm.read(gwin[1]), cm.new((3, D_MODEL, SHARD_W), BF16), 0, half_rows)
    (dxl, gng1), _, (recv_half,) = _b2a(dp1, w_all1, x1, dxl, norm_g3, 1, comm=cm)
    gl[1] = small_grads(sm1, gng1)
    part_proj[1], part_wo[1] = by_chip(gwp1, gwo1)
    gwo0 = _gwo(mm0, dxl, 0)
    cm = _Comm()
    cm.scatter(cm.read(gwin[1]), cm.through(recv_half), half_rows, half_rows)
    cm.scatter(cm.read(part_proj[1]), cm.new(*recv_like(part_proj[1])))
    cm.scatter(cm.read(part_wo[1]), cm.new(*recv_like(part_wo[1])))
    (dp0, gwp0, *sm0), (recv_win[1],), (recv_proj[1], recv_wo[1]) = _b1(p0, dxl, ya0, yb0, yc0, lw, 0, comm=cm)
    half = sums(1, [None, None, None])
    part_proj[0], part_wo[0] = by_chip(gwp0, gwo0)

    gl[0] = small_grads(sm0, jnp.zeros((1, D_MODEL), F32))
    st = lambda name: jnp.stack([gl[l][name] for l in range(L)])
    gsm = dict(norm_g=st("norm_g"), ln_g=st("ln_g"), ln_b=st("ln_b"), w_s=st("w_s"), b_s=st("b_s"),
               conv_b=st("conv_b"), w_pool=st("w_pool"), pool_scale=st("pool_scale"), final_g=g_final[0])
    sizes = [W[nm].size // 128 for nm in _SMALL]
    n_small = sum(sizes)
    gpack = jnp.concatenate([_rows(gsm[nm]) for nm in _SMALL] + [_rows(st("conv_w"))], axis=0)
    slice_rows = -(-gpack.shape[0] // 64) * 8
    gpack = jnp.pad(gpack, ((0, 8 * slice_rows - gpack.shape[0]), (0, 0)))

    cm = _Comm()
    cm.scatter(cm.read(part_proj[0]), cm.new(*recv_like(part_proj[0])))
    cm.scatter(cm.read(part_wo[0]), cm.new(*recv_like(part_wo[0])))
    for a in range(3):
        cm.sibling(cm.read(half[a]), cm.new(half[a].shape, F32), part_rows[a], part_rows[a])
    cm.slices_out(cm.read(gpack), cm.new((7, slice_rows, 128), F32), slice_rows)
    (gwin[0],), _, (recv_proj[0], recv_wo[0], *other, slices_in) = _b2b(h0, dp0, 0, comm=cm)
    me = (4 * lax.axis_index("x") + 2 * lax.axis_index("y") + lax.axis_index("c")).astype(jnp.int32).reshape(1)
    gsum = _sum8("sum_small", gpack, me, slices_in)
    cm = _Comm()
    cm.scatter(cm.read(gwin[0]), cm.new((3, D_MODEL, SHARD_W), BF16))
    cm.slices_back(cm.through(gsum), slice_rows)
    (dxl, gng0), (gred,), (recv_win[0],) = _b2a(dp0, w_all0, xs, dxl, norm_g3, 0, comm=cm)
    grad_x = dxl.reshape(1, s, D_MODEL)

    half = sums(0, half)
    cm = _Comm()
    for a in range(3):
        cm.sibling(cm.read(half[a]), cm.through(other[a]), 0, part_rows[a])
    other = _comm_only("swap_halves0", cm)
    gng0 = _all_reduce_small("all_reduce_norm_g0", gng0.reshape(8, 128))
    gred = lax.dynamic_update_slice(gred, gng0, (0, 0))

    outs = {}
    shard2d = dict(w_in=(L * D_MODEL, SHARD_W), w_o=(L * 256, D_MODEL))
    for a, name in ((0, "w_in"), (2, "w_o")):
        r2 = shard2d[name]
        res = _adamw(f"adamw_{name}", W[name].reshape(r2), M[name].reshape(r2), Vv[name].reshape(r2),
                     [half[a], other[a]])
        outs[name] = [o.reshape(W[name].shape) for o in res]
    res = _adamw("adamw_proj", pcat(W), pcat(M), pcat(Vv), [half[1], other[1]])
    for i, name in enumerate(("w_pa", "w_pb", "w_pc")):
        outs[name] = [o.reshape(L, 3, WIDTH, 256)[:, i] for o in res]

    g_cw_full = gred[n_small:n_small + L * 3 * 4].reshape(L, 3, N_CHIPS, 128)
    g_cw_mine = lax.dynamic_index_in_dim(g_cw_full, k_me, axis=2, keepdims=False)
    pack = lambda d: _pad_rows(jnp.concatenate([_rows(d[nm]) for nm in _SMALL] + [_rows(d["conv_w"])], axis=0))
    gp2 = _pad_rows(jnp.concatenate([gred[:n_small], _rows(g_cw_mine)], axis=0))
    res = _adamw("adamw_small", pack(W), pack(M), pack(Vv), [gp2])
    off = 0
    for nm, sz in zip(_SMALL, sizes):
        outs[nm] = [o[off:off + sz].reshape(W[nm].shape) for o in res]
        off += sz
    outs["conv_w"] = [o[off:off + L * 3].reshape(W["conv_w"].shape) for o in res]

    order = ("norm_g", "w_in", "ln_g", "ln_b", "w_s", "b_s", "conv_w", "conv_b", "w_pool", "pool_scale",
             "w_pa", "w_pb", "w_pc", "w_o", "final_g")
    return (loss, grad_x, *[outs[nm][0] for nm in order], *[outs[nm][1] for nm in order],
            *[outs[nm][2] for nm in order], *[outs[nm][3] for nm in order])
```

```python
import functools

import jax
import jax.numpy as jnp
from jax import lax
from jax.experimental import pallas as pl
from jax.experimental.pallas import tpu as pltpu

F32 = jnp.float32
BF16 = jnp.bfloat16

D_MODEL = 1024
DEPTH = 2
CHUNK = 128
WIDTH = 512
POOL_WINDOWS = (2, 4, 8, 16)
IN_TOTAL = 7680
N_CHIPS = 4
SHARD_W = IN_TOTAL // N_CHIPS
RMS_EPS = 1e-6
LN_EPS = 1e-5
HALO = 16

U, V, ZA, XB, BG, CG, ZB, XC, ZC, GA, GB, GC = (0, 512, 1024, 1536, 2048, 2560, 3072, 3584, 4096, 4608, 5632, 6656)

ADAM_LR = 0.001
ADAM_B1 = 0.9
ADAM_B2 = 0.999
ADAM_EPS = 1e-08
ADAM_WD = 0.01
ADAM_STEP = 10

VMEM_LIMIT = 56 * 1024 * 1024
MESH = pl.DeviceIdType.MESH
ANY = pl.BlockSpec(memory_space=pl.ANY)
NT = (((1,), (1,)), ((), ()))
TN = (((0,), (0,)), ((), ()))


def _params(sem=None):
    kw = dict(vmem_limit_bytes=VMEM_LIMIT)
    if sem is not None:
        kw["dimension_semantics"] = sem
    return pltpu.CompilerParams(**kw)


def _dot(a, b):
    return jnp.dot(a, b, preferred_element_type=F32)


def _dotg(a, b, dims):
    return lax.dot_general(a, b, dims, preferred_element_type=F32)


def _sigmoid(x):
    return 1.0 / (1.0 + jnp.exp(-x))


_GELU_K = 0.7978845608028654


def _gelu(x):
    th = jnp.tanh(_GELU_K * (x + 0.044715 * (x * x * x)))
    return 0.5 * x * (1.0 + th), th


def _gelu_grad(x, th):
    return 0.5 * (1.0 + th) + 0.5 * x * (1.0 - th * th) * (_GELU_K * (1.0 + 3.0 * 0.044715 * (x * x)))


def _colsum8(x):
    t, c = x.shape
    return jnp.sum(x.reshape(t // 8, 8, c), axis=0)


def _branches_fwd(p_ref, hxb_ref, hcg_ref, hxc_ref, first, tstart, w, sg_scr):
    t = p_ref.shape[0]
    nch = t // CHUNK

    def seg(o, width=WIDTH):
        return p_ref[:, o:o + width].astype(F32)

    lo = lax.broadcasted_iota(jnp.int32, (CHUNK, CHUNK), 1) < 64
    r = {}
    pu = seg(U)
    u_act, th_u = _gelu(pu)
    pv = seg(V)
    vg, th_v = _gelu(pv)
    mu = jnp.mean(vg, axis=-1, keepdims=True)
    xc = vg - mu
    var = jnp.mean(xc * xc, axis=-1, keepdims=True)
    rs = lax.rsqrt(var + LN_EPS)
    vhat = xc * rs
    vn = vhat * w["ln_g"][...] + w["ln_b"][...]
    vnb = vn.astype(BF16)
    for n in range(nch):
        for j in range(4):
            vb = vnb[n * CHUNK:(n + 1) * CHUNK, j * 128:(j + 1) * 128]
            z = _dot(w["w2"][j], vb)
            sg_scr[n * CHUNK:(n + 1) * CHUNK, j * 128:(j + 1) * 128] = (
                jnp.where(lo, z[:CHUNK], z[CHUNK:]) + w["bst"][:, j * 128:(j + 1) * 128])
    sg = sg_scr[...]
    a_br = u_act * sg
    za = seg(ZA)
    sa = _sigmoid(za)
    r.update(pu=pu, th_u=th_u, pv=pv, th_v=th_v, rs=rs, vhat=vhat, vnb=vnb, u_act=u_act, sg=sg,
             a_br=a_br, za=za, sa=sa, a_in=a_br * (za * sa))

    xb = seg(XB)
    cg = seg(CG)
    yb0 = cg * xb
    hal = hcg_ref[...].astype(F32) * hxb_ref[...].astype(F32)
    hal = jnp.where(first, 0.0, hal)
    ext = jnp.concatenate([hal, yb0], axis=0)
    y1 = pltpu.roll(ext, 1, 0)[HALO:]
    y2 = pltpu.roll(ext, 2, 0)[HALO:]
    cw = w["cw"]
    conv = cw[0:1, :] * y2 + cw[1:2, :] * y1 + cw[2:3, :] * yb0 + w["cb"][...]
    bg = seg(BG)
    b_br = bg * conv
    zb = seg(ZB)
    sb = _sigmoid(zb)
    r.update(xb=xb, cg=cg, yb0=yb0, y1=y1, y2=y2, conv=conv, bg=bg, b_br=b_br, zb=zb, sb=sb,
             b_in=b_br * (zb * sb))

    xcv = seg(XC)
    hxc = jnp.where(first, 0.0, hxc_ref[...].astype(F32))
    extc = jnp.concatenate([hxc, xcv], axis=0)
    tpos = tstart + lax.broadcasted_iota(jnp.int32, (t, 1), 0) + 1
    pooled, inv, q = [], [], []
    for gi, win in enumerate(POOL_WINDOWS):
        s = extc[:, gi * 128:(gi + 1) * 128]
        sh = 1
        while sh < win:
            s = s + pltpu.roll(s, sh, 0)
            sh *= 2
        inv_g = jnp.where(tpos >= win, 1.0 / win, 1.0 / jnp.minimum(tpos, win).astype(F32))
        pg = s[HALO:] * inv_g - xcv[:, gi * 128:(gi + 1) * 128]
        pooled.append(pg)
        inv.append(inv_g)
        q.append(_dot(pg.astype(BF16), w["wpool"][gi]))
    qv = jnp.concatenate(q, axis=1)
    c_br = qv * w["ps"][...]
    zc = seg(ZC)
    sc = _sigmoid(zc)
    r.update(pooled=pooled, inv=inv, q=qv, c_br=c_br, zc=zc, sc=sc, c_in=c_br * (zc * sc))
    return r


def _halo_specs(t, rev_n=None):
    def imap(col):
        def f(i):
            ti = i if rev_n is None else rev_n - 1 - i
            return (jnp.maximum(ti * (t // HALO) - 1, 0), col)
        return f
    return [pl.BlockSpec((HALO, WIDTH), imap(XB // WIDTH)),
            pl.BlockSpec((HALO, WIDTH), imap(CG // WIDTH)),
            pl.BlockSpec((HALO, WIDTH), imap(XC // WIDTH))]


def _const_spec(shape):
    nd = len(shape)
    return pl.BlockSpec(shape, lambda *_: (0,) * nd)


def _place():
    return lax.axis_index("x"), lax.axis_index("y"), lax.axis_index("c")


def _chip_peer(x, y, jm):
    px = (1 - x) if (jm & 2) else x
    py = (1 - y) if (jm & 1) else y
    return px, py


def _chip_part(buf, k, r0=0, nr=None):
    if len(buf.shape) == 3:
        return buf.at[k] if nr is None else buf.at[k, pl.ds(r0, nr)]
    cols = buf.shape[1] // N_CHIPS
    rows = pl.ds(0, buf.shape[0]) if nr is None else pl.ds(r0, nr)
    return buf.at[rows, pl.ds(pl.multiple_of(k * cols, 128), cols)]


def _row_range(ref, r0, nr):
    return ref if nr is None else ref.at[pl.ds(r0, nr)]


class _Comm:
    def __init__(self):
        self.ins, self.thru, self.fresh, self.moves, self.n = [], [], [], [], 0

    def read(self, arr):
        self.ins.append(arr)
        return ("ins", len(self.ins) - 1)

    def through(self, arr):
        self.thru.append(arr)
        return ("thru", len(self.thru) - 1)

    def new(self, shape, dtype):
        self.fresh.append(jax.ShapeDtypeStruct(tuple(shape), dtype))
        return ("fresh", len(self.fresh) - 1)

    def _add(self, n, *move):
        self.moves.append(move)
        self.n += n

    def gather(self, buf, r0=0, nr=None):
        self._add(3, "gather", buf, r0, nr)

    def scatter(self, src, dst, r0=0, nr=None):
        self._add(3, "scatter", src, dst, r0, nr)

    def sibling(self, src, dst, r0=0, nr=None):
        self._add(1, "sibling", src, dst, r0, nr)

    def slices_out(self, src, dst, rows):
        self._add(7, "slices_out", src, dst, rows)

    def slices_back(self, buf, rows):
        self._add(7, "slices_back", buf, rows)

    def copies(self, bufs, x, y, c):
        ref = lambda h: bufs[h[0]][h[1]]
        k_me = 2 * x + y
        me = 4 * x + 2 * y + c
        cps = []
        for move in self.moves:
            kind = move[0]
            if kind in ("gather", "scatter"):
                for jj, jm in enumerate((1, 2, 3)):
                    px, py = _chip_peer(x, y, jm)
                    k_peer = 2 * px + py
                    if kind == "gather":
                        _, buf, r0, nr = move
                        mine = _chip_part(ref(buf), k_me, r0, nr)
                        cps.append((mine, mine, _chip_part(ref(buf), k_peer, r0, nr), (px, py, c)))
                    else:
                        _, src, dst, r0, nr = move
                        slot = ref(dst).at[jj] if nr is None else ref(dst).at[jj, pl.ds(r0, nr)]
                        cps.append((_chip_part(ref(src), k_peer, r0, nr), slot, slot, (px, py, c)))
            elif kind == "sibling":
                _, src, dst, r0, nr = move
                land = _row_range(ref(dst), r0, nr)
                cps.append((_row_range(ref(src), r0, nr), land, land, (x, y, 1 - c)))
            else:
                for j in range(1, 8):
                    px = (1 - x) if (j & 4) else x
                    py = (1 - y) if (j & 2) else y
                    pc = (1 - c) if (j & 1) else c
                    peer = 4 * px + 2 * py + pc
                    if kind == "slices_out":
                        _, src, dst, rows = move
                        slot = ref(dst).at[j - 1]
                        cps.append((ref(src).at[pl.ds(pl.multiple_of(peer * rows, 8), rows)], slot, slot,
                                    (px, py, pc)))
                    else:
                        _, buf, rows = move
                        mine = ref(buf).at[pl.ds(pl.multiple_of(me * rows, 8), rows)]
                        cps.append((mine, mine, ref(buf).at[pl.ds(pl.multiple_of(peer * rows, 8), rows)],
                                    (px, py, pc)))
        assert len(cps) == self.n
        return cps


def _pcall(body, *, name, grid, in_specs, out_specs, out_shape, scratch, sem, args, comm=None, prefetch=()):
    n_pre = len(prefetch)
    n_in, n_out, n_scr = len(in_specs), len(out_specs), len(scratch)
    if comm is None or comm.n == 0:
        gs = pltpu.PrefetchScalarGridSpec(num_scalar_prefetch=n_pre, grid=grid, in_specs=list(in_specs),
                                          out_specs=list(out_specs), scratch_shapes=list(scratch))
        outs = pl.pallas_call(body, name=name, grid_spec=gs, out_shape=list(out_shape),
                              compiler_params=_params(sem))(*prefetch, *args)
        return list(outs), [], []
    n_ci, n_ct, n_cf, n_cp = len(comm.ins), len(comm.thru), len(comm.fresh), comm.n

    def wrapped(*refs):
        pos = 0
        def take(n):
            nonlocal pos
            got = refs[pos:pos + n]
            pos += n
            return got
        pre = take(n_pre)
        a, ci, _ = take(n_in), take(n_ci), take(n_ct)
        o, ct, cf = take(n_out), take(n_ct), take(n_cf)
        scr = take(n_scr)
        send_sems, recv_sems = take(2)
        first = functools.reduce(jnp.logical_and, [pl.program_id(d) == 0 for d in range(len(grid))])
        last = functools.reduce(jnp.logical_and, [pl.program_id(d) == grid[d] - 1 for d in range(len(grid))])
        x, y, c = _place()
        cps = comm.copies(dict(ins=ci, thru=ct, fresh=cf), x, y, c)

        def copy(i, src, dst, dev):
            return pltpu.make_async_remote_copy(src_ref=src, dst_ref=dst, send_sem=send_sems.at[i],
                                                recv_sem=recv_sems.at[i], device_id=dev, device_id_type=MESH)

        @pl.when(first)
        def _():
            for i, (src, dst, _, dev) in enumerate(cps):
                copy(i, src, dst, dev).start()

        body(*pre, *a, *o, *scr)

        @pl.when(last)
        def _():
            for i, (src, _, land, dev) in enumerate(cps):
                copy(i, src, land, dev).wait_recv()
            for i, (src, dst, _, dev) in enumerate(cps):
                copy(i, src, dst, dev).wait_send()

    thru_shapes = [jax.ShapeDtypeStruct(t.shape, t.dtype) for t in comm.thru]
    gs = pltpu.PrefetchScalarGridSpec(
        num_scalar_prefetch=n_pre, grid=grid,
        in_specs=list(in_specs) + [ANY] * (n_ci + n_ct),
        out_specs=list(out_specs) + [ANY] * (n_ct + n_cf),
        scratch_shapes=list(scratch) + [pltpu.SemaphoreType.DMA((n_cp,)), pltpu.SemaphoreType.DMA((n_cp,))])
    outs = pl.pallas_call(
        wrapped, name=name, grid_spec=gs,
        out_shape=list(out_shape) + thru_shapes + comm.fresh,
        input_output_aliases={n_pre + n_in + n_ci + t: n_out + t for t in range(n_ct)},
        compiler_params=_params(sem),
    )(*prefetch, *args, *comm.ins, *comm.thru)
    outs = list(outs)
    return outs[:n_out], outs[n_out:n_out + n_ct], outs[n_out + n_ct:]


def _comm_only(name, comm):
    def body():
        pass
    _, thru, _ = _pcall(body, name=name, grid=(1,), in_specs=[], out_specs=[], out_shape=[], scratch=[],
                        sem=("arbitrary",), args=[], comm=comm)
    return thru


def _gather_first(name, w_all, pieces=4):
    hr = w_all.shape[0] // 2
    pr = hr // pieces
    n = 3 * pieces

    def body(_, w_ref, send1, recv1, send2, recv2):
        x, y, c = _place()
        k_me = 2 * x + y

        def part(k, core, q):
            return _chip_part(w_ref, k, pl.multiple_of(core * hr + q * pr, 16), pr)

        def copy(src, dst, send, recv, i, dev):
            return pltpu.make_async_remote_copy(src_ref=src, dst_ref=dst, send_sem=send.at[i], recv_sem=recv.at[i],
                                                device_id=dev, device_id_type=MESH)

        over_ici, landing, onward, from_sibling = [], [], [], []
        for jj, jm in enumerate((1, 2, 3)):
            px, py = _chip_peer(x, y, jm)
            k_peer = 2 * px + py
            for q in range(pieces):
                i = jj * pieces + q
                over_ici.append(copy(part(k_me, c, q), part(k_me, c, q), send1, recv1, i, (px, py, c)))
                landing.append(copy(part(k_me, c, q), part(k_peer, c, q), send1, recv1, i, (px, py, c)))
                onward.append(copy(part(k_peer, c, q), part(k_peer, c, q), send2, recv2, i, (x, y, 1 - c)))
                from_sibling.append(copy(part(k_peer, c, q), part(k_peer, 1 - c, q), send2, recv2, i, (x, y, 1 - c)))
        for cp in over_ici:
            cp.start()
        for i in range(n):
            landing[i].wait_recv()
            onward[i].start()
        for i in range(n):
            from_sibling[i].wait_recv()
        for i in range(n):
            over_ici[i].wait_send()
            onward[i].wait_send()

    return pl.pallas_call(
        body, name=name, in_specs=[ANY], out_specs=ANY, out_shape=jax.ShapeDtypeStruct(w_all.shape, w_all.dtype),
        input_output_aliases={0: 0}, scratch_shapes=[pltpu.SemaphoreType.DMA((n,))] * 4,
    )(w_all)


def _f1(x, norm_g3, w_all, l, comm=None, tm=256):
    s = x.shape[0]

    def body(x_ref, g_ref, w_hbm, p_ref, h_ref, w_vmem):
        @pl.when(pl.program_id(0) == 0)
        def _():
            pltpu.sync_copy(w_hbm, w_vmem)
        xv = x_ref[...]
        r = lax.rsqrt(jnp.mean(xv * xv, axis=-1, keepdims=True) + RMS_EPS)
        hb = ((xv * r) * g_ref[...]).astype(BF16)
        h_ref[...] = hb
        p_ref[...] = _dot(hb, w_vmem[...]).astype(BF16)

    return _pcall(
        body, name=f"f1_l{l}", grid=(s // tm,),
        in_specs=[pl.BlockSpec((tm, D_MODEL), lambda i: (i, 0)),
                  pl.BlockSpec((None, 1, D_MODEL), lambda i: (l, 0, 0)), ANY],
        out_specs=[pl.BlockSpec((tm, IN_TOTAL), lambda i: (i, 0)),
                   pl.BlockSpec((tm, D_MODEL), lambda i: (i, 0))],
        out_shape=[jax.ShapeDtypeStruct((s, IN_TOTAL), BF16), jax.ShapeDtypeStruct((s, D_MODEL), BF16)],
        scratch=[pltpu.VMEM((D_MODEL, IN_TOTAL), BF16)], sem=("arbitrary",), args=[x, norm_g3, w_all], comm=comm)


def _f2(x, p, lw, l, comm=None, t=256):
    s = x.shape[0]
    n = s // t

    def body(p_ref, hxb_ref, hcg_ref, hxc_ref, x_ref, lng, lnb, w2, bst, cw, cb, wpool, ps,
             proj, wo, ya_ref, yb_ref, yc_ref, m_ref, xo_ref, sg_scr):
        i = pl.program_id(0)
        w = dict(ln_g=lng, ln_b=lnb, w2=w2, bst=bst, cw=cw, cb=cb, wpool=wpool, ps=ps)
        r = _branches_fwd(p_ref, hxb_ref, hcg_ref, hxc_ref, i == 0, i * t, w, sg_scr)

        def project(act, b):
            ab = act.astype(BF16)
            return jnp.concatenate([_dot(ab, proj[k, b * WIDTH:(b + 1) * WIDTH, :]) for k in range(N_CHIPS)], axis=1)

        ya = project(r["a_in"], 0)
        yb = project(r["b_in"], 1)
        yc = project(r["c_in"], 2)
        ya_ref[...] = ya.astype(BF16)
        yb_ref[...] = yb.astype(BF16)
        yc_ref[...] = yc.astype(BF16)
        m = (_sigmoid(p_ref[:, GA:GA + D_MODEL].astype(F32)) * ya
             + _sigmoid(p_ref[:, GB:GB + D_MODEL].astype(F32)) * yb
             + _sigmoid(p_ref[:, GC:GC + D_MODEL].astype(F32)) * yc)
        mb = m.astype(BF16)
        m_ref[...] = mb
        xo_ref[...] = x_ref[...] + _dot(mb, wo[...])

    tile = lambda c: pl.BlockSpec((t, c), lambda i: (i, 0))
    lsel = lambda *blk: pl.BlockSpec((None,) + blk, lambda i: (l,) + (0,) * len(blk))
    act = jax.ShapeDtypeStruct((s, D_MODEL), BF16)
    return _pcall(
        body, name=f"f2_l{l}", grid=(n,),
        in_specs=[tile(IN_TOTAL)] + _halo_specs(t) + [
            tile(D_MODEL), lsel(1, WIDTH), lsel(1, WIDTH), lsel(4, 256, 128), lsel(CHUNK, WIDTH),
            lsel(8, WIDTH), lsel(1, WIDTH), lsel(4, 128, 128), lsel(1, WIDTH),
            _const_spec((N_CHIPS, 3 * WIDTH, 256)), _const_spec((D_MODEL, D_MODEL))],
        out_specs=[tile(D_MODEL)] * 5,
        out_shape=[act, act, act, act, jax.ShapeDtypeStruct((s, D_MODEL), F32)],
        scratch=[pltpu.VMEM((t, WIDTH), F32)], sem=("arbitrary",),
        args=[p, p, p, p, x, lw["ln_g"], lw["ln_b"], lw["w2"], lw["bst"], lw["cw"], lw["cb"], lw["wpool"], lw["ps"],
              lw["proj"][l], lw["wo"][l]], comm=comm)


def _b1(p, dout, ya, yb, yc, lw, l, comm=None, t=256):
    s = p.shape[0]
    n = s // t
    nch = t // CHUNK

    def body(p_ref, hxb_ref, hcg_ref, hxc_ref, do_ref, ya_ref, yb_ref, yc_ref,
             lng, lnb, w2, wt2, bst, cw, cb, wpool, ps, proj_h, wo_h, sel_ref,
             dp_ref, gwp_h, gws_ref, gbs_ref, glng_ref, glnb_ref, gcw_ref, gcb_ref,
             gwpool_ref, gps_ref,
             wpa, wpb, wpc, wo, gwpa, gwpb, gwpc, gbs_acc, vec_acc, sg_scr, dvn_scr, car_dc, car_e, psem):
        i = pl.program_id(0)
        ti = n - 1 - i

        def by_chip_copies(vmem_bufs, hbm, to_hbm):
            cps = []
            for b, buf in enumerate(vmem_bufs):
                for k in range(N_CHIPS):
                    v = buf.at[:, pl.ds(k * 256, 256)]
                    h = hbm.at[k, pl.ds(b * WIDTH, WIDTH)]
                    cps.append(pltpu.make_async_copy(v, h, psem.at[b * N_CHIPS + k]) if to_hbm
                               else pltpu.make_async_copy(h, v, psem.at[b * N_CHIPS + k]))
            return cps

        @pl.when(i == 0)
        def _():
            loads = by_chip_copies((wpa, wpb, wpc), proj_h, False)
            for cp in loads:
                cp.start()
            pltpu.sync_copy(wo_h, wo)
            for cp in loads:
                cp.wait()
            for acc in (gwpa, gwpb, gwpc, gbs_acc, vec_acc, car_dc, car_e):
                acc[...] = jnp.zeros(acc.shape, acc.dtype)
            gws_ref[...] = jnp.zeros(gws_ref.shape, F32)
            gwpool_ref[...] = jnp.zeros(gwpool_ref.shape, F32)

        w = dict(ln_g=lng, ln_b=lnb, w2=w2, bst=bst, cw=cw, cb=cb, wpool=wpool, ps=ps)
        r = _branches_fwd(p_ref, hxb_ref, hcg_ref, hxc_ref, ti == 0, ti * t, w, sg_scr)

        def seg(o, width=WIDTH):
            return p_ref[:, o:o + width].astype(F32)

        def put(o, val):
            dp_ref[:, o:o + val.shape[1]] = val.astype(BF16)

        dob = do_ref[...].astype(BF16)
        dm = _dotg(dob, wo[...], NT)

        def merge_bwd(goff, y_ref, xin, wp, gwp):
            sx = _sigmoid(seg(goff, D_MODEL))
            dmy = dm * sx
            put(goff, dmy * y_ref[...].astype(F32) * (1.0 - sx))
            dyb = dmy.astype(BF16)
            gwp[...] += _dotg(xin.astype(BF16), dyb, TN)
            return _dotg(dyb, wp[...], NT)

        d_ain = merge_bwd(GA, ya_ref, r["a_in"], wpa, gwpa)
        d_bin = merge_bwd(GB, yb_ref, r["b_in"], wpb, gwpb)
        d_cin = merge_bwd(GC, yc_ref, r["c_in"], wpc, gwpc)

        def dsilu(z, sz):
            return sz * (1.0 + z * (1.0 - sz))

        za, sa = r["za"], r["sa"]
        d_abr = d_ain * (za * sa)
        put(ZA, d_ain * r["a_br"] * dsilu(za, sa))
        put(U, d_abr * r["sg"] * _gelu_grad(r["pu"], r["th_u"]))
        d_sg = d_abr * r["u_act"]
        dsgb = d_sg.astype(BF16)
        lo = lax.broadcasted_iota(jnp.int32, (CHUNK, CHUNK), 1) < 64
        zero = jnp.zeros((CHUNK, CHUNK), BF16)
        for c in range(nch):
            rows = slice(c * CHUNK, (c + 1) * CHUNK)
            gbs_acc[...] += d_sg[rows]
            for j in range(4):
                cols = slice(j * 128, (j + 1) * 128)
                dj = dsgb[rows, cols]
                zt = _dot(wt2[j], dj)
                dvn_scr[rows, cols] = jnp.where(lo, zt[:CHUNK], zt[CHUNK:])
                stacked = jnp.concatenate([jnp.where(lo, dj, zero), jnp.where(lo, zero, dj)], axis=0)
                gws_ref[j] += _dotg(stacked, r["vnb"][rows, cols], NT)
        d_vn = dvn_scr[...]
        vhat = r["vhat"]
        vec_acc[0] += _colsum8(d_vn * vhat)
        vec_acc[1] += _colsum8(d_vn)
        d_vhat = d_vn * lng[...]
        d_vg = r["rs"] * (d_vhat - jnp.mean(d_vhat, axis=-1, keepdims=True)
                          - vhat * jnp.mean(d_vhat * vhat, axis=-1, keepdims=True))
        put(V, d_vg * _gelu_grad(r["pv"], r["th_v"]))

        zb, sb = r["zb"], r["sb"]
        d_bbr = d_bin * (zb * sb)
        put(ZB, d_bin * r["b_br"] * dsilu(zb, sb))
        put(BG, d_bbr * r["conv"])
        dc = d_bbr * r["bg"]
        vec_acc[2] += _colsum8(dc)
        vec_acc[3] += _colsum8(dc * r["y2"])
        vec_acc[4] += _colsum8(dc * r["y1"])
        vec_acc[5] += _colsum8(dc * r["yb0"])
        ext = jnp.concatenate([dc, car_dc[...]], axis=0)
        ne = t + HALO
        d1 = pltpu.roll(ext, ne - 1, 0)[:t]
        d2 = pltpu.roll(ext, ne - 2, 0)[:t]
        d_yb0 = cw[2:3, :] * dc + cw[1:2, :] * d1 + cw[0:1, :] * d2
        put(CG, d_yb0 * r["xb"])
        put(XB, d_yb0 * r["cg"])
        car_dc[...] = dc[:HALO]

        zc, sc = r["zc"], r["sc"]
        d_cbr = d_cin * (zc * sc)
        put(ZC, d_cin * r["c_br"] * dsilu(zc, sc))
        vec_acc[6] += _colsum8(d_cbr * r["q"])
        d_q = d_cbr * ps[...]
        for gi, win in enumerate(POOL_WINDOWS):
            cols = slice(gi * 128, (gi + 1) * 128)
            dqb = d_q[:, cols].astype(BF16)
            d_pool = _dotg(dqb, wpool[gi], NT)
            gwpool_ref[gi] += _dotg(r["pooled"][gi].astype(BF16), dqb, TN)
            e = d_pool * r["inv"][gi]
            sx = jnp.concatenate([e, car_e[:, cols]], axis=0)
            sh = 1
            while sh < win:
                sx = sx + pltpu.roll(sx, ne - sh, 0)
                sh *= 2
            put(XC + gi * 128, sx[:t] - d_pool)
            car_e[:, cols] = e[:HALO]

        @pl.when(i == n - 1)
        def _():
            for acc, stage in ((gwpa, wpa), (gwpb, wpb), (gwpc, wpc)):
                stage[...] = acc[...].astype(BF16)
            stores = by_chip_copies((wpa, wpb, wpc), gwp_h, True)
            for cp in stores:
                cp.start()
            for cp in stores:
                cp.wait()
            gbs_ref[...] = jnp.dot(gbs_acc[...], sel_ref[...], preferred_element_type=F32,
                                   precision=lax.Precision.HIGHEST)
            red = lambda k: jnp.sum(vec_acc[k], axis=0, keepdims=True)
            glng_ref[...] = red(0)
            glnb_ref[...] = red(1)
            gcb_ref[...] = red(2)
            gcw_ref[...] = jnp.zeros(gcw_ref.shape, F32)
            for k in range(3):
                gcw_ref[k:k + 1, :] = red(3 + k)
            gps_ref[...] = red(6)
            tt = lax.broadcasted_iota(jnp.int32, (2 * CHUNK, CHUNK), 0) % CHUNK
            ss = lax.broadcasted_iota(jnp.int32, (2 * CHUNK, CHUNK), 1)
            for j in range(4):
                gws_ref[j] = jnp.where(tt >= ss, gws_ref[j], 0.0)

    rtile = lambda c: pl.BlockSpec((t, c), lambda i: (n - 1 - i, 0))
    lsel = lambda *blk: pl.BlockSpec((None,) + blk, lambda i: (l,) + (0,) * len(blk))
    f32s = lambda *shape: jax.ShapeDtypeStruct(shape, F32)
    return _pcall(
        body, name=f"b1_l{l}", grid=(n,),
        in_specs=[rtile(IN_TOTAL)] + _halo_specs(t, rev_n=n) + [rtile(D_MODEL)] * 4 + [
            lsel(1, WIDTH), lsel(1, WIDTH), lsel(4, 256, 128), lsel(4, 256, 128), lsel(CHUNK, WIDTH),
            lsel(8, WIDTH), lsel(1, WIDTH), lsel(4, 128, 128), lsel(1, WIDTH),
            ANY, ANY, _const_spec((WIDTH, 128))],
        out_specs=[rtile(IN_TOTAL), ANY,
                   _const_spec((4, 256, 128)), _const_spec((CHUNK, 128)), _const_spec((1, WIDTH)),
                   _const_spec((1, WIDTH)), _const_spec((8, WIDTH)), _const_spec((1, WIDTH)),
                   _const_spec((4, 128, 128)), _const_spec((1, WIDTH))],
        out_shape=[jax.ShapeDtypeStruct((s, IN_TOTAL), BF16), jax.ShapeDtypeStruct((N_CHIPS, 3 * WIDTH, 256), BF16),
                   f32s(4, 256, 128), f32s(CHUNK, 128),
                   f32s(1, WIDTH), f32s(1, WIDTH), f32s(8, WIDTH), f32s(1, WIDTH), f32s(4, 128, 128),
                   f32s(1, WIDTH)],
        scratch=[pltpu.VMEM((WIDTH, D_MODEL), BF16), pltpu.VMEM((WIDTH, D_MODEL), BF16),
                 pltpu.VMEM((WIDTH, D_MODEL), BF16), pltpu.VMEM((D_MODEL, D_MODEL), BF16),
                 pltpu.VMEM((WIDTH, D_MODEL), F32),
                 pltpu.VMEM((WIDTH, D_MODEL), F32), pltpu.VMEM((WIDTH, D_MODEL), F32),
                 pltpu.VMEM((CHUNK, WIDTH), F32), pltpu.VMEM((8, 8, WIDTH), F32),
                 pltpu.VMEM((t, WIDTH), F32), pltpu.VMEM((t, WIDTH), F32),
                 pltpu.VMEM((HALO, WIDTH), F32), pltpu.VMEM((HALO, WIDTH), F32),
                 pltpu.SemaphoreType.DMA((3 * N_CHIPS,))],
        sem=("arbitrary",),
        args=[p, p, p, p, dout, ya, yb, yc, lw["ln_g"], lw["ln_b"], lw["w2"], lw["wt2"], lw["bst"], lw["cw"],
              lw["cb"], lw["wpool"], lw["ps"], lw["proj"][l], lw["wo"][l], lw["sel"]], comm=comm)


def _rms_bwd(xv, g, dh):
    r = lax.rsqrt(jnp.mean(xv * xv, axis=-1, keepdims=True) + RMS_EPS)
    xhat = xv * r
    dxh = dh * g
    dx = r * (dxh - xhat * jnp.mean(dxh * xhat, axis=-1, keepdims=True))
    return dx, dh * xhat


def _b2a(dp, w_all, x, dout, norm_g3, l, comm=None, tm=256):
    s = x.shape[0]
    nm = s // tm

    def body(dp_ref, w_hbm, x_ref, do_ref, g_ref, dx_ref, gg_ref, w_vmem, gacc):
        i = pl.program_id(0)

        @pl.when(i == 0)
        def _():
            pltpu.sync_copy(w_hbm, w_vmem)
            gacc[...] = jnp.zeros(gacc.shape, F32)

        dh = _dotg(dp_ref[...], w_vmem[...], NT)
        dx, gx = _rms_bwd(x_ref[...], g_ref[...], dh)
        dx_ref[...] = do_ref[...] + dx
        gacc[...] += _colsum8(gx)

        @pl.when(i == nm - 1)
        def _():
            gg_ref[...] = jnp.sum(gacc[...], axis=0, keepdims=True)

    return _pcall(
        body, name=f"b2a_l{l}", grid=(nm,),
        in_specs=[pl.BlockSpec((tm, IN_TOTAL), lambda i: (i, 0)), ANY,
                  pl.BlockSpec((tm, D_MODEL), lambda i: (i, 0)),
                  pl.BlockSpec((tm, D_MODEL), lambda i: (i, 0)),
                  pl.BlockSpec((None, 1, D_MODEL), lambda i: (l, 0, 0))],
        out_specs=[pl.BlockSpec((tm, D_MODEL), lambda i: (i, 0)),
                   pl.BlockSpec((1, D_MODEL), lambda i: (0, 0))],
        out_shape=[jax.ShapeDtypeStruct((s, D_MODEL), F32), jax.ShapeDtypeStruct((1, D_MODEL), F32)],
        scratch=[pltpu.VMEM((D_MODEL, IN_TOTAL), BF16), pltpu.VMEM((8, D_MODEL), F32)],
        sem=("arbitrary",), args=[dp, w_all, x, dout, norm_g3], comm=comm)


def _b2b(h, dp, name, chip, part="all", comm=None, tk=1024):
    s = h.shape[0]
    nk = s // tk
    first, count = dict(all=(0, 4), others=(1, 3), own=(0, 1))[part]
    cols_of = lambda k, idx: (idx[0] + first + k) % N_CHIPS

    def body(idx_ref, h_ref, dp_ref, g_ref, acc):
        kk = pl.program_id(1)

        @pl.when(kk == 0)
        def _():
            acc[...] = jnp.zeros(acc.shape, F32)

        acc[...] += _dotg(h_ref[...], dp_ref[...], TN)

        @pl.when(kk == nk - 1)
        def _():
            g_ref[...] = acc[...].astype(BF16)

    own = part == "own"
    return _pcall(
        body, name=name, grid=(count, nk),
        in_specs=[pl.BlockSpec((tk, D_MODEL), lambda k, kk, idx: (kk, 0)),
                  pl.BlockSpec((tk, SHARD_W), lambda k, kk, idx: (kk, cols_of(k, idx)))],
        out_specs=[pl.BlockSpec((D_MODEL, SHARD_W), lambda k, kk, idx: (0, 0 if own else cols_of(k, idx)))],
        out_shape=[jax.ShapeDtypeStruct((D_MODEL, SHARD_W if own else IN_TOTAL), BF16)],
        scratch=[pltpu.VMEM((D_MODEL, SHARD_W), F32)],
        sem=("arbitrary", "arbitrary"), args=[h, dp], comm=comm, prefetch=[chip])


def _gwo(m, dout, l, tk=512):
    s = m.shape[0]
    nk = s // tk

    def body(m_ref, do_ref, g_ref, acc):
        kk = pl.program_id(0)
        prod = _dotg(m_ref[...], do_ref[...].astype(BF16), TN)

        @pl.when(kk == 0)
        def _():
            acc[...] = prod

        @pl.when(kk > 0)
        def _():
            acc[...] += prod

        @pl.when(kk == nk - 1)
        def _():
            g_ref[...] = acc[...].astype(BF16)

    return pl.pallas_call(
        body, name=f"gwo_l{l}", grid=(nk,),
        in_specs=[pl.BlockSpec((tk, D_MODEL), lambda kk: (kk, 0)), pl.BlockSpec((tk, D_MODEL), lambda kk: (kk, 0))],
        out_specs=_const_spec((D_MODEL, D_MODEL)),
        out_shape=jax.ShapeDtypeStruct((D_MODEL, D_MODEL), BF16),
        scratch_shapes=[pltpu.VMEM((D_MODEL, D_MODEL), F32)],
        compiler_params=_params(("arbitrary",)),
    )(m, dout)


def _loss_head(x, tgt, final_g2, tm=512):
    s = x.shape[0]
    nm = s // tm

    def body(x_ref, t_ref, g_ref, dx_ref, loss_ref, gg_ref, lacc, gacc):
        i = pl.program_id(0)

        @pl.when(i == 0)
        def _():
            lacc[...] = jnp.zeros(lacc.shape, F32)
            gacc[...] = jnp.zeros(gacc.shape, F32)

        xv = x_ref[...]
        g = g_ref[...]
        r = lax.rsqrt(jnp.mean(xv * xv, axis=-1, keepdims=True) + RMS_EPS)
        err = (xv * r) * g - t_ref[...]
        lacc[...] += _colsum8(err * err)
        dx, gx = _rms_bwd(xv, g, err * (1.0 / D_MODEL))
        dx_ref[...] = dx
        gacc[...] += _colsum8(gx)

        @pl.when(i == nm - 1)
        def _():
            tot = jnp.sum(jnp.sum(lacc[...], axis=0, keepdims=True), axis=1, keepdims=True)
            loss_ref[...] = jnp.broadcast_to(tot * (0.5 / D_MODEL), loss_ref.shape)
            gg_ref[...] = jnp.sum(gacc[...], axis=0, keepdims=True)

    return pl.pallas_call(
        body, name="loss_head", grid=(nm,),
        in_specs=[pl.BlockSpec((tm, D_MODEL), lambda i: (i, 0)), pl.BlockSpec((tm, D_MODEL), lambda i: (i, 0)),
                  _const_spec((1, D_MODEL))],
        out_specs=[pl.BlockSpec((tm, D_MODEL), lambda i: (i, 0)), _const_spec((8, 128)),
                   _const_spec((1, D_MODEL))],
        out_shape=[jax.ShapeDtypeStruct((s, D_MODEL), F32), jax.ShapeDtypeStruct((8, 128), F32),
                   jax.ShapeDtypeStruct((1, D_MODEL), F32)],
        scratch_shapes=[pltpu.VMEM((8, D_MODEL), F32), pltpu.VMEM((8, D_MODEL), F32)],
        compiler_params=_params(("arbitrary",)),
    )(x, tgt, final_g2)


def _row_block(rows, cols, n_arrays):
    budget = VMEM_LIMIT // 3 // (2 * 4 * n_arrays * cols)
    rb = rows
    while rb > budget and rb % 16 == 0:
        rb //= 2
    return rb


def _cast_slot(name, a, l, chip):
    _, rows, cols = a.shape
    rb = _row_block(rows, cols, 2)

    def body(idx_ref, a_ref, o_ref):
        o_ref[...] = a_ref[...].astype(BF16)

    gs = pltpu.PrefetchScalarGridSpec(
        num_scalar_prefetch=1, grid=(rows // rb,),
        in_specs=[pl.BlockSpec((None, rb, cols), lambda i, idx: (l, i, 0))],
        out_specs=pl.BlockSpec((None, rb, cols), lambda i, idx: (idx[0], i, 0)))
    return pl.pallas_call(body, name=name, grid_spec=gs, out_shape=jax.ShapeDtypeStruct((N_CHIPS, rows, cols), BF16),
                          compiler_params=_params(("arbitrary",)))(chip, a)


def _cast_cols(name, a, l, chip):
    _, rows, cols = a.shape
    rb = _row_block(rows, cols, 2)

    def body(idx_ref, a_ref, o_ref):
        o_ref[...] = a_ref[...].astype(BF16)

    gs = pltpu.PrefetchScalarGridSpec(
        num_scalar_prefetch=1, grid=(rows // rb,),
        in_specs=[pl.BlockSpec((None, rb, cols), lambda i, idx: (l, i, 0))],
        out_specs=pl.BlockSpec((rb, cols), lambda i, idx: (i, idx[0])))
    return pl.pallas_call(body, name=name, grid_spec=gs, out_shape=jax.ShapeDtypeStruct((rows, N_CHIPS * cols), BF16),
                          compiler_params=_params(("arbitrary",)))(chip, a)


def _sum4(name, own, chip, recv, l, n_layers, prev=None):
    _, rows, cols = recv.shape
    rb = _row_block(rows, cols, 5)
    nb = rows // rb
    if len(own.shape) == 3:
        own_spec = pl.BlockSpec((None, rb, cols), lambda i, idx: (idx[0], i, 0))
    elif own.shape[1] == cols:
        own_spec = pl.BlockSpec((rb, cols), lambda i, idx: (i, 0))
    else:
        own_spec = pl.BlockSpec((rb, cols), lambda i, idx: (i, idx[0]))

    def body(idx_ref, own_ref, r_ref, *rest):
        o_ref = rest[-1]
        o_ref[...] = ((own_ref[...].astype(F32) + r_ref[0].astype(F32)) + r_ref[1].astype(F32)) + r_ref[2].astype(F32)

    gs = pltpu.PrefetchScalarGridSpec(
        num_scalar_prefetch=1, grid=(nb,),
        in_specs=[own_spec,
                  pl.BlockSpec((3, rb, cols), lambda i, idx: (0, i, 0))] + ([ANY] if prev is not None else []),
        out_specs=pl.BlockSpec((rb, cols), lambda i, idx: (l * nb + i, 0)))
    args = (chip, own, recv) + ((prev,) if prev is not None else ())
    return pl.pallas_call(body, name=name, grid_spec=gs,
                          out_shape=jax.ShapeDtypeStruct((n_layers * rows, cols), F32),
                          input_output_aliases=({3: 0} if prev is not None else {}),
                          compiler_params=_params(("arbitrary",)))(*args)


def _sum8(name, pack, me, recv):
    rows = recv.shape[1]

    def body(idx_ref, own_ref, r_ref, o_ref):
        acc = own_ref[...]
        for j in range(7):
            acc = acc + r_ref[j]
        o_ref[...] = acc

    gs = pltpu.PrefetchScalarGridSpec(
        num_scalar_prefetch=1, grid=(1,),
        in_specs=[pl.BlockSpec((rows, 128), lambda i, idx: (idx[0], 0)),
                  pl.BlockSpec((7, rows, 128), lambda i, idx: (0, 0, 0))],
        out_specs=pl.BlockSpec((rows, 128), lambda i, idx: (idx[0], 0)))
    return pl.pallas_call(body, name=name, grid_spec=gs, out_shape=jax.ShapeDtypeStruct(pack.shape, F32),
                          compiler_params=_params(("arbitrary",)))(me, pack, recv)


_SMALL = ("norm_g", "ln_g", "ln_b", "w_s", "b_s", "conv_b", "w_pool", "pool_scale", "final_g", "conv_w")
_SMALL_SHAPES = dict(norm_g=(DEPTH, D_MODEL), ln_g=(DEPTH, WIDTH), ln_b=(DEPTH, WIDTH), w_s=(DEPTH, 8, CHUNK, CHUNK),
                     b_s=(DEPTH, 8, CHUNK), conv_b=(DEPTH, WIDTH), w_pool=(DEPTH, 4, 128, 128),
                     pool_scale=(DEPTH, WIDTH), final_g=(1, D_MODEL), conv_w=(DEPTH, 3, WIDTH))


def _small_rows():
    base, r = {}, 0
    for nm in _SMALL:
        base[nm] = r
        size = 1
        for d in _SMALL_SHAPES[nm]:
            size *= d
        r += size // 128
    return base, -(-r // 64) * 64


def _pack_small(name, raw, gng1, g_final):
    base, rows = _small_rows()
    n_l = len(raw[0])

    def body(*refs):
        o = refs[-1]
        per_layer = [refs[l * n_l:(l + 1) * n_l] for l in range(DEPTH)]
        gng1_ref, gfin_ref = refs[DEPTH * n_l], refs[DEPTH * n_l + 1]
        o[...] = jnp.zeros(o.shape, F32)

        def put_row_vector(r0, ref, width):
            for j in range(width // 128):
                o[r0 + j:r0 + j + 1, :] = ref[0:1, j * 128:(j + 1) * 128]

        put_row_vector(base["norm_g"] + D_MODEL // 128, gng1_ref, D_MODEL)
        put_row_vector(base["final_g"], gfin_ref, D_MODEL)
        for l in range(DEPTH):
            gws, gbs, glng, glnb, gcw, gcb, gwpool, gps = per_layer[l]
            put_row_vector(base["ln_g"] + l * 4, glng, WIDTH)
            put_row_vector(base["ln_b"] + l * 4, glnb, WIDTH)
            put_row_vector(base["conv_b"] + l * 4, gcb, WIDTH)
            put_row_vector(base["pool_scale"] + l * 4, gps, WIDTH)
            o[base["w_s"] + l * 1024:base["w_s"] + (l + 1) * 1024, :] = gws[...].reshape(1024, 128)
            o[base["w_pool"] + l * 512:base["w_pool"] + (l + 1) * 512, :] = gwpool[...].reshape(512, 128)
            o[base["b_s"] + l * 8:base["b_s"] + (l + 1) * 8, :] = gbs[...].T[0:8, :]
            for k in range(3):
                for ch in range(N_CHIPS):
                    r = base["conv_w"] + (l * 3 + k) * N_CHIPS + ch
                    o[r:r + 1, :] = gcw[k:k + 1, ch * 128:(ch + 1) * 128]

    args = [a for l in range(DEPTH) for a in raw[l]] + [gng1, g_final]
    vm = pl.BlockSpec(memory_space=pltpu.VMEM)
    return pl.pallas_call(body, name=name, in_specs=[vm] * len(args), out_specs=vm,
                          out_shape=jax.ShapeDtypeStruct((rows, 128), F32), compiler_params=_params())(*args)


def _adamw_small(name, gred, chip, w, m, v):
    base, _ = _small_rows()

    def body(chip_ref, g_ref, *refs):
        n = len(_SMALL)
        w_r, m_r, v_r = refs[:n], refs[n:2 * n], refs[2 * n:3 * n]
        out = refs[3 * n:]

        def update(i, idx, g):
            d, mn, vn = _adamw_math(w_r[i][idx], g, m_r[i][idx], v_r[i][idx])
            for o, val in zip(out[4 * i:4 * i + 4], (g, d, mn, vn)):
                o[idx] = val

        for i, nm in enumerate(_SMALL):
            shape = _SMALL_SHAPES[nm]
            if nm == "conv_w":
                for l in range(DEPTH):
                    for k in range(3):
                        row = base[nm] + (l * 3 + k) * N_CHIPS + chip_ref[0]
                        update(i, (l, slice(k, k + 1), slice(None)), g_ref[pl.ds(row, 1), :])
            elif len(shape) == 2:
                per = shape[1] // 128
                for l in range(shape[0]):
                    for j in range(per):
                        r = base[nm] + l * per + j
                        update(i, (slice(l, l + 1), slice(j * 128, (j + 1) * 128)), g_ref[r:r + 1, :])
            else:
                rows = 1
                for dim in shape[:-1]:
                    rows *= dim
                update(i, (Ellipsis,), g_ref[base[nm]:base[nm] + rows, :].reshape(shape))

    arrs = [d[nm] for d in (w, m, v) for nm in _SMALL]
    vm = pl.BlockSpec(memory_space=pltpu.VMEM)
    gs = pltpu.PrefetchScalarGridSpec(num_scalar_prefetch=1, grid=(1,), in_specs=[vm] * (1 + len(arrs)),
                                      out_specs=[vm] * (4 * len(_SMALL)))
    outs = pl.pallas_call(
        body, name=name, grid_spec=gs,
        out_shape=[jax.ShapeDtypeStruct(w[nm].shape, F32) for nm in _SMALL for _ in range(4)],
        compiler_params=_params(("arbitrary",)))(chip, gred, *arrs)
    return {nm: list(outs[4 * i:4 * i + 4]) for i, nm in enumerate(_SMALL)}


def _adamw_math(w, g, m, v):
    m = ADAM_B1 * m + (1.0 - ADAM_B1) * g
    v = ADAM_B2 * v + (1.0 - ADAM_B2) * (g * g)
    m_hat = m / (1.0 - ADAM_B1 ** ADAM_STEP)
    v_hat = v / (1.0 - ADAM_B2 ** ADAM_STEP)
    delta = -ADAM_LR * (m_hat / (jnp.sqrt(v_hat) + ADAM_EPS) + ADAM_WD * w)
    return delta, m, v


def _adamw(name, w, m, v, g_parts):
    rows, cols = w.shape
    np_ = len(g_parts)
    rb = _row_block(rows, cols, 7 + np_)

    def body(*refs):
        w_ref, m_ref, v_ref = refs[:3]
        g_refs = refs[3:3 + np_]
        go_ref, d_ref, mo_ref, vo_ref = refs[3 + np_:]
        g = g_refs[0][...]
        for gr in g_refs[1:]:
            g = g + gr[...]
        d, mn, vn = _adamw_math(w_ref[...], g, m_ref[...], v_ref[...])
        go_ref[...] = g
        d_ref[...] = d
        mo_ref[...] = mn
        vo_ref[...] = vn

    spec = pl.BlockSpec((rb, cols), lambda i: (i, 0))
    shp = jax.ShapeDtypeStruct((rows, cols), F32)
    return pl.pallas_call(body, name=name, grid=(rows // rb,), in_specs=[spec] * (3 + np_),
                          out_specs=[spec] * 4, out_shape=[shp] * 4,
                          compiler_params=_params(("arbitrary",)))(w, m, v, *g_parts)


def _all_reduce_small(name, a):
    def body(a_ref, o_ref, buf, send_sems, recv_sems):
        x, y, c = _place()
        o_ref[...] = a_ref[...]
        for rnd, peer in enumerate(((x, y, 1 - c), (x, 1 - y, c), (1 - x, y, c))):
            cp = pltpu.make_async_remote_copy(
                src_ref=o_ref, dst_ref=buf.at[rnd], send_sem=send_sems.at[rnd], recv_sem=recv_sems.at[rnd],
                device_id=peer, device_id_type=MESH)
            cp.start()
            cp.wait_recv()
            cp.wait_send()
            o_ref[...] = o_ref[...] + buf[rnd]

    vm = pl.BlockSpec(memory_space=pltpu.VMEM)
    return pl.pallas_call(
        body, name=name, in_specs=[vm], out_specs=vm, out_shape=jax.ShapeDtypeStruct(a.shape, F32),
        scratch_shapes=[pltpu.VMEM((3,) + a.shape, F32), pltpu.SemaphoreType.DMA((3,)),
                        pltpu.SemaphoreType.DMA((3,))],
        compiler_params=_params(),
    )(a)


def kernel(x, norm_g, w_in, ln_g, ln_b, w_s, b_s, conv_w, conv_b, w_pool, pool_scale, w_pa, w_pb, w_pc, w_o, final_g, loss_target, m_norm_g, m_w_in, m_ln_g, m_ln_b, m_w_s, m_b_s, m_conv_w, m_conv_b, m_w_pool, m_pool_scale, m_w_pa, m_w_pb, m_w_pc, m_w_o, m_final_g, v_norm_g, v_w_in, v_ln_g, v_ln_b, v_w_s, v_b_s, v_conv_w, v_conv_b, v_w_pool, v_pool_scale, v_w_pa, v_w_pb, v_w_pc, v_w_o, v_final_g):
    W = dict(norm_g=norm_g, w_in=w_in, ln_g=ln_g, ln_b=ln_b, w_s=w_s, b_s=b_s, conv_w=conv_w, conv_b=conv_b,
             w_pool=w_pool, pool_scale=pool_scale, w_pa=w_pa, w_pb=w_pb, w_pc=w_pc, w_o=w_o, final_g=final_g)
    M = dict(norm_g=m_norm_g, w_in=m_w_in, ln_g=m_ln_g, ln_b=m_ln_b, w_s=m_w_s, b_s=m_b_s, conv_w=m_conv_w,
             conv_b=m_conv_b, w_pool=m_w_pool, pool_scale=m_pool_scale, w_pa=m_w_pa, w_pb=m_w_pb, w_pc=m_w_pc,
             w_o=m_w_o, final_g=m_final_g)
    Vv = dict(norm_g=v_norm_g, w_in=v_w_in, ln_g=v_ln_g, ln_b=v_ln_b, w_s=v_w_s, b_s=v_b_s, conv_w=v_conv_w,
              conv_b=v_conv_b, w_pool=v_w_pool, pool_scale=v_pool_scale, w_pa=v_w_pa, w_pb=v_w_pb, w_pc=v_w_pc,
              w_o=v_w_o, final_g=v_final_g)
    L = DEPTH
    s = x.shape[1]
    xs = x.reshape(s, D_MODEL)
    tgt = loss_target.reshape(s, D_MODEL)
    k_me = (2 * lax.axis_index("x") + lax.axis_index("y")).astype(jnp.int32)

    chip = k_me.reshape(1)
    assert L == 2
    half_rows = D_MODEL // 2

    land_win = [_cast_cols(f"cast_w_in{l}", w_in, l, chip) for l in range(L)]
    pcat = lambda d: jnp.concatenate([d[b][l] for l in range(L) for b in ("w_pa", "w_pb", "w_pc")], axis=0)
    pcat_w = pcat(W).reshape(L, 3 * WIDTH, 256)
    land_proj = [_cast_slot(f"cast_proj{l}", pcat_w, l, chip) for l in range(L)]
    land_wo = [_cast_slot(f"cast_w_o{l}", w_o, l, chip) for l in range(L)]
    cw_sh = jnp.pad(conv_w, ((0, 0), (0, 5), (0, 0))).reshape(1, L * 8, 128)
    land_cw = lax.dynamic_update_slice(jnp.zeros((N_CHIPS, L * 8, 128), F32), cw_sh, (k_me, 0, 0))
    w_all0 = _gather_first("gather_w_in0", land_win[0])

    causal = jnp.tril(jnp.ones((CHUNK, CHUNK), dtype=bool))
    w_m = jnp.where(causal, w_s, 0.0)
    lw = dict(
        ln_g=ln_g.reshape(L, 1, WIDTH), ln_b=ln_b.reshape(L, 1, WIDTH),
        w2=w_m.reshape(L, 4, 256, CHUNK).astype(BF16),
        wt2=jnp.swapaxes(w_m, -1, -2).reshape(L, 4, 256, CHUNK).astype(BF16),
        bst=jnp.repeat(jnp.swapaxes(b_s, -1, -2), 64, axis=-1),
        cb=conv_b.reshape(L, 1, WIDTH), wpool=w_pool.astype(BF16), ps=pool_scale.reshape(L, 1, WIDTH),
        sel=(jnp.arange(WIDTH)[:, None] // 64 == jnp.arange(128)[None, :]).astype(F32))
    norm_g3 = norm_g.reshape(L, 1, D_MODEL)

    cm = _Comm()
    for buf in (land_proj[0], land_wo[0], land_cw):
        cm.gather(cm.through(buf))
    cm.gather(cm.through(land_win[1]), 0, half_rows)
    (p0, h0), (g_proj0, g_wo0, g_cw, w_half1), _ = _f1(xs, norm_g3, w_all0, 0, comm=cm)
    lw["proj"] = [g_proj0, None]
    lw["wo"] = [g_wo0.reshape(D_MODEL, D_MODEL), None]
    lw["cw"] = g_cw.reshape(N_CHIPS, L, 8, 128).transpose(1, 2, 0, 3).reshape(L, 8, WIDTH)
    cm = _Comm()
    cm.gather(cm.through(w_half1), half_rows, half_rows)
    (ya0, yb0, yc0, mm0, x1), (w_all1,), _ = _f2(xs, p0, lw, 0, comm=cm)
    cm = _Comm()
    for buf in (land_proj[1], land_wo[1]):
        cm.gather(cm.through(buf))
    (p1, h1), (g_proj1, g_wo1), _ = _f1(x1, norm_g3, w_all1, 1, comm=cm)
    lw["proj"][1] = g_proj1
    lw["wo"][1] = g_wo1.reshape(D_MODEL, D_MODEL)
    (ya1, yb1, yc1, mm1, x2), _, _ = _f2(x1, p1, lw, 1)
    dxl, loss_blk, g_final = _loss_head(x2, tgt, final_g.reshape(1, D_MODEL))
    loss = lax.psum(loss_blk[0, 0], ("x", "y", "c"))

    by_chip = lambda gwp, gwo: (gwp, gwo.reshape(N_CHIPS, 256, D_MODEL))
    recv_like = lambda a: ((3,) + a.shape[1:], a.dtype)
    part_rows = (D_MODEL, 3 * WIDTH, 256)
    gwin, recv_win, part_proj, part_wo, recv_proj, recv_wo = ([None] * L for _ in range(6))

    def sums(l, half):
        return [_sum4(f"sum_w_in{l}", gwin[l], chip, recv_win[l], l, L, half[0]),
                _sum4(f"sum_proj{l}", part_proj[l], chip, recv_proj[l], l, L, half[1]),
                _sum4(f"sum_w_o{l}", part_wo[l], chip, recv_wo[l], l, L, half[2])]

    gwo1 = _gwo(mm1, dxl, 1)
    (dp1, gwp1, *sm1), _, _ = _b1(p1, dxl, ya1, yb1, yc1, lw, 1)
    (gwin[1],), _, _ = _b2b(h1, dp1, "b2b_l1", chip)
    cm = _Comm()
    cm.scatter(cm.read(gwin[1]), cm.new((3, D_MODEL, SHARD_W), BF16), 0, half_rows)
    (dxl, gng1), _, (recv_half,) = _b2a(dp1, w_all1, x1, dxl, norm_g3, 1, comm=cm)
    part_proj[1], part_wo[1] = by_chip(gwp1, gwo1)
    gwo0 = _gwo(mm0, dxl, 0)
    cm = _Comm()
    cm.scatter(cm.read(gwin[1]), cm.through(recv_half), half_rows, half_rows)
    cm.scatter(cm.read(part_proj[1]), cm.new(*recv_like(part_proj[1])))
    cm.scatter(cm.read(part_wo[1]), cm.new(*recv_like(part_wo[1])))
    (dp0, gwp0, *sm0), (recv_win[1],), (recv_proj[1], recv_wo[1]) = _b1(p0, dxl, ya0, yb0, yc0, lw, 0, comm=cm)
    half = sums(1, [None, None, None])
    part_proj[0], part_wo[0] = by_chip(gwp0, gwo0)

    gpack = _pack_small("pack_small", [sm0, sm1], gng1, g_final)
    slice_rows = gpack.shape[0] // 8

    cm = _Comm()
    cm.scatter(cm.read(part_proj[0]), cm.new(*recv_like(part_proj[0])))
    cm.scatter(cm.read(part_wo[0]), cm.new(*recv_like(part_wo[0])))
    for a in range(3):
        cm.sibling(cm.read(half[a]), cm.new(half[a].shape, F32), part_rows[a], part_rows[a])
    cm.slices_out(cm.read(gpack), cm.new((7, slice_rows, 128), F32), slice_rows)
    (g_others,), _, (recv_proj[0], recv_wo[0], *other, slices_in) = _b2b(h0, dp0, "b2b_l0_others", chip, "others",
                                                                        comm=cm)
    me = (4 * lax.axis_index("x") + 2 * lax.axis_index("y") + lax.axis_index("c")).astype(jnp.int32).reshape(1)
    gsum = _sum8("sum_small", gpack, me, slices_in)
    early_rows = D_MODEL // 4
    cm = _Comm()
    cm.scatter(cm.read(g_others), cm.new((3, D_MODEL, SHARD_W), BF16), 0, early_rows)
    cm.slices_back(cm.through(gsum), slice_rows)
    (gwin[0],), (gred,), (recv_early,) = _b2b(h0, dp0, "b2b_l0_own", chip, "own", comm=cm)
    cm = _Comm()
    cm.scatter(cm.read(g_others), cm.through(recv_early), early_rows, D_MODEL - early_rows)
    (dxl, gng0), (recv_win[0],), _ = _b2a(dp0, w_all0, xs, dxl, norm_g3, 0, comm=cm)
    grad_x = dxl.reshape(1, s, D_MODEL)

    half = sums(0, half)
    cm = _Comm()
    for a in range(3):
        cm.sibling(cm.read(half[a]), cm.through(other[a]), 0, part_rows[a])
    other = _comm_only("swap_halves0", cm)
    gng0 = _all_reduce_small("all_reduce_norm_g0", gng0.reshape(8, 128))
    gred = lax.dynamic_update_slice(gred, gng0, (0, 0))

    outs = {}
    shard2d = dict(w_in=(L * D_MODEL, SHARD_W), w_o=(L * 256, D_MODEL))
    for a, name in ((0, "w_in"), (2, "w_o")):
        r2 = shard2d[name]
        res = _adamw(f"adamw_{name}", W[name].reshape(r2), M[name].reshape(r2), Vv[name].reshape(r2),
                     [half[a], other[a]])
        outs[name] = [o.reshape(W[name].shape) for o in res]
    res = _adamw("adamw_proj", pcat(W), pcat(M), pcat(Vv), [half[1], other[1]])
    for i, name in enumerate(("w_pa", "w_pb", "w_pc")):
        outs[name] = [o.reshape(L, 3, WIDTH, 256)[:, i] for o in res]

    as_rows = lambda d: {nm: (d[nm].reshape(1, D_MODEL) if nm == "final_g" else d[nm]) for nm in _SMALL}
    outs.update(_adamw_small("adamw_small", gred, chip, as_rows(W), as_rows(M), as_rows(Vv)))
    outs["final_g"] = [o.reshape(D_MODEL) for o in outs["final_g"]]

    order = ("norm_g", "w_in", "ln_g", "ln_b", "w_s", "b_s", "conv_w", "conv_b", "w_pool", "pool_scale",
             "w_pa", "w_pb", "w_pc", "w_o", "final_g")
    return (loss, grad_x, *[outs[nm][0] for nm in order], *[outs[nm][1] for nm in order],
            *[outs[nm][2] for nm in order], *[outs[nm][3] for nm in order])
```

```python
import functools

import jax
import jax.numpy as jnp
from jax import lax
from jax.experimental import pallas as pl
from jax.experimental.pallas import tpu as pltpu

F32 = jnp.float32
BF16 = jnp.bfloat16

D_MODEL = 1024
DEPTH = 2
CHUNK = 128
WIDTH = 512
POOL_WINDOWS = (2, 4, 8, 16)
IN_TOTAL = 7680
N_CHIPS = 4
SHARD_W = IN_TOTAL // N_CHIPS
RMS_EPS = 1e-6
LN_EPS = 1e-5
HALO = 16

U, V, ZA, XB, BG, CG, ZB, XC, ZC, GA, GB, GC = (0, 512, 1024, 1536, 2048, 2560, 3072, 3584, 4096, 4608, 5632, 6656)

ADAM_LR = 0.001
ADAM_B1 = 0.9
ADAM_B2 = 0.999
ADAM_EPS = 1e-08
ADAM_WD = 0.01
ADAM_STEP = 10

VMEM_LIMIT = 56 * 1024 * 1024
MESH = pl.DeviceIdType.MESH
ANY = pl.BlockSpec(memory_space=pl.ANY)
NT = (((1,), (1,)), ((), ()))
TN = (((0,), (0,)), ((), ()))


def _params(sem=None):
    kw = dict(vmem_limit_bytes=VMEM_LIMIT)
    if sem is not None:
        kw["dimension_semantics"] = sem
    return pltpu.CompilerParams(**kw)


def _dot(a, b):
    return jnp.dot(a, b, preferred_element_type=F32)


def _dotg(a, b, dims):
    return lax.dot_general(a, b, dims, preferred_element_type=F32)


def _sigmoid(x):
    return 1.0 / (1.0 + jnp.exp(-x))


_GELU_K = 0.7978845608028654


def _gelu(x):
    th = jnp.tanh(_GELU_K * (x + 0.044715 * (x * x * x)))
    return 0.5 * x * (1.0 + th), th


def _gelu_grad(x, th):
    return 0.5 * (1.0 + th) + 0.5 * x * (1.0 - th * th) * (_GELU_K * (1.0 + 3.0 * 0.044715 * (x * x)))


def _colsum8(x):
    t, c = x.shape
    return jnp.sum(x.reshape(t // 8, 8, c), axis=0)


def _branches_fwd(p_ref, hxb_ref, hcg_ref, hxc_ref, first, tstart, w, sg_scr):
    t = p_ref.shape[0]
    nch = t // CHUNK

    def seg(o, width=WIDTH):
        return p_ref[:, o:o + width].astype(F32)

    lo = lax.broadcasted_iota(jnp.int32, (CHUNK, CHUNK), 1) < 64
    r = {}
    pu = seg(U)
    u_act, th_u = _gelu(pu)
    pv = seg(V)
    vg, th_v = _gelu(pv)
    mu = jnp.mean(vg, axis=-1, keepdims=True)
    xc = vg - mu
    var = jnp.mean(xc * xc, axis=-1, keepdims=True)
    rs = lax.rsqrt(var + LN_EPS)
    vhat = xc * rs
    vn = vhat * w["ln_g"][...] + w["ln_b"][...]
    vnb = vn.astype(BF16)
    for n in range(nch):
        for j in range(4):
            vb = vnb[n * CHUNK:(n + 1) * CHUNK, j * 128:(j + 1) * 128]
            z = _dot(w["w2"][j], vb)
            sg_scr[n * CHUNK:(n + 1) * CHUNK, j * 128:(j + 1) * 128] = (
                jnp.where(lo, z[:CHUNK], z[CHUNK:]) + w["bst"][:, j * 128:(j + 1) * 128])
    sg = sg_scr[...]
    a_br = u_act * sg
    za = seg(ZA)
    sa = _sigmoid(za)
    r.update(pu=pu, th_u=th_u, pv=pv, th_v=th_v, rs=rs, vhat=vhat, vnb=vnb, u_act=u_act, sg=sg,
             a_br=a_br, za=za, sa=sa, a_in=a_br * (za * sa))

    xb = seg(XB)
    cg = seg(CG)
    yb0 = cg * xb
    hal = hcg_ref[...].astype(F32) * hxb_ref[...].astype(F32)
    hal = jnp.where(first, 0.0, hal)
    ext = jnp.concatenate([hal, yb0], axis=0)
    y1 = pltpu.roll(ext, 1, 0)[HALO:]
    y2 = pltpu.roll(ext, 2, 0)[HALO:]
    cw = w["cw"]
    conv = cw[0:1, :] * y2 + cw[1:2, :] * y1 + cw[2:3, :] * yb0 + w["cb"][...]
    bg = seg(BG)
    b_br = bg * conv
    zb = seg(ZB)
    sb = _sigmoid(zb)
    r.update(xb=xb, cg=cg, yb0=yb0, y1=y1, y2=y2, conv=conv, bg=bg, b_br=b_br, zb=zb, sb=sb,
             b_in=b_br * (zb * sb))

    xcv = seg(XC)
    hxc = jnp.where(first, 0.0, hxc_ref[...].astype(F32))
    extc = jnp.concatenate([hxc, xcv], axis=0)
    tpos = tstart + lax.broadcasted_iota(jnp.int32, (t, 1), 0) + 1
    pooled, inv, q = [], [], []
    for gi, win in enumerate(POOL_WINDOWS):
        s = extc[:, gi * 128:(gi + 1) * 128]
        sh = 1
        while sh < win:
            s = s + pltpu.roll(s, sh, 0)
            sh *= 2
        inv_g = jnp.where(tpos >= win, 1.0 / win, 1.0 / jnp.minimum(tpos, win).astype(F32))
        pg = s[HALO:] * inv_g - xcv[:, gi * 128:(gi + 1) * 128]
        pooled.append(pg)
        inv.append(inv_g)
        q.append(_dot(pg.astype(BF16), w["wpool"][gi]))
    qv = jnp.concatenate(q, axis=1)
    c_br = qv * w["ps"][...]
    zc = seg(ZC)
    sc = _sigmoid(zc)
    r.update(pooled=pooled, inv=inv, q=qv, c_br=c_br, zc=zc, sc=sc, c_in=c_br * (zc * sc))
    return r


def _halo_specs(t, rev_n=None):
    def imap(col):
        def f(i):
            ti = i if rev_n is None else rev_n - 1 - i
            return (jnp.maximum(ti * (t // HALO) - 1, 0), col)
        return f
    return [pl.BlockSpec((HALO, WIDTH), imap(XB // WIDTH)),
            pl.BlockSpec((HALO, WIDTH), imap(CG // WIDTH)),
            pl.BlockSpec((HALO, WIDTH), imap(XC // WIDTH))]


def _const_spec(shape):
    nd = len(shape)
    return pl.BlockSpec(shape, lambda *_: (0,) * nd)


def _place():
    return lax.axis_index("x"), lax.axis_index("y"), lax.axis_index("c")


def _chip_peer(x, y, jm):
    px = (1 - x) if (jm & 2) else x
    py = (1 - y) if (jm & 1) else y
    return px, py


def _chip_part(buf, k, r0=0, nr=None):
    if len(buf.shape) == 3:
        return buf.at[k] if nr is None else buf.at[k, pl.ds(r0, nr)]
    cols = buf.shape[1] // N_CHIPS
    rows = pl.ds(0, buf.shape[0]) if nr is None else pl.ds(r0, nr)
    return buf.at[rows, pl.ds(pl.multiple_of(k * cols, 128), cols)]


def _row_range(ref, r0, nr):
    return ref if nr is None else ref.at[pl.ds(r0, nr)]


class _Comm:
    def __init__(self):
        self.ins, self.thru, self.fresh, self.moves, self.n = [], [], [], [], 0

    def read(self, arr):
        self.ins.append(arr)
        return ("ins", len(self.ins) - 1)

    def through(self, arr):
        self.thru.append(arr)
        return ("thru", len(self.thru) - 1)

    def new(self, shape, dtype):
        self.fresh.append(jax.ShapeDtypeStruct(tuple(shape), dtype))
        return ("fresh", len(self.fresh) - 1)

    def _add(self, n, *move):
        self.moves.append(move)
        self.n += n

    def gather(self, buf, r0=0, nr=None):
        self._add(3, "gather", buf, r0, nr)

    def scatter(self, src, dst, r0=0, nr=None):
        self._add(3, "scatter", src, dst, r0, nr)

    def sibling(self, src, dst, r0=0, nr=None):
        self._add(1, "sibling", src, dst, r0, nr)

    def gather_half(self, buf):
        self._add(3, "gather_half", buf)

    def forward_half(self, buf):
        self._add(3, "forward_half", buf)

    def slices_out(self, src, dst, rows):
        self._add(7, "slices_out", src, dst, rows)

    def slices_back(self, buf, rows):
        self._add(7, "slices_back", buf, rows)

    def copies(self, bufs, x, y, c):
        ref = lambda h: bufs[h[0]][h[1]]
        k_me = 2 * x + y
        me = 4 * x + 2 * y + c
        cps = []
        for move in self.moves:
            kind = move[0]
            if kind in ("gather", "scatter"):
                for jj, jm in enumerate((1, 2, 3)):
                    px, py = _chip_peer(x, y, jm)
                    k_peer = 2 * px + py
                    if kind == "gather":
                        _, buf, r0, nr = move
                        mine = _chip_part(ref(buf), k_me, r0, nr)
                        cps.append((mine, mine, _chip_part(ref(buf), k_peer, r0, nr), (px, py, c)))
                    else:
                        _, src, dst, r0, nr = move
                        slot = ref(dst).at[jj] if nr is None else ref(dst).at[jj, pl.ds(r0, nr)]
                        cps.append((_chip_part(ref(src), k_peer, r0, nr), slot, slot, (px, py, c)))
            elif kind in ("gather_half", "forward_half"):
                buf = ref(move[1])
                hr = buf.shape[-2] // 2
                mine_r0, other_r0 = pl.multiple_of(c * hr, 16), pl.multiple_of((1 - c) * hr, 16)
                for jm in (1, 2, 3):
                    px, py = _chip_peer(x, y, jm)
                    k_peer = 2 * px + py
                    if kind == "gather_half":
                        part = _chip_part(buf, k_me, mine_r0, hr)
                        cps.append((part, part, _chip_part(buf, k_peer, mine_r0, hr), (px, py, c)))
                    else:
                        part = _chip_part(buf, k_peer, mine_r0, hr)
                        cps.append((part, part, _chip_part(buf, k_peer, other_r0, hr), (x, y, 1 - c)))
            elif kind == "sibling":
                _, src, dst, r0, nr = move
                land = _row_range(ref(dst), r0, nr)
                cps.append((_row_range(ref(src), r0, nr), land, land, (x, y, 1 - c)))
            else:
                for j in range(1, 8):
                    px = (1 - x) if (j & 4) else x
                    py = (1 - y) if (j & 2) else y
                    pc = (1 - c) if (j & 1) else c
                    peer = 4 * px + 2 * py + pc
                    if kind == "slices_out":
                        _, src, dst, rows = move
                        slot = ref(dst).at[j - 1]
                        cps.append((ref(src).at[pl.ds(pl.multiple_of(peer * rows, 8), rows)], slot, slot,
                                    (px, py, pc)))
                    else:
                        _, buf, rows = move
                        mine = ref(buf).at[pl.ds(pl.multiple_of(me * rows, 8), rows)]
                        cps.append((mine, mine, ref(buf).at[pl.ds(pl.multiple_of(peer * rows, 8), rows)],
                                    (px, py, pc)))
        assert len(cps) == self.n
        return cps


def _pcall(body, *, name, grid, in_specs, out_specs, out_shape, scratch, sem, args, comm=None, prefetch=(),
           aliases=None):
    n_pre = len(prefetch)
    n_in, n_out, n_scr = len(in_specs), len(out_specs), len(scratch)
    io_alias = {n_pre + i: o for i, o in (aliases or {}).items()}
    if comm is None or comm.n == 0:
        gs = pltpu.PrefetchScalarGridSpec(num_scalar_prefetch=n_pre, grid=grid, in_specs=list(in_specs),
                                          out_specs=list(out_specs), scratch_shapes=list(scratch))
        outs = pl.pallas_call(body, name=name, grid_spec=gs, out_shape=list(out_shape), input_output_aliases=io_alias,
                              compiler_params=_params(sem))(*prefetch, *args)
        return list(outs), [], []
    n_ci, n_ct, n_cf, n_cp = len(comm.ins), len(comm.thru), len(comm.fresh), comm.n

    def wrapped(*refs):
        pos = 0
        def take(n):
            nonlocal pos
            got = refs[pos:pos + n]
            pos += n
            return got
        pre = take(n_pre)
        a, ci, _ = take(n_in), take(n_ci), take(n_ct)
        o, ct, cf = take(n_out), take(n_ct), take(n_cf)
        scr = take(n_scr)
        send_sems, recv_sems = take(2)
        first = functools.reduce(jnp.logical_and, [pl.program_id(d) == 0 for d in range(len(grid))])
        last = functools.reduce(jnp.logical_and, [pl.program_id(d) == grid[d] - 1 for d in range(len(grid))])
        x, y, c = _place()
        cps = comm.copies(dict(ins=ci, thru=ct, fresh=cf), x, y, c)

        def copy(i, src, dst, dev):
            return pltpu.make_async_remote_copy(src_ref=src, dst_ref=dst, send_sem=send_sems.at[i],
                                                recv_sem=recv_sems.at[i], device_id=dev, device_id_type=MESH)

        @pl.when(first)
        def _():
            for i, (src, dst, _, dev) in enumerate(cps):
                copy(i, src, dst, dev).start()

        body(*pre, *a, *o, *scr)

        @pl.when(last)
        def _():
            for i, (src, _, land, dev) in enumerate(cps):
                copy(i, src, land, dev).wait_recv()
            for i, (src, dst, _, dev) in enumerate(cps):
                copy(i, src, dst, dev).wait_send()

    thru_shapes = [jax.ShapeDtypeStruct(t.shape, t.dtype) for t in comm.thru]
    gs = pltpu.PrefetchScalarGridSpec(
        num_scalar_prefetch=n_pre, grid=grid,
        in_specs=list(in_specs) + [ANY] * (n_ci + n_ct),
        out_specs=list(out_specs) + [ANY] * (n_ct + n_cf),
        scratch_shapes=list(scratch) + [pltpu.SemaphoreType.DMA((n_cp,)), pltpu.SemaphoreType.DMA((n_cp,))])
    outs = pl.pallas_call(
        wrapped, name=name, grid_spec=gs,
        out_shape=list(out_shape) + thru_shapes + comm.fresh,
        input_output_aliases={**io_alias, **{n_pre + n_in + n_ci + t: n_out + t for t in range(n_ct)}},
        compiler_params=_params(sem),
    )(*prefetch, *args, *comm.ins, *comm.thru)
    outs = list(outs)
    return outs[:n_out], outs[n_out:n_out + n_ct], outs[n_out + n_ct:]


def _comm_only(name, comm):
    def body():
        pass
    _, thru, _ = _pcall(body, name=name, grid=(1,), in_specs=[], out_specs=[], out_shape=[], scratch=[],
                        sem=("arbitrary",), args=[], comm=comm)
    return thru


def _f1(x, norm_g3, w_all, l, comm=None, tm=256):
    s = x.shape[0]

    def body(x_ref, g_ref, w_hbm, p_ref, h_ref, w_vmem):
        @pl.when(pl.program_id(0) == 0)
        def _():
            pltpu.sync_copy(w_hbm, w_vmem)
        xv = x_ref[...]
        r = lax.rsqrt(jnp.mean(xv * xv, axis=-1, keepdims=True) + RMS_EPS)
        hb = ((xv * r) * g_ref[...]).astype(BF16)
        h_ref[...] = hb
        p_ref[...] = _dot(hb, w_vmem[...]).astype(BF16)

    return _pcall(
        body, name=f"f1_l{l}", grid=(s // tm,),
        in_specs=[pl.BlockSpec((tm, D_MODEL), lambda i: (i, 0)),
                  pl.BlockSpec((None, 1, D_MODEL), lambda i: (l, 0, 0)), ANY],
        out_specs=[pl.BlockSpec((tm, IN_TOTAL), lambda i: (i, 0)),
                   pl.BlockSpec((tm, D_MODEL), lambda i: (i, 0))],
        out_shape=[jax.ShapeDtypeStruct((s, IN_TOTAL), BF16), jax.ShapeDtypeStruct((s, D_MODEL), BF16)],
        scratch=[pltpu.VMEM((D_MODEL, IN_TOTAL), BF16)], sem=("arbitrary",), args=[x, norm_g3, w_all], comm=comm)


def _f1_own(x, norm_g3, w_in, chip, comm, tm=256):
    s = x.shape[0]

    def body(idx_ref, x_ref, g_ref, w_hbm, p_ref, h_ref, w_f32, w_vmem):
        @pl.when(pl.program_id(0) == 0)
        def _():
            pltpu.sync_copy(w_hbm.at[0], w_f32)
            w_vmem[...] = w_f32[...].astype(BF16)
        xv = x_ref[...]
        r = lax.rsqrt(jnp.mean(xv * xv, axis=-1, keepdims=True) + RMS_EPS)
        hb = ((xv * r) * g_ref[...]).astype(BF16)
        h_ref[...] = hb
        p_ref[...] = _dot(hb, w_vmem[...]).astype(BF16)

    return _pcall(
        body, name="f1_l0_own", grid=(s // tm,),
        in_specs=[pl.BlockSpec((tm, D_MODEL), lambda i, idx: (i, 0)),
                  pl.BlockSpec((None, 1, D_MODEL), lambda i, idx: (0, 0, 0)), ANY],
        out_specs=[pl.BlockSpec((tm, SHARD_W), lambda i, idx: (i, idx[0])),
                   pl.BlockSpec((tm, D_MODEL), lambda i, idx: (i, 0))],
        out_shape=[jax.ShapeDtypeStruct((s, IN_TOTAL), BF16), jax.ShapeDtypeStruct((s, D_MODEL), BF16)],
        scratch=[pltpu.VMEM((D_MODEL, SHARD_W), F32), pltpu.VMEM((D_MODEL, SHARD_W), BF16)],
        sem=("arbitrary",), args=[x, norm_g3, w_in], comm=comm, prefetch=[chip])


def _f1_others(h, p, w_all, chip, comm, tm=256):
    s = h.shape[0]
    cols_of = lambda j, idx: (idx[0] + 1 + j) % N_CHIPS

    def body(idx_ref, h_ref, w_hbm, _, p_ref, w_vmem, sems):
        j = pl.program_id(1)

        @pl.when(jnp.logical_and(pl.program_id(0) == 0, j == 0))
        def _():
            loads = [pltpu.make_async_copy(
                w_hbm.at[:, pl.ds(pl.multiple_of(((idx_ref[0] + 1 + q) % N_CHIPS) * SHARD_W, 128), SHARD_W)],
                w_vmem.at[q], sems.at[q]) for q in range(N_CHIPS - 1)]
            for cp in loads:
                cp.start()
            for cp in loads:
                cp.wait()

        p_ref[...] = _dot(h_ref[...], w_vmem[j]).astype(BF16)

    return _pcall(
        body, name="f1_l0_others", grid=(s // tm, N_CHIPS - 1),
        in_specs=[pl.BlockSpec((tm, D_MODEL), lambda i, j, idx: (i, 0)), ANY, ANY],
        out_specs=[pl.BlockSpec((tm, SHARD_W), lambda i, j, idx: (i, cols_of(j, idx)))],
        out_shape=[jax.ShapeDtypeStruct((s, IN_TOTAL), BF16)],
        scratch=[pltpu.VMEM((N_CHIPS - 1, D_MODEL, SHARD_W), BF16), pltpu.SemaphoreType.DMA((N_CHIPS - 1,))],
        sem=("arbitrary", "arbitrary"), args=[h, w_all, p], comm=comm, prefetch=[chip], aliases={2: 0})


def _f2(x, p, lw, l, comm=None, t=256):
    s = x.shape[0]
    n = s // t

    def body(p_ref, hxb_ref, hcg_ref, hxc_ref, x_ref, lng, lnb, w2, bst, cw, cb, wpool, ps,
             proj, wo, ya_ref, yb_ref, yc_ref, m_ref, xo_ref, sg_scr):
        i = pl.program_id(0)
        w = dict(ln_g=lng, ln_b=lnb, w2=w2, bst=bst, cw=cw, cb=cb, wpool=wpool, ps=ps)
        r = _branches_fwd(p_ref, hxb_ref, hcg_ref, hxc_ref, i == 0, i * t, w, sg_scr)

        def project(act, b):
            ab = act.astype(BF16)
            return jnp.concatenate([_dot(ab, proj[k, b * WIDTH:(b + 1) * WIDTH, :]) for k in range(N_CHIPS)], axis=1)

        ya = project(r["a_in"], 0)
        yb = project(r["b_in"], 1)
        yc = project(r["c_in"], 2)
        ya_ref[...] = ya.astype(BF16)
        yb_ref[...] = yb.astype(BF16)
        yc_ref[...] = yc.astype(BF16)
        m = (_sigmoid(p_ref[:, GA:GA + D_MODEL].astype(F32)) * ya
             + _sigmoid(p_ref[:, GB:GB + D_MODEL].astype(F32)) * yb
             + _sigmoid(p_ref[:, GC:GC + D_MODEL].astype(F32)) * yc)
        mb = m.astype(BF16)
        m_ref[...] = mb
        xo_ref[...] = x_ref[...] + _dot(mb, wo[...])

    tile = lambda c: pl.BlockSpec((t, c), lambda i: (i, 0))
    lsel = lambda *blk: pl.BlockSpec((None,) + blk, lambda i: (l,) + (0,) * len(blk))
    act = jax.ShapeDtypeStruct((s, D_MODEL), BF16)
    return _pcall(
        body, name=f"f2_l{l}", grid=(n,),
        in_specs=[tile(IN_TOTAL)] + _halo_specs(t) + [
            tile(D_MODEL), lsel(1, WIDTH), lsel(1, WIDTH), lsel(4, 256, 128), lsel(CHUNK, WIDTH),
            lsel(8, WIDTH), lsel(1, WIDTH), lsel(4, 128, 128), lsel(1, WIDTH),
            _const_spec((N_CHIPS, 3 * WIDTH, 256)), _const_spec((D_MODEL, D_MODEL))],
        out_specs=[tile(D_MODEL)] * 5,
        out_shape=[act, act, act, act, jax.ShapeDtypeStruct((s, D_MODEL), F32)],
        scratch=[pltpu.VMEM((t, WIDTH), F32)], sem=("arbitrary",),
        args=[p, p, p, p, x, lw["ln_g"], lw["ln_b"], lw["w2"], lw["bst"], lw["cw"], lw["cb"], lw["wpool"], lw["ps"],
              lw["proj"][l], lw["wo"][l]], comm=comm)


def _b1(p, dout, ya, yb, yc, lw, l, comm=None, t=256):
    s = p.shape[0]
    n = s // t
    nch = t // CHUNK

    def body(p_ref, hxb_ref, hcg_ref, hxc_ref, do_ref, ya_ref, yb_ref, yc_ref,
             lng, lnb, w2, wt2, bst, cw, cb, wpool, ps, proj_h, wo_h, sel_ref,
             dp_ref, gwp_h, gws_ref, gbs_ref, glng_ref, glnb_ref, gcw_ref, gcb_ref,
             gwpool_ref, gps_ref,
             wpa, wpb, wpc, wo, gwpa, gwpb, gwpc, gbs_acc, vec_acc, sg_scr, dvn_scr, car_dc, car_e, psem):
        i = pl.program_id(0)
        ti = n - 1 - i

        def by_chip_copies(vmem_bufs, hbm, to_hbm):
            cps = []
            for b, buf in enumerate(vmem_bufs):
                for k in range(N_CHIPS):
                    v = buf.at[:, pl.ds(k * 256, 256)]
                    h = hbm.at[k, pl.ds(b * WIDTH, WIDTH)]
                    cps.append(pltpu.make_async_copy(v, h, psem.at[b * N_CHIPS + k]) if to_hbm
                               else pltpu.make_async_copy(h, v, psem.at[b * N_CHIPS + k]))
            return cps

        @pl.when(i == 0)
        def _():
            loads = by_chip_copies((wpa, wpb, wpc), proj_h, False)
            for cp in loads:
                cp.start()
            pltpu.sync_copy(wo_h, wo)
            for cp in loads:
                cp.wait()
            for acc in (gwpa, gwpb, gwpc, gbs_acc, vec_acc, car_dc, car_e):
                acc[...] = jnp.zeros(acc.shape, acc.dtype)
            gws_ref[...] = jnp.zeros(gws_ref.shape, F32)
            gwpool_ref[...] = jnp.zeros(gwpool_ref.shape, F32)

        w = dict(ln_g=lng, ln_b=lnb, w2=w2, bst=bst, cw=cw, cb=cb, wpool=wpool, ps=ps)
        r = _branches_fwd(p_ref, hxb_ref, hcg_ref, hxc_ref, ti == 0, ti * t, w, sg_scr)

        def seg(o, width=WIDTH):
            return p_ref[:, o:o + width].astype(F32)

        def put(o, val):
            dp_ref[:, o:o + val.shape[1]] = val.astype(BF16)

        dob = do_ref[...].astype(BF16)
        dm = _dotg(dob, wo[...], NT)

        def merge_bwd(goff, y_ref, xin, wp, gwp):
            sx = _sigmoid(seg(goff, D_MODEL))
            dmy = dm * sx
            put(goff, dmy * y_ref[...].astype(F32) * (1.0 - sx))
            dyb = dmy.astype(BF16)
            gwp[...] += _dotg(xin.astype(BF16), dyb, TN)
            return _dotg(dyb, wp[...], NT)

        d_ain = merge_bwd(GA, ya_ref, r["a_in"], wpa, gwpa)
        d_bin = merge_bwd(GB, yb_ref, r["b_in"], wpb, gwpb)
        d_cin = merge_bwd(GC, yc_ref, r["c_in"], wpc, gwpc)

        def dsilu(z, sz):
            return sz * (1.0 + z * (1.0 - sz))

        za, sa = r["za"], r["sa"]
        d_abr = d_ain * (za * sa)
        put(ZA, d_ain * r["a_br"] * dsilu(za, sa))
        put(U, d_abr * r["sg"] * _gelu_grad(r["pu"], r["th_u"]))
        d_sg = d_abr * r["u_act"]
        dsgb = d_sg.astype(BF16)
        lo = lax.broadcasted_iota(jnp.int32, (CHUNK, CHUNK), 1) < 64
        zero = jnp.zeros((CHUNK, CHUNK), BF16)
        for c in range(nch):
            rows = slice(c * CHUNK, (c + 1) * CHUNK)
            gbs_acc[...] += d_sg[rows]
            for j in range(4):
                cols = slice(j * 128, (j + 1) * 128)
                dj = dsgb[rows, cols]
                zt = _dot(wt2[j], dj)
                dvn_scr[rows, cols] = jnp.where(lo, zt[:CHUNK], zt[CHUNK:])
                stacked = jnp.concatenate([jnp.where(lo, dj, zero), jnp.where(lo, zero, dj)], axis=0)
                gws_ref[j] += _dotg(stacked, r["vnb"][rows, cols], NT)
        d_vn = dvn_scr[...]
        vhat = r["vhat"]
        vec_acc[0] += _colsum8(d_vn * vhat)
        vec_acc[1] += _colsum8(d_vn)
        d_vhat = d_vn * lng[...]
        d_vg = r["rs"] * (d_vhat - jnp.mean(d_vhat, axis=-1, keepdims=True)
                          - vhat * jnp.mean(d_vhat * vhat, axis=-1, keepdims=True))
        put(V, d_vg * _gelu_grad(r["pv"], r["th_v"]))

        zb, sb = r["zb"], r["sb"]
        d_bbr = d_bin * (zb * sb)
        put(ZB, d_bin * r["b_br"] * dsilu(zb, sb))
        put(BG, d_bbr * r["conv"])
        dc = d_bbr * r["bg"]
        vec_acc[2] += _colsum8(dc)
        vec_acc[3] += _colsum8(dc * r["y2"])
        vec_acc[4] += _colsum8(dc * r["y1"])
        vec_acc[5] += _colsum8(dc * r["yb0"])
        ext = jnp.concatenate([dc, car_dc[...]], axis=0)
        ne = t + HALO
        d1 = pltpu.roll(ext, ne - 1, 0)[:t]
        d2 = pltpu.roll(ext, ne - 2, 0)[:t]
        d_yb0 = cw[2:3, :] * dc + cw[1:2, :] * d1 + cw[0:1, :] * d2
        put(CG, d_yb0 * r["xb"])
        put(XB, d_yb0 * r["cg"])
        car_dc[...] = dc[:HALO]

        zc, sc = r["zc"], r["sc"]
        d_cbr = d_cin * (zc * sc)
        put(ZC, d_cin * r["c_br"] * dsilu(zc, sc))
        vec_acc[6] += _colsum8(d_cbr * r["q"])
        d_q = d_cbr * ps[...]
        for gi, win in enumerate(POOL_WINDOWS):
            cols = slice(gi * 128, (gi + 1) * 128)
            dqb = d_q[:, cols].astype(BF16)
            d_pool = _dotg(dqb, wpool[gi], NT)
            gwpool_ref[gi] += _dotg(r["pooled"][gi].astype(BF16), dqb, TN)
            e = d_pool * r["inv"][gi]
            sx = jnp.concatenate([e, car_e[:, cols]], axis=0)
            sh = 1
            while sh < win:
                sx = sx + pltpu.roll(sx, ne - sh, 0)
                sh *= 2
            put(XC + gi * 128, sx[:t] - d_pool)
            car_e[:, cols] = e[:HALO]

        @pl.when(i == n - 1)
        def _():
            for acc, stage in ((gwpa, wpa), (gwpb, wpb), (gwpc, wpc)):
                stage[...] = acc[...].astype(BF16)
            stores = by_chip_copies((wpa, wpb, wpc), gwp_h, True)
            for cp in stores:
                cp.start()
            for cp in stores:
                cp.wait()
            gbs_ref[...] = jnp.dot(gbs_acc[...], sel_ref[...], preferred_element_type=F32,
                                   precision=lax.Precision.HIGHEST)
            red = lambda k: jnp.sum(vec_acc[k], axis=0, keepdims=True)
            glng_ref[...] = red(0)
            glnb_ref[...] = red(1)
            gcb_ref[...] = red(2)
            gcw_ref[...] = jnp.zeros(gcw_ref.shape, F32)
            for k in range(3):
                gcw_ref[k:k + 1, :] = red(3 + k)
            gps_ref[...] = red(6)
            tt = lax.broadcasted_iota(jnp.int32, (2 * CHUNK, CHUNK), 0) % CHUNK
            ss = lax.broadcasted_iota(jnp.int32, (2 * CHUNK, CHUNK), 1)
            for j in range(4):
                gws_ref[j] = jnp.where(tt >= ss, gws_ref[j], 0.0)

    rtile = lambda c: pl.BlockSpec((t, c), lambda i: (n - 1 - i, 0))
    lsel = lambda *blk: pl.BlockSpec((None,) + blk, lambda i: (l,) + (0,) * len(blk))
    f32s = lambda *shape: jax.ShapeDtypeStruct(shape, F32)
    return _pcall(
        body, name=f"b1_l{l}", grid=(n,),
        in_specs=[rtile(IN_TOTAL)] + _halo_specs(t, rev_n=n) + [rtile(D_MODEL)] * 4 + [
            lsel(1, WIDTH), lsel(1, WIDTH), lsel(4, 256, 128), lsel(4, 256, 128), lsel(CHUNK, WIDTH),
            lsel(8, WIDTH), lsel(1, WIDTH), lsel(4, 128, 128), lsel(1, WIDTH),
            ANY, ANY, _const_spec((WIDTH, 128))],
        out_specs=[rtile(IN_TOTAL), ANY,
                   _const_spec((4, 256, 128)), _const_spec((CHUNK, 128)), _const_spec((1, WIDTH)),
                   _const_spec((1, WIDTH)), _const_spec((8, WIDTH)), _const_spec((1, WIDTH)),
                   _const_spec((4, 128, 128)), _const_spec((1, WIDTH))],
        out_shape=[jax.ShapeDtypeStruct((s, IN_TOTAL), BF16), jax.ShapeDtypeStruct((N_CHIPS, 3 * WIDTH, 256), BF16),
                   f32s(4, 256, 128), f32s(CHUNK, 128),
                   f32s(1, WIDTH), f32s(1, WIDTH), f32s(8, WIDTH), f32s(1, WIDTH), f32s(4, 128, 128),
                   f32s(1, WIDTH)],
        scratch=[pltpu.VMEM((WIDTH, D_MODEL), BF16), pltpu.VMEM((WIDTH, D_MODEL), BF16),
                 pltpu.VMEM((WIDTH, D_MODEL), BF16), pltpu.VMEM((D_MODEL, D_MODEL), BF16),
                 pltpu.VMEM((WIDTH, D_MODEL), F32),
                 pltpu.VMEM((WIDTH, D_MODEL), F32), pltpu.VMEM((WIDTH, D_MODEL), F32),
                 pltpu.VMEM((CHUNK, WIDTH), F32), pltpu.VMEM((8, 8, WIDTH), F32),
                 pltpu.VMEM((t, WIDTH), F32), pltpu.VMEM((t, WIDTH), F32),
                 pltpu.VMEM((HALO, WIDTH), F32), pltpu.VMEM((HALO, WIDTH), F32),
                 pltpu.SemaphoreType.DMA((3 * N_CHIPS,))],
        sem=("arbitrary",),
        args=[p, p, p, p, dout, ya, yb, yc, lw["ln_g"], lw["ln_b"], lw["w2"], lw["wt2"], lw["bst"], lw["cw"],
              lw["cb"], lw["wpool"], lw["ps"], lw["proj"][l], lw["wo"][l], lw["sel"]], comm=comm)


def _rms_bwd(xv, g, dh):
    r = lax.rsqrt(jnp.mean(xv * xv, axis=-1, keepdims=True) + RMS_EPS)
    xhat = xv * r
    dxh = dh * g
    dx = r * (dxh - xhat * jnp.mean(dxh * xhat, axis=-1, keepdims=True))
    return dx, dh * xhat


def _b2a(dp, w_all, x, dout, norm_g3, l, comm=None, tm=256):
    s = x.shape[0]
    nm = s // tm

    def body(dp_ref, w_hbm, x_ref, do_ref, g_ref, dx_ref, gg_ref, w_vmem, gacc):
        i = pl.program_id(0)

        @pl.when(i == 0)
        def _():
            pltpu.sync_copy(w_hbm, w_vmem)
            gacc[...] = jnp.zeros(gacc.shape, F32)

        dh = _dotg(dp_ref[...], w_vmem[...], NT)
        dx, gx = _rms_bwd(x_ref[...], g_ref[...], dh)
        dx_ref[...] = do_ref[...] + dx
        gacc[...] += _colsum8(gx)

        @pl.when(i == nm - 1)
        def _():
            gg_ref[...] = jnp.sum(gacc[...], axis=0, keepdims=True)

    return _pcall(
        body, name=f"b2a_l{l}", grid=(nm,),
        in_specs=[pl.BlockSpec((tm, IN_TOTAL), lambda i: (i, 0)), ANY,
                  pl.BlockSpec((tm, D_MODEL), lambda i: (i, 0)),
                  pl.BlockSpec((tm, D_MODEL), lambda i: (i, 0)),
                  pl.BlockSpec((None, 1, D_MODEL), lambda i: (l, 0, 0))],
        out_specs=[pl.BlockSpec((tm, D_MODEL), lambda i: (i, 0)),
                   pl.BlockSpec((1, D_MODEL), lambda i: (0, 0))],
        out_shape=[jax.ShapeDtypeStruct((s, D_MODEL), F32), jax.ShapeDtypeStruct((1, D_MODEL), F32)],
        scratch=[pltpu.VMEM((D_MODEL, IN_TOTAL), BF16), pltpu.VMEM((8, D_MODEL), F32)],
        sem=("arbitrary",), args=[dp, w_all, x, dout, norm_g3], comm=comm)


def _b2b(h, dp, name, chip, part="all", comm=None, tk=1024):
    s = h.shape[0]
    nk = s // tk
    first, count = dict(all=(0, 4), others=(1, 3), own=(0, 1))[part]
    cols_of = lambda k, idx: (idx[0] + first + k) % N_CHIPS

    def body(idx_ref, h_ref, dp_ref, g_ref, acc):
        kk = pl.program_id(1)

        @pl.when(kk == 0)
        def _():
            acc[...] = jnp.zeros(acc.shape, F32)

        acc[...] += _dotg(h_ref[...], dp_ref[...], TN)

        @pl.when(kk == nk - 1)
        def _():
            g_ref[...] = acc[...].astype(BF16)

    own = part == "own"
    return _pcall(
        body, name=name, grid=(count, nk),
        in_specs=[pl.BlockSpec((tk, D_MODEL), lambda k, kk, idx: (kk, 0)),
                  pl.BlockSpec((tk, SHARD_W), lambda k, kk, idx: (kk, cols_of(k, idx)))],
        out_specs=[pl.BlockSpec((D_MODEL, SHARD_W), lambda k, kk, idx: (0, 0 if own else cols_of(k, idx)))],
        out_shape=[jax.ShapeDtypeStruct((D_MODEL, SHARD_W if own else IN_TOTAL), BF16)],
        scratch=[pltpu.VMEM((D_MODEL, SHARD_W), F32)],
        sem=("arbitrary", "arbitrary"), args=[h, dp], comm=comm, prefetch=[chip])


def _gwo(m, dout, l, tk=512):
    s = m.shape[0]
    nk = s // tk

    def body(m_ref, do_ref, g_ref, acc):
        kk = pl.program_id(0)
        prod = _dotg(m_ref[...], do_ref[...].astype(BF16), TN)

        @pl.when(kk == 0)
        def _():
            acc[...] = prod

        @pl.when(kk > 0)
        def _():
            acc[...] += prod

        @pl.when(kk == nk - 1)
        def _():
            g_ref[...] = acc[...].astype(BF16)

    return pl.pallas_call(
        body, name=f"gwo_l{l}", grid=(nk,),
        in_specs=[pl.BlockSpec((tk, D_MODEL), lambda kk: (kk, 0)), pl.BlockSpec((tk, D_MODEL), lambda kk: (kk, 0))],
        out_specs=_const_spec((D_MODEL, D_MODEL)),
        out_shape=jax.ShapeDtypeStruct((D_MODEL, D_MODEL), BF16),
        scratch_shapes=[pltpu.VMEM((D_MODEL, D_MODEL), F32)],
        compiler_params=_params(("arbitrary",)),
    )(m, dout)


def _loss_head(x, tgt, final_g2, tm=512):
    s = x.shape[0]
    nm = s // tm

    def body(x_ref, t_ref, g_ref, dx_ref, loss_ref, gg_ref, lacc, gacc):
        i = pl.program_id(0)

        @pl.when(i == 0)
        def _():
            lacc[...] = jnp.zeros(lacc.shape, F32)
            gacc[...] = jnp.zeros(gacc.shape, F32)

        xv = x_ref[...]
        g = g_ref[...]
        r = lax.rsqrt(jnp.mean(xv * xv, axis=-1, keepdims=True) + RMS_EPS)
        err = (xv * r) * g - t_ref[...]
        lacc[...] += _colsum8(err * err)
        dx, gx = _rms_bwd(xv, g, err * (1.0 / D_MODEL))
        dx_ref[...] = dx
        gacc[...] += _colsum8(gx)

        @pl.when(i == nm - 1)
        def _():
            tot = jnp.sum(jnp.sum(lacc[...], axis=0, keepdims=True), axis=1, keepdims=True)
            loss_ref[...] = jnp.broadcast_to(tot * (0.5 / D_MODEL), loss_ref.shape)
            gg_ref[...] = jnp.sum(gacc[...], axis=0, keepdims=True)

    return pl.pallas_call(
        body, name="loss_head", grid=(nm,),
        in_specs=[pl.BlockSpec((tm, D_MODEL), lambda i: (i, 0)), pl.BlockSpec((tm, D_MODEL), lambda i: (i, 0)),
                  _const_spec((1, D_MODEL))],
        out_specs=[pl.BlockSpec((tm, D_MODEL), lambda i: (i, 0)), _const_spec((8, 128)),
                   _const_spec((1, D_MODEL))],
        out_shape=[jax.ShapeDtypeStruct((s, D_MODEL), F32), jax.ShapeDtypeStruct((8, 128), F32),
                   jax.ShapeDtypeStruct((1, D_MODEL), F32)],
        scratch_shapes=[pltpu.VMEM((8, D_MODEL), F32), pltpu.VMEM((8, D_MODEL), F32)],
        compiler_params=_params(("arbitrary",)),
    )(x, tgt, final_g2)


def _row_block(rows, cols, n_arrays):
    budget = VMEM_LIMIT // 3 // (2 * 4 * n_arrays * cols)
    rb = rows
    while rb > budget and rb % 16 == 0:
        rb //= 2
    return rb


def _cast_slot(name, a, l, chip):
    _, rows, cols = a.shape
    rb = _row_block(rows, cols, 2)

    def body(idx_ref, a_ref, o_ref):
        o_ref[...] = a_ref[...].astype(BF16)

    gs = pltpu.PrefetchScalarGridSpec(
        num_scalar_prefetch=1, grid=(rows // rb,),
        in_specs=[pl.BlockSpec((None, rb, cols), lambda i, idx: (l, i, 0))],
        out_specs=pl.BlockSpec((None, rb, cols), lambda i, idx: (idx[0], i, 0)))
    return pl.pallas_call(body, name=name, grid_spec=gs, out_shape=jax.ShapeDtypeStruct((N_CHIPS, rows, cols), BF16),
                          compiler_params=_params(("arbitrary",)))(chip, a)


def _cast_cols(name, a, l, chip):
    _, rows, cols = a.shape
    rb = _row_block(rows, cols, 2)

    def body(idx_ref, a_ref, o_ref):
        o_ref[...] = a_ref[...].astype(BF16)

    gs = pltpu.PrefetchScalarGridSpec(
        num_scalar_prefetch=1, grid=(rows // rb,),
        in_specs=[pl.BlockSpec((None, rb, cols), lambda i, idx: (l, i, 0))],
        out_specs=pl.BlockSpec((rb, cols), lambda i, idx: (i, idx[0])))
    return pl.pallas_call(body, name=name, grid_spec=gs, out_shape=jax.ShapeDtypeStruct((rows, N_CHIPS * cols), BF16),
                          compiler_params=_params(("arbitrary",)))(chip, a)


def _sum4(name, own, chip, recv, l, n_layers, prev=None):
    _, rows, cols = recv.shape
    rb = _row_block(rows, cols, 5)
    nb = rows // rb
    if len(own.shape) == 3:
        own_spec = pl.BlockSpec((None, rb, cols), lambda i, idx: (idx[0], i, 0))
    elif own.shape[1] == cols:
        own_spec = pl.BlockSpec((rb, cols), lambda i, idx: (i, 0))
    else:
        own_spec = pl.BlockSpec((rb, cols), lambda i, idx: (i, idx[0]))

    def body(idx_ref, own_ref, r_ref, *rest):
        o_ref = rest[-1]
        o_ref[...] = ((own_ref[...].astype(F32) + r_ref[0].astype(F32)) + r_ref[1].astype(F32)) + r_ref[2].astype(F32)

    gs = pltpu.PrefetchScalarGridSpec(
        num_scalar_prefetch=1, grid=(nb,),
        in_specs=[own_spec,
                  pl.BlockSpec((3, rb, cols), lambda i, idx: (0, i, 0))] + ([ANY] if prev is not None else []),
        out_specs=pl.BlockSpec((rb, cols), lambda i, idx: (l * nb + i, 0)))
    args = (chip, own, recv) + ((prev,) if prev is not None else ())
    return pl.pallas_call(body, name=name, grid_spec=gs,
                          out_shape=jax.ShapeDtypeStruct((n_layers * rows, cols), F32),
                          input_output_aliases=({3: 0} if prev is not None else {}),
                          compiler_params=_params(("arbitrary",)))(*args)


def _sum8(name, pack, me, recv):
    rows = recv.shape[1]

    def body(idx_ref, own_ref, r_ref, o_ref):
        acc = own_ref[...]
        for j in range(7):
            acc = acc + r_ref[j]
        o_ref[...] = acc

    gs = pltpu.PrefetchScalarGridSpec(
        num_scalar_prefetch=1, grid=(1,),
        in_specs=[pl.BlockSpec((rows, 128), lambda i, idx: (idx[0], 0)),
                  pl.BlockSpec((7, rows, 128), lambda i, idx: (0, 0, 0))],
        out_specs=pl.BlockSpec((rows, 128), lambda i, idx: (idx[0], 0)))
    return pl.pallas_call(body, name=name, grid_spec=gs, out_shape=jax.ShapeDtypeStruct(pack.shape, F32),
                          compiler_params=_params(("arbitrary",)))(me, pack, recv)


_SMALL = ("norm_g", "ln_g", "ln_b", "w_s", "b_s", "conv_b", "w_pool", "pool_scale", "final_g", "conv_w")
_SMALL_SHAPES = dict(norm_g=(DEPTH, D_MODEL), ln_g=(DEPTH, WIDTH), ln_b=(DEPTH, WIDTH), w_s=(DEPTH, 8, CHUNK, CHUNK),
                     b_s=(DEPTH, 8, CHUNK), conv_b=(DEPTH, WIDTH), w_pool=(DEPTH, 4, 128, 128),
                     pool_scale=(DEPTH, WIDTH), final_g=(1, D_MODEL), conv_w=(DEPTH, 3, WIDTH))


def _small_rows():
    base, r = {}, 0
    for nm in _SMALL:
        base[nm] = r
        size = 1
        for d in _SMALL_SHAPES[nm]:
            size *= d
        r += size // 128
    return base, -(-r // 64) * 64


def _pack_small(name, raw, gng1, g_final):
    base, rows = _small_rows()
    n_l = len(raw[0])

    def body(*refs):
        o = refs[-1]
        per_layer = [refs[l * n_l:(l + 1) * n_l] for l in range(DEPTH)]
        gng1_ref, gfin_ref = refs[DEPTH * n_l], refs[DEPTH * n_l + 1]
        o[...] = jnp.zeros(o.shape, F32)

        def put_row_vector(r0, ref, width):
            for j in range(width // 128):
                o[r0 + j:r0 + j + 1, :] = ref[0:1, j * 128:(j + 1) * 128]

        put_row_vector(base["norm_g"] + D_MODEL // 128, gng1_ref, D_MODEL)
        put_row_vector(base["final_g"], gfin_ref, D_MODEL)
        for l in range(DEPTH):
            gws, gbs, glng, glnb, gcw, gcb, gwpool, gps = per_layer[l]
            put_row_vector(base["ln_g"] + l * 4, glng, WIDTH)
            put_row_vector(base["ln_b"] + l * 4, glnb, WIDTH)
            put_row_vector(base["conv_b"] + l * 4, gcb, WIDTH)
            put_row_vector(base["pool_scale"] + l * 4, gps, WIDTH)
            o[base["w_s"] + l * 1024:base["w_s"] + (l + 1) * 1024, :] = gws[...].reshape(1024, 128)
            o[base["w_pool"] + l * 512:base["w_pool"] + (l + 1) * 512, :] = gwpool[...].reshape(512, 128)
            o[base["b_s"] + l * 8:base["b_s"] + (l + 1) * 8, :] = gbs[...].T[0:8, :]
            for k in range(3):
                for ch in range(N_CHIPS):
                    r = base["conv_w"] + (l * 3 + k) * N_CHIPS + ch
                    o[r:r + 1, :] = gcw[k:k + 1, ch * 128:(ch + 1) * 128]

    args = [a for l in range(DEPTH) for a in raw[l]] + [gng1, g_final]
    vm = pl.BlockSpec(memory_space=pltpu.VMEM)
    return pl.pallas_call(body, name=name, in_specs=[vm] * len(args), out_specs=vm,
                          out_shape=jax.ShapeDtypeStruct((rows, 128), F32), compiler_params=_params())(*args)


def _adamw_small(name, gred, chip, w, m, v):
    base, _ = _small_rows()

    def body(chip_ref, g_ref, *refs):
        n = len(_SMALL)
        w_r, m_r, v_r = refs[:n], refs[n:2 * n], refs[2 * n:3 * n]
        out = refs[3 * n:]

        def update(i, idx, g):
            d, mn, vn = _adamw_math(w_r[i][idx], g, m_r[i][idx], v_r[i][idx])
            for o, val in zip(out[4 * i:4 * i + 4], (g, d, mn, vn)):
                o[idx] = val

        for i, nm in enumerate(_SMALL):
            shape = _SMALL_SHAPES[nm]
            if nm == "conv_w":
                for l in range(DEPTH):
                    for k in range(3):
                        row = base[nm] + (l * 3 + k) * N_CHIPS + chip_ref[0]
                        update(i, (l, slice(k, k + 1), slice(None)), g_ref[pl.ds(row, 1), :])
            elif len(shape) == 2:
                per = shape[1] // 128
                for l in range(shape[0]):
                    for j in range(per):
                        r = base[nm] + l * per + j
                        update(i, (slice(l, l + 1), slice(j * 128, (j + 1) * 128)), g_ref[r:r + 1, :])
            else:
                rows = 1
                for dim in shape[:-1]:
                    rows *= dim
                update(i, (Ellipsis,), g_ref[base[nm]:base[nm] + rows, :].reshape(shape))

    arrs = [d[nm] for d in (w, m, v) for nm in _SMALL]
    vm = pl.BlockSpec(memory_space=pltpu.VMEM)
    gs = pltpu.PrefetchScalarGridSpec(num_scalar_prefetch=1, grid=(1,), in_specs=[vm] * (1 + len(arrs)),
                                      out_specs=[vm] * (4 * len(_SMALL)))
    outs = pl.pallas_call(
        body, name=name, grid_spec=gs,
        out_shape=[jax.ShapeDtypeStruct(w[nm].shape, F32) for nm in _SMALL for _ in range(4)],
        compiler_params=_params(("arbitrary",)))(chip, gred, *arrs)
    return {nm: list(outs[4 * i:4 * i + 4]) for i, nm in enumerate(_SMALL)}


def _adamw_math(w, g, m, v):
    m = ADAM_B1 * m + (1.0 - ADAM_B1) * g
    v = ADAM_B2 * v + (1.0 - ADAM_B2) * (g * g)
    m_hat = m / (1.0 - ADAM_B1 ** ADAM_STEP)
    v_hat = v / (1.0 - ADAM_B2 ** ADAM_STEP)
    delta = -ADAM_LR * (m_hat / (jnp.sqrt(v_hat) + ADAM_EPS) + ADAM_WD * w)
    return delta, m, v


def _adamw(name, w, m, v, g_parts):
    rows, cols = w.shape
    np_ = len(g_parts)
    rb = _row_block(rows, cols, 7 + np_)

    def body(*refs):
        w_ref, m_ref, v_ref = refs[:3]
        g_refs = refs[3:3 + np_]
        go_ref, d_ref, mo_ref, vo_ref = refs[3 + np_:]
        g = g_refs[0][...]
        for gr in g_refs[1:]:
            g = g + gr[...]
        d, mn, vn = _adamw_math(w_ref[...], g, m_ref[...], v_ref[...])
        go_ref[...] = g
        d_ref[...] = d
        mo_ref[...] = mn
        vo_ref[...] = vn

    spec = pl.BlockSpec((rb, cols), lambda i: (i, 0))
    shp = jax.ShapeDtypeStruct((rows, cols), F32)
    return pl.pallas_call(body, name=name, grid=(rows // rb,), in_specs=[spec] * (3 + np_),
                          out_specs=[spec] * 4, out_shape=[shp] * 4,
                          compiler_params=_params(("arbitrary",)))(w, m, v, *g_parts)


def _all_reduce_small(name, a):
    def body(a_ref, o_ref, buf, send_sems, recv_sems):
        x, y, c = _place()
        o_ref[...] = a_ref[...]
        for rnd, peer in enumerate(((x, y, 1 - c), (x, 1 - y, c), (1 - x, y, c))):
            cp = pltpu.make_async_remote_copy(
                src_ref=o_ref, dst_ref=buf.at[rnd], send_sem=send_sems.at[rnd], recv_sem=recv_sems.at[rnd],
                device_id=peer, device_id_type=MESH)
            cp.start()
            cp.wait_recv()
            cp.wait_send()
            o_ref[...] = o_ref[...] + buf[rnd]

    vm = pl.BlockSpec(memory_space=pltpu.VMEM)
    return pl.pallas_call(
        body, name=name, in_specs=[vm], out_specs=vm, out_shape=jax.ShapeDtypeStruct(a.shape, F32),
        scratch_shapes=[pltpu.VMEM((3,) + a.shape, F32), pltpu.SemaphoreType.DMA((3,)),
                        pltpu.SemaphoreType.DMA((3,))],
        compiler_params=_params(),
    )(a)


def kernel(x, norm_g, w_in, ln_g, ln_b, w_s, b_s, conv_w, conv_b, w_pool, pool_scale, w_pa, w_pb, w_pc, w_o, final_g, loss_target, m_norm_g, m_w_in, m_ln_g, m_ln_b, m_w_s, m_b_s, m_conv_w, m_conv_b, m_w_pool, m_pool_scale, m_w_pa, m_w_pb, m_w_pc, m_w_o, m_final_g, v_norm_g, v_w_in, v_ln_g, v_ln_b, v_w_s, v_b_s, v_conv_w, v_conv_b, v_w_pool, v_pool_scale, v_w_pa, v_w_pb, v_w_pc, v_w_o, v_final_g):
    W = dict(norm_g=norm_g, w_in=w_in, ln_g=ln_g, ln_b=ln_b, w_s=w_s, b_s=b_s, conv_w=conv_w, conv_b=conv_b,
             w_pool=w_pool, pool_scale=pool_scale, w_pa=w_pa, w_pb=w_pb, w_pc=w_pc, w_o=w_o, final_g=final_g)
    M = dict(norm_g=m_norm_g, w_in=m_w_in, ln_g=m_ln_g, ln_b=m_ln_b, w_s=m_w_s, b_s=m_b_s, conv_w=m_conv_w,
             conv_b=m_conv_b, w_pool=m_w_pool, pool_scale=m_pool_scale, w_pa=m_w_pa, w_pb=m_w_pb, w_pc=m_w_pc,
             w_o=m_w_o, final_g=m_final_g)
    Vv = dict(norm_g=v_norm_g, w_in=v_w_in, ln_g=v_ln_g, ln_b=v_ln_b, w_s=v_w_s, b_s=v_b_s, conv_w=v_conv_w,
              conv_b=v_conv_b, w_pool=v_w_pool, pool_scale=v_pool_scale, w_pa=v_w_pa, w_pb=v_w_pb, w_pc=v_w_pc,
              w_o=v_w_o, final_g=v_final_g)
    L = DEPTH
    s = x.shape[1]
    xs = x.reshape(s, D_MODEL)
    tgt = loss_target.reshape(s, D_MODEL)
    k_me = (2 * lax.axis_index("x") + lax.axis_index("y")).astype(jnp.int32)

    chip = k_me.reshape(1)
    assert L == 2
    half_rows = D_MODEL // 2

    land_win = [_cast_cols(f"cast_w_in{l}", w_in, l, chip) for l in range(L)]
    pcat = lambda d: jnp.concatenate([d[b][l] for l in range(L) for b in ("w_pa", "w_pb", "w_pc")], axis=0)
    pcat_w = pcat(W).reshape(L, 3 * WIDTH, 256)
    land_proj = [_cast_slot(f"cast_proj{l}", pcat_w, l, chip) for l in range(L)]
    land_wo = [_cast_slot(f"cast_w_o{l}", w_o, l, chip) for l in range(L)]
    cw_sh = jnp.pad(conv_w, ((0, 0), (0, 5), (0, 0))).reshape(1, L * 8, 128)
    land_cw = lax.dynamic_update_slice(jnp.zeros((N_CHIPS, L * 8, 128), F32), cw_sh, (k_me, 0, 0))

    causal = jnp.tril(jnp.ones((CHUNK, CHUNK), dtype=bool))
    w_m = jnp.where(causal, w_s, 0.0)
    lw = dict(
        ln_g=ln_g.reshape(L, 1, WIDTH), ln_b=ln_b.reshape(L, 1, WIDTH),
        w2=w_m.reshape(L, 4, 256, CHUNK).astype(BF16),
        wt2=jnp.swapaxes(w_m, -1, -2).reshape(L, 4, 256, CHUNK).astype(BF16),
        bst=jnp.repeat(jnp.swapaxes(b_s, -1, -2), 64, axis=-1),
        cb=conv_b.reshape(L, 1, WIDTH), wpool=w_pool.astype(BF16), ps=pool_scale.reshape(L, 1, WIDTH),
        sel=(jnp.arange(WIDTH)[:, None] // 64 == jnp.arange(128)[None, :]).astype(F32))
    norm_g3 = norm_g.reshape(L, 1, D_MODEL)

    cm = _Comm()
    cm.gather_half(cm.through(land_win[0]))
    (p0, h0), (w_half0,), _ = _f1_own(xs, norm_g3, w_in, chip, cm)
    cm = _Comm()
    cm.forward_half(cm.through(w_half0))
    (w_all0,) = _comm_only("forward_w_in0", cm)
    first_rows = D_MODEL // 4
    cm = _Comm()
    for buf in (land_proj[0], land_wo[0], land_cw):
        cm.gather(cm.through(buf))
    cm.gather(cm.through(land_win[1]), 0, first_rows)
    (p0,), (g_proj0, g_wo0, g_cw, w_part1), _ = _f1_others(h0, p0, w_all0, chip, cm)
    lw["proj"] = [g_proj0, None]
    lw["wo"] = [g_wo0.reshape(D_MODEL, D_MODEL), None]
    lw["cw"] = g_cw.reshape(N_CHIPS, L, 8, 128).transpose(1, 2, 0, 3).reshape(L, 8, WIDTH)
    cm = _Comm()
    cm.gather(cm.through(w_part1), first_rows, D_MODEL - first_rows)
    (ya0, yb0, yc0, mm0, x1), (w_all1,), _ = _f2(xs, p0, lw, 0, comm=cm)
    cm = _Comm()
    for buf in (land_proj[1], land_wo[1]):
        cm.gather(cm.through(buf))
    (p1, h1), (g_proj1, g_wo1), _ = _f1(x1, norm_g3, w_all1, 1, comm=cm)
    lw["proj"][1] = g_proj1
    lw["wo"][1] = g_wo1.reshape(D_MODEL, D_MODEL)
    (ya1, yb1, yc1, mm1, x2), _, _ = _f2(x1, p1, lw, 1)
    dxl, loss_blk, g_final = _loss_head(x2, tgt, final_g.reshape(1, D_MODEL))
    loss = lax.psum(loss_blk[0, 0], ("x", "y", "c"))

    by_chip = lambda gwp, gwo: (gwp, gwo.reshape(N_CHIPS, 256, D_MODEL))
    recv_like = lambda a: ((3,) + a.shape[1:], a.dtype)
    part_rows = (D_MODEL, 3 * WIDTH, 256)
    gwin, recv_win, part_proj, part_wo, recv_proj, recv_wo = ([None] * L for _ in range(6))

    def sums(l, half):
        return [_sum4(f"sum_w_in{l}", gwin[l], chip, recv_win[l], l, L, half[0]),
                _sum4(f"sum_proj{l}", part_proj[l], chip, recv_proj[l], l, L, half[1]),
                _sum4(f"sum_w_o{l}", part_wo[l], chip, recv_wo[l], l, L, half[2])]

    gwo1 = _gwo(mm1, dxl, 1)
    (dp1, gwp1, *sm1), _, _ = _b1(p1, dxl, ya1, yb1, yc1, lw, 1)
    (gwin[1],), _, _ = _b2b(h1, dp1, "b2b_l1", chip)
    cm = _Comm()
    cm.scatter(cm.read(gwin[1]), cm.new((3, D_MODEL, SHARD_W), BF16), 0, half_rows)
    (dxl, gng1), _, (recv_half,) = _b2a(dp1, w_all1, x1, dxl, norm_g3, 1, comm=cm)
    part_proj[1], part_wo[1] = by_chip(gwp1, gwo1)
    gwo0 = _gwo(mm0, dxl, 0)
    cm = _Comm()
    cm.scatter(cm.read(gwin[1]), cm.through(recv_half), half_rows, half_rows)
    cm.scatter(cm.read(part_proj[1]), cm.new(*recv_like(part_proj[1])))
    cm.scatter(cm.read(part_wo[1]), cm.new(*recv_like(part_wo[1])))
    (dp0, gwp0, *sm0), (recv_win[1],), (recv_proj[1], recv_wo[1]) = _b1(p0, dxl, ya0, yb0, yc0, lw, 0, comm=cm)
    half = sums(1, [None, None, None])
    part_proj[0], part_wo[0] = by_chip(gwp0, gwo0)

    gpack = _pack_small("pack_small", [sm0, sm1], gng1, g_final)
    slice_rows = gpack.shape[0] // 8

    cm = _Comm()
    cm.scatter(cm.read(part_proj[0]), cm.new(*recv_like(part_proj[0])))
    cm.scatter(cm.read(part_wo[0]), cm.new(*recv_like(part_wo[0])))
    for a in range(3):
        cm.sibling(cm.read(half[a]), cm.new(half[a].shape, F32), part_rows[a], part_rows[a])
    cm.slices_out(cm.read(gpack), cm.new((7, slice_rows, 128), F32), slice_rows)
    (g_others,), _, (recv_proj[0], recv_wo[0], *other, slices_in) = _b2b(h0, dp0, "b2b_l0_others", chip, "others",
                                                                        comm=cm)
    me = (4 * lax.axis_index("x") + 2 * lax.axis_index("y") + lax.axis_index("c")).astype(jnp.int32).reshape(1)
    gsum = _sum8("sum_small", gpack, me, slices_in)
    early_rows = D_MODEL // 8
    cm = _Comm()
    cm.scatter(cm.read(g_others), cm.new((3, D_MODEL, SHARD_W), BF16), 0, early_rows)
    cm.slices_back(cm.through(gsum), slice_rows)
    (gwin[0],), (gred,), (recv_early,) = _b2b(h0, dp0, "b2b_l0_own", chip, "own", comm=cm)
    cm = _Comm()
    cm.scatter(cm.read(g_others), cm.through(recv_early), early_rows, D_MODEL - early_rows)
    (dxl, gng0), (recv_win[0],), _ = _b2a(dp0, w_all0, xs, dxl, norm_g3, 0, comm=cm)
    grad_x = dxl.reshape(1, s, D_MODEL)

    half = sums(0, half)
    cm = _Comm()
    for a in range(3):
        cm.sibling(cm.read(half[a]), cm.through(other[a]), 0, part_rows[a])
    other = _comm_only("swap_halves0", cm)
    gng0 = _all_reduce_small("all_reduce_norm_g0", gng0.reshape(8, 128))
    gred = lax.dynamic_update_slice(gred, gng0, (0, 0))

    outs = {}
    shard2d = dict(w_in=(L * D_MODEL, SHARD_W), w_o=(L * 256, D_MODEL))
    for a, name in ((0, "w_in"), (2, "w_o")):
        r2 = shard2d[name]
        res = _adamw(f"adamw_{name}", W[name].reshape(r2), M[name].reshape(r2), Vv[name].reshape(r2),
                     [half[a], other[a]])
        outs[name] = [o.reshape(W[name].shape) for o in res]
    res = _adamw("adamw_proj", pcat(W), pcat(M), pcat(Vv), [half[1], other[1]])
    for i, name in enumerate(("w_pa", "w_pb", "w_pc")):
        outs[name] = [o.reshape(L, 3, WIDTH, 256)[:, i] for o in res]

    as_rows = lambda d: {nm: (d[nm].reshape(1, D_MODEL) if nm == "final_g" else d[nm]) for nm in _SMALL}
    outs.update(_adamw_small("adamw_small", gred, chip, as_rows(W), as_rows(M), as_rows(Vv)))
    outs["final_g"] = [o.reshape(D_MODEL) for o in outs["final_g"]]

    order = ("norm_g", "w_in", "ln_g", "ln_b", "w_s", "b_s", "conv_w", "conv_b", "w_pool", "pool_scale",
             "w_pa", "w_pb", "w_pc", "w_o", "final_g")
    return (loss, grad_x, *[outs[nm][0] for nm in order], *[outs[nm][1] for nm in order],
            *[outs[nm][2] for nm in order], *[outs[nm][3] for nm in order])
```

```python
import functools

import jax
import jax.numpy as jnp
from jax import lax
from jax.experimental import pallas as pl
from jax.experimental.pallas import tpu as pltpu

F32 = jnp.float32
BF16 = jnp.bfloat16

D_MODEL = 1024
DEPTH = 2
CHUNK = 128
WIDTH = 512
POOL_WINDOWS = (2, 4, 8, 16)
IN_TOTAL = 7680
N_CHIPS = 4
SHARD_W = IN_TOTAL // N_CHIPS
RMS_EPS = 1e-6
LN_EPS = 1e-5
HALO = 16

U, V, ZA, XB, BG, CG, ZB, XC, ZC, GA, GB, GC = (0, 512, 1024, 1536, 2048, 2560, 3072, 3584, 4096, 4608, 5632, 6656)

ADAM_LR = 0.001
ADAM_B1 = 0.9
ADAM_B2 = 0.999
ADAM_EPS = 1e-08
ADAM_WD = 0.01
ADAM_STEP = 10

VMEM_LIMIT = 56 * 1024 * 1024
MESH = pl.DeviceIdType.MESH
ANY = pl.BlockSpec(memory_space=pl.ANY)
NT = (((1,), (1,)), ((), ()))
TN = (((0,), (0,)), ((), ()))


def _params(sem=None):
    kw = dict(vmem_limit_bytes=VMEM_LIMIT)
    if sem is not None:
        kw["dimension_semantics"] = sem
    return pltpu.CompilerParams(**kw)


def _dot(a, b):
    return jnp.dot(a, b, preferred_element_type=F32)


def _dotg(a, b, dims):
    return lax.dot_general(a, b, dims, preferred_element_type=F32)


def _sigmoid(x):
    return 1.0 / (1.0 + jnp.exp(-x))


_GELU_K = 0.7978845608028654


def _gelu(x):
    th = jnp.tanh(_GELU_K * (x + 0.044715 * (x * x * x)))
    return 0.5 * x * (1.0 + th), th


def _gelu_grad(x, th):
    return 0.5 * (1.0 + th) + 0.5 * x * (1.0 - th * th) * (_GELU_K * (1.0 + 3.0 * 0.044715 * (x * x)))


def _colsum8(x):
    t, c = x.shape
    return jnp.sum(x.reshape(t // 8, 8, c), axis=0)


def _branches_fwd(p_ref, hxb_ref, hcg_ref, hxc_ref, first, tstart, w, sg_scr):
    t = p_ref.shape[0]
    nch = t // CHUNK

    def seg(o, width=WIDTH):
        return p_ref[:, o:o + width].astype(F32)

    lo = lax.broadcasted_iota(jnp.int32, (CHUNK, CHUNK), 1) < 64
    r = {}
    pu = seg(U)
    u_act, th_u = _gelu(pu)
    pv = seg(V)
    vg, th_v = _gelu(pv)
    mu = jnp.mean(vg, axis=-1, keepdims=True)
    xc = vg - mu
    var = jnp.mean(xc * xc, axis=-1, keepdims=True)
    rs = lax.rsqrt(var + LN_EPS)
    vhat = xc * rs
    vn = vhat * w["ln_g"][...] + w["ln_b"][...]
    vnb = vn.astype(BF16)
    for n in range(nch):
        for j in range(4):
            vb = vnb[n * CHUNK:(n + 1) * CHUNK, j * 128:(j + 1) * 128]
            z = _dot(w["w2"][j], vb)
            sg_scr[n * CHUNK:(n + 1) * CHUNK, j * 128:(j + 1) * 128] = (
                jnp.where(lo, z[:CHUNK], z[CHUNK:]) + w["bst"][:, j * 128:(j + 1) * 128])
    sg = sg_scr[...]
    a_br = u_act * sg
    za = seg(ZA)
    sa = _sigmoid(za)
    r.update(pu=pu, th_u=th_u, pv=pv, th_v=th_v, rs=rs, vhat=vhat, vnb=vnb, u_act=u_act, sg=sg,
             a_br=a_br, za=za, sa=sa, a_in=a_br * (za * sa))

    xb = seg(XB)
    cg = seg(CG)
    yb0 = cg * xb
    hal = hcg_ref[...].astype(F32) * hxb_ref[...].astype(F32)
    hal = jnp.where(first, 0.0, hal)
    ext = jnp.concatenate([hal, yb0], axis=0)
    y1 = pltpu.roll(ext, 1, 0)[HALO:]
    y2 = pltpu.roll(ext, 2, 0)[HALO:]
    cw = w["cw"]
    conv = cw[0:1, :] * y2 + cw[1:2, :] * y1 + cw[2:3, :] * yb0 + w["cb"][...]
    bg = seg(BG)
    b_br = bg * conv
    zb = seg(ZB)
    sb = _sigmoid(zb)
    r.update(xb=xb, cg=cg, yb0=yb0, y1=y1, y2=y2, conv=conv, bg=bg, b_br=b_br, zb=zb, sb=sb,
             b_in=b_br * (zb * sb))

    xcv = seg(XC)
    hxc = jnp.where(first, 0.0, hxc_ref[...].astype(F32))
    extc = jnp.concatenate([hxc, xcv], axis=0)
    tpos = tstart + lax.broadcasted_iota(jnp.int32, (t, 1), 0) + 1
    pooled, inv, q = [], [], []
    for gi, win in enumerate(POOL_WINDOWS):
        s = extc[:, gi * 128:(gi + 1) * 128]
        sh = 1
        while sh < win:
            s = s + pltpu.roll(s, sh, 0)
            sh *= 2
        inv_g = jnp.where(tpos >= win, 1.0 / win, 1.0 / jnp.minimum(tpos, win).astype(F32))
        pg = s[HALO:] * inv_g - xcv[:, gi * 128:(gi + 1) * 128]
        pooled.append(pg)
        inv.append(inv_g)
        q.append(_dot(pg.astype(BF16), w["wpool"][gi]))
    qv = jnp.concatenate(q, axis=1)
    c_br = qv * w["ps"][...]
    zc = seg(ZC)
    sc = _sigmoid(zc)
    r.update(pooled=pooled, inv=inv, q=qv, c_br=c_br, zc=zc, sc=sc, c_in=c_br * (zc * sc))
    return r


def _halo_specs(t, rev_n=None):
    def imap(col):
        def f(i):
            ti = i if rev_n is None else rev_n - 1 - i
            return (jnp.maximum(ti * (t // HALO) - 1, 0), col)
        return f
    return [pl.BlockSpec((HALO, WIDTH), imap(XB // WIDTH)),
            pl.BlockSpec((HALO, WIDTH), imap(CG // WIDTH)),
            pl.BlockSpec((HALO, WIDTH), imap(XC // WIDTH))]


def _const_spec(shape):
    nd = len(shape)
    return pl.BlockSpec(shape, lambda *_: (0,) * nd)


def _place():
    return lax.axis_index("x"), lax.axis_index("y"), lax.axis_index("c")


def _chip_peer(x, y, jm):
    px = (1 - x) if (jm & 2) else x
    py = (1 - y) if (jm & 1) else y
    return px, py


def _chip_part(buf, k, r0=0, nr=None):
    if len(buf.shape) == 3:
        return buf.at[k] if nr is None else buf.at[k, pl.ds(r0, nr)]
    cols = buf.shape[1] // N_CHIPS
    rows = pl.ds(0, buf.shape[0]) if nr is None else pl.ds(r0, nr)
    return buf.at[rows, pl.ds(pl.multiple_of(k * cols, 128), cols)]


def _row_range(ref, r0, nr):
    return ref if nr is None else ref.at[pl.ds(r0, nr)]


class _Comm:
    def __init__(self):
        self.ins, self.thru, self.fresh, self.moves, self.n = [], [], [], [], 0

    def read(self, arr):
        self.ins.append(arr)
        return ("ins", len(self.ins) - 1)

    def through(self, arr):
        self.thru.append(arr)
        return ("thru", len(self.thru) - 1)

    def new(self, shape, dtype):
        self.fresh.append(jax.ShapeDtypeStruct(tuple(shape), dtype))
        return ("fresh", len(self.fresh) - 1)

    def _add(self, n, *move):
        self.moves.append(move)
        self.n += n

    def gather(self, buf, r0=0, nr=None):
        self._add(3, "gather", buf, r0, nr)

    def scatter(self, src, dst, r0=0, nr=None):
        self._add(3, "scatter", src, dst, r0, nr)

    def sibling(self, src, dst, r0=0, nr=None):
        self._add(1, "sibling", src, dst, r0, nr)

    def gather_half(self, buf):
        self._add(3, "gather_half", buf)

    def forward_half(self, buf):
        self._add(3, "forward_half", buf)

    def slices_out(self, src, dst, rows):
        self._add(7, "slices_out", src, dst, rows)

    def slices_back(self, buf, rows):
        self._add(7, "slices_back", buf, rows)

    def copies(self, bufs, x, y, c):
        ref = lambda h: bufs[h[0]][h[1]]
        k_me = 2 * x + y
        me = 4 * x + 2 * y + c
        cps = []
        for move in self.moves:
            kind = move[0]
            if kind in ("gather", "scatter"):
                for jj, jm in enumerate((1, 2, 3)):
                    px, py = _chip_peer(x, y, jm)
                    k_peer = 2 * px + py
                    if kind == "gather":
                        _, buf, r0, nr = move
                        mine = _chip_part(ref(buf), k_me, r0, nr)
                        cps.append((mine, mine, _chip_part(ref(buf), k_peer, r0, nr), (px, py, c)))
                    else:
                        _, src, dst, r0, nr = move
                        slot = ref(dst).at[jj] if nr is None else ref(dst).at[jj, pl.ds(r0, nr)]
                        cps.append((_chip_part(ref(src), k_peer, r0, nr), slot, slot, (px, py, c)))
            elif kind in ("gather_half", "forward_half"):
                buf = ref(move[1])
                hr = buf.shape[-2] // 2
                mine_r0, other_r0 = pl.multiple_of(c * hr, 16), pl.multiple_of((1 - c) * hr, 16)
                for jm in (1, 2, 3):
                    px, py = _chip_peer(x, y, jm)
                    k_peer = 2 * px + py
                    if kind == "gather_half":
                        part = _chip_part(buf, k_me, mine_r0, hr)
                        cps.append((part, part, _chip_part(buf, k_peer, mine_r0, hr), (px, py, c)))
                    else:
                        part = _chip_part(buf, k_peer, mine_r0, hr)
                        cps.append((part, part, _chip_part(buf, k_peer, other_r0, hr), (x, y, 1 - c)))
            elif kind == "sibling":
                _, src, dst, r0, nr = move
                land = _row_range(ref(dst), r0, nr)
                cps.append((_row_range(ref(src), r0, nr), land, land, (x, y, 1 - c)))
            else:
                for j in range(1, 8):
                    px = (1 - x) if (j & 4) else x
                    py = (1 - y) if (j & 2) else y
                    pc = (1 - c) if (j & 1) else c
                    peer = 4 * px + 2 * py + pc
                    if kind == "slices_out":
                        _, src, dst, rows = move
                        slot = ref(dst).at[j - 1]
                        cps.append((ref(src).at[pl.ds(pl.multiple_of(peer * rows, 8), rows)], slot, slot,
                                    (px, py, pc)))
                    else:
                        _, buf, rows = move
                        mine = ref(buf).at[pl.ds(pl.multiple_of(me * rows, 8), rows)]
                        cps.append((mine, mine, ref(buf).at[pl.ds(pl.multiple_of(peer * rows, 8), rows)],
                                    (px, py, pc)))
        assert len(cps) == self.n
        return cps


def _pcall(body, *, name, grid, in_specs, out_specs, out_shape, scratch, sem, args, comm=None, prefetch=(),
           aliases=None):
    n_pre = len(prefetch)
    n_in, n_out, n_scr = len(in_specs), len(out_specs), len(scratch)
    io_alias = {n_pre + i: o for i, o in (aliases or {}).items()}
    if comm is None or comm.n == 0:
        gs = pltpu.PrefetchScalarGridSpec(num_scalar_prefetch=n_pre, grid=grid, in_specs=list(in_specs),
                                          out_specs=list(out_specs), scratch_shapes=list(scratch))
        outs = pl.pallas_call(body, name=name, grid_spec=gs, out_shape=list(out_shape), input_output_aliases=io_alias,
                              compiler_params=_params(sem))(*prefetch, *args)
        return list(outs), [], []
    n_ci, n_ct, n_cf, n_cp = len(comm.ins), len(comm.thru), len(comm.fresh), comm.n

    def wrapped(*refs):
        pos = 0
        def take(n):
            nonlocal pos
            got = refs[pos:pos + n]
            pos += n
            return got
        pre = take(n_pre)
        a, ci, _ = take(n_in), take(n_ci), take(n_ct)
        o, ct, cf = take(n_out), take(n_ct), take(n_cf)
        scr = take(n_scr)
        send_sems, recv_sems = take(2)
        first = functools.reduce(jnp.logical_and, [pl.program_id(d) == 0 for d in range(len(grid))])
        last = functools.reduce(jnp.logical_and, [pl.program_id(d) == grid[d] - 1 for d in range(len(grid))])
        x, y, c = _place()
        cps = comm.copies(dict(ins=ci, thru=ct, fresh=cf), x, y, c)

        def copy(i, src, dst, dev):
            return pltpu.make_async_remote_copy(src_ref=src, dst_ref=dst, send_sem=send_sems.at[i],
                                                recv_sem=recv_sems.at[i], device_id=dev, device_id_type=MESH)

        @pl.when(first)
        def _():
            for i, (src, dst, _, dev) in enumerate(cps):
                copy(i, src, dst, dev).start()

        body(*pre, *a, *o, *scr)

        @pl.when(last)
        def _():
            for i, (src, _, land, dev) in enumerate(cps):
                copy(i, src, land, dev).wait_recv()
            for i, (src, dst, _, dev) in enumerate(cps):
                copy(i, src, dst, dev).wait_send()

    thru_shapes = [jax.ShapeDtypeStruct(t.shape, t.dtype) for t in comm.thru]
    gs = pltpu.PrefetchScalarGridSpec(
        num_scalar_prefetch=n_pre, grid=grid,
        in_specs=list(in_specs) + [ANY] * (n_ci + n_ct),
        out_specs=list(out_specs) + [ANY] * (n_ct + n_cf),
        scratch_shapes=list(scratch) + [pltpu.SemaphoreType.DMA((n_cp,)), pltpu.SemaphoreType.DMA((n_cp,))])
    outs = pl.pallas_call(
        wrapped, name=name, grid_spec=gs,
        out_shape=list(out_shape) + thru_shapes + comm.fresh,
        input_output_aliases={**io_alias, **{n_pre + n_in + n_ci + t: n_out + t for t in range(n_ct)}},
        compiler_params=_params(sem),
    )(*prefetch, *args, *comm.ins, *comm.thru)
    outs = list(outs)
    return outs[:n_out], outs[n_out:n_out + n_ct], outs[n_out + n_ct:]


def _comm_only(name, comm):
    def body():
        pass
    _, thru, _ = _pcall(body, name=name, grid=(1,), in_specs=[], out_specs=[], out_shape=[], scratch=[],
                        sem=("arbitrary",), args=[], comm=comm)
    return thru


def _f1(x, norm_g3, w_all, l, comm=None, tm=256):
    s = x.shape[0]

    def body(x_ref, g_ref, w_hbm, p_ref, h_ref, w_vmem):
        @pl.when(pl.program_id(0) == 0)
        def _():
            pltpu.sync_copy(w_hbm, w_vmem)
        xv = x_ref[...]
        r = lax.rsqrt(jnp.mean(xv * xv, axis=-1, keepdims=True) + RMS_EPS)
        hb = ((xv * r) * g_ref[...]).astype(BF16)
        h_ref[...] = hb
        p_ref[...] = _dot(hb, w_vmem[...]).astype(BF16)

    return _pcall(
        body, name=f"f1_l{l}", grid=(s // tm,),
        in_specs=[pl.BlockSpec((tm, D_MODEL), lambda i: (i, 0)),
                  pl.BlockSpec((None, 1, D_MODEL), lambda i: (l, 0, 0)), ANY],
        out_specs=[pl.BlockSpec((tm, IN_TOTAL), lambda i: (i, 0)),
                   pl.BlockSpec((tm, D_MODEL), lambda i: (i, 0))],
        out_shape=[jax.ShapeDtypeStruct((s, IN_TOTAL), BF16), jax.ShapeDtypeStruct((s, D_MODEL), BF16)],
        scratch=[pltpu.VMEM((D_MODEL, IN_TOTAL), BF16)], sem=("arbitrary",), args=[x, norm_g3, w_all], comm=comm)


def _f1_own(x, norm_g3, w_in, chip, comm, tm=256):
    s = x.shape[0]

    def body(idx_ref, x_ref, g_ref, w_hbm, p_ref, h_ref, w_f32, w_vmem):
        @pl.when(pl.program_id(0) == 0)
        def _():
            pltpu.sync_copy(w_hbm.at[0], w_f32)
            w_vmem[...] = w_f32[...].astype(BF16)
        xv = x_ref[...]
        r = lax.rsqrt(jnp.mean(xv * xv, axis=-1, keepdims=True) + RMS_EPS)
        hb = ((xv * r) * g_ref[...]).astype(BF16)
        h_ref[...] = hb
        p_ref[...] = _dot(hb, w_vmem[...]).astype(BF16)

    return _pcall(
        body, name="f1_l0_own", grid=(s // tm,),
        in_specs=[pl.BlockSpec((tm, D_MODEL), lambda i, idx: (i, 0)),
                  pl.BlockSpec((None, 1, D_MODEL), lambda i, idx: (0, 0, 0)), ANY],
        out_specs=[pl.BlockSpec((tm, SHARD_W), lambda i, idx: (i, idx[0])),
                   pl.BlockSpec((tm, D_MODEL), lambda i, idx: (i, 0))],
        out_shape=[jax.ShapeDtypeStruct((s, IN_TOTAL), BF16), jax.ShapeDtypeStruct((s, D_MODEL), BF16)],
        scratch=[pltpu.VMEM((D_MODEL, SHARD_W), F32), pltpu.VMEM((D_MODEL, SHARD_W), BF16)],
        sem=("arbitrary",), args=[x, norm_g3, w_in], comm=comm, prefetch=[chip])


def _f1_others(h, p, w_all, chip, comm, tm=1024):
    s = h.shape[0]
    tm = min(tm, s)
    cols_of = lambda j, idx: (idx[0] + 1 + j) % N_CHIPS

    def body(idx_ref, h_ref, w_hbm, _, p_ref, w_vmem, sems):
        j = pl.program_id(1)

        @pl.when(jnp.logical_and(pl.program_id(0) == 0, j == 0))
        def _():
            loads = [pltpu.make_async_copy(
                w_hbm.at[:, pl.ds(pl.multiple_of(((idx_ref[0] + 1 + q) % N_CHIPS) * SHARD_W, 128), SHARD_W)],
                w_vmem.at[q], sems.at[q]) for q in range(N_CHIPS - 1)]
            for cp in loads:
                cp.start()
            for cp in loads:
                cp.wait()

        p_ref[...] = _dot(h_ref[...], w_vmem[j]).astype(BF16)

    return _pcall(
        body, name="f1_l0_others", grid=(s // tm, N_CHIPS - 1),
        in_specs=[pl.BlockSpec((tm, D_MODEL), lambda i, j, idx: (i, 0)), ANY, ANY],
        out_specs=[pl.BlockSpec((tm, SHARD_W), lambda i, j, idx: (i, cols_of(j, idx)))],
        out_shape=[jax.ShapeDtypeStruct((s, IN_TOTAL), BF16)],
        scratch=[pltpu.VMEM((N_CHIPS - 1, D_MODEL, SHARD_W), BF16), pltpu.SemaphoreType.DMA((N_CHIPS - 1,))],
        sem=("arbitrary", "arbitrary"), args=[h, w_all, p], comm=comm, prefetch=[chip], aliases={2: 0})


def _f2(x, p, lw, l, comm=None, t=256):
    s = x.shape[0]
    n = s // t

    def body(p_ref, hxb_ref, hcg_ref, hxc_ref, x_ref, lng, lnb, w2, bst, cw, cb, wpool, ps,
             proj, wo, ya_ref, yb_ref, yc_ref, m_ref, xo_ref, sg_scr):
        i = pl.program_id(0)
        w = dict(ln_g=lng, ln_b=lnb, w2=w2, bst=bst, cw=cw, cb=cb, wpool=wpool, ps=ps)
        r = _branches_fwd(p_ref, hxb_ref, hcg_ref, hxc_ref, i == 0, i * t, w, sg_scr)

        def project(act, b):
            ab = act.astype(BF16)
            return jnp.concatenate([_dot(ab, proj[k, b * WIDTH:(b + 1) * WIDTH, :]) for k in range(N_CHIPS)], axis=1)

        ya = project(r["a_in"], 0)
        yb = project(r["b_in"], 1)
        yc = project(r["c_in"], 2)
        ya_ref[...] = ya.astype(BF16)
        yb_ref[...] = yb.astype(BF16)
        yc_ref[...] = yc.astype(BF16)
        m = (_sigmoid(p_ref[:, GA:GA + D_MODEL].astype(F32)) * ya
             + _sigmoid(p_ref[:, GB:GB + D_MODEL].astype(F32)) * yb
             + _sigmoid(p_ref[:, GC:GC + D_MODEL].astype(F32)) * yc)
        mb = m.astype(BF16)
        m_ref[...] = mb
        xo_ref[...] = x_ref[...] + _dot(mb, wo[...])

    tile = lambda c: pl.BlockSpec((t, c), lambda i: (i, 0))
    lsel = lambda *blk: pl.BlockSpec((None,) + blk, lambda i: (l,) + (0,) * len(blk))
    act = jax.ShapeDtypeStruct((s, D_MODEL), BF16)
    return _pcall(
        body, name=f"f2_l{l}", grid=(n,),
        in_specs=[tile(IN_TOTAL)] + _halo_specs(t) + [
            tile(D_MODEL), lsel(1, WIDTH), lsel(1, WIDTH), lsel(4, 256, 128), lsel(CHUNK, WIDTH),
            lsel(8, WIDTH), lsel(1, WIDTH), lsel(4, 128, 128), lsel(1, WIDTH),
            _const_spec((N_CHIPS, 3 * WIDTH, 256)), _const_spec((D_MODEL, D_MODEL))],
        out_specs=[tile(D_MODEL)] * 5,
        out_shape=[act, act, act, act, jax.ShapeDtypeStruct((s, D_MODEL), F32)],
        scratch=[pltpu.VMEM((t, WIDTH), F32)], sem=("arbitrary",),
        args=[p, p, p, p, x, lw["ln_g"], lw["ln_b"], lw["w2"], lw["bst"], lw["cw"], lw["cb"], lw["wpool"], lw["ps"],
              lw["proj"][l], lw["wo"][l]], comm=comm)


def _b1(p, dout, ya, yb, yc, lw, l, comm=None, t=256):
    s = p.shape[0]
    n = s // t
    nch = t // CHUNK

    def body(p_ref, hxb_ref, hcg_ref, hxc_ref, do_ref, ya_ref, yb_ref, yc_ref,
             lng, lnb, w2, wt2, bst, cw, cb, wpool, ps, proj_h, wo_h, sel_ref,
             dp_ref, gwp_h, gws_ref, gbs_ref, glng_ref, glnb_ref, gcw_ref, gcb_ref,
             gwpool_ref, gps_ref,
             wpa, wpb, wpc, wo, gwpa, gwpb, gwpc, gbs_acc, vec_acc, sg_scr, dvn_scr, car_dc, car_e, psem):
        i = pl.program_id(0)
        ti = n - 1 - i

        def by_chip_copies(vmem_bufs, hbm, to_hbm):
            cps = []
            for b, buf in enumerate(vmem_bufs):
                for k in range(N_CHIPS):
                    v = buf.at[:, pl.ds(k * 256, 256)]
                    h = hbm.at[k, pl.ds(b * WIDTH, WIDTH)]
                    cps.append(pltpu.make_async_copy(v, h, psem.at[b * N_CHIPS + k]) if to_hbm
                               else pltpu.make_async_copy(h, v, psem.at[b * N_CHIPS + k]))
            return cps

        @pl.when(i == 0)
        def _():
            loads = by_chip_copies((wpa, wpb, wpc), proj_h, False)
            for cp in loads:
                cp.start()
            pltpu.sync_copy(wo_h, wo)
            for cp in loads:
                cp.wait()
            for acc in (gwpa, gwpb, gwpc, gbs_acc, vec_acc, car_dc, car_e):
                acc[...] = jnp.zeros(acc.shape, acc.dtype)
            gws_ref[...] = jnp.zeros(gws_ref.shape, F32)
            gwpool_ref[...] = jnp.zeros(gwpool_ref.shape, F32)

        w = dict(ln_g=lng, ln_b=lnb, w2=w2, bst=bst, cw=cw, cb=cb, wpool=wpool, ps=ps)
        r = _branches_fwd(p_ref, hxb_ref, hcg_ref, hxc_ref, ti == 0, ti * t, w, sg_scr)

        def seg(o, width=WIDTH):
            return p_ref[:, o:o + width].astype(F32)

        def put(o, val):
            dp_ref[:, o:o + val.shape[1]] = val.astype(BF16)

        dob = do_ref[...].astype(BF16)
        dm = _dotg(dob, wo[...], NT)

        def merge_bwd(goff, y_ref, xin, wp, gwp):
            sx = _sigmoid(seg(goff, D_MODEL))
            dmy = dm * sx
            put(goff, dmy * y_ref[...].astype(F32) * (1.0 - sx))
            dyb = dmy.astype(BF16)
            gwp[...] += _dotg(xin.astype(BF16), dyb, TN)
            return _dotg(dyb, wp[...], NT)

        d_ain = merge_bwd(GA, ya_ref, r["a_in"], wpa, gwpa)
        d_bin = merge_bwd(GB, yb_ref, r["b_in"], wpb, gwpb)
        d_cin = merge_bwd(GC, yc_ref, r["c_in"], wpc, gwpc)

        def dsilu(z, sz):
            return sz * (1.0 + z * (1.0 - sz))

        za, sa = r["za"], r["sa"]
        d_abr = d_ain * (za * sa)
        put(ZA, d_ain * r["a_br"] * dsilu(za, sa))
        put(U, d_abr * r["sg"] * _gelu_grad(r["pu"], r["th_u"]))
        d_sg = d_abr * r["u_act"]
        dsgb = d_sg.astype(BF16)
        lo = lax.broadcasted_iota(jnp.int32, (CHUNK, CHUNK), 1) < 64
        zero = jnp.zeros((CHUNK, CHUNK), BF16)
        for c in range(nch):
            rows = slice(c * CHUNK, (c + 1) * CHUNK)
            gbs_acc[...] += d_sg[rows]
            for j in range(4):
                cols = slice(j * 128, (j + 1) * 128)
                dj = dsgb[rows, cols]
                zt = _dot(wt2[j], dj)
                dvn_scr[rows, cols] = jnp.where(lo, zt[:CHUNK], zt[CHUNK:])
                stacked = jnp.concatenate([jnp.where(lo, dj, zero), jnp.where(lo, zero, dj)], axis=0)
                gws_ref[j] += _dotg(stacked, r["vnb"][rows, cols], NT)
        d_vn = dvn_scr[...]
        vhat = r["vhat"]
        vec_acc[0] += _colsum8(d_vn * vhat)
        vec_acc[1] += _colsum8(d_vn)
        d_vhat = d_vn * lng[...]
        d_vg = r["rs"] * (d_vhat - jnp.mean(d_vhat, axis=-1, keepdims=True)
                          - vhat * jnp.mean(d_vhat * vhat, axis=-1, keepdims=True))
        put(V, d_vg * _gelu_grad(r["pv"], r["th_v"]))

        zb, sb = r["zb"], r["sb"]
        d_bbr = d_bin * (zb * sb)
        put(ZB, d_bin * r["b_br"] * dsilu(zb, sb))
        put(BG, d_bbr * r["conv"])
        dc = d_bbr * r["bg"]
        vec_acc[2] += _colsum8(dc)
        vec_acc[3] += _colsum8(dc * r["y2"])
        vec_acc[4] += _colsum8(dc * r["y1"])
        vec_acc[5] += _colsum8(dc * r["yb0"])
        ext = jnp.concatenate([dc, car_dc[...]], axis=0)
        ne = t + HALO
        d1 = pltpu.roll(ext, ne - 1, 0)[:t]
        d2 = pltpu.roll(ext, ne - 2, 0)[:t]
        d_yb0 = cw[2:3, :] * dc + cw[1:2, :] * d1 + cw[0:1, :] * d2
        put(CG, d_yb0 * r["xb"])
        put(XB, d_yb0 * r["cg"])
        car_dc[...] = dc[:HALO]

        zc, sc = r["zc"], r["sc"]
        d_cbr = d_cin * (zc * sc)
        put(ZC, d_cin * r["c_br"] * dsilu(zc, sc))
        vec_acc[6] += _colsum8(d_cbr * r["q"])
        d_q = d_cbr * ps[...]
        for gi, win in enumerate(POOL_WINDOWS):
            cols = slice(gi * 128, (gi + 1) * 128)
            dqb = d_q[:, cols].astype(BF16)
            d_pool = _dotg(dqb, wpool[gi], NT)
            gwpool_ref[gi] += _dotg(r["pooled"][gi].astype(BF16), dqb, TN)
            e = d_pool * r["inv"][gi]
            sx = jnp.concatenate([e, car_e[:, cols]], axis=0)
            sh = 1
            while sh < win:
                sx = sx + pltpu.roll(sx, ne - sh, 0)
                sh *= 2
            put(XC + gi * 128, sx[:t] - d_pool)
            car_e[:, cols] = e[:HALO]

        @pl.when(i == n - 1)
        def _():
            for acc, stage in ((gwpa, wpa), (gwpb, wpb), (gwpc, wpc)):
                stage[...] = acc[...].astype(BF16)
            stores = by_chip_copies((wpa, wpb, wpc), gwp_h, True)
            for cp in stores:
                cp.start()
            for cp in stores:
                cp.wait()
            gbs_ref[...] = jnp.dot(gbs_acc[...], sel_ref[...], preferred_element_type=F32,
                                   precision=lax.Precision.HIGHEST)
            red = lambda k: jnp.sum(vec_acc[k], axis=0, keepdims=True)
            glng_ref[...] = red(0)
            glnb_ref[...] = red(1)
            gcb_ref[...] = red(2)
            gcw_ref[...] = jnp.zeros(gcw_ref.shape, F32)
            for k in range(3):
                gcw_ref[k:k + 1, :] = red(3 + k)
            gps_ref[...] = red(6)
            tt = lax.broadcasted_iota(jnp.int32, (2 * CHUNK, CHUNK), 0) % CHUNK
            ss = lax.broadcasted_iota(jnp.int32, (2 * CHUNK, CHUNK), 1)
            for j in range(4):
                gws_ref[j] = jnp.where(tt >= ss, gws_ref[j], 0.0)

    rtile = lambda c: pl.BlockSpec((t, c), lambda i: (n - 1 - i, 0))
    lsel = lambda *blk: pl.BlockSpec((None,) + blk, lambda i: (l,) + (0,) * len(blk))
    f32s = lambda *shape: jax.ShapeDtypeStruct(shape, F32)
    return _pcall(
        body, name=f"b1_l{l}", grid=(n,),
        in_specs=[rtile(IN_TOTAL)] + _halo_specs(t, rev_n=n) + [rtile(D_MODEL)] * 4 + [
            lsel(1, WIDTH), lsel(1, WIDTH), lsel(4, 256, 128), lsel(4, 256, 128), lsel(CHUNK, WIDTH),
            lsel(8, WIDTH), lsel(1, WIDTH), lsel(4, 128, 128), lsel(1, WIDTH),
            ANY, ANY, _const_spec((WIDTH, 128))],
        out_specs=[rtile(IN_TOTAL), ANY,
                   _const_spec((4, 256, 128)), _const_spec((CHUNK, 128)), _const_spec((1, WIDTH)),
                   _const_spec((1, WIDTH)), _const_spec((8, WIDTH)), _const_spec((1, WIDTH)),
                   _const_spec((4, 128, 128)), _const_spec((1, WIDTH))],
        out_shape=[jax.ShapeDtypeStruct((s, IN_TOTAL), BF16), jax.ShapeDtypeStruct((N_CHIPS, 3 * WIDTH, 256), BF16),
                   f32s(4, 256, 128), f32s(CHUNK, 128),
                   f32s(1, WIDTH), f32s(1, WIDTH), f32s(8, WIDTH), f32s(1, WIDTH), f32s(4, 128, 128),
                   f32s(1, WIDTH)],
        scratch=[pltpu.VMEM((WIDTH, D_MODEL), BF16), pltpu.VMEM((WIDTH, D_MODEL), BF16),
                 pltpu.VMEM((WIDTH, D_MODEL), BF16), pltpu.VMEM((D_MODEL, D_MODEL), BF16),
                 pltpu.VMEM((WIDTH, D_MODEL), F32),
                 pltpu.VMEM((WIDTH, D_MODEL), F32), pltpu.VMEM((WIDTH, D_MODEL), F32),
                 pltpu.VMEM((CHUNK, WIDTH), F32), pltpu.VMEM((8, 8, WIDTH), F32),
                 pltpu.VMEM((t, WIDTH), F32), pltpu.VMEM((t, WIDTH), F32),
                 pltpu.VMEM((HALO, WIDTH), F32), pltpu.VMEM((HALO, WIDTH), F32),
                 pltpu.SemaphoreType.DMA((3 * N_CHIPS,))],
        sem=("arbitrary",),
        args=[p, p, p, p, dout, ya, yb, yc, lw["ln_g"], lw["ln_b"], lw["w2"], lw["wt2"], lw["bst"], lw["cw"],
              lw["cb"], lw["wpool"], lw["ps"], lw["proj"][l], lw["wo"][l], lw["sel"]], comm=comm)


def _rms_bwd(xv, g, dh):
    r = lax.rsqrt(jnp.mean(xv * xv, axis=-1, keepdims=True) + RMS_EPS)
    xhat = xv * r
    dxh = dh * g
    dx = r * (dxh - xhat * jnp.mean(dxh * xhat, axis=-1, keepdims=True))
    return dx, dh * xhat


def _b2a(dp, w_all, x, dout, norm_g3, l, comm=None, tm=256):
    s = x.shape[0]
    nm = s // tm

    def body(dp_ref, w_hbm, x_ref, do_ref, g_ref, dx_ref, gg_ref, w_vmem, gacc):
        i = pl.program_id(0)

        @pl.when(i == 0)
        def _():
            pltpu.sync_copy(w_hbm, w_vmem)
            gacc[...] = jnp.zeros(gacc.shape, F32)

        dh = _dotg(dp_ref[...], w_vmem[...], NT)
        dx, gx = _rms_bwd(x_ref[...], g_ref[...], dh)
        dx_ref[...] = do_ref[...] + dx
        gacc[...] += _colsum8(gx)

        @pl.when(i == nm - 1)
        def _():
            gg_ref[...] = jnp.sum(gacc[...], axis=0, keepdims=True)

    return _pcall(
        body, name=f"b2a_l{l}", grid=(nm,),
        in_specs=[pl.BlockSpec((tm, IN_TOTAL), lambda i: (i, 0)), ANY,
                  pl.BlockSpec((tm, D_MODEL), lambda i: (i, 0)),
                  pl.BlockSpec((tm, D_MODEL), lambda i: (i, 0)),
                  pl.BlockSpec((None, 1, D_MODEL), lambda i: (l, 0, 0))],
        out_specs=[pl.BlockSpec((tm, D_MODEL), lambda i: (i, 0)),
                   pl.BlockSpec((1, D_MODEL), lambda i: (0, 0))],
        out_shape=[jax.ShapeDtypeStruct((s, D_MODEL), F32), jax.ShapeDtypeStruct((1, D_MODEL), F32)],
        scratch=[pltpu.VMEM((D_MODEL, IN_TOTAL), BF16), pltpu.VMEM((8, D_MODEL), F32)],
        sem=("arbitrary",), args=[dp, w_all, x, dout, norm_g3], comm=comm)


def _b2b(h, dp, name, chip, part="all", comm=None, tk=2048):
    s = h.shape[0]
    tk = min(tk, s)
    nk = s // tk
    first, count = dict(all=(0, 4), others=(1, 3), own=(0, 1))[part]
    cols_of = lambda k, idx: (idx[0] + first + k) % N_CHIPS

    def body(idx_ref, h_ref, dp_ref, g_ref, acc):
        kk = pl.program_id(1)

        @pl.when(kk == 0)
        def _():
            acc[...] = jnp.zeros(acc.shape, F32)

        acc[...] += _dotg(h_ref[...], dp_ref[...], TN)

        @pl.when(kk == nk - 1)
        def _():
            g_ref[...] = acc[...].astype(BF16)

    own = part == "own"
    return _pcall(
        body, name=name, grid=(count, nk),
        in_specs=[pl.BlockSpec((tk, D_MODEL), lambda k, kk, idx: (kk, 0)),
                  pl.BlockSpec((tk, SHARD_W), lambda k, kk, idx: (kk, cols_of(k, idx)))],
        out_specs=[pl.BlockSpec((D_MODEL, SHARD_W), lambda k, kk, idx: (0, 0 if own else cols_of(k, idx)))],
        out_shape=[jax.ShapeDtypeStruct((D_MODEL, SHARD_W if own else IN_TOTAL), BF16)],
        scratch=[pltpu.VMEM((D_MODEL, SHARD_W), F32)],
        sem=("arbitrary", "arbitrary"), args=[h, dp], comm=comm, prefetch=[chip])


def _gwo(m, dout, l, tk=2048):
    s = m.shape[0]
    tk = min(tk, s)
    nk = s // tk

    def body(m_ref, do_ref, g_ref, acc):
        kk = pl.program_id(0)

        @pl.when(kk == 0)
        def _():
            acc[...] = jnp.zeros(acc.shape, F32)

        acc[...] += _dotg(m_ref[...], do_ref[...].astype(BF16), TN)

        @pl.when(kk == nk - 1)
        def _():
            g_ref[...] = acc[...].astype(BF16)

    return pl.pallas_call(
        body, name=f"gwo_l{l}", grid=(nk,),
        in_specs=[pl.BlockSpec((tk, D_MODEL), lambda kk: (kk, 0)), pl.BlockSpec((tk, D_MODEL), lambda kk: (kk, 0))],
        out_specs=_const_spec((D_MODEL, D_MODEL)),
        out_shape=jax.ShapeDtypeStruct((D_MODEL, D_MODEL), BF16),
        scratch_shapes=[pltpu.VMEM((D_MODEL, D_MODEL), F32)],
        compiler_params=_params(("arbitrary",)),
    )(m, dout)


def _loss_head(x, tgt, final_g2, tm=512):
    s = x.shape[0]
    nm = s // tm

    def body(x_ref, t_ref, g_ref, dx_ref, loss_ref, gg_ref, lacc, gacc):
        i = pl.program_id(0)

        @pl.when(i == 0)
        def _():
            lacc[...] = jnp.zeros(lacc.shape, F32)
            gacc[...] = jnp.zeros(gacc.shape, F32)

        xv = x_ref[...]
        g = g_ref[...]
        r = lax.rsqrt(jnp.mean(xv * xv, axis=-1, keepdims=True) + RMS_EPS)
        err = (xv * r) * g - t_ref[...]
        lacc[...] += _colsum8(err * err)
        dx, gx = _rms_bwd(xv, g, err * (1.0 / D_MODEL))
        dx_ref[...] = dx
        gacc[...] += _colsum8(gx)

        @pl.when(i == nm - 1)
        def _():
            tot = jnp.sum(jnp.sum(lacc[...], axis=0, keepdims=True), axis=1, keepdims=True)
            loss_ref[...] = jnp.broadcast_to(tot * (0.5 / D_MODEL), loss_ref.shape)
            gg_ref[...] = jnp.sum(gacc[...], axis=0, keepdims=True)

    return pl.pallas_call(
        body, name="loss_head", grid=(nm,),
        in_specs=[pl.BlockSpec((tm, D_MODEL), lambda i: (i, 0)), pl.BlockSpec((tm, D_MODEL), lambda i: (i, 0)),
                  _const_spec((1, D_MODEL))],
        out_specs=[pl.BlockSpec((tm, D_MODEL), lambda i: (i, 0)), _const_spec((8, 128)),
                   _const_spec((1, D_MODEL))],
        out_shape=[jax.ShapeDtypeStruct((s, D_MODEL), F32), jax.ShapeDtypeStruct((8, 128), F32),
                   jax.ShapeDtypeStruct((1, D_MODEL), F32)],
        scratch_shapes=[pltpu.VMEM((8, D_MODEL), F32), pltpu.VMEM((8, D_MODEL), F32)],
        compiler_params=_params(("arbitrary",)),
    )(x, tgt, final_g2)


def _row_block(rows, cols, n_arrays):
    budget = VMEM_LIMIT // 3 // (2 * 4 * n_arrays * cols)
    rb = rows
    while rb > budget and rb % 16 == 0:
        rb //= 2
    return rb


def _cast_slot(name, a, l, chip):
    _, rows, cols = a.shape
    rb = _row_block(rows, cols, 2)

    def body(idx_ref, a_ref, o_ref):
        o_ref[...] = a_ref[...].astype(BF16)

    gs = pltpu.PrefetchScalarGridSpec(
        num_scalar_prefetch=1, grid=(rows // rb,),
        in_specs=[pl.BlockSpec((None, rb, cols), lambda i, idx: (l, i, 0))],
        out_specs=pl.BlockSpec((None, rb, cols), lambda i, idx: (idx[0], i, 0)))
    return pl.pallas_call(body, name=name, grid_spec=gs, out_shape=jax.ShapeDtypeStruct((N_CHIPS, rows, cols), BF16),
                          compiler_params=_params(("arbitrary",)))(chip, a)


def _cast_cols(name, a, l, chip):
    _, rows, cols = a.shape
    rb = _row_block(rows, cols, 2)

    def body(idx_ref, a_ref, o_ref):
        o_ref[...] = a_ref[...].astype(BF16)

    gs = pltpu.PrefetchScalarGridSpec(
        num_scalar_prefetch=1, grid=(rows // rb,),
        in_specs=[pl.BlockSpec((None, rb, cols), lambda i, idx: (l, i, 0))],
        out_specs=pl.BlockSpec((rb, cols), lambda i, idx: (i, idx[0])))
    return pl.pallas_call(body, name=name, grid_spec=gs, out_shape=jax.ShapeDtypeStruct((rows, N_CHIPS * cols), BF16),
                          compiler_params=_params(("arbitrary",)))(chip, a)


def _sum4(name, own, chip, recv, l, n_layers, prev=None):
    _, rows, cols = recv.shape
    rb = _row_block(rows, cols, 5)
    nb = rows // rb
    if len(own.shape) == 3:
        own_spec = pl.BlockSpec((None, rb, cols), lambda i, idx: (idx[0], i, 0))
    elif own.shape[1] == cols:
        own_spec = pl.BlockSpec((rb, cols), lambda i, idx: (i, 0))
    else:
        own_spec = pl.BlockSpec((rb, cols), lambda i, idx: (i, idx[0]))

    def body(idx_ref, own_ref, r_ref, *rest):
        o_ref = rest[-1]
        o_ref[...] = ((own_ref[...].astype(F32) + r_ref[0].astype(F32)) + r_ref[1].astype(F32)) + r_ref[2].astype(F32)

    gs = pltpu.PrefetchScalarGridSpec(
        num_scalar_prefetch=1, grid=(nb,),
        in_specs=[own_spec,
                  pl.BlockSpec((3, rb, cols), lambda i, idx: (0, i, 0))] + ([ANY] if prev is not None else []),
        out_specs=pl.BlockSpec((rb, cols), lambda i, idx: (l * nb + i, 0)))
    args = (chip, own, recv) + ((prev,) if prev is not None else ())
    return pl.pallas_call(body, name=name, grid_spec=gs,
                          out_shape=jax.ShapeDtypeStruct((n_layers * rows, cols), F32),
                          input_output_aliases=({3: 0} if prev is not None else {}),
                          compiler_params=_params(("arbitrary",)))(*args)


def _sum8(name, pack, me, recv):
    rows = recv.shape[1]

    def body(idx_ref, own_ref, r_ref, o_ref):
        acc = own_ref[...]
        for j in range(7):
            acc = acc + r_ref[j]
        o_ref[...] = acc

    gs = pltpu.PrefetchScalarGridSpec(
        num_scalar_prefetch=1, grid=(1,),
        in_specs=[pl.BlockSpec((rows, 128), lambda i, idx: (idx[0], 0)),
                  pl.BlockSpec((7, rows, 128), lambda i, idx: (0, 0, 0))],
        out_specs=pl.BlockSpec((rows, 128), lambda i, idx: (idx[0], 0)))
    return pl.pallas_call(body, name=name, grid_spec=gs, out_shape=jax.ShapeDtypeStruct(pack.shape, F32),
                          compiler_params=_params(("arbitrary",)))(me, pack, recv)


_SMALL = ("norm_g", "ln_g", "ln_b", "w_s", "b_s", "conv_b", "w_pool", "pool_scale", "final_g", "conv_w")
_SMALL_SHAPES = dict(norm_g=(DEPTH, D_MODEL), ln_g=(DEPTH, WIDTH), ln_b=(DEPTH, WIDTH), w_s=(DEPTH, 8, CHUNK, CHUNK),
                     b_s=(DEPTH, 8, CHUNK), conv_b=(DEPTH, WIDTH), w_pool=(DEPTH, 4, 128, 128),
                     pool_scale=(DEPTH, WIDTH), final_g=(1, D_MODEL), conv_w=(DEPTH, 3, WIDTH))


def _small_rows():
    base, r = {}, 0
    for nm in _SMALL:
        base[nm] = r
        size = 1
        for d in _SMALL_SHAPES[nm]:
            size *= d
        r += size // 128
    return base, -(-r // 64) * 64


def _pack_small(name, raw, gng1, g_final):
    base, rows = _small_rows()
    n_l = len(raw[0])

    def body(*refs):
        o = refs[-1]
        per_layer = [refs[l * n_l:(l + 1) * n_l] for l in range(DEPTH)]
        gng1_ref, gfin_ref = refs[DEPTH * n_l], refs[DEPTH * n_l + 1]
        o[...] = jnp.zeros(o.shape, F32)

        def put_row_vector(r0, ref, width):
            for j in range(width // 128):
                o[r0 + j:r0 + j + 1, :] = ref[0:1, j * 128:(j + 1) * 128]

        put_row_vector(base["norm_g"] + D_MODEL // 128, gng1_ref, D_MODEL)
        put_row_vector(base["final_g"], gfin_ref, D_MODEL)
        for l in range(DEPTH):
            gws, gbs, glng, glnb, gcw, gcb, gwpool, gps = per_layer[l]
            put_row_vector(base["ln_g"] + l * 4, glng, WIDTH)
            put_row_vector(base["ln_b"] + l * 4, glnb, WIDTH)
            put_row_vector(base["conv_b"] + l * 4, gcb, WIDTH)
            put_row_vector(base["pool_scale"] + l * 4, gps, WIDTH)
            o[base["w_s"] + l * 1024:base["w_s"] + (l + 1) * 1024, :] = gws[...].reshape(1024, 128)
            o[base["w_pool"] + l * 512:base["w_pool"] + (l + 1) * 512, :] = gwpool[...].reshape(512, 128)
            o[base["b_s"] + l * 8:base["b_s"] + (l + 1) * 8, :] = gbs[...].T[0:8, :]
            for k in range(3):
                for ch in range(N_CHIPS):
                    r = base["conv_w"] + (l * 3 + k) * N_CHIPS + ch
                    o[r:r + 1, :] = gcw[k:k + 1, ch * 128:(ch + 1) * 128]

    args = [a for l in range(DEPTH) for a in raw[l]] + [gng1, g_final]
    vm = pl.BlockSpec(memory_space=pltpu.VMEM)
    return pl.pallas_call(body, name=name, in_specs=[vm] * len(args), out_specs=vm,
                          out_shape=jax.ShapeDtypeStruct((rows, 128), F32), compiler_params=_params())(*args)


def _adamw_small(name, gred, chip, w, m, v):
    base, _ = _small_rows()

    def body(chip_ref, g_ref, *refs):
        n = len(_SMALL)
        w_r, m_r, v_r = refs[:n], refs[n:2 * n], refs[2 * n:3 * n]
        out = refs[3 * n:]

        def update(i, idx, g):
            d, mn, vn = _adamw_math(w_r[i][idx], g, m_r[i][idx], v_r[i][idx])
            for o, val in zip(out[4 * i:4 * i + 4], (g, d, mn, vn)):
                o[idx] = val

        for i, nm in enumerate(_SMALL):
            shape = _SMALL_SHAPES[nm]
            if nm == "conv_w":
                for l in range(DEPTH):
                    for k in range(3):
                        row = base[nm] + (l * 3 + k) * N_CHIPS + chip_ref[0]
                        update(i, (l, slice(k, k + 1), slice(None)), g_ref[pl.ds(row, 1), :])
            elif len(shape) == 2:
                per = shape[1] // 128
                for l in range(shape[0]):
                    for j in range(per):
                        r = base[nm] + l * per + j
                        update(i, (slice(l, l + 1), slice(j * 128, (j + 1) * 128)), g_ref[r:r + 1, :])
            else:
                rows = 1
                for dim in shape[:-1]:
                    rows *= dim
                update(i, (Ellipsis,), g_ref[base[nm]:base[nm] + rows, :].reshape(shape))

    arrs = [d[nm] for d in (w, m, v) for nm in _SMALL]
    vm = pl.BlockSpec(memory_space=pltpu.VMEM)
    gs = pltpu.PrefetchScalarGridSpec(num_scalar_prefetch=1, grid=(1,), in_specs=[vm] * (1 + len(arrs)),
                                      out_specs=[vm] * (4 * len(_SMALL)))
    outs = pl.pallas_call(
        body, name=name, grid_spec=gs,
        out_shape=[jax.ShapeDtypeStruct(w[nm].shape, F32) for nm in _SMALL for _ in range(4)],
        compiler_params=_params(("arbitrary",)))(chip, gred, *arrs)
    return {nm: list(outs[4 * i:4 * i + 4]) for i, nm in enumerate(_SMALL)}


def _adamw_math(w, g, m, v):
    m = ADAM_B1 * m + (1.0 - ADAM_B1) * g
    v = ADAM_B2 * v + (1.0 - ADAM_B2) * (g * g)
    m_hat = m / (1.0 - ADAM_B1 ** ADAM_STEP)
    v_hat = v / (1.0 - ADAM_B2 ** ADAM_STEP)
    delta = -ADAM_LR * (m_hat / (jnp.sqrt(v_hat) + ADAM_EPS) + ADAM_WD * w)
    return delta, m, v


def _adamw(name, w, m, v, g_parts):
    rows, cols = w.shape
    np_ = len(g_parts)
    rb = _row_block(rows, cols, 7 + np_)

    def body(*refs):
        w_ref, m_ref, v_ref = refs[:3]
        g_refs = refs[3:3 + np_]
        go_ref, d_ref, mo_ref, vo_ref = refs[3 + np_:]
        g = g_refs[0][...]
        for gr in g_refs[1:]:
            g = g + gr[...]
        d, mn, vn = _adamw_math(w_ref[...], g, m_ref[...], v_ref[...])
        go_ref[...] = g
        d_ref[...] = d
        mo_ref[...] = mn
        vo_ref[...] = vn

    spec = pl.BlockSpec((rb, cols), lambda i: (i, 0))
    shp = jax.ShapeDtypeStruct((rows, cols), F32)
    return pl.pallas_call(body, name=name, grid=(rows // rb,), in_specs=[spec] * (3 + np_),
                          out_specs=[spec] * 4, out_shape=[shp] * 4,
                          compiler_params=_params(("arbitrary",)))(w, m, v, *g_parts)


def _all_reduce_small(name, a):
    def body(a_ref, o_ref, buf, send_sems, recv_sems):
        x, y, c = _place()
        o_ref[...] = a_ref[...]
        for rnd, peer in enumerate(((x, y, 1 - c), (x, 1 - y, c), (1 - x, y, c))):
            cp = pltpu.make_async_remote_copy(
                src_ref=o_ref, dst_ref=buf.at[rnd], send_sem=send_sems.at[rnd], recv_sem=recv_sems.at[rnd],
                device_id=peer, device_id_type=MESH)
            cp.start()
            cp.wait_recv()
            cp.wait_send()
            o_ref[...] = o_ref[...] + buf[rnd]

    vm = pl.BlockSpec(memory_space=pltpu.VMEM)
    return pl.pallas_call(
        body, name=name, in_specs=[vm], out_specs=vm, out_shape=jax.ShapeDtypeStruct(a.shape, F32),
        scratch_shapes=[pltpu.VMEM((3,) + a.shape, F32), pltpu.SemaphoreType.DMA((3,)),
                        pltpu.SemaphoreType.DMA((3,))],
        compiler_params=_params(),
    )(a)


def kernel(x, norm_g, w_in, ln_g, ln_b, w_s, b_s, conv_w, conv_b, w_pool, pool_scale, w_pa, w_pb, w_pc, w_o, final_g, loss_target, m_norm_g, m_w_in, m_ln_g, m_ln_b, m_w_s, m_b_s, m_conv_w, m_conv_b, m_w_pool, m_pool_scale, m_w_pa, m_w_pb, m_w_pc, m_w_o, m_final_g, v_norm_g, v_w_in, v_ln_g, v_ln_b, v_w_s, v_b_s, v_conv_w, v_conv_b, v_w_pool, v_pool_scale, v_w_pa, v_w_pb, v_w_pc, v_w_o, v_final_g):
    W = dict(norm_g=norm_g, w_in=w_in, ln_g=ln_g, ln_b=ln_b, w_s=w_s, b_s=b_s, conv_w=conv_w, conv_b=conv_b,
             w_pool=w_pool, pool_scale=pool_scale, w_pa=w_pa, w_pb=w_pb, w_pc=w_pc, w_o=w_o, final_g=final_g)
    M = dict(norm_g=m_norm_g, w_in=m_w_in, ln_g=m_ln_g, ln_b=m_ln_b, w_s=m_w_s, b_s=m_b_s, conv_w=m_conv_w,
             conv_b=m_conv_b, w_pool=m_w_pool, pool_scale=m_pool_scale, w_pa=m_w_pa, w_pb=m_w_pb, w_pc=m_w_pc,
             w_o=m_w_o, final_g=m_final_g)
    Vv = dict(norm_g=v_norm_g, w_in=v_w_in, ln_g=v_ln_g, ln_b=v_ln_b, w_s=v_w_s, b_s=v_b_s, conv_w=v_conv_w,
              conv_b=v_conv_b, w_pool=v_w_pool, pool_scale=v_pool_scale, w_pa=v_w_pa, w_pb=v_w_pb, w_pc=v_w_pc,
              w_o=v_w_o, final_g=v_final_g)
    L = DEPTH
    s = x.shape[1]
    xs = x.reshape(s, D_MODEL)
    tgt = loss_target.reshape(s, D_MODEL)
    k_me = (2 * lax.axis_index("x") + lax.axis_index("y")).astype(jnp.int32)

    chip = k_me.reshape(1)
    assert L == 2
    half_rows = D_MODEL // 2

    land_win = [_cast_cols(f"cast_w_in{l}", w_in, l, chip) for l in range(L)]
    pcat = lambda d: jnp.concatenate([d[b][l] for l in range(L) for b in ("w_pa", "w_pb", "w_pc")], axis=0)
    pcat_w = pcat(W).reshape(L, 3 * WIDTH, 256)
    land_proj = [_cast_slot(f"cast_proj{l}", pcat_w, l, chip) for l in range(L)]
    land_wo = [_cast_slot(f"cast_w_o{l}", w_o, l, chip) for l in range(L)]
    cw_sh = jnp.pad(conv_w, ((0, 0), (0, 5), (0, 0))).reshape(1, L * 8, 128)
    land_cw = lax.dynamic_update_slice(jnp.zeros((N_CHIPS, L * 8, 128), F32), cw_sh, (k_me, 0, 0))

    causal = jnp.tril(jnp.ones((CHUNK, CHUNK), dtype=bool))
    w_m = jnp.where(causal, w_s, 0.0)
    lw = dict(
        ln_g=ln_g.reshape(L, 1, WIDTH), ln_b=ln_b.reshape(L, 1, WIDTH),
        w2=w_m.reshape(L, 4, 256, CHUNK).astype(BF16),
        wt2=jnp.swapaxes(w_m, -1, -2).reshape(L, 4, 256, CHUNK).astype(BF16),
        bst=jnp.repeat(jnp.swapaxes(b_s, -1, -2), 64, axis=-1),
        cb=conv_b.reshape(L, 1, WIDTH), wpool=w_pool.astype(BF16), ps=pool_scale.reshape(L, 1, WIDTH),
        sel=(jnp.arange(WIDTH)[:, None] // 64 == jnp.arange(128)[None, :]).astype(F32))
    norm_g3 = norm_g.reshape(L, 1, D_MODEL)

    cm = _Comm()
    cm.gather_half(cm.through(land_win[0]))
    (p0, h0), (w_half0,), _ = _f1_own(xs, norm_g3, w_in, chip, cm)
    cm = _Comm()
    cm.forward_half(cm.through(w_half0))
    (w_all0,) = _comm_only("forward_w_in0", cm)
    first_rows = D_MODEL // 4
    cm = _Comm()
    for buf in (land_proj[0], land_wo[0], land_cw):
        cm.gather(cm.through(buf))
    cm.gather(cm.through(land_win[1]), 0, first_rows)
    (p0,), (g_proj0, g_wo0, g_cw, w_part1), _ = _f1_others(h0, p0, w_all0, chip, cm)
    lw["proj"] = [g_proj0, None]
    lw["wo"] = [g_wo0.reshape(D_MODEL, D_MODEL), None]
    lw["cw"] = g_cw.reshape(N_CHIPS, L, 8, 128).transpose(1, 2, 0, 3).reshape(L, 8, WIDTH)
    cm = _Comm()
    cm.gather(cm.through(w_part1), first_rows, D_MODEL - first_rows)
    (ya0, yb0, yc0, mm0, x1), (w_all1,), _ = _f2(xs, p0, lw, 0, comm=cm)
    cm = _Comm()
    for buf in (land_proj[1], land_wo[1]):
        cm.gather(cm.through(buf))
    (p1, h1), (g_proj1, g_wo1), _ = _f1(x1, norm_g3, w_all1, 1, comm=cm)
    lw["proj"][1] = g_proj1
    lw["wo"][1] = g_wo1.reshape(D_MODEL, D_MODEL)
    (ya1, yb1, yc1, mm1, x2), _, _ = _f2(x1, p1, lw, 1)
    dxl, loss_blk, g_final = _loss_head(x2, tgt, final_g.reshape(1, D_MODEL))
    loss = lax.psum(loss_blk[0, 0], ("x", "y", "c"))

    by_chip = lambda gwp, gwo: (gwp, gwo.reshape(N_CHIPS, 256, D_MODEL))
    recv_like = lambda a: ((3,) + a.shape[1:], a.dtype)
    part_rows = (D_MODEL, 3 * WIDTH, 256)
    gwin, recv_win, part_proj, part_wo, recv_proj, recv_wo = ([None] * L for _ in range(6))

    def sums(l, half):
        return [_sum4(f"sum_w_in{l}", gwin[l], chip, recv_win[l], l, L, half[0]),
                _sum4(f"sum_proj{l}", part_proj[l], chip, recv_proj[l], l, L, half[1]),
                _sum4(f"sum_w_o{l}", part_wo[l], chip, recv_wo[l], l, L, half[2])]

    gwo1 = _gwo(mm1, dxl, 1)
    (dp1, gwp1, *sm1), _, _ = _b1(p1, dxl, ya1, yb1, yc1, lw, 1)
    (gwin[1],), _, _ = _b2b(h1, dp1, "b2b_l1", chip)
    cm = _Comm()
    cm.scatter(cm.read(gwin[1]), cm.new((3, D_MODEL, SHARD_W), BF16), 0, half_rows)
    (dxl, gng1), _, (recv_half,) = _b2a(dp1, w_all1, x1, dxl, norm_g3, 1, comm=cm)
    part_proj[1], part_wo[1] = by_chip(gwp1, gwo1)
    gwo0 = _gwo(mm0, dxl, 0)
    cm = _Comm()
    cm.scatter(cm.read(gwin[1]), cm.through(recv_half), half_rows, half_rows)
    cm.scatter(cm.read(part_proj[1]), cm.new(*recv_like(part_proj[1])))
    cm.scatter(cm.read(part_wo[1]), cm.new(*recv_like(part_wo[1])))
    (dp0, gwp0, *sm0), (recv_win[1],), (recv_proj[1], recv_wo[1]) = _b1(p0, dxl, ya0, yb0, yc0, lw, 0, comm=cm)
    half = sums(1, [None, None, None])
    part_proj[0], part_wo[0] = by_chip(gwp0, gwo0)

    gpack = _pack_small("pack_small", [sm0, sm1], gng1, g_final)
    slice_rows = gpack.shape[0] // 8

    cm = _Comm()
    cm.scatter(cm.read(part_proj[0]), cm.new(*recv_like(part_proj[0])))
    cm.scatter(cm.read(part_wo[0]), cm.new(*recv_like(part_wo[0])))
    for a in range(3):
        cm.sibling(cm.read(half[a]), cm.new(half[a].shape, F32), part_rows[a], part_rows[a])
    cm.slices_out(cm.read(gpack), cm.new((7, slice_rows, 128), F32), slice_rows)
    (g_others,), _, (recv_proj[0], recv_wo[0], *other, slices_in) = _b2b(h0, dp0, "b2b_l0_others", chip, "others",
                                                                        comm=cm)
    me = (4 * lax.axis_index("x") + 2 * lax.axis_index("y") + lax.axis_index("c")).astype(jnp.int32).reshape(1)
    gsum = _sum8("sum_small", gpack, me, slices_in)
    early_rows = D_MODEL // 8
    cm = _Comm()
    cm.scatter(cm.read(g_others), cm.new((3, D_MODEL, SHARD_W), BF16), 0, early_rows)
    cm.slices_back(cm.through(gsum), slice_rows)
    (gwin[0],), (gred,), (recv_early,) = _b2b(h0, dp0, "b2b_l0_own", chip, "own", comm=cm)
    cm = _Comm()
    cm.scatter(cm.read(g_others), cm.through(recv_early), early_rows, D_MODEL - early_rows)
    (dxl, gng0), (recv_win[0],), _ = _b2a(dp0, w_all0, xs, dxl, norm_g3, 0, comm=cm)
    grad_x = dxl.reshape(1, s, D_MODEL)

    half = sums(0, half)
    cm = _Comm()
    for a in range(3):
        cm.sibling(cm.read(half[a]), cm.through(other[a]), 0, part_rows[a])
    other = _comm_only("swap_halves0", cm)
    gng0 = _all_reduce_small("all_reduce_norm_g0", gng0.reshape(8, 128))
    gred = lax.dynamic_update_slice(gred, gng0, (0, 0))

    outs = {}
    shard2d = dict(w_in=(L * D_MODEL, SHARD_W), w_o=(L * 256, D_MODEL))
    for a, name in ((0, "w_in"), (2, "w_o")):
        r2 = shard2d[name]
        res = _adamw(f"adamw_{name}", W[name].reshape(r2), M[name].reshape(r2), Vv[name].reshape(r2),
                     [half[a], other[a]])
        outs[name] = [o.reshape(W[name].shape) for o in res]
    res = _adamw("adamw_proj", pcat(W), pcat(M), pcat(Vv), [half[1], other[1]])
    for i, name in enumerate(("w_pa", "w_pb", "w_pc")):
        outs[name] = [o.reshape(L, 3, WIDTH, 256)[:, i] for o in res]

    as_rows = lambda d: {nm: (d[nm].reshape(1, D_MODEL) if nm == "final_g" else d[nm]) for nm in _SMALL}
    outs.update(_adamw_small("adamw_small", gred, chip, as_rows(W), as_rows(M), as_rows(Vv)))
    outs["final_g"] = [o.reshape(D_MODEL) for o in outs["final_g"]]

    order = ("norm_g", "w_in", "ln_g", "ln_b", "w_s", "b_s", "conv_w", "conv_b", "w_pool", "pool_scale",
             "w_pa", "w_pb", "w_pc", "w_o", "final_g")
    return (loss, grad_x, *[outs[nm][0] for nm in order], *[outs[nm][1] for nm in order],
            *[outs[nm][2] for nm in order], *[outs[nm][3] for nm in order])
```

```python
import functools

import jax
import jax.numpy as jnp
from jax import lax
from jax.experimental import pallas as pl
from jax.experimental.pallas import tpu as pltpu

F32 = jnp.float32
BF16 = jnp.bfloat16

D_MODEL = 1024
DEPTH = 2
CHUNK = 128
WIDTH = 512
POOL_WINDOWS = (2, 4, 8, 16)
IN_TOTAL = 7680
N_CHIPS = 4
SHARD_W = IN_TOTAL // N_CHIPS
RMS_EPS = 1e-6
LN_EPS = 1e-5
HALO = 16

U, V, ZA, XB, BG, CG, ZB, XC, ZC, GA, GB, GC = (0, 512, 1024, 1536, 2048, 2560, 3072, 3584, 4096, 4608, 5632, 6656)

ADAM_LR = 0.001
ADAM_B1 = 0.9
ADAM_B2 = 0.999
ADAM_EPS = 1e-08
ADAM_WD = 0.01
ADAM_STEP = 10

VMEM_LIMIT = 56 * 1024 * 1024
MESH = pl.DeviceIdType.MESH
ANY = pl.BlockSpec(memory_space=pl.ANY)
NT = (((1,), (1,)), ((), ()))
TN = (((0,), (0,)), ((), ()))


def _params(sem=None):
    kw = dict(vmem_limit_bytes=VMEM_LIMIT)
    if sem is not None:
        kw["dimension_semantics"] = sem
    return pltpu.CompilerParams(**kw)


def _dot(a, b):
    return jnp.dot(a, b, preferred_element_type=F32)


def _dotg(a, b, dims):
    return lax.dot_general(a, b, dims, preferred_element_type=F32)


def _sigmoid(x):
    return 1.0 / (1.0 + jnp.exp(-x))


_GELU_K = 0.7978845608028654


def _gelu(x):
    th = jnp.tanh(_GELU_K * (x + 0.044715 * (x * x * x)))
    return 0.5 * x * (1.0 + th), th


def _gelu_grad(x, th):
    return 0.5 * (1.0 + th) + 0.5 * x * (1.0 - th * th) * (_GELU_K * (1.0 + 3.0 * 0.044715 * (x * x)))


def _colsum8(x):
    t, c = x.shape
    return jnp.sum(x.reshape(t // 8, 8, c), axis=0)


def _branches_fwd(p_ref, hxb_ref, hcg_ref, hxc_ref, first, tstart, w, sg_scr):
    t = p_ref.shape[0]
    nch = t // CHUNK

    def seg(o, width=WIDTH):
        return p_ref[:, o:o + width].astype(F32)

    lo = lax.broadcasted_iota(jnp.int32, (CHUNK, CHUNK), 1) < 64
    r = {}
    pu = seg(U)
    u_act, th_u = _gelu(pu)
    pv = seg(V)
    vg, th_v = _gelu(pv)
    mu = jnp.mean(vg, axis=-1, keepdims=True)
    xc = vg - mu
    var = jnp.mean(xc * xc, axis=-1, keepdims=True)
    rs = lax.rsqrt(var + LN_EPS)
    vhat = xc * rs
    vn = vhat * w["ln_g"][...] + w["ln_b"][...]
    vnb = vn.astype(BF16)
    for n in range(nch):
        for j in range(4):
            vb = vnb[n * CHUNK:(n + 1) * CHUNK, j * 128:(j + 1) * 128]
            z = _dot(w["w2"][j], vb)
            sg_scr[n * CHUNK:(n + 1) * CHUNK, j * 128:(j + 1) * 128] = (
                jnp.where(lo, z[:CHUNK], z[CHUNK:]) + w["bst"][:, j * 128:(j + 1) * 128])
    sg = sg_scr[...]
    a_br = u_act * sg
    za = seg(ZA)
    sa = _sigmoid(za)
    r.update(pu=pu, th_u=th_u, pv=pv, th_v=th_v, rs=rs, vhat=vhat, vnb=vnb, u_act=u_act, sg=sg,
             a_br=a_br, za=za, sa=sa, a_in=a_br * (za * sa))

    xb = seg(XB)
    cg = seg(CG)
    yb0 = cg * xb
    hal = hcg_ref[...].astype(F32) * hxb_ref[...].astype(F32)
    hal = jnp.where(first, 0.0, hal)
    ext = jnp.concatenate([hal, yb0], axis=0)
    y1 = pltpu.roll(ext, 1, 0)[HALO:]
    y2 = pltpu.roll(ext, 2, 0)[HALO:]
    cw = w["cw"]
    conv = cw[0:1, :] * y2 + cw[1:2, :] * y1 + cw[2:3, :] * yb0 + w["cb"][...]
    bg = seg(BG)
    b_br = bg * conv
    zb = seg(ZB)
    sb = _sigmoid(zb)
    r.update(xb=xb, cg=cg, yb0=yb0, y1=y1, y2=y2, conv=conv, bg=bg, b_br=b_br, zb=zb, sb=sb,
             b_in=b_br * (zb * sb))

    xcv = seg(XC)
    hxc = jnp.where(first, 0.0, hxc_ref[...].astype(F32))
    extc = jnp.concatenate([hxc, xcv], axis=0)
    tpos = tstart + lax.broadcasted_iota(jnp.int32, (t, 1), 0) + 1
    pooled, inv, q = [], [], []
    for gi, win in enumerate(POOL_WINDOWS):
        s = extc[:, gi * 128:(gi + 1) * 128]
        sh = 1
        while sh < win:
            s = s + pltpu.roll(s, sh, 0)
            sh *= 2
        inv_g = jnp.where(tpos >= win, 1.0 / win, 1.0 / jnp.minimum(tpos, win).astype(F32))
        pg = s[HALO:] * inv_g - xcv[:, gi * 128:(gi + 1) * 128]
        pooled.append(pg)
        inv.append(inv_g)
        q.append(_dot(pg.astype(BF16), w["wpool"][gi]))
    qv = jnp.concatenate(q, axis=1)
    c_br = qv * w["ps"][...]
    zc = seg(ZC)
    sc = _sigmoid(zc)
    r.update(pooled=pooled, inv=inv, q=qv, c_br=c_br, zc=zc, sc=sc, c_in=c_br * (zc * sc))
    return r


def _halo_specs(t, rev_n=None):
    def imap(col):
        def f(i):
            ti = i if rev_n is None else rev_n - 1 - i
            return (jnp.maximum(ti * (t // HALO) - 1, 0), col)
        return f
    return [pl.BlockSpec((HALO, WIDTH), imap(XB // WIDTH)),
            pl.BlockSpec((HALO, WIDTH), imap(CG // WIDTH)),
            pl.BlockSpec((HALO, WIDTH), imap(XC // WIDTH))]


def _const_spec(shape):
    nd = len(shape)
    return pl.BlockSpec(shape, lambda *_: (0,) * nd)


def _place():
    return lax.axis_index("x"), lax.axis_index("y"), lax.axis_index("c")


def _chip_peer(x, y, jm):
    px = (1 - x) if (jm & 2) else x
    py = (1 - y) if (jm & 1) else y
    return px, py


def _chip_part(buf, k, r0=0, nr=None):
    if len(buf.shape) == 3:
        return buf.at[k] if nr is None else buf.at[k, pl.ds(r0, nr)]
    cols = buf.shape[1] // N_CHIPS
    rows = pl.ds(0, buf.shape[0]) if nr is None else pl.ds(r0, nr)
    return buf.at[rows, pl.ds(pl.multiple_of(k * cols, 128), cols)]


def _row_range(ref, r0, nr):
    return ref if nr is None else ref.at[pl.ds(r0, nr)]


class _Comm:
    def __init__(self):
        self.ins, self.thru, self.fresh, self.moves, self.n = [], [], [], [], 0

    def read(self, arr):
        self.ins.append(arr)
        return ("ins", len(self.ins) - 1)

    def through(self, arr):
        self.thru.append(arr)
        return ("thru", len(self.thru) - 1)

    def new(self, shape, dtype):
        self.fresh.append(jax.ShapeDtypeStruct(tuple(shape), dtype))
        return ("fresh", len(self.fresh) - 1)

    def _add(self, n, *move):
        self.moves.append(move)
        self.n += n

    def gather(self, buf, r0=0, nr=None):
        self._add(3, "gather", buf, r0, nr)

    def scatter(self, src, dst, r0=0, nr=None):
        self._add(3, "scatter", src, dst, r0, nr)

    def sibling(self, src, dst, r0=0, nr=None):
        self._add(1, "sibling", src, dst, r0, nr)

    def gather_half(self, buf):
        self._add(3, "gather_half", buf)

    def forward_half(self, buf):
        self._add(3, "forward_half", buf)

    def slices_out(self, src, dst, rows):
        self._add(7, "slices_out", src, dst, rows)

    def slices_back(self, buf, rows):
        self._add(7, "slices_back", buf, rows)

    def copies(self, bufs, x, y, c):
        ref = lambda h: bufs[h[0]][h[1]]
        k_me = 2 * x + y
        me = 4 * x + 2 * y + c
        cps = []
        for move in self.moves:
            kind = move[0]
            if kind in ("gather", "scatter"):
                for jj, jm in enumerate((1, 2, 3)):
                    px, py = _chip_peer(x, y, jm)
                    k_peer = 2 * px + py
                    if kind == "gather":
                        _, buf, r0, nr = move
                        mine = _chip_part(ref(buf), k_me, r0, nr)
                        cps.append((mine, mine, _chip_part(ref(buf), k_peer, r0, nr), (px, py, c)))
                    else:
                        _, src, dst, r0, nr = move
                        slot = ref(dst).at[jj] if nr is None else ref(dst).at[jj, pl.ds(r0, nr)]
                        cps.append((_chip_part(ref(src), k_peer, r0, nr), slot, slot, (px, py, c)))
            elif kind in ("gather_half", "forward_half"):
                buf = ref(move[1])
                hr = buf.shape[-2] // 2
                mine_r0, other_r0 = pl.multiple_of(c * hr, 16), pl.multiple_of((1 - c) * hr, 16)
                for jm in (1, 2, 3):
                    px, py = _chip_peer(x, y, jm)
                    k_peer = 2 * px + py
                    if kind == "gather_half":
                        part = _chip_part(buf, k_me, mine_r0, hr)
                        cps.append((part, part, _chip_part(buf, k_peer, mine_r0, hr), (px, py, c)))
                    else:
                        part = _chip_part(buf, k_peer, mine_r0, hr)
                        cps.append((part, part, _chip_part(buf, k_peer, other_r0, hr), (x, y, 1 - c)))
            elif kind == "sibling":
                _, src, dst, r0, nr = move
                land = _row_range(ref(dst), r0, nr)
                cps.append((_row_range(ref(src), r0, nr), land, land, (x, y, 1 - c)))
            else:
                for j in range(1, 8):
                    px = (1 - x) if (j & 4) else x
                    py = (1 - y) if (j & 2) else y
                    pc = (1 - c) if (j & 1) else c
                    peer = 4 * px + 2 * py + pc
                    if kind == "slices_out":
                        _, src, dst, rows = move
                        slot = ref(dst).at[j - 1]
                        cps.append((ref(src).at[pl.ds(pl.multiple_of(peer * rows, 8), rows)], slot, slot,
                                    (px, py, pc)))
                    else:
                        _, buf, rows = move
                        mine = ref(buf).at[pl.ds(pl.multiple_of(me * rows, 8), rows)]
                        cps.append((mine, mine, ref(buf).at[pl.ds(pl.multiple_of(peer * rows, 8), rows)],
                                    (px, py, pc)))
        assert len(cps) == self.n
        return cps


def _pcall(body, *, name, grid, in_specs, out_specs, out_shape, scratch, sem, args, comm=None, prefetch=(),
           aliases=None):
    n_pre = len(prefetch)
    n_in, n_out, n_scr = len(in_specs), len(out_specs), len(scratch)
    io_alias = {n_pre + i: o for i, o in (aliases or {}).items()}
    if comm is None or comm.n == 0:
        gs = pltpu.PrefetchScalarGridSpec(num_scalar_prefetch=n_pre, grid=grid, in_specs=list(in_specs),
                                          out_specs=list(out_specs), scratch_shapes=list(scratch))
        outs = pl.pallas_call(body, name=name, grid_spec=gs, out_shape=list(out_shape), input_output_aliases=io_alias,
                              compiler_params=_params(sem))(*prefetch, *args)
        return list(outs), [], []
    n_ci, n_ct, n_cf, n_cp = len(comm.ins), len(comm.thru), len(comm.fresh), comm.n

    def wrapped(*refs):
        pos = 0
        def take(n):
            nonlocal pos
            got = refs[pos:pos + n]
            pos += n
            return got
        pre = take(n_pre)
        a, ci, _ = take(n_in), take(n_ci), take(n_ct)
        o, ct, cf = take(n_out), take(n_ct), take(n_cf)
        scr = take(n_scr)
        send_sems, recv_sems = take(2)
        first = functools.reduce(jnp.logical_and, [pl.program_id(d) == 0 for d in range(len(grid))])
        last = functools.reduce(jnp.logical_and, [pl.program_id(d) == grid[d] - 1 for d in range(len(grid))])
        x, y, c = _place()
        cps = comm.copies(dict(ins=ci, thru=ct, fresh=cf), x, y, c)

        def copy(i, src, dst, dev):
            return pltpu.make_async_remote_copy(src_ref=src, dst_ref=dst, send_sem=send_sems.at[i],
                                                recv_sem=recv_sems.at[i], device_id=dev, device_id_type=MESH)

        @pl.when(first)
        def _():
            for i, (src, dst, _, dev) in enumerate(cps):
                copy(i, src, dst, dev).start()

        body(*pre, *a, *o, *scr)

        @pl.when(last)
        def _():
            for i, (src, _, land, dev) in enumerate(cps):
                copy(i, src, land, dev).wait_recv()
            for i, (src, dst, _, dev) in enumerate(cps):
                copy(i, src, dst, dev).wait_send()

    thru_shapes = [jax.ShapeDtypeStruct(t.shape, t.dtype) for t in comm.thru]
    gs = pltpu.PrefetchScalarGridSpec(
        num_scalar_prefetch=n_pre, grid=grid,
        in_specs=list(in_specs) + [ANY] * (n_ci + n_ct),
        out_specs=list(out_specs) + [ANY] * (n_ct + n_cf),
        scratch_shapes=list(scratch) + [pltpu.SemaphoreType.DMA((n_cp,)), pltpu.SemaphoreType.DMA((n_cp,))])
    outs = pl.pallas_call(
        wrapped, name=name, grid_spec=gs,
        out_shape=list(out_shape) + thru_shapes + comm.fresh,
        input_output_aliases={**io_alias, **{n_pre + n_in + n_ci + t: n_out + t for t in range(n_ct)}},
        compiler_params=_params(sem),
    )(*prefetch, *args, *comm.ins, *comm.thru)
    outs = list(outs)
    return outs[:n_out], outs[n_out:n_out + n_ct], outs[n_out + n_ct:]


def _comm_only(name, comm):
    def body():
        pass
    _, thru, _ = _pcall(body, name=name, grid=(1,), in_specs=[], out_specs=[], out_shape=[], scratch=[],
                        sem=("arbitrary",), args=[], comm=comm)
    return thru


def _f1(x, norm_g3, w_all, l, comm=None, tm=256):
    s = x.shape[0]

    def body(x_ref, g_ref, w_hbm, p_ref, h_ref, w_vmem):
        @pl.when(pl.program_id(0) == 0)
        def _():
            pltpu.sync_copy(w_hbm, w_vmem)
        xv = x_ref[...]
        r = lax.rsqrt(jnp.mean(xv * xv, axis=-1, keepdims=True) + RMS_EPS)
        hb = ((xv * r) * g_ref[...]).astype(BF16)
        h_ref[...] = hb
        p_ref[...] = _dot(hb, w_vmem[...]).astype(BF16)

    return _pcall(
        body, name=f"f1_l{l}", grid=(s // tm,),
        in_specs=[pl.BlockSpec((tm, D_MODEL), lambda i: (i, 0)),
                  pl.BlockSpec((None, 1, D_MODEL), lambda i: (l, 0, 0)), ANY],
        out_specs=[pl.BlockSpec((tm, IN_TOTAL), lambda i: (i, 0)),
                   pl.BlockSpec((tm, D_MODEL), lambda i: (i, 0))],
        out_shape=[jax.ShapeDtypeStruct((s, IN_TOTAL), BF16), jax.ShapeDtypeStruct((s, D_MODEL), BF16)],
        scratch=[pltpu.VMEM((D_MODEL, IN_TOTAL), BF16)], sem=("arbitrary",), args=[x, norm_g3, w_all], comm=comm)


def _f1_own(x, norm_g3, w_in, chip, comm, tm=256):
    s = x.shape[0]

    def body(idx_ref, x_ref, g_ref, w_hbm, p_ref, h_ref, w_f32, w_vmem):
        @pl.when(pl.program_id(0) == 0)
        def _():
            pltpu.sync_copy(w_hbm.at[0], w_f32)
            w_vmem[...] = w_f32[...].astype(BF16)
        xv = x_ref[...]
        r = lax.rsqrt(jnp.mean(xv * xv, axis=-1, keepdims=True) + RMS_EPS)
        hb = ((xv * r) * g_ref[...]).astype(BF16)
        h_ref[...] = hb
        p_ref[...] = _dot(hb, w_vmem[...]).astype(BF16)

    return _pcall(
        body, name="f1_l0_own", grid=(s // tm,),
        in_specs=[pl.BlockSpec((tm, D_MODEL), lambda i, idx: (i, 0)),
                  pl.BlockSpec((None, 1, D_MODEL), lambda i, idx: (0, 0, 0)), ANY],
        out_specs=[pl.BlockSpec((tm, SHARD_W), lambda i, idx: (i, idx[0])),
                   pl.BlockSpec((tm, D_MODEL), lambda i, idx: (i, 0))],
        out_shape=[jax.ShapeDtypeStruct((s, IN_TOTAL), BF16), jax.ShapeDtypeStruct((s, D_MODEL), BF16)],
        scratch=[pltpu.VMEM((D_MODEL, SHARD_W), F32), pltpu.VMEM((D_MODEL, SHARD_W), BF16)],
        sem=("arbitrary",), args=[x, norm_g3, w_in], comm=comm, prefetch=[chip])


def _f1_others(h, p, w_all, chip, comm, tm=1024):
    s = h.shape[0]
    tm = min(tm, s)
    cols_of = lambda j, idx: (idx[0] + 1 + j) % N_CHIPS

    def body(idx_ref, h_ref, w_hbm, _, p_ref, w_vmem, sems):
        j = pl.program_id(1)

        @pl.when(jnp.logical_and(pl.program_id(0) == 0, j == 0))
        def _():
            loads = [pltpu.make_async_copy(
                w_hbm.at[:, pl.ds(pl.multiple_of(((idx_ref[0] + 1 + q) % N_CHIPS) * SHARD_W, 128), SHARD_W)],
                w_vmem.at[q], sems.at[q]) for q in range(N_CHIPS - 1)]
            for cp in loads:
                cp.start()
            for cp in loads:
                cp.wait()

        p_ref[...] = _dot(h_ref[...], w_vmem[j]).astype(BF16)

    return _pcall(
        body, name="f1_l0_others", grid=(s // tm, N_CHIPS - 1),
        in_specs=[pl.BlockSpec((tm, D_MODEL), lambda i, j, idx: (i, 0)), ANY, ANY],
        out_specs=[pl.BlockSpec((tm, SHARD_W), lambda i, j, idx: (i, cols_of(j, idx)))],
        out_shape=[jax.ShapeDtypeStruct((s, IN_TOTAL), BF16)],
        scratch=[pltpu.VMEM((N_CHIPS - 1, D_MODEL, SHARD_W), BF16), pltpu.SemaphoreType.DMA((N_CHIPS - 1,))],
        sem=("arbitrary", "arbitrary"), args=[h, w_all, p], comm=comm, prefetch=[chip], aliases={2: 0})


def _f2(x, p, lw, l, comm=None, loss=None, t=256):
    s = x.shape[0]
    n = s // t

    def body(p_ref, hxb_ref, hcg_ref, hxc_ref, x_ref, lng, lnb, w2, bst, cw, cb, wpool, ps,
             proj, wo, *rest):
        if loss is None:
            ya_ref, yb_ref, yc_ref, m_ref, xo_ref, sg_scr = rest
        else:
            t_ref, fg_ref, ya_ref, yb_ref, yc_ref, m_ref, xo_ref, loss_ref, gg_ref, sg_scr, lacc, gacc = rest
        i = pl.program_id(0)
        w = dict(ln_g=lng, ln_b=lnb, w2=w2, bst=bst, cw=cw, cb=cb, wpool=wpool, ps=ps)
        r = _branches_fwd(p_ref, hxb_ref, hcg_ref, hxc_ref, i == 0, i * t, w, sg_scr)

        def project(act, b):
            ab = act.astype(BF16)
            return jnp.concatenate([_dot(ab, proj[k, b * WIDTH:(b + 1) * WIDTH, :]) for k in range(N_CHIPS)], axis=1)

        ya = project(r["a_in"], 0)
        yb = project(r["b_in"], 1)
        yc = project(r["c_in"], 2)
        ya_ref[...] = ya.astype(BF16)
        yb_ref[...] = yb.astype(BF16)
        yc_ref[...] = yc.astype(BF16)
        m = (_sigmoid(p_ref[:, GA:GA + D_MODEL].astype(F32)) * ya
             + _sigmoid(p_ref[:, GB:GB + D_MODEL].astype(F32)) * yb
             + _sigmoid(p_ref[:, GC:GC + D_MODEL].astype(F32)) * yc)
        mb = m.astype(BF16)
        m_ref[...] = mb
        xo = x_ref[...] + _dot(mb, wo[...])
        if loss is None:
            xo_ref[...] = xo
        else:
            @pl.when(i == 0)
            def _():
                lacc[...] = jnp.zeros(lacc.shape, F32)
                gacc[...] = jnp.zeros(gacc.shape, F32)

            g = fg_ref[...]
            rr = lax.rsqrt(jnp.mean(xo * xo, axis=-1, keepdims=True) + RMS_EPS)
            err = (xo * rr) * g - t_ref[...]
            lacc[...] += _colsum8(err * err)
            dx, gx = _rms_bwd(xo, g, err * (1.0 / D_MODEL))
            xo_ref[...] = dx
            gacc[...] += _colsum8(gx)

            @pl.when(i == n - 1)
            def _():
                tot = jnp.sum(jnp.sum(lacc[...], axis=0, keepdims=True), axis=1, keepdims=True)
                loss_ref[...] = jnp.broadcast_to(tot * (0.5 / D_MODEL), loss_ref.shape)
                gg_ref[...] = jnp.sum(gacc[...], axis=0, keepdims=True)

    tile = lambda c: pl.BlockSpec((t, c), lambda i: (i, 0))
    lsel = lambda *blk: pl.BlockSpec((None,) + blk, lambda i: (l,) + (0,) * len(blk))
    act = jax.ShapeDtypeStruct((s, D_MODEL), BF16)
    f32s = lambda *shape: jax.ShapeDtypeStruct(shape, F32)
    with_loss = loss is not None
    return _pcall(
        body, name=f"f2_l{l}", grid=(n,),
        in_specs=[tile(IN_TOTAL)] + _halo_specs(t) + [
            tile(D_MODEL), lsel(1, WIDTH), lsel(1, WIDTH), lsel(4, 256, 128), lsel(CHUNK, WIDTH),
            lsel(8, WIDTH), lsel(1, WIDTH), lsel(4, 128, 128), lsel(1, WIDTH),
            _const_spec((N_CHIPS, 3 * WIDTH, 256)), _const_spec((D_MODEL, D_MODEL))]
        + ([tile(D_MODEL), _const_spec((1, D_MODEL))] if with_loss else []),
        out_specs=[tile(D_MODEL)] * 5 + ([_const_spec((8, 128)), _const_spec((1, D_MODEL))] if with_loss else []),
        out_shape=[act, act, act, act, f32s(s, D_MODEL)] + ([f32s(8, 128), f32s(1, D_MODEL)] if with_loss else []),
        scratch=[pltpu.VMEM((t, WIDTH), F32)]
        + ([pltpu.VMEM((8, D_MODEL), F32), pltpu.VMEM((8, D_MODEL), F32)] if with_loss else []),
        sem=("arbitrary",),
        args=[p, p, p, p, x, lw["ln_g"], lw["ln_b"], lw["w2"], lw["bst"], lw["cw"], lw["cb"], lw["wpool"], lw["ps"],
              lw["proj"][l], lw["wo"][l]] + (list(loss) if with_loss else []), comm=comm)


def _b1(p, dout, ya, yb, yc, lw, l, comm=None, t=256):
    s = p.shape[0]
    n = s // t
    nch = t // CHUNK

    def body(p_ref, hxb_ref, hcg_ref, hxc_ref, do_ref, ya_ref, yb_ref, yc_ref,
             lng, lnb, w2, wt2, bst, cw, cb, wpool, ps, proj_h, wo_h, sel_ref,
             dp_ref, gwp_h, gws_ref, gbs_ref, glng_ref, glnb_ref, gcw_ref, gcb_ref,
             gwpool_ref, gps_ref,
             wpa, wpb, wpc, wo, gwpa, gwpb, gwpc, gbs_acc, vec_acc, sg_scr, dvn_scr, car_dc, car_e, psem):
        i = pl.program_id(0)
        ti = n - 1 - i

        def by_chip_copies(vmem_bufs, hbm, to_hbm):
            cps = []
            for b, buf in enumerate(vmem_bufs):
                for k in range(N_CHIPS):
                    v = buf.at[:, pl.ds(k * 256, 256)]
                    h = hbm.at[k, pl.ds(b * WIDTH, WIDTH)]
                    cps.append(pltpu.make_async_copy(v, h, psem.at[b * N_CHIPS + k]) if to_hbm
                               else pltpu.make_async_copy(h, v, psem.at[b * N_CHIPS + k]))
            return cps

        @pl.when(i == 0)
        def _():
            loads = by_chip_copies((wpa, wpb, wpc), proj_h, False)
            for cp in loads:
                cp.start()
            pltpu.sync_copy(wo_h, wo)
            for cp in loads:
                cp.wait()
            for acc in (gwpa, gwpb, gwpc, gbs_acc, vec_acc, car_dc, car_e):
                acc[...] = jnp.zeros(acc.shape, acc.dtype)
            gws_ref[...] = jnp.zeros(gws_ref.shape, F32)
            gwpool_ref[...] = jnp.zeros(gwpool_ref.shape, F32)

        w = dict(ln_g=lng, ln_b=lnb, w2=w2, bst=bst, cw=cw, cb=cb, wpool=wpool, ps=ps)
        r = _branches_fwd(p_ref, hxb_ref, hcg_ref, hxc_ref, ti == 0, ti * t, w, sg_scr)

        def seg(o, width=WIDTH):
            return p_ref[:, o:o + width].astype(F32)

        def put(o, val):
            dp_ref[:, o:o + val.shape[1]] = val.astype(BF16)

        dob = do_ref[...].astype(BF16)
        dm = _dotg(dob, wo[...], NT)

        def merge_bwd(goff, y_ref, xin, wp, gwp):
            sx = _sigmoid(seg(goff, D_MODEL))
            dmy = dm * sx
            put(goff, dmy * y_ref[...].astype(F32) * (1.0 - sx))
            dyb = dmy.astype(BF16)
            gwp[...] += _dotg(xin.astype(BF16), dyb, TN)
            return _dotg(dyb, wp[...], NT)

        d_ain = merge_bwd(GA, ya_ref, r["a_in"], wpa, gwpa)
        d_bin = merge_bwd(GB, yb_ref, r["b_in"], wpb, gwpb)
        d_cin = merge_bwd(GC, yc_ref, r["c_in"], wpc, gwpc)

        def dsilu(z, sz):
            return sz * (1.0 + z * (1.0 - sz))

        za, sa = r["za"], r["sa"]
        d_abr = d_ain * (za * sa)
        put(ZA, d_ain * r["a_br"] * dsilu(za, sa))
        put(U, d_abr * r["sg"] * _gelu_grad(r["pu"], r["th_u"]))
        d_sg = d_abr * r["u_act"]
        dsgb = d_sg.astype(BF16)
        lo = lax.broadcasted_iota(jnp.int32, (CHUNK, CHUNK), 1) < 64
        zero = jnp.zeros((CHUNK, CHUNK), BF16)
        for c in range(nch):
            rows = slice(c * CHUNK, (c + 1) * CHUNK)
            gbs_acc[...] += d_sg[rows]
            for j in range(4):
                cols = slice(j * 128, (j + 1) * 128)
                dj = dsgb[rows, cols]
                zt = _dot(wt2[j], dj)
                dvn_scr[rows, cols] = jnp.where(lo, zt[:CHUNK], zt[CHUNK:])
                stacked = jnp.concatenate([jnp.where(lo, dj, zero), jnp.where(lo, zero, dj)], axis=0)
                gws_ref[j] += _dotg(stacked, r["vnb"][rows, cols], NT)
        d_vn = dvn_scr[...]
        vhat = r["vhat"]
        vec_acc[0] += _colsum8(d_vn * vhat)
        vec_acc[1] += _colsum8(d_vn)
        d_vhat = d_vn * lng[...]
        d_vg = r["rs"] * (d_vhat - jnp.mean(d_vhat, axis=-1, keepdims=True)
                          - vhat * jnp.mean(d_vhat * vhat, axis=-1, keepdims=True))
        put(V, d_vg * _gelu_grad(r["pv"], r["th_v"]))

        zb, sb = r["zb"], r["sb"]
        d_bbr = d_bin * (zb * sb)
        put(ZB, d_bin * r["b_br"] * dsilu(zb, sb))
        put(BG, d_bbr * r["conv"])
        dc = d_bbr * r["bg"]
        vec_acc[2] += _colsum8(dc)
        vec_acc[3] += _colsum8(dc * r["y2"])
        vec_acc[4] += _colsum8(dc * r["y1"])
        vec_acc[5] += _colsum8(dc * r["yb0"])
        ext = jnp.concatenate([dc, car_dc[...]], axis=0)
        ne = t + HALO
        d1 = pltpu.roll(ext, ne - 1, 0)[:t]
        d2 = pltpu.roll(ext, ne - 2, 0)[:t]
        d_yb0 = cw[2:3, :] * dc + cw[1:2, :] * d1 + cw[0:1, :] * d2
        put(CG, d_yb0 * r["xb"])
        put(XB, d_yb0 * r["cg"])
        car_dc[...] = dc[:HALO]

        zc, sc = r["zc"], r["sc"]
        d_cbr = d_cin * (zc * sc)
        put(ZC, d_cin * r["c_br"] * dsilu(zc, sc))
        vec_acc[6] += _colsum8(d_cbr * r["q"])
        d_q = d_cbr * ps[...]
        for gi, win in enumerate(POOL_WINDOWS):
            cols = slice(gi * 128, (gi + 1) * 128)
            dqb = d_q[:, cols].astype(BF16)
            d_pool = _dotg(dqb, wpool[gi], NT)
            gwpool_ref[gi] += _dotg(r["pooled"][gi].astype(BF16), dqb, TN)
            e = d_pool * r["inv"][gi]
            sx = jnp.concatenate([e, car_e[:, cols]], axis=0)
            sh = 1
            while sh < win:
                sx = sx + pltpu.roll(sx, ne - sh, 0)
                sh *= 2
            put(XC + gi * 128, sx[:t] - d_pool)
            car_e[:, cols] = e[:HALO]

        @pl.when(i == n - 1)
        def _():
            for acc, stage in ((gwpa, wpa), (gwpb, wpb), (gwpc, wpc)):
                stage[...] = acc[...].astype(BF16)
            stores = by_chip_copies((wpa, wpb, wpc), gwp_h, True)
            for cp in stores:
                cp.start()
            for cp in stores:
                cp.wait()
            gbs_ref[...] = jnp.dot(gbs_acc[...], sel_ref[...], preferred_element_type=F32,
                                   precision=lax.Precision.HIGHEST)
            red = lambda k: jnp.sum(vec_acc[k], axis=0, keepdims=True)
            glng_ref[...] = red(0)
            glnb_ref[...] = red(1)
            gcb_ref[...] = red(2)
            gcw_ref[...] = jnp.zeros(gcw_ref.shape, F32)
            for k in range(3):
                gcw_ref[k:k + 1, :] = red(3 + k)
            gps_ref[...] = red(6)
            tt = lax.broadcasted_iota(jnp.int32, (2 * CHUNK, CHUNK), 0) % CHUNK
            ss = lax.broadcasted_iota(jnp.int32, (2 * CHUNK, CHUNK), 1)
            for j in range(4):
                gws_ref[j] = jnp.where(tt >= ss, gws_ref[j], 0.0)

    rtile = lambda c: pl.BlockSpec((t, c), lambda i: (n - 1 - i, 0))
    lsel = lambda *blk: pl.BlockSpec((None,) + blk, lambda i: (l,) + (0,) * len(blk))
    f32s = lambda *shape: jax.ShapeDtypeStruct(shape, F32)
    return _pcall(
        body, name=f"b1_l{l}", grid=(n,),
        in_specs=[rtile(IN_TOTAL)] + _halo_specs(t, rev_n=n) + [rtile(D_MODEL)] * 4 + [
            lsel(1, WIDTH), lsel(1, WIDTH), lsel(4, 256, 128), lsel(4, 256, 128), lsel(CHUNK, WIDTH),
            lsel(8, WIDTH), lsel(1, WIDTH), lsel(4, 128, 128), lsel(1, WIDTH),
            ANY, ANY, _const_spec((WIDTH, 128))],
        out_specs=[rtile(IN_TOTAL), ANY,
                   _const_spec((4, 256, 128)), _const_spec((CHUNK, 128)), _const_spec((1, WIDTH)),
                   _const_spec((1, WIDTH)), _const_spec((8, WIDTH)), _const_spec((1, WIDTH)),
                   _const_spec((4, 128, 128)), _const_spec((1, WIDTH))],
        out_shape=[jax.ShapeDtypeStruct((s, IN_TOTAL), BF16), jax.ShapeDtypeStruct((N_CHIPS, 3 * WIDTH, 256), BF16),
                   f32s(4, 256, 128), f32s(CHUNK, 128),
                   f32s(1, WIDTH), f32s(1, WIDTH), f32s(8, WIDTH), f32s(1, WIDTH), f32s(4, 128, 128),
                   f32s(1, WIDTH)],
        scratch=[pltpu.VMEM((WIDTH, D_MODEL), BF16), pltpu.VMEM((WIDTH, D_MODEL), BF16),
                 pltpu.VMEM((WIDTH, D_MODEL), BF16), pltpu.VMEM((D_MODEL, D_MODEL), BF16),
                 pltpu.VMEM((WIDTH, D_MODEL), F32),
                 pltpu.VMEM((WIDTH, D_MODEL), F32), pltpu.VMEM((WIDTH, D_MODEL), F32),
                 pltpu.VMEM((CHUNK, WIDTH), F32), pltpu.VMEM((8, 8, WIDTH), F32),
                 pltpu.VMEM((t, WIDTH), F32), pltpu.VMEM((t, WIDTH), F32),
                 pltpu.VMEM((HALO, WIDTH), F32), pltpu.VMEM((HALO, WIDTH), F32),
                 pltpu.SemaphoreType.DMA((3 * N_CHIPS,))],
        sem=("arbitrary",),
        args=[p, p, p, p, dout, ya, yb, yc, lw["ln_g"], lw["ln_b"], lw["w2"], lw["wt2"], lw["bst"], lw["cw"],
              lw["cb"], lw["wpool"], lw["ps"], lw["proj"][l], lw["wo"][l], lw["sel"]], comm=comm)


def _rms_bwd(xv, g, dh):
    r = lax.rsqrt(jnp.mean(xv * xv, axis=-1, keepdims=True) + RMS_EPS)
    xhat = xv * r
    dxh = dh * g
    dx = r * (dxh - xhat * jnp.mean(dxh * xhat, axis=-1, keepdims=True))
    return dx, dh * xhat


def _b2a(dp, w_all, x, dout, norm_g3, l, comm=None, tm=256):
    s = x.shape[0]
    nm = s // tm

    def body(dp_ref, w_hbm, x_ref, do_ref, g_ref, dx_ref, gg_ref, w_vmem, gacc):
        i = pl.program_id(0)

        @pl.when(i == 0)
        def _():
            pltpu.sync_copy(w_hbm, w_vmem)
            gacc[...] = jnp.zeros(gacc.shape, F32)

        dh = _dotg(dp_ref[...], w_vmem[...], NT)
        dx, gx = _rms_bwd(x_ref[...], g_ref[...], dh)
        dx_ref[...] = do_ref[...] + dx
        gacc[...] += _colsum8(gx)

        @pl.when(i == nm - 1)
        def _():
            gg_ref[...] = jnp.sum(gacc[...], axis=0, keepdims=True)

    return _pcall(
        body, name=f"b2a_l{l}", grid=(nm,),
        in_specs=[pl.BlockSpec((tm, IN_TOTAL), lambda i: (i, 0)), ANY,
                  pl.BlockSpec((tm, D_MODEL), lambda i: (i, 0)),
                  pl.BlockSpec((tm, D_MODEL), lambda i: (i, 0)),
                  pl.BlockSpec((None, 1, D_MODEL), lambda i: (l, 0, 0))],
        out_specs=[pl.BlockSpec((tm, D_MODEL), lambda i: (i, 0)),
                   pl.BlockSpec((1, D_MODEL), lambda i: (0, 0))],
        out_shape=[jax.ShapeDtypeStruct((s, D_MODEL), F32), jax.ShapeDtypeStruct((1, D_MODEL), F32)],
        scratch=[pltpu.VMEM((D_MODEL, IN_TOTAL), BF16), pltpu.VMEM((8, D_MODEL), F32)],
        sem=("arbitrary",), args=[dp, w_all, x, dout, norm_g3], comm=comm)


def _b2b(h, dp, name, chip, part="all", comm=None, tk=2048):
    s = h.shape[0]
    tk = min(tk, s)
    nk = s // tk
    first, count = dict(all=(0, 4), others=(1, 3), own=(0, 1))[part]
    cols_of = lambda k, idx: (idx[0] + first + k) % N_CHIPS

    def body(idx_ref, h_ref, dp_ref, g_ref, acc):
        kk = pl.program_id(1)

        @pl.when(kk == 0)
        def _():
            acc[...] = jnp.zeros(acc.shape, F32)

        acc[...] += _dotg(h_ref[...], dp_ref[...], TN)

        @pl.when(kk == nk - 1)
        def _():
            g_ref[...] = acc[...].astype(BF16)

    own = part == "own"
    return _pcall(
        body, name=name, grid=(count, nk),
        in_specs=[pl.BlockSpec((tk, D_MODEL), lambda k, kk, idx: (kk, 0)),
                  pl.BlockSpec((tk, SHARD_W), lambda k, kk, idx: (kk, cols_of(k, idx)))],
        out_specs=[pl.BlockSpec((D_MODEL, SHARD_W), lambda k, kk, idx: (0, 0 if own else cols_of(k, idx)))],
        out_shape=[jax.ShapeDtypeStruct((D_MODEL, SHARD_W if own else IN_TOTAL), BF16)],
        scratch=[pltpu.VMEM((D_MODEL, SHARD_W), F32)],
        sem=("arbitrary", "arbitrary"), args=[h, dp], comm=comm, prefetch=[chip])


def _gwo(m, dout, l, tk=2048):
    s = m.shape[0]
    tk = min(tk, s)
    nk = s // tk

    def body(m_ref, do_ref, g_ref, acc):
        kk = pl.program_id(0)

        @pl.when(kk == 0)
        def _():
            acc[...] = jnp.zeros(acc.shape, F32)

        acc[...] += _dotg(m_ref[...], do_ref[...].astype(BF16), TN)

        @pl.when(kk == nk - 1)
        def _():
            g_ref[...] = acc[...].astype(BF16)

    return pl.pallas_call(
        body, name=f"gwo_l{l}", grid=(nk,),
        in_specs=[pl.BlockSpec((tk, D_MODEL), lambda kk: (kk, 0)), pl.BlockSpec((tk, D_MODEL), lambda kk: (kk, 0))],
        out_specs=_const_spec((D_MODEL, D_MODEL)),
        out_shape=jax.ShapeDtypeStruct((D_MODEL, D_MODEL), BF16),
        scratch_shapes=[pltpu.VMEM((D_MODEL, D_MODEL), F32)],
        compiler_params=_params(("arbitrary",)),
    )(m, dout)


def _row_block(rows, cols, n_arrays):
    budget = VMEM_LIMIT // 3 // (2 * 4 * n_arrays * cols)
    rb = rows
    while rb > budget and rb % 16 == 0:
        rb //= 2
    return rb


def _cast_slot(name, a, l, chip):
    _, rows, cols = a.shape
    rb = _row_block(rows, cols, 2)

    def body(idx_ref, a_ref, o_ref):
        o_ref[...] = a_ref[...].astype(BF16)

    gs = pltpu.PrefetchScalarGridSpec(
        num_scalar_prefetch=1, grid=(rows // rb,),
        in_specs=[pl.BlockSpec((None, rb, cols), lambda i, idx: (l, i, 0))],
        out_specs=pl.BlockSpec((None, rb, cols), lambda i, idx: (idx[0], i, 0)))
    return pl.pallas_call(body, name=name, grid_spec=gs, out_shape=jax.ShapeDtypeStruct((N_CHIPS, rows, cols), BF16),
                          compiler_params=_params(("arbitrary",)))(chip, a)


def _cast_cols(name, a, l, chip):
    _, rows, cols = a.shape
    rb = _row_block(rows, cols, 2)

    def body(idx_ref, a_ref, o_ref):
        o_ref[...] = a_ref[...].astype(BF16)

    gs = pltpu.PrefetchScalarGridSpec(
        num_scalar_prefetch=1, grid=(rows // rb,),
        in_specs=[pl.BlockSpec((None, rb, cols), lambda i, idx: (l, i, 0))],
        out_specs=pl.BlockSpec((rb, cols), lambda i, idx: (i, idx[0])))
    return pl.pallas_call(body, name=name, grid_spec=gs, out_shape=jax.ShapeDtypeStruct((rows, N_CHIPS * cols), BF16),
                          compiler_params=_params(("arbitrary",)))(chip, a)


def _sum4(name, own, chip, recv, l, n_layers, prev=None):
    _, rows, cols = recv.shape
    rb = _row_block(rows, cols, 5)
    nb = rows // rb
    if len(own.shape) == 3:
        own_spec = pl.BlockSpec((None, rb, cols), lambda i, idx: (idx[0], i, 0))
    elif own.shape[1] == cols:
        own_spec = pl.BlockSpec((rb, cols), lambda i, idx: (i, 0))
    else:
        own_spec = pl.BlockSpec((rb, cols), lambda i, idx: (i, idx[0]))

    def body(idx_ref, own_ref, r_ref, *rest):
        o_ref = rest[-1]
        o_ref[...] = ((own_ref[...].astype(F32) + r_ref[0].astype(F32)) + r_ref[1].astype(F32)) + r_ref[2].astype(F32)

    gs = pltpu.PrefetchScalarGridSpec(
        num_scalar_prefetch=1, grid=(nb,),
        in_specs=[own_spec,
                  pl.BlockSpec((3, rb, cols), lambda i, idx: (0, i, 0))] + ([ANY] if prev is not None else []),
        out_specs=pl.BlockSpec((rb, cols), lambda i, idx: (l * nb + i, 0)))
    args = (chip, own, recv) + ((prev,) if prev is not None else ())
    return pl.pallas_call(body, name=name, grid_spec=gs,
                          out_shape=jax.ShapeDtypeStruct((n_layers * rows, cols), F32),
                          input_output_aliases=({3: 0} if prev is not None else {}),
                          compiler_params=_params(("arbitrary",)))(*args)


def _sum8(name, pack, me, recv):
    rows = recv.shape[1]

    def body(idx_ref, own_ref, r_ref, o_ref):
        acc = own_ref[...]
        for j in range(7):
            acc = acc + r_ref[j]
        o_ref[...] = acc

    gs = pltpu.PrefetchScalarGridSpec(
        num_scalar_prefetch=1, grid=(1,),
        in_specs=[pl.BlockSpec((rows, 128), lambda i, idx: (idx[0], 0)),
                  pl.BlockSpec((7, rows, 128), lambda i, idx: (0, 0, 0))],
        out_specs=pl.BlockSpec((rows, 128), lambda i, idx: (idx[0], 0)))
    return pl.pallas_call(body, name=name, grid_spec=gs, out_shape=jax.ShapeDtypeStruct(pack.shape, F32),
                          compiler_params=_params(("arbitrary",)))(me, pack, recv)


_SMALL = ("norm_g", "ln_g", "ln_b", "w_s", "b_s", "conv_b", "w_pool", "pool_scale", "final_g", "conv_w")
_SMALL_SHAPES = dict(norm_g=(DEPTH, D_MODEL), ln_g=(DEPTH, WIDTH), ln_b=(DEPTH, WIDTH), w_s=(DEPTH, 8, CHUNK, CHUNK),
                     b_s=(DEPTH, 8, CHUNK), conv_b=(DEPTH, WIDTH), w_pool=(DEPTH, 4, 128, 128),
                     pool_scale=(DEPTH, WIDTH), final_g=(1, D_MODEL), conv_w=(DEPTH, 3, WIDTH))


def _small_rows():
    base, r = {}, 0
    for nm in _SMALL:
        base[nm] = r
        size = 1
        for d in _SMALL_SHAPES[nm]:
            size *= d
        r += size // 128
    return base, -(-r // 64) * 64


def _pack_small(name, raw, gng1, g_final):
    base, rows = _small_rows()
    n_l = len(raw[0])

    def body(*refs):
        o = refs[-1]
        per_layer = [refs[l * n_l:(l + 1) * n_l] for l in range(DEPTH)]
        gng1_ref, gfin_ref = refs[DEPTH * n_l], refs[DEPTH * n_l + 1]
        o[...] = jnp.zeros(o.shape, F32)

        def put_row_vector(r0, ref, width):
            for j in range(width // 128):
                o[r0 + j:r0 + j + 1, :] = ref[0:1, j * 128:(j + 1) * 128]

        put_row_vector(base["norm_g"] + D_MODEL // 128, gng1_ref, D_MODEL)
        put_row_vector(base["final_g"], gfin_ref, D_MODEL)
        for l in range(DEPTH):
            gws, gbs, glng, glnb, gcw, gcb, gwpool, gps = per_layer[l]
            put_row_vector(base["ln_g"] + l * 4, glng, WIDTH)
            put_row_vector(base["ln_b"] + l * 4, glnb, WIDTH)
            put_row_vector(base["conv_b"] + l * 4, gcb, WIDTH)
            put_row_vector(base["pool_scale"] + l * 4, gps, WIDTH)
            o[base["w_s"] + l * 1024:base["w_s"] + (l + 1) * 1024, :] = gws[...].reshape(1024, 128)
            o[base["w_pool"] + l * 512:base["w_pool"] + (l + 1) * 512, :] = gwpool[...].reshape(512, 128)
            o[base["b_s"] + l * 8:base["b_s"] + (l + 1) * 8, :] = gbs[...].T[0:8, :]
            for k in range(3):
                for ch in range(N_CHIPS):
                    r = base["conv_w"] + (l * 3 + k) * N_CHIPS + ch
                    o[r:r + 1, :] = gcw[k:k + 1, ch * 128:(ch + 1) * 128]

    args = [a for l in range(DEPTH) for a in raw[l]] + [gng1, g_final]
    vm = pl.BlockSpec(memory_space=pltpu.VMEM)
    return pl.pallas_call(body, name=name, in_specs=[vm] * len(args), out_specs=vm,
                          out_shape=jax.ShapeDtypeStruct((rows, 128), F32), compiler_params=_params())(*args)


def _adamw_small(name, gred, chip, w, m, v):
    base, _ = _small_rows()

    def body(chip_ref, g_ref, *refs):
        n = len(_SMALL)
        w_r, m_r, v_r = refs[:n], refs[n:2 * n], refs[2 * n:3 * n]
        out = refs[3 * n:]

        def update(i, idx, g):
            d, mn, vn = _adamw_math(w_r[i][idx], g, m_r[i][idx], v_r[i][idx])
            for o, val in zip(out[4 * i:4 * i + 4], (g, d, mn, vn)):
                o[idx] = val

        for i, nm in enumerate(_SMALL):
            shape = _SMALL_SHAPES[nm]
            if nm == "conv_w":
                for l in range(DEPTH):
                    for k in range(3):
                        row = base[nm] + (l * 3 + k) * N_CHIPS + chip_ref[0]
                        update(i, (l, slice(k, k + 1), slice(None)), g_ref[pl.ds(row, 1), :])
            elif len(shape) == 2:
                per = shape[1] // 128
                for l in range(shape[0]):
                    for j in range(per):
                        r = base[nm] + l * per + j
                        update(i, (slice(l, l + 1), slice(j * 128, (j + 1) * 128)), g_ref[r:r + 1, :])
            else:
                rows = 1
                for dim in shape[:-1]:
                    rows *= dim
                update(i, (Ellipsis,), g_ref[base[nm]:base[nm] + rows, :].reshape(shape))

    arrs = [d[nm] for d in (w, m, v) for nm in _SMALL]
    vm = pl.BlockSpec(memory_space=pltpu.VMEM)
    gs = pltpu.PrefetchScalarGridSpec(num_scalar_prefetch=1, grid=(1,), in_specs=[vm] * (1 + len(arrs)),
                                      out_specs=[vm] * (4 * len(_SMALL)))
    outs = pl.pallas_call(
        body, name=name, grid_spec=gs,
        out_shape=[jax.ShapeDtypeStruct(w[nm].shape, F32) for nm in _SMALL for _ in range(4)],
        compiler_params=_params(("arbitrary",)))(chip, gred, *arrs)
    return {nm: list(outs[4 * i:4 * i + 4]) for i, nm in enumerate(_SMALL)}


def _adamw_math(w, g, m, v):
    m = ADAM_B1 * m + (1.0 - ADAM_B1) * g
    v = ADAM_B2 * v + (1.0 - ADAM_B2) * (g * g)
    m_hat = m / (1.0 - ADAM_B1 ** ADAM_STEP)
    v_hat = v / (1.0 - ADAM_B2 ** ADAM_STEP)
    delta = -ADAM_LR * (m_hat / (jnp.sqrt(v_hat) + ADAM_EPS) + ADAM_WD * w)
    return delta, m, v


def _adamw(name, w, m, v, g_parts):
    rows, cols = w.shape
    np_ = len(g_parts)
    rb = _row_block(rows, cols, 7 + np_)

    def body(*refs):
        w_ref, m_ref, v_ref = refs[:3]
        g_refs = refs[3:3 + np_]
        go_ref, d_ref, mo_ref, vo_ref = refs[3 + np_:]
        g = g_refs[0][...]
        for gr in g_refs[1:]:
            g = g + gr[...]
        d, mn, vn = _adamw_math(w_ref[...], g, m_ref[...], v_ref[...])
        go_ref[...] = g
        d_ref[...] = d
        mo_ref[...] = mn
        vo_ref[...] = vn

    spec = pl.BlockSpec((rb, cols), lambda i: (i, 0))
    shp = jax.ShapeDtypeStruct((rows, cols), F32)
    return pl.pallas_call(body, name=name, grid=(rows // rb,), in_specs=[spec] * (3 + np_),
                          out_specs=[spec] * 4, out_shape=[shp] * 4,
                          compiler_params=_params(("arbitrary",)))(w, m, v, *g_parts)


def _all_reduce_small(name, a):
    def body(a_ref, o_ref, buf, send_sems, recv_sems):
        x, y, c = _place()
        o_ref[...] = a_ref[...]
        for rnd, peer in enumerate(((x, y, 1 - c), (x, 1 - y, c), (1 - x, y, c))):
            cp = pltpu.make_async_remote_copy(
                src_ref=o_ref, dst_ref=buf.at[rnd], send_sem=send_sems.at[rnd], recv_sem=recv_sems.at[rnd],
                device_id=peer, device_id_type=MESH)
            cp.start()
            cp.wait_recv()
            cp.wait_send()
            o_ref[...] = o_ref[...] + buf[rnd]

    vm = pl.BlockSpec(memory_space=pltpu.VMEM)
    return pl.pallas_call(
        body, name=name, in_specs=[vm], out_specs=vm, out_shape=jax.ShapeDtypeStruct(a.shape, F32),
        scratch_shapes=[pltpu.VMEM((3,) + a.shape, F32), pltpu.SemaphoreType.DMA((3,)),
                        pltpu.SemaphoreType.DMA((3,))],
        compiler_params=_params(),
    )(a)


def kernel(x, norm_g, w_in, ln_g, ln_b, w_s, b_s, conv_w, conv_b, w_pool, pool_scale, w_pa, w_pb, w_pc, w_o, final_g, loss_target, m_norm_g, m_w_in, m_ln_g, m_ln_b, m_w_s, m_b_s, m_conv_w, m_conv_b, m_w_pool, m_pool_scale, m_w_pa, m_w_pb, m_w_pc, m_w_o, m_final_g, v_norm_g, v_w_in, v_ln_g, v_ln_b, v_w_s, v_b_s, v_conv_w, v_conv_b, v_w_pool, v_pool_scale, v_w_pa, v_w_pb, v_w_pc, v_w_o, v_final_g):
    W = dict(norm_g=norm_g, w_in=w_in, ln_g=ln_g, ln_b=ln_b, w_s=w_s, b_s=b_s, conv_w=conv_w, conv_b=conv_b,
             w_pool=w_pool, pool_scale=pool_scale, w_pa=w_pa, w_pb=w_pb, w_pc=w_pc, w_o=w_o, final_g=final_g)
    M = dict(norm_g=m_norm_g, w_in=m_w_in, ln_g=m_ln_g, ln_b=m_ln_b, w_s=m_w_s, b_s=m_b_s, conv_w=m_conv_w,
             conv_b=m_conv_b, w_pool=m_w_pool, pool_scale=m_pool_scale, w_pa=m_w_pa, w_pb=m_w_pb, w_pc=m_w_pc,
             w_o=m_w_o, final_g=m_final_g)
    Vv = dict(norm_g=v_norm_g, w_in=v_w_in, ln_g=v_ln_g, ln_b=v_ln_b, w_s=v_w_s, b_s=v_b_s, conv_w=v_conv_w,
              conv_b=v_conv_b, w_pool=v_w_pool, pool_scale=v_pool_scale, w_pa=v_w_pa, w_pb=v_w_pb, w_pc=v_w_pc,
              w_o=v_w_o, final_g=v_final_g)
    L = DEPTH
    s = x.shape[1]
    xs = x.reshape(s, D_MODEL)
    tgt = loss_target.reshape(s, D_MODEL)
    k_me = (2 * lax.axis_index("x") + lax.axis_index("y")).astype(jnp.int32)

    chip = k_me.reshape(1)
    assert L == 2
    half_rows = D_MODEL // 2

    land_win = [_cast_cols(f"cast_w_in{l}", w_in, l, chip) for l in range(L)]
    pcat = lambda d: jnp.concatenate([d[b][l] for l in range(L) for b in ("w_pa", "w_pb", "w_pc")], axis=0)
    pcat_w = pcat(W).reshape(L, 3 * WIDTH, 256)
    land_proj = [_cast_slot(f"cast_proj{l}", pcat_w, l, chip) for l in range(L)]
    land_wo = [_cast_slot(f"cast_w_o{l}", w_o, l, chip) for l in range(L)]
    cw_sh = jnp.pad(conv_w, ((0, 0), (0, 5), (0, 0))).reshape(1, L * 8, 128)
    land_cw = lax.dynamic_update_slice(jnp.zeros((N_CHIPS, L * 8, 128), F32), cw_sh, (k_me, 0, 0))

    causal = jnp.tril(jnp.ones((CHUNK, CHUNK), dtype=bool))
    w_m = jnp.where(causal, w_s, 0.0)
    lw = dict(
        ln_g=ln_g.reshape(L, 1, WIDTH), ln_b=ln_b.reshape(L, 1, WIDTH),
        w2=w_m.reshape(L, 4, 256, CHUNK).astype(BF16),
        wt2=jnp.swapaxes(w_m, -1, -2).reshape(L, 4, 256, CHUNK).astype(BF16),
        bst=jnp.repeat(jnp.swapaxes(b_s, -1, -2), 64, axis=-1),
        cb=conv_b.reshape(L, 1, WIDTH), wpool=w_pool.astype(BF16), ps=pool_scale.reshape(L, 1, WIDTH),
        sel=(jnp.arange(WIDTH)[:, None] // 64 == jnp.arange(128)[None, :]).astype(F32))
    norm_g3 = norm_g.reshape(L, 1, D_MODEL)

    cm = _Comm()
    cm.gather_half(cm.through(land_win[0]))
    (p0, h0), (w_half0,), _ = _f1_own(xs, norm_g3, w_in, chip, cm)
    cm = _Comm()
    cm.forward_half(cm.through(w_half0))
    (w_all0,) = _comm_only("forward_w_in0", cm)
    first_rows = D_MODEL // 4
    cm = _Comm()
    for buf in (land_proj[0], land_wo[0], land_cw):
        cm.gather(cm.through(buf))
    cm.gather(cm.through(land_win[1]), 0, first_rows)
    (p0,), (g_proj0, g_wo0, g_cw, w_part1), _ = _f1_others(h0, p0, w_all0, chip, cm)
    lw["proj"] = [g_proj0, None]
    lw["wo"] = [g_wo0.reshape(D_MODEL, D_MODEL), None]
    lw["cw"] = g_cw.reshape(N_CHIPS, L, 8, 128).transpose(1, 2, 0, 3).reshape(L, 8, WIDTH)
    cm = _Comm()
    cm.gather(cm.through(w_part1), first_rows, D_MODEL - first_rows)
    (ya0, yb0, yc0, mm0, x1), (w_all1,), _ = _f2(xs, p0, lw, 0, comm=cm)
    cm = _Comm()
    for buf in (land_proj[1], land_wo[1]):
        cm.gather(cm.through(buf))
    (p1, h1), (g_proj1, g_wo1), _ = _f1(x1, norm_g3, w_all1, 1, comm=cm)
    lw["proj"][1] = g_proj1
    lw["wo"][1] = g_wo1.reshape(D_MODEL, D_MODEL)
    (ya1, yb1, yc1, mm1, dxl, loss_blk, g_final), _, _ = _f2(x1, p1, lw, 1,
                                                              loss=(tgt, final_g.reshape(1, D_MODEL)))
    loss = lax.psum(loss_blk[0, 0], ("x", "y", "c"))

    by_chip = lambda gwp, gwo: (gwp, gwo.reshape(N_CHIPS, 256, D_MODEL))
    recv_like = lambda a: ((3,) + a.shape[1:], a.dtype)
    part_rows = (D_MODEL, 3 * WIDTH, 256)
    gwin, recv_win, part_proj, part_wo, recv_proj, recv_wo = ([None] * L for _ in range(6))

    def sums(l, half):
        return [_sum4(f"sum_w_in{l}", gwin[l], chip, recv_win[l], l, L, half[0]),
                _sum4(f"sum_proj{l}", part_proj[l], chip, recv_proj[l], l, L, half[1]),
                _sum4(f"sum_w_o{l}", part_wo[l], chip, recv_wo[l], l, L, half[2])]

    gwo1 = _gwo(mm1, dxl, 1)
    (dp1, gwp1, *sm1), _, _ = _b1(p1, dxl, ya1, yb1, yc1, lw, 1)
    (gwin[1],), _, _ = _b2b(h1, dp1, "b2b_l1", chip)
    cm = _Comm()
    cm.scatter(cm.read(gwin[1]), cm.new((3, D_MODEL, SHARD_W), BF16), 0, half_rows)
    (dxl, gng1), _, (recv_half,) = _b2a(dp1, w_all1, x1, dxl, norm_g3, 1, comm=cm)
    part_proj[1], part_wo[1] = by_chip(gwp1, gwo1)
    gwo0 = _gwo(mm0, dxl, 0)
    cm = _Comm()
    cm.scatter(cm.read(gwin[1]), cm.through(recv_half), half_rows, half_rows)
    cm.scatter(cm.read(part_proj[1]), cm.new(*recv_like(part_proj[1])))
    cm.scatter(cm.read(part_wo[1]), cm.new(*recv_like(part_wo[1])))
    (dp0, gwp0, *sm0), (recv_win[1],), (recv_proj[1], recv_wo[1]) = _b1(p0, dxl, ya0, yb0, yc0, lw, 0, comm=cm)
    half = sums(1, [None, None, None])
    part_proj[0], part_wo[0] = by_chip(gwp0, gwo0)

    gpack = _pack_small("pack_small", [sm0, sm1], gng1, g_final)
    slice_rows = gpack.shape[0] // 8

    cm = _Comm()
    cm.scatter(cm.read(part_proj[0]), cm.new(*recv_like(part_proj[0])))
    cm.scatter(cm.read(part_wo[0]), cm.new(*recv_like(part_wo[0])))
    for a in range(3):
        cm.sibling(cm.read(half[a]), cm.new(half[a].shape, F32), part_rows[a], part_rows[a])
    cm.slices_out(cm.read(gpack), cm.new((7, slice_rows, 128), F32), slice_rows)
    (g_others,), _, (recv_proj[0], recv_wo[0], *other, slices_in) = _b2b(h0, dp0, "b2b_l0_others", chip, "others",
                                                                        comm=cm)
    me = (4 * lax.axis_index("x") + 2 * lax.axis_index("y") + lax.axis_index("c")).astype(jnp.int32).reshape(1)
    gsum = _sum8("sum_small", gpack, me, slices_in)
    early_rows = D_MODEL // 8
    cm = _Comm()
    cm.scatter(cm.read(g_others), cm.new((3, D_MODEL, SHARD_W), BF16), 0, early_rows)
    cm.slices_back(cm.through(gsum), slice_rows)
    (gwin[0],), (gred,), (recv_early,) = _b2b(h0, dp0, "b2b_l0_own", chip, "own", comm=cm)
    cm = _Comm()
    cm.scatter(cm.read(g_others), cm.through(recv_early), early_rows, D_MODEL - early_rows)
    (dxl, gng0), (recv_win[0],), _ = _b2a(dp0, w_all0, xs, dxl, norm_g3, 0, comm=cm)
    grad_x = dxl.reshape(1, s, D_MODEL)

    half = sums(0, half)
    cm = _Comm()
    for a in range(3):
        cm.sibling(cm.read(half[a]), cm.through(other[a]), 0, part_rows[a])
    other = _comm_only("swap_halves0", cm)
    gng0 = _all_reduce_small("all_reduce_norm_g0", gng0.reshape(8, 128))
    gred = lax.dynamic_update_slice(gred, gng0, (0, 0))

    outs = {}
    shard2d = dict(w_in=(L * D_MODEL, SHARD_W), w_o=(L * 256, D_MODEL))
    for a, name in ((0, "w_in"), (2, "w_o")):
        r2 = shard2d[name]
        res = _adamw(f"adamw_{name}", W[name].reshape(r2), M[name].reshape(r2), Vv[name].reshape(r2),
                     [half[a], other[a]])
        outs[name] = [o.reshape(W[name].shape) for o in res]
    res = _adamw("adamw_proj", pcat(W), pcat(M), pcat(Vv), [half[1], other[1]])
    for i, name in enumerate(("w_pa", "w_pb", "w_pc")):
        outs[name] = [o.reshape(L, 3, WIDTH, 256)[:, i] for o in res]

    as_rows = lambda d: {nm: (d[nm].reshape(1, D_MODEL) if nm == "final_g" else d[nm]) for nm in _SMALL}
    outs.update(_adamw_small("adamw_small", gred, chip, as_rows(W), as_rows(M), as_rows(Vv)))
    outs["final_g"] = [o.reshape(D_MODEL) for o in outs["final_g"]]

    order = ("norm_g", "w_in", "ln_g", "ln_b", "w_s", "b_s", "conv_w", "conv_b", "w_pool", "pool_scale",
             "w_pa", "w_pb", "w_pc", "w_o", "final_g")
    return (loss, grad_x, *[outs[nm][0] for nm in order], *[outs[nm][1] for nm in order],
            *[outs[nm][2] for nm in order], *[outs[nm][3] for nm in order])
```

```python
import functools

import jax
import jax.numpy as jnp
from jax import lax
from jax.experimental import pallas as pl
from jax.experimental.pallas import tpu as pltpu

F32 = jnp.float32
BF16 = jnp.bfloat16

D_MODEL = 1024
DEPTH = 2
CHUNK = 128
WIDTH = 512
POOL_WINDOWS = (2, 4, 8, 16)
IN_TOTAL = 7680
N_CHIPS = 4
SHARD_W = IN_TOTAL // N_CHIPS
RMS_EPS = 1e-6
LN_EPS = 1e-5
HALO = 16

U, V, ZA, XB, BG, CG, ZB, XC, ZC, GA, GB, GC = (0, 512, 1024, 1536, 2048, 2560, 3072, 3584, 4096, 4608, 5632, 6656)

ADAM_LR = 0.001
ADAM_B1 = 0.9
ADAM_B2 = 0.999
ADAM_EPS = 1e-08
ADAM_WD = 0.01
ADAM_STEP = 10

VMEM_LIMIT = 56 * 1024 * 1024
MESH = pl.DeviceIdType.MESH
ANY = pl.BlockSpec(memory_space=pl.ANY)
NT = (((1,), (1,)), ((), ()))
TN = (((0,), (0,)), ((), ()))


def _params(sem=None):
    kw = dict(vmem_limit_bytes=VMEM_LIMIT)
    if sem is not None:
        kw["dimension_semantics"] = sem
    return pltpu.CompilerParams(**kw)


def _dot(a, b):
    return jnp.dot(a, b, preferred_element_type=F32)


def _dotg(a, b, dims):
    return lax.dot_general(a, b, dims, preferred_element_type=F32)


def _sigmoid(x):
    return 1.0 / (1.0 + jnp.exp(-x))


_GELU_K = 0.7978845608028654


def _gelu(x):
    th = jnp.tanh(_GELU_K * (x + 0.044715 * (x * x * x)))
    return 0.5 * x * (1.0 + th), th


def _gelu_grad(x, th):
    return 0.5 * (1.0 + th) + 0.5 * x * (1.0 - th * th) * (_GELU_K * (1.0 + 3.0 * 0.044715 * (x * x)))


def _colsum8(x):
    t, c = x.shape
    return jnp.sum(x.reshape(t // 8, 8, c), axis=0)


def _branches_fwd(p_ref, hxb_ref, hcg_ref, hxc_ref, first, tstart, w, sg_scr):
    t = p_ref.shape[0]
    nch = t // CHUNK

    def seg(o, width=WIDTH):
        return p_ref[:, o:o + width].astype(F32)

    lo = lax.broadcasted_iota(jnp.int32, (CHUNK, CHUNK), 1) < 64
    r = {}
    pu = seg(U)
    u_act, th_u = _gelu(pu)
    pv = seg(V)
    vg, th_v = _gelu(pv)
    mu = jnp.mean(vg, axis=-1, keepdims=True)
    xc = vg - mu
    var = jnp.mean(xc * xc, axis=-1, keepdims=True)
    rs = lax.rsqrt(var + LN_EPS)
    vhat = xc * rs
    vn = vhat * w["ln_g"][...] + w["ln_b"][...]
    vnb = vn.astype(BF16)
    for n in range(nch):
        for j in range(4):
            vb = vnb[n * CHUNK:(n + 1) * CHUNK, j * 128:(j + 1) * 128]
            z = _dot(w["w2"][j], vb)
            sg_scr[n * CHUNK:(n + 1) * CHUNK, j * 128:(j + 1) * 128] = (
                jnp.where(lo, z[:CHUNK], z[CHUNK:]) + w["bst"][:, j * 128:(j + 1) * 128])
    sg = sg_scr[...]
    a_br = u_act * sg
    za = seg(ZA)
    sa = _sigmoid(za)
    r.update(pu=pu, th_u=th_u, pv=pv, th_v=th_v, rs=rs, vhat=vhat, vnb=vnb, u_act=u_act, sg=sg,
             a_br=a_br, za=za, sa=sa, a_in=a_br * (za * sa))

    xb = seg(XB)
    cg = seg(CG)
    yb0 = cg * xb
    hal = hcg_ref[...].astype(F32) * hxb_ref[...].astype(F32)
    hal = jnp.where(first, 0.0, hal)
    ext = jnp.concatenate([hal, yb0], axis=0)
    y1 = pltpu.roll(ext, 1, 0)[HALO:]
    y2 = pltpu.roll(ext, 2, 0)[HALO:]
    cw = w["cw"]
    conv = cw[0:1, :] * y2 + cw[1:2, :] * y1 + cw[2:3, :] * yb0 + w["cb"][...]
    bg = seg(BG)
    b_br = bg * conv
    zb = seg(ZB)
    sb = _sigmoid(zb)
    r.update(xb=xb, cg=cg, yb0=yb0, y1=y1, y2=y2, conv=conv, bg=bg, b_br=b_br, zb=zb, sb=sb,
             b_in=b_br * (zb * sb))

    xcv = seg(XC)
    hxc = jnp.where(first, 0.0, hxc_ref[...].astype(F32))
    extc = jnp.concatenate([hxc, xcv], axis=0)
    tpos = tstart + lax.broadcasted_iota(jnp.int32, (t, 1), 0) + 1
    pooled, inv, q = [], [], []
    for gi, win in enumerate(POOL_WINDOWS):
        s = extc[:, gi * 128:(gi + 1) * 128]
        sh = 1
        while sh < win:
            s = s + pltpu.roll(s, sh, 0)
            sh *= 2
        inv_g = jnp.where(tpos >= win, 1.0 / win, 1.0 / jnp.minimum(tpos, win).astype(F32))
        pg = s[HALO:] * inv_g - xcv[:, gi * 128:(gi + 1) * 128]
        pooled.append(pg)
        inv.append(inv_g)
        q.append(_dot(pg.astype(BF16), w["wpool"][gi]))
    qv = jnp.concatenate(q, axis=1)
    c_br = qv * w["ps"][...]
    zc = seg(ZC)
    sc = _sigmoid(zc)
    r.update(pooled=pooled, inv=inv, q=qv, c_br=c_br, zc=zc, sc=sc, c_in=c_br * (zc * sc))
    return r


def _halo_specs(t, rev_n=None):
    def imap(col):
        def f(i):
            ti = i if rev_n is None else rev_n - 1 - i
            return (jnp.maximum(ti * (t // HALO) - 1, 0), col)
        return f
    return [pl.BlockSpec((HALO, WIDTH), imap(XB // WIDTH)),
            pl.BlockSpec((HALO, WIDTH), imap(CG // WIDTH)),
            pl.BlockSpec((HALO, WIDTH), imap(XC // WIDTH))]


def _const_spec(shape):
    nd = len(shape)
    return pl.BlockSpec(shape, lambda *_: (0,) * nd)


def _place():
    return lax.axis_index("x"), lax.axis_index("y"), lax.axis_index("c")


def _chip_peer(x, y, jm):
    px = (1 - x) if (jm & 2) else x
    py = (1 - y) if (jm & 1) else y
    return px, py


def _chip_part(buf, k, r0=0, nr=None):
    if len(buf.shape) == 3:
        return buf.at[k] if nr is None else buf.at[k, pl.ds(r0, nr)]
    cols = buf.shape[1] // N_CHIPS
    rows = pl.ds(0, buf.shape[0]) if nr is None else pl.ds(r0, nr)
    return buf.at[rows, pl.ds(pl.multiple_of(k * cols, 128), cols)]


def _row_range(ref, r0, nr):
    return ref if nr is None else ref.at[pl.ds(r0, nr)]


class _Comm:
    def __init__(self):
        self.ins, self.thru, self.fresh, self.moves, self.n = [], [], [], [], 0

    def read(self, arr):
        self.ins.append(arr)
        return ("ins", len(self.ins) - 1)

    def through(self, arr):
        self.thru.append(arr)
        return ("thru", len(self.thru) - 1)

    def new(self, shape, dtype):
        self.fresh.append(jax.ShapeDtypeStruct(tuple(shape), dtype))
        return ("fresh", len(self.fresh) - 1)

    def _add(self, n, *move):
        self.moves.append(move)
        self.n += n

    def gather(self, buf, r0=0, nr=None):
        self._add(3, "gather", buf, r0, nr)

    def scatter(self, src, dst, r0=0, nr=None):
        self._add(3, "scatter", src, dst, r0, nr)

    def sibling(self, src, dst, r0=0, nr=None):
        self._add(1, "sibling", src, dst, r0, nr)

    def gather_half(self, buf):
        self._add(3, "gather_half", buf)

    def forward_half(self, buf):
        self._add(3, "forward_half", buf)

    def slices_out(self, src, dst, rows):
        self._add(7, "slices_out", src, dst, rows)

    def slices_back(self, buf, rows):
        self._add(7, "slices_back", buf, rows)

    def copies(self, bufs, x, y, c):
        ref = lambda h: bufs[h[0]][h[1]]
        k_me = 2 * x + y
        me = 4 * x + 2 * y + c
        cps = []
        for move in self.moves:
            kind = move[0]
            if kind in ("gather", "scatter"):
                for jj, jm in enumerate((1, 2, 3)):
                    px, py = _chip_peer(x, y, jm)
                    k_peer = 2 * px + py
                    if kind == "gather":
                        _, buf, r0, nr = move
                        mine = _chip_part(ref(buf), k_me, r0, nr)
                        cps.append((mine, mine, _chip_part(ref(buf), k_peer, r0, nr), (px, py, c)))
                    else:
                        _, src, dst, r0, nr = move
                        slot = ref(dst).at[jj] if nr is None else ref(dst).at[jj, pl.ds(r0, nr)]
                        cps.append((_chip_part(ref(src), k_peer, r0, nr), slot, slot, (px, py, c)))
            elif kind in ("gather_half", "forward_half"):
                buf = ref(move[1])
                hr = buf.shape[-2] // 2
                mine_r0, other_r0 = pl.multiple_of(c * hr, 16), pl.multiple_of((1 - c) * hr, 16)
                for jm in (1, 2, 3):
                    px, py = _chip_peer(x, y, jm)
                    k_peer = 2 * px + py
                    if kind == "gather_half":
                        part = _chip_part(buf, k_me, mine_r0, hr)
                        cps.append((part, part, _chip_part(buf, k_peer, mine_r0, hr), (px, py, c)))
                    else:
                        part = _chip_part(buf, k_peer, mine_r0, hr)
                        cps.append((part, part, _chip_part(buf, k_peer, other_r0, hr), (x, y, 1 - c)))
            elif kind == "sibling":
                _, src, dst, r0, nr = move
                land = _row_range(ref(dst), r0, nr)
                cps.append((_row_range(ref(src), r0, nr), land, land, (x, y, 1 - c)))
            else:
                for j in range(1, 8):
                    px = (1 - x) if (j & 4) else x
                    py = (1 - y) if (j & 2) else y
                    pc = (1 - c) if (j & 1) else c
                    peer = 4 * px + 2 * py + pc
                    if kind == "slices_out":
                        _, src, dst, rows = move
                        slot = ref(dst).at[j - 1]
                        cps.append((ref(src).at[pl.ds(pl.multiple_of(peer * rows, 8), rows)], slot, slot,
                                    (px, py, pc)))
                    else:
                        _, buf, rows = move
                        mine = ref(buf).at[pl.ds(pl.multiple_of(me * rows, 8), rows)]
                        cps.append((mine, mine, ref(buf).at[pl.ds(pl.multiple_of(peer * rows, 8), rows)],
                                    (px, py, pc)))
        assert len(cps) == self.n
        return cps


def _pcall(body, *, name, grid, in_specs, out_specs, out_shape, scratch, sem, args, comm=None, prefetch=(),
           aliases=None):
    n_pre = len(prefetch)
    n_in, n_out, n_scr = len(in_specs), len(out_specs), len(scratch)
    io_alias = {n_pre + i: o for i, o in (aliases or {}).items()}
    if comm is None or comm.n == 0:
        gs = pltpu.PrefetchScalarGridSpec(num_scalar_prefetch=n_pre, grid=grid, in_specs=list(in_specs),
                                          out_specs=list(out_specs), scratch_shapes=list(scratch))
        outs = pl.pallas_call(body, name=name, grid_spec=gs, out_shape=list(out_shape), input_output_aliases=io_alias,
                              compiler_params=_params(sem))(*prefetch, *args)
        return list(outs), [], []
    n_ci, n_ct, n_cf, n_cp = len(comm.ins), len(comm.thru), len(comm.fresh), comm.n

    def wrapped(*refs):
        pos = 0
        def take(n):
            nonlocal pos
            got = refs[pos:pos + n]
            pos += n
            return got
        pre = take(n_pre)
        a, ci, _ = take(n_in), take(n_ci), take(n_ct)
        o, ct, cf = take(n_out), take(n_ct), take(n_cf)
        scr = take(n_scr)
        send_sems, recv_sems = take(2)
        first = functools.reduce(jnp.logical_and, [pl.program_id(d) == 0 for d in range(len(grid))])
        last = functools.reduce(jnp.logical_and, [pl.program_id(d) == grid[d] - 1 for d in range(len(grid))])
        x, y, c = _place()
        cps = comm.copies(dict(ins=ci, thru=ct, fresh=cf), x, y, c)

        def copy(i, src, dst, dev):
            return pltpu.make_async_remote_copy(src_ref=src, dst_ref=dst, send_sem=send_sems.at[i],
                                                recv_sem=recv_sems.at[i], device_id=dev, device_id_type=MESH)

        @pl.when(first)
        def _():
            for i, (src, dst, _, dev) in enumerate(cps):
                copy(i, src, dst, dev).start()

        body(*pre, *a, *o, *scr)

        @pl.when(last)
        def _():
            for i, (src, _, land, dev) in enumerate(cps):
                copy(i, src, land, dev).wait_recv()
            for i, (src, dst, _, dev) in enumerate(cps):
                copy(i, src, dst, dev).wait_send()

    thru_shapes = [jax.ShapeDtypeStruct(t.shape, t.dtype) for t in comm.thru]
    gs = pltpu.PrefetchScalarGridSpec(
        num_scalar_prefetch=n_pre, grid=grid,
        in_specs=list(in_specs) + [ANY] * (n_ci + n_ct),
        out_specs=list(out_specs) + [ANY] * (n_ct + n_cf),
        scratch_shapes=list(scratch) + [pltpu.SemaphoreType.DMA((n_cp,)), pltpu.SemaphoreType.DMA((n_cp,))])
    outs = pl.pallas_call(
        wrapped, name=name, grid_spec=gs,
        out_shape=list(out_shape) + thru_shapes + comm.fresh,
        input_output_aliases={**io_alias, **{n_pre + n_in + n_ci + t: n_out + t for t in range(n_ct)}},
        compiler_params=_params(sem),
    )(*prefetch, *args, *comm.ins, *comm.thru)
    outs = list(outs)
    return outs[:n_out], outs[n_out:n_out + n_ct], outs[n_out + n_ct:]


def _comm_only(name, comm):
    def body():
        pass
    _, thru, _ = _pcall(body, name=name, grid=(1,), in_specs=[], out_specs=[], out_shape=[], scratch=[],
                        sem=("arbitrary",), args=[], comm=comm)
    return thru


def _f1(x, norm_g3, w_all, l, comm=None, tm=256):
    s = x.shape[0]

    def body(x_ref, g_ref, w_hbm, p_ref, h_ref, w_vmem):
        @pl.when(pl.program_id(0) == 0)
        def _():
            pltpu.sync_copy(w_hbm, w_vmem)
        xv = x_ref[...]
        r = lax.rsqrt(jnp.mean(xv * xv, axis=-1, keepdims=True) + RMS_EPS)
        hb = ((xv * r) * g_ref[...]).astype(BF16)
        h_ref[...] = hb
        p_ref[...] = _dot(hb, w_vmem[...]).astype(BF16)

    return _pcall(
        body, name=f"f1_l{l}", grid=(s // tm,),
        in_specs=[pl.BlockSpec((tm, D_MODEL), lambda i: (i, 0)),
                  pl.BlockSpec((None, 1, D_MODEL), lambda i: (l, 0, 0)), ANY],
        out_specs=[pl.BlockSpec((tm, IN_TOTAL), lambda i: (i, 0)),
                   pl.BlockSpec((tm, D_MODEL), lambda i: (i, 0))],
        out_shape=[jax.ShapeDtypeStruct((s, IN_TOTAL), BF16), jax.ShapeDtypeStruct((s, D_MODEL), BF16)],
        scratch=[pltpu.VMEM((D_MODEL, IN_TOTAL), BF16)], sem=("arbitrary",), args=[x, norm_g3, w_all], comm=comm)


def _f1_own(x, norm_g3, w_in, chip, comm, tm=256):
    s = x.shape[0]

    def body(idx_ref, x_ref, g_ref, w_hbm, p_ref, h_ref, w_f32, w_vmem):
        @pl.when(pl.program_id(0) == 0)
        def _():
            pltpu.sync_copy(w_hbm.at[0], w_f32)
            w_vmem[...] = w_f32[...].astype(BF16)
        xv = x_ref[...]
        r = lax.rsqrt(jnp.mean(xv * xv, axis=-1, keepdims=True) + RMS_EPS)
        hb = ((xv * r) * g_ref[...]).astype(BF16)
        h_ref[...] = hb
        p_ref[...] = _dot(hb, w_vmem[...]).astype(BF16)

    return _pcall(
        body, name="f1_l0_own", grid=(s // tm,),
        in_specs=[pl.BlockSpec((tm, D_MODEL), lambda i, idx: (i, 0)),
                  pl.BlockSpec((None, 1, D_MODEL), lambda i, idx: (0, 0, 0)), ANY],
        out_specs=[pl.BlockSpec((tm, SHARD_W), lambda i, idx: (i, idx[0])),
                   pl.BlockSpec((tm, D_MODEL), lambda i, idx: (i, 0))],
        out_shape=[jax.ShapeDtypeStruct((s, IN_TOTAL), BF16), jax.ShapeDtypeStruct((s, D_MODEL), BF16)],
        scratch=[pltpu.VMEM((D_MODEL, SHARD_W), F32), pltpu.VMEM((D_MODEL, SHARD_W), BF16)],
        sem=("arbitrary",), args=[x, norm_g3, w_in], comm=comm, prefetch=[chip])


def _f1_others(h, p, w_all, chip, comm, tm=1024):
    s = h.shape[0]
    tm = min(tm, s)
    cols_of = lambda j, idx: (idx[0] + 1 + j) % N_CHIPS

    def body(idx_ref, h_ref, w_hbm, _, p_ref, w_vmem, sems):
        j = pl.program_id(1)

        @pl.when(jnp.logical_and(pl.program_id(0) == 0, j == 0))
        def _():
            loads = [pltpu.make_async_copy(
                w_hbm.at[:, pl.ds(pl.multiple_of(((idx_ref[0] + 1 + q) % N_CHIPS) * SHARD_W, 128), SHARD_W)],
                w_vmem.at[q], sems.at[q]) for q in range(N_CHIPS - 1)]
            for cp in loads:
                cp.start()
            for cp in loads:
                cp.wait()

        p_ref[...] = _dot(h_ref[...], w_vmem[j]).astype(BF16)

    return _pcall(
        body, name="f1_l0_others", grid=(s // tm, N_CHIPS - 1),
        in_specs=[pl.BlockSpec((tm, D_MODEL), lambda i, j, idx: (i, 0)), ANY, ANY],
        out_specs=[pl.BlockSpec((tm, SHARD_W), lambda i, j, idx: (i, cols_of(j, idx)))],
        out_shape=[jax.ShapeDtypeStruct((s, IN_TOTAL), BF16)],
        scratch=[pltpu.VMEM((N_CHIPS - 1, D_MODEL, SHARD_W), BF16), pltpu.SemaphoreType.DMA((N_CHIPS - 1,))],
        sem=("arbitrary", "arbitrary"), args=[h, w_all, p], comm=comm, prefetch=[chip], aliases={2: 0})


def _f2(x, p, lw, l, comm=None, loss=None, t=256):
    s = x.shape[0]
    n = s // t

    def body(p_ref, hxb_ref, hcg_ref, hxc_ref, x_ref, lng, lnb, w2, bst, cw, cb, wpool, ps,
             proj, wo, *rest):
        if loss is None:
            ya_ref, yb_ref, yc_ref, m_ref, xo_ref, sg_scr = rest
        else:
            t_ref, fg_ref, ya_ref, yb_ref, yc_ref, m_ref, xo_ref, loss_ref, gg_ref, sg_scr, lacc, gacc = rest
        i = pl.program_id(0)
        w = dict(ln_g=lng, ln_b=lnb, w2=w2, bst=bst, cw=cw, cb=cb, wpool=wpool, ps=ps)
        r = _branches_fwd(p_ref, hxb_ref, hcg_ref, hxc_ref, i == 0, i * t, w, sg_scr)

        def project(act, b):
            ab = act.astype(BF16)
            return jnp.concatenate([_dot(ab, proj[k, b * WIDTH:(b + 1) * WIDTH, :]) for k in range(N_CHIPS)], axis=1)

        ya = project(r["a_in"], 0)
        yb = project(r["b_in"], 1)
        yc = project(r["c_in"], 2)
        ya_ref[...] = ya.astype(BF16)
        yb_ref[...] = yb.astype(BF16)
        yc_ref[...] = yc.astype(BF16)
        m = (_sigmoid(p_ref[:, GA:GA + D_MODEL].astype(F32)) * ya
             + _sigmoid(p_ref[:, GB:GB + D_MODEL].astype(F32)) * yb
             + _sigmoid(p_ref[:, GC:GC + D_MODEL].astype(F32)) * yc)
        mb = m.astype(BF16)
        m_ref[...] = mb
        xo = x_ref[...] + _dot(mb, wo[...])
        if loss is None:
            xo_ref[...] = xo
        else:
            @pl.when(i == 0)
            def _():
                lacc[...] = jnp.zeros(lacc.shape, F32)
                gacc[...] = jnp.zeros(gacc.shape, F32)

            g = fg_ref[...]
            rr = lax.rsqrt(jnp.mean(xo * xo, axis=-1, keepdims=True) + RMS_EPS)
            err = (xo * rr) * g - t_ref[...]
            lacc[...] += _colsum8(err * err)
            dx, gx = _rms_bwd(xo, g, err * (1.0 / D_MODEL))
            xo_ref[...] = dx
            gacc[...] += _colsum8(gx)

            @pl.when(i == n - 1)
            def _():
                tot = jnp.sum(jnp.sum(lacc[...], axis=0, keepdims=True), axis=1, keepdims=True)
                loss_ref[...] = jnp.broadcast_to(tot * (0.5 / D_MODEL), loss_ref.shape)
                gg_ref[...] = jnp.sum(gacc[...], axis=0, keepdims=True)

    tile = lambda c: pl.BlockSpec((t, c), lambda i: (i, 0))
    lsel = lambda *blk: pl.BlockSpec((None,) + blk, lambda i: (l,) + (0,) * len(blk))
    act = jax.ShapeDtypeStruct((s, D_MODEL), BF16)
    f32s = lambda *shape: jax.ShapeDtypeStruct(shape, F32)
    with_loss = loss is not None
    return _pcall(
        body, name=f"f2_l{l}", grid=(n,),
        in_specs=[tile(IN_TOTAL)] + _halo_specs(t) + [
            tile(D_MODEL), lsel(1, WIDTH), lsel(1, WIDTH), lsel(4, 256, 128), lsel(CHUNK, WIDTH),
            lsel(8, WIDTH), lsel(1, WIDTH), lsel(4, 128, 128), lsel(1, WIDTH),
            _const_spec((N_CHIPS, 3 * WIDTH, 256)), _const_spec((D_MODEL, D_MODEL))]
        + ([tile(D_MODEL), _const_spec((1, D_MODEL))] if with_loss else []),
        out_specs=[tile(D_MODEL)] * 5 + ([_const_spec((8, 128)), _const_spec((1, D_MODEL))] if with_loss else []),
        out_shape=[act, act, act, act, f32s(s, D_MODEL)] + ([f32s(8, 128), f32s(1, D_MODEL)] if with_loss else []),
        scratch=[pltpu.VMEM((t, WIDTH), F32)]
        + ([pltpu.VMEM((8, D_MODEL), F32), pltpu.VMEM((8, D_MODEL), F32)] if with_loss else []),
        sem=("arbitrary",),
        args=[p, p, p, p, x, lw["ln_g"], lw["ln_b"], lw["w2"], lw["bst"], lw["cw"], lw["cb"], lw["wpool"], lw["ps"],
              lw["proj"][l], lw["wo"][l]] + (list(loss) if with_loss else []), comm=comm)


def _b1(p, dout, ya, yb, yc, lw, l, comm=None, t=256):
    s = p.shape[0]
    n = s // t
    nch = t // CHUNK

    def body(p_ref, hxb_ref, hcg_ref, hxc_ref, do_ref, ya_ref, yb_ref, yc_ref,
             lng, lnb, w2, wt2, bst, cw, cb, wpool, ps, proj_h, wo_h, sel_ref,
             dp_ref, gwp_h, gws_ref, gbs_ref, glng_ref, glnb_ref, gcw_ref, gcb_ref,
             gwpool_ref, gps_ref,
             wpa, wpb, wpc, wo, gwpa, gwpb, gwpc, gbs_acc, vec_acc, sg_scr, dvn_scr, car_dc, car_e, psem):
        i = pl.program_id(0)
        ti = n - 1 - i

        def by_chip_copies(vmem_bufs, hbm, to_hbm):
            cps = []
            for b, buf in enumerate(vmem_bufs):
                for k in range(N_CHIPS):
                    v = buf.at[:, pl.ds(k * 256, 256)]
                    h = hbm.at[k, pl.ds(b * WIDTH, WIDTH)]
                    cps.append(pltpu.make_async_copy(v, h, psem.at[b * N_CHIPS + k]) if to_hbm
                               else pltpu.make_async_copy(h, v, psem.at[b * N_CHIPS + k]))
            return cps

        @pl.when(i == 0)
        def _():
            loads = by_chip_copies((wpa, wpb, wpc), proj_h, False)
            for cp in loads:
                cp.start()
            pltpu.sync_copy(wo_h, wo)
            for cp in loads:
                cp.wait()
            for acc in (gwpa, gwpb, gwpc, gbs_acc, vec_acc, car_dc, car_e):
                acc[...] = jnp.zeros(acc.shape, acc.dtype)
            gws_ref[...] = jnp.zeros(gws_ref.shape, F32)
            gwpool_ref[...] = jnp.zeros(gwpool_ref.shape, F32)

        w = dict(ln_g=lng, ln_b=lnb, w2=w2, bst=bst, cw=cw, cb=cb, wpool=wpool, ps=ps)
        r = _branches_fwd(p_ref, hxb_ref, hcg_ref, hxc_ref, ti == 0, ti * t, w, sg_scr)

        def seg(o, width=WIDTH):
            return p_ref[:, o:o + width].astype(F32)

        def put(o, val):
            dp_ref[:, o:o + val.shape[1]] = val.astype(BF16)

        dob = do_ref[...].astype(BF16)
        dm = _dotg(dob, wo[...], NT)

        def merge_bwd(goff, y_ref, xin, wp, gwp):
            sx = _sigmoid(seg(goff, D_MODEL))
            dmy = dm * sx
            put(goff, dmy * y_ref[...].astype(F32) * (1.0 - sx))
            dyb = dmy.astype(BF16)
            gwp[...] += _dotg(xin.astype(BF16), dyb, TN)
            return _dotg(dyb, wp[...], NT)

        d_ain = merge_bwd(GA, ya_ref, r["a_in"], wpa, gwpa)
        d_bin = merge_bwd(GB, yb_ref, r["b_in"], wpb, gwpb)
        d_cin = merge_bwd(GC, yc_ref, r["c_in"], wpc, gwpc)

        def dsilu(z, sz):
            return sz * (1.0 + z * (1.0 - sz))

        za, sa = r["za"], r["sa"]
        d_abr = d_ain * (za * sa)
        put(ZA, d_ain * r["a_br"] * dsilu(za, sa))
        put(U, d_abr * r["sg"] * _gelu_grad(r["pu"], r["th_u"]))
        d_sg = d_abr * r["u_act"]
        dsgb = d_sg.astype(BF16)
        lo = lax.broadcasted_iota(jnp.int32, (CHUNK, CHUNK), 1) < 64
        zero = jnp.zeros((CHUNK, CHUNK), BF16)
        for c in range(nch):
            rows = slice(c * CHUNK, (c + 1) * CHUNK)
            gbs_acc[...] += d_sg[rows]
            for j in range(4):
                cols = slice(j * 128, (j + 1) * 128)
                dj = dsgb[rows, cols]
                zt = _dot(wt2[j], dj)
                dvn_scr[rows, cols] = jnp.where(lo, zt[:CHUNK], zt[CHUNK:])
                stacked = jnp.concatenate([jnp.where(lo, dj, zero), jnp.where(lo, zero, dj)], axis=0)
                gws_ref[j] += _dotg(stacked, r["vnb"][rows, cols], NT)
        d_vn = dvn_scr[...]
        vhat = r["vhat"]
        vec_acc[0] += _colsum8(d_vn * vhat)
        vec_acc[1] += _colsum8(d_vn)
        d_vhat = d_vn * lng[...]
        d_vg = r["rs"] * (d_vhat - jnp.mean(d_vhat, axis=-1, keepdims=True)
                          - vhat * jnp.mean(d_vhat * vhat, axis=-1, keepdims=True))
        put(V, d_vg * _gelu_grad(r["pv"], r["th_v"]))

        zb, sb = r["zb"], r["sb"]
        d_bbr = d_bin * (zb * sb)
        put(ZB, d_bin * r["b_br"] * dsilu(zb, sb))
        put(BG, d_bbr * r["conv"])
        dc = d_bbr * r["bg"]
        vec_acc[2] += _colsum8(dc)
        vec_acc[3] += _colsum8(dc * r["y2"])
        vec_acc[4] += _colsum8(dc * r["y1"])
        vec_acc[5] += _colsum8(dc * r["yb0"])
        ext = jnp.concatenate([dc, car_dc[...]], axis=0)
        ne = t + HALO
        d1 = pltpu.roll(ext, ne - 1, 0)[:t]
        d2 = pltpu.roll(ext, ne - 2, 0)[:t]
        d_yb0 = cw[2:3, :] * dc + cw[1:2, :] * d1 + cw[0:1, :] * d2
        put(CG, d_yb0 * r["xb"])
        put(XB, d_yb0 * r["cg"])
        car_dc[...] = dc[:HALO]

        zc, sc = r["zc"], r["sc"]
        d_cbr = d_cin * (zc * sc)
        put(ZC, d_cin * r["c_br"] * dsilu(zc, sc))
        vec_acc[6] += _colsum8(d_cbr * r["q"])
        d_q = d_cbr * ps[...]
        for gi, win in enumerate(POOL_WINDOWS):
            cols = slice(gi * 128, (gi + 1) * 128)
            dqb = d_q[:, cols].astype(BF16)
            d_pool = _dotg(dqb, wpool[gi], NT)
            gwpool_ref[gi] += _dotg(r["pooled"][gi].astype(BF16), dqb, TN)
            e = d_pool * r["inv"][gi]
            sx = jnp.concatenate([e, car_e[:, cols]], axis=0)
            sh = 1
            while sh < win:
                sx = sx + pltpu.roll(sx, ne - sh, 0)
                sh *= 2
            put(XC + gi * 128, sx[:t] - d_pool)
            car_e[:, cols] = e[:HALO]

        @pl.when(i == n - 1)
        def _():
            for acc, stage in ((gwpa, wpa), (gwpb, wpb), (gwpc, wpc)):
                stage[...] = acc[...].astype(BF16)
            stores = by_chip_copies((wpa, wpb, wpc), gwp_h, True)
            for cp in stores:
                cp.start()
            for cp in stores:
                cp.wait()
            gbs_ref[...] = jnp.dot(gbs_acc[...], sel_ref[...], preferred_element_type=F32,
                                   precision=lax.Precision.HIGHEST)
            red = lambda k: jnp.sum(vec_acc[k], axis=0, keepdims=True)
            glng_ref[...] = red(0)
            glnb_ref[...] = red(1)
            gcb_ref[...] = red(2)
            gcw_ref[...] = jnp.zeros(gcw_ref.shape, F32)
            for k in range(3):
                gcw_ref[k:k + 1, :] = red(3 + k)
            gps_ref[...] = red(6)
            tt = lax.broadcasted_iota(jnp.int32, (2 * CHUNK, CHUNK), 0) % CHUNK
            ss = lax.broadcasted_iota(jnp.int32, (2 * CHUNK, CHUNK), 1)
            for j in range(4):
                gws_ref[j] = jnp.where(tt >= ss, gws_ref[j], 0.0)

    rtile = lambda c: pl.BlockSpec((t, c), lambda i: (n - 1 - i, 0))
    lsel = lambda *blk: pl.BlockSpec((None,) + blk, lambda i: (l,) + (0,) * len(blk))
    f32s = lambda *shape: jax.ShapeDtypeStruct(shape, F32)
    return _pcall(
        body, name=f"b1_l{l}", grid=(n,),
        in_specs=[rtile(IN_TOTAL)] + _halo_specs(t, rev_n=n) + [rtile(D_MODEL)] * 4 + [
            lsel(1, WIDTH), lsel(1, WIDTH), lsel(4, 256, 128), lsel(4, 256, 128), lsel(CHUNK, WIDTH),
            lsel(8, WIDTH), lsel(1, WIDTH), lsel(4, 128, 128), lsel(1, WIDTH),
            ANY, ANY, _const_spec((WIDTH, 128))],
        out_specs=[rtile(IN_TOTAL), ANY,
                   _const_spec((4, 256, 128)), _const_spec((CHUNK, 128)), _const_spec((1, WIDTH)),
                   _const_spec((1, WIDTH)), _const_spec((8, WIDTH)), _const_spec((1, WIDTH)),
                   _const_spec((4, 128, 128)), _const_spec((1, WIDTH))],
        out_shape=[jax.ShapeDtypeStruct((s, IN_TOTAL), BF16), jax.ShapeDtypeStruct((N_CHIPS, 3 * WIDTH, 256), BF16),
                   f32s(4, 256, 128), f32s(CHUNK, 128),
                   f32s(1, WIDTH), f32s(1, WIDTH), f32s(8, WIDTH), f32s(1, WIDTH), f32s(4, 128, 128),
                   f32s(1, WIDTH)],
        scratch=[pltpu.VMEM((WIDTH, D_MODEL), BF16), pltpu.VMEM((WIDTH, D_MODEL), BF16),
                 pltpu.VMEM((WIDTH, D_MODEL), BF16), pltpu.VMEM((D_MODEL, D_MODEL), BF16),
                 pltpu.VMEM((WIDTH, D_MODEL), F32),
                 pltpu.VMEM((WIDTH, D_MODEL), F32), pltpu.VMEM((WIDTH, D_MODEL), F32),
                 pltpu.VMEM((CHUNK, WIDTH), F32), pltpu.VMEM((8, 8, WIDTH), F32),
                 pltpu.VMEM((t, WIDTH), F32), pltpu.VMEM((t, WIDTH), F32),
                 pltpu.VMEM((HALO, WIDTH), F32), pltpu.VMEM((HALO, WIDTH), F32),
                 pltpu.SemaphoreType.DMA((3 * N_CHIPS,))],
        sem=("arbitrary",),
        args=[p, p, p, p, dout, ya, yb, yc, lw["ln_g"], lw["ln_b"], lw["w2"], lw["wt2"], lw["bst"], lw["cw"],
              lw["cb"], lw["wpool"], lw["ps"], lw["proj"][l], lw["wo"][l], lw["sel"]], comm=comm)


def _rms_bwd(xv, g, dh):
    r = lax.rsqrt(jnp.mean(xv * xv, axis=-1, keepdims=True) + RMS_EPS)
    xhat = xv * r
    dxh = dh * g
    dx = r * (dxh - xhat * jnp.mean(dxh * xhat, axis=-1, keepdims=True))
    return dx, dh * xhat


def _b2a(dp, w_all, x, dout, norm_g3, l, comm=None, tm=256):
    s = x.shape[0]
    nm = s // tm

    def body(dp_ref, w_hbm, x_ref, do_ref, g_ref, dx_ref, gg_ref, w_vmem, gacc):
        i = pl.program_id(0)

        @pl.when(i == 0)
        def _():
            pltpu.sync_copy(w_hbm, w_vmem)
            gacc[...] = jnp.zeros(gacc.shape, F32)

        dh = _dotg(dp_ref[...], w_vmem[...], NT)
        dx, gx = _rms_bwd(x_ref[...], g_ref[...], dh)
        dx_ref[...] = do_ref[...] + dx
        gacc[...] += _colsum8(gx)

        @pl.when(i == nm - 1)
        def _():
            gg_ref[...] = jnp.sum(gacc[...], axis=0, keepdims=True)

    return _pcall(
        body, name=f"b2a_l{l}", grid=(nm,),
        in_specs=[pl.BlockSpec((tm, IN_TOTAL), lambda i: (i, 0)), ANY,
                  pl.BlockSpec((tm, D_MODEL), lambda i: (i, 0)),
                  pl.BlockSpec((tm, D_MODEL), lambda i: (i, 0)),
                  pl.BlockSpec((None, 1, D_MODEL), lambda i: (l, 0, 0))],
        out_specs=[pl.BlockSpec((tm, D_MODEL), lambda i: (i, 0)),
                   pl.BlockSpec((1, D_MODEL), lambda i: (0, 0))],
        out_shape=[jax.ShapeDtypeStruct((s, D_MODEL), F32), jax.ShapeDtypeStruct((1, D_MODEL), F32)],
        scratch=[pltpu.VMEM((D_MODEL, IN_TOTAL), BF16), pltpu.VMEM((8, D_MODEL), F32)],
        sem=("arbitrary",), args=[dp, w_all, x, dout, norm_g3], comm=comm)


def _b2b(h, dp, name, chip, part="all", comm=None, tk=2048):
    s = h.shape[0]
    tk = min(tk, s)
    nk = s // tk
    first, count = dict(all=(0, 4), others=(1, 3), own=(0, 1))[part]
    cols_of = lambda k, idx: (idx[0] + first + k) % N_CHIPS

    def body(idx_ref, h_ref, dp_ref, g_ref, acc):
        kk = pl.program_id(1)

        @pl.when(kk == 0)
        def _():
            acc[...] = jnp.zeros(acc.shape, F32)

        acc[...] += _dotg(h_ref[...], dp_ref[...], TN)

        @pl.when(kk == nk - 1)
        def _():
            g_ref[...] = acc[...].astype(BF16)

    own = part == "own"
    return _pcall(
        body, name=name, grid=(count, nk),
        in_specs=[pl.BlockSpec((tk, D_MODEL), lambda k, kk, idx: (kk, 0)),
                  pl.BlockSpec((tk, SHARD_W), lambda k, kk, idx: (kk, cols_of(k, idx)))],
        out_specs=[pl.BlockSpec((D_MODEL, SHARD_W), lambda k, kk, idx: (0, 0 if own else cols_of(k, idx)))],
        out_shape=[jax.ShapeDtypeStruct((D_MODEL, SHARD_W if own else IN_TOTAL), BF16)],
        scratch=[pltpu.VMEM((D_MODEL, SHARD_W), F32)],
        sem=("arbitrary", "arbitrary"), args=[h, dp], comm=comm, prefetch=[chip])


def _gwo(m, dout, l, tk=2048):
    s = m.shape[0]
    tk = min(tk, s)
    nk = s // tk

    def body(m_ref, do_ref, g_ref, acc):
        kk = pl.program_id(0)

        @pl.when(kk == 0)
        def _():
            acc[...] = jnp.zeros(acc.shape, F32)

        acc[...] += _dotg(m_ref[...], do_ref[...].astype(BF16), TN)

        @pl.when(kk == nk - 1)
        def _():
            g_ref[...] = acc[...].astype(BF16)

    return pl.pallas_call(
        body, name=f"gwo_l{l}", grid=(nk,),
        in_specs=[pl.BlockSpec((tk, D_MODEL), lambda kk: (kk, 0)), pl.BlockSpec((tk, D_MODEL), lambda kk: (kk, 0))],
        out_specs=_const_spec((D_MODEL, D_MODEL)),
        out_shape=jax.ShapeDtypeStruct((D_MODEL, D_MODEL), BF16),
        scratch_shapes=[pltpu.VMEM((D_MODEL, D_MODEL), F32)],
        compiler_params=_params(("arbitrary",)),
    )(m, dout)


def _row_block(rows, cols, n_arrays):
    budget = VMEM_LIMIT // 3 // (2 * 4 * n_arrays * cols)
    rb = rows
    while rb > budget and rb % 16 == 0:
        rb //= 2
    return rb


def _cast_slot(name, a, l, chip):
    _, rows, cols = a.shape
    rb = _row_block(rows, cols, 2)

    def body(idx_ref, a_ref, o_ref):
        o_ref[...] = a_ref[...].astype(BF16)

    gs = pltpu.PrefetchScalarGridSpec(
        num_scalar_prefetch=1, grid=(rows // rb,),
        in_specs=[pl.BlockSpec((None, rb, cols), lambda i, idx: (l, i, 0))],
        out_specs=pl.BlockSpec((None, rb, cols), lambda i, idx: (idx[0], i, 0)))
    return pl.pallas_call(body, name=name, grid_spec=gs, out_shape=jax.ShapeDtypeStruct((N_CHIPS, rows, cols), BF16),
                          compiler_params=_params(("arbitrary",)))(chip, a)


def _cast_cols(name, a, l, chip):
    _, rows, cols = a.shape
    rb = _row_block(rows, cols, 2)

    def body(idx_ref, a_ref, o_ref):
        o_ref[...] = a_ref[...].astype(BF16)

    gs = pltpu.PrefetchScalarGridSpec(
        num_scalar_prefetch=1, grid=(rows // rb,),
        in_specs=[pl.BlockSpec((None, rb, cols), lambda i, idx: (l, i, 0))],
        out_specs=pl.BlockSpec((rb, cols), lambda i, idx: (i, idx[0])))
    return pl.pallas_call(body, name=name, grid_spec=gs, out_shape=jax.ShapeDtypeStruct((rows, N_CHIPS * cols), BF16),
                          compiler_params=_params(("arbitrary",)))(chip, a)


def _sum4(name, own, chip, recv, l, n_layers, prev=None):
    _, rows, cols = recv.shape
    rb = _row_block(rows, cols, 5)
    nb = rows // rb
    if len(own.shape) == 3:
        own_spec = pl.BlockSpec((None, rb, cols), lambda i, idx: (idx[0], i, 0))
    elif own.shape[1] == cols:
        own_spec = pl.BlockSpec((rb, cols), lambda i, idx: (i, 0))
    else:
        own_spec = pl.BlockSpec((rb, cols), lambda i, idx: (i, idx[0]))

    def body(idx_ref, own_ref, r_ref, *rest):
        o_ref = rest[-1]
        o_ref[...] = (((own_ref[...].astype(F32) + r_ref[0].astype(F32)) + r_ref[1].astype(F32))
                      + r_ref[2].astype(F32)).astype(BF16)

    gs = pltpu.PrefetchScalarGridSpec(
        num_scalar_prefetch=1, grid=(nb,),
        in_specs=[own_spec,
                  pl.BlockSpec((3, rb, cols), lambda i, idx: (0, i, 0))] + ([ANY] if prev is not None else []),
        out_specs=pl.BlockSpec((rb, cols), lambda i, idx: (l * nb + i, 0)))
    args = (chip, own, recv) + ((prev,) if prev is not None else ())
    return pl.pallas_call(body, name=name, grid_spec=gs,
                          out_shape=jax.ShapeDtypeStruct((n_layers * rows, cols), BF16),
                          input_output_aliases=({3: 0} if prev is not None else {}),
                          compiler_params=_params(("arbitrary",)))(*args)


def _sum8(name, pack, me, recv):
    rows = recv.shape[1]

    def body(idx_ref, own_ref, r_ref, o_ref):
        acc = own_ref[...]
        for j in range(7):
            acc = acc + r_ref[j]
        o_ref[...] = acc

    gs = pltpu.PrefetchScalarGridSpec(
        num_scalar_prefetch=1, grid=(1,),
        in_specs=[pl.BlockSpec((rows, 128), lambda i, idx: (idx[0], 0)),
                  pl.BlockSpec((7, rows, 128), lambda i, idx: (0, 0, 0))],
        out_specs=pl.BlockSpec((rows, 128), lambda i, idx: (idx[0], 0)))
    return pl.pallas_call(body, name=name, grid_spec=gs, out_shape=jax.ShapeDtypeStruct(pack.shape, F32),
                          compiler_params=_params(("arbitrary",)))(me, pack, recv)


_SMALL = ("norm_g", "ln_g", "ln_b", "w_s", "b_s", "conv_b", "w_pool", "pool_scale", "final_g", "conv_w")
_SMALL_SHAPES = dict(norm_g=(DEPTH, D_MODEL), ln_g=(DEPTH, WIDTH), ln_b=(DEPTH, WIDTH), w_s=(DEPTH, 8, CHUNK, CHUNK),
                     b_s=(DEPTH, 8, CHUNK), conv_b=(DEPTH, WIDTH), w_pool=(DEPTH, 4, 128, 128),
                     pool_scale=(DEPTH, WIDTH), final_g=(1, D_MODEL), conv_w=(DEPTH, 3, WIDTH))


def _small_rows():
    base, r = {}, 0
    for nm in _SMALL:
        base[nm] = r
        size = 1
        for d in _SMALL_SHAPES[nm]:
            size *= d
        r += size // 128
    return base, -(-r // 64) * 64


def _pack_small(name, raw, gng1, g_final):
    base, rows = _small_rows()
    n_l = len(raw[0])

    def body(*refs):
        o = refs[-1]
        per_layer = [refs[l * n_l:(l + 1) * n_l] for l in range(DEPTH)]
        gng1_ref, gfin_ref = refs[DEPTH * n_l], refs[DEPTH * n_l + 1]
        o[...] = jnp.zeros(o.shape, F32)

        def put_row_vector(r0, ref, width):
            for j in range(width // 128):
                o[r0 + j:r0 + j + 1, :] = ref[0:1, j * 128:(j + 1) * 128]

        put_row_vector(base["norm_g"] + D_MODEL // 128, gng1_ref, D_MODEL)
        put_row_vector(base["final_g"], gfin_ref, D_MODEL)
        for l in range(DEPTH):
            gws, gbs, glng, glnb, gcw, gcb, gwpool, gps = per_layer[l]
            put_row_vector(base["ln_g"] + l * 4, glng, WIDTH)
            put_row_vector(base["ln_b"] + l * 4, glnb, WIDTH)
            put_row_vector(base["conv_b"] + l * 4, gcb, WIDTH)
            put_row_vector(base["pool_scale"] + l * 4, gps, WIDTH)
            o[base["w_s"] + l * 1024:base["w_s"] + (l + 1) * 1024, :] = gws[...].reshape(1024, 128)
            o[base["w_pool"] + l * 512:base["w_pool"] + (l + 1) * 512, :] = gwpool[...].reshape(512, 128)
            o[base["b_s"] + l * 8:base["b_s"] + (l + 1) * 8, :] = gbs[...].T[0:8, :]
            for k in range(3):
                for ch in range(N_CHIPS):
                    r = base["conv_w"] + (l * 3 + k) * N_CHIPS + ch
                    o[r:r + 1, :] = gcw[k:k + 1, ch * 128:(ch + 1) * 128]

    args = [a for l in range(DEPTH) for a in raw[l]] + [gng1, g_final]
    vm = pl.BlockSpec(memory_space=pltpu.VMEM)
    return pl.pallas_call(body, name=name, in_specs=[vm] * len(args), out_specs=vm,
                          out_shape=jax.ShapeDtypeStruct((rows, 128), F32), compiler_params=_params())(*args)


def _adamw_small(name, gred, chip, w, m, v):
    base, _ = _small_rows()

    def body(chip_ref, g_ref, *refs):
        n = len(_SMALL)
        w_r, m_r, v_r = refs[:n], refs[n:2 * n], refs[2 * n:3 * n]
        out = refs[3 * n:]

        def update(i, idx, g):
            d, mn, vn = _adamw_math(w_r[i][idx], g, m_r[i][idx], v_r[i][idx])
            for o, val in zip(out[4 * i:4 * i + 4], (g, d, mn, vn)):
                o[idx] = val

        for i, nm in enumerate(_SMALL):
            shape = _SMALL_SHAPES[nm]
            if nm == "conv_w":
                for l in range(DEPTH):
                    for k in range(3):
                        row = base[nm] + (l * 3 + k) * N_CHIPS + chip_ref[0]
                        update(i, (l, slice(k, k + 1), slice(None)), g_ref[pl.ds(row, 1), :])
            elif len(shape) == 2:
                per = shape[1] // 128
                for l in range(shape[0]):
                    for j in range(per):
                        r = base[nm] + l * per + j
                        update(i, (slice(l, l + 1), slice(j * 128, (j + 1) * 128)), g_ref[r:r + 1, :])
            else:
                rows = 1
                for dim in shape[:-1]:
                    rows *= dim
                update(i, (Ellipsis,), g_ref[base[nm]:base[nm] + rows, :].reshape(shape))

    arrs = [d[nm] for d in (w, m, v) for nm in _SMALL]
    vm = pl.BlockSpec(memory_space=pltpu.VMEM)
    gs = pltpu.PrefetchScalarGridSpec(num_scalar_prefetch=1, grid=(1,), in_specs=[vm] * (1 + len(arrs)),
                                      out_specs=[vm] * (4 * len(_SMALL)))
    outs = pl.pallas_call(
        body, name=name, grid_spec=gs,
        out_shape=[jax.ShapeDtypeStruct(w[nm].shape, F32) for nm in _SMALL for _ in range(4)],
        compiler_params=_params(("arbitrary",)))(chip, gred, *arrs)
    return {nm: list(outs[4 * i:4 * i + 4]) for i, nm in enumerate(_SMALL)}


def _adamw_math(w, g, m, v):
    m = ADAM_B1 * m + (1.0 - ADAM_B1) * g
    v = ADAM_B2 * v + (1.0 - ADAM_B2) * (g * g)
    m_hat = m / (1.0 - ADAM_B1 ** ADAM_STEP)
    v_hat = v / (1.0 - ADAM_B2 ** ADAM_STEP)
    delta = -ADAM_LR * (m_hat / (jnp.sqrt(v_hat) + ADAM_EPS) + ADAM_WD * w)
    return delta, m, v


def _adamw(name, w, m, v, g_parts):
    rows, cols = w.shape
    np_ = len(g_parts)
    rb = _row_block(rows, cols, 7 + np_)

    def body(*refs):
        w_ref, m_ref, v_ref = refs[:3]
        g_refs = refs[3:3 + np_]
        go_ref, d_ref, mo_ref, vo_ref = refs[3 + np_:]
        g = g_refs[0][...].astype(F32)
        for gr in g_refs[1:]:
            g = g + gr[...].astype(F32)
        d, mn, vn = _adamw_math(w_ref[...], g, m_ref[...], v_ref[...])
        go_ref[...] = g
        d_ref[...] = d
        mo_ref[...] = mn
        vo_ref[...] = vn

    spec = pl.BlockSpec((rb, cols), lambda i: (i, 0))
    shp = jax.ShapeDtypeStruct((rows, cols), F32)
    return pl.pallas_call(body, name=name, grid=(rows // rb,), in_specs=[spec] * (3 + np_),
                          out_specs=[spec] * 4, out_shape=[shp] * 4,
                          compiler_params=_params(("arbitrary",)))(w, m, v, *g_parts)


def _all_reduce_small(name, a):
    def body(a_ref, o_ref, buf, send_sems, recv_sems):
        x, y, c = _place()
        o_ref[...] = a_ref[...]
        for rnd, peer in enumerate(((x, y, 1 - c), (x, 1 - y, c), (1 - x, y, c))):
            cp = pltpu.make_async_remote_copy(
                src_ref=o_ref, dst_ref=buf.at[rnd], send_sem=send_sems.at[rnd], recv_sem=recv_sems.at[rnd],
                device_id=peer, device_id_type=MESH)
            cp.start()
            cp.wait_recv()
            cp.wait_send()
            o_ref[...] = o_ref[...] + buf[rnd]

    vm = pl.BlockSpec(memory_space=pltpu.VMEM)
    return pl.pallas_call(
        body, name=name, in_specs=[vm], out_specs=vm, out_shape=jax.ShapeDtypeStruct(a.shape, F32),
        scratch_shapes=[pltpu.VMEM((3,) + a.shape, F32), pltpu.SemaphoreType.DMA((3,)),
                        pltpu.SemaphoreType.DMA((3,))],
        compiler_params=_params(),
    )(a)


def kernel(x, norm_g, w_in, ln_g, ln_b, w_s, b_s, conv_w, conv_b, w_pool, pool_scale, w_pa, w_pb, w_pc, w_o, final_g, loss_target, m_norm_g, m_w_in, m_ln_g, m_ln_b, m_w_s, m_b_s, m_conv_w, m_conv_b, m_w_pool, m_pool_scale, m_w_pa, m_w_pb, m_w_pc, m_w_o, m_final_g, v_norm_g, v_w_in, v_ln_g, v_ln_b, v_w_s, v_b_s, v_conv_w, v_conv_b, v_w_pool, v_pool_scale, v_w_pa, v_w_pb, v_w_pc, v_w_o, v_final_g):
    W = dict(norm_g=norm_g, w_in=w_in, ln_g=ln_g, ln_b=ln_b, w_s=w_s, b_s=b_s, conv_w=conv_w, conv_b=conv_b,
             w_pool=w_pool, pool_scale=pool_scale, w_pa=w_pa, w_pb=w_pb, w_pc=w_pc, w_o=w_o, final_g=final_g)
    M = dict(norm_g=m_norm_g, w_in=m_w_in, ln_g=m_ln_g, ln_b=m_ln_b, w_s=m_w_s, b_s=m_b_s, conv_w=m_conv_w,
             conv_b=m_conv_b, w_pool=m_w_pool, pool_scale=m_pool_scale, w_pa=m_w_pa, w_pb=m_w_pb, w_pc=m_w_pc,
             w_o=m_w_o, final_g=m_final_g)
    Vv = dict(norm_g=v_norm_g, w_in=v_w_in, ln_g=v_ln_g, ln_b=v_ln_b, w_s=v_w_s, b_s=v_b_s, conv_w=v_conv_w,
              conv_b=v_conv_b, w_pool=v_w_pool, pool_scale=v_pool_scale, w_pa=v_w_pa, w_pb=v_w_pb, w_pc=v_w_pc,
              w_o=v_w_o, final_g=v_final_g)
    L = DEPTH
    s = x.shape[1]
    xs = x.reshape(s, D_MODEL)
    tgt = loss_target.reshape(s, D_MODEL)
    k_me = (2 * lax.axis_index("x") + lax.axis_index("y")).astype(jnp.int32)

    chip = k_me.reshape(1)
    assert L == 2
    half_rows = D_MODEL // 2

    land_win = [_cast_cols(f"cast_w_in{l}", w_in, l, chip) for l in range(L)]
    pcat = lambda d: jnp.concatenate([d[b][l] for l in range(L) for b in ("w_pa", "w_pb", "w_pc")], axis=0)
    pcat_w = pcat(W).reshape(L, 3 * WIDTH, 256)
    land_proj = [_cast_slot(f"cast_proj{l}", pcat_w, l, chip) for l in range(L)]
    land_wo = [_cast_slot(f"cast_w_o{l}", w_o, l, chip) for l in range(L)]
    cw_sh = jnp.pad(conv_w, ((0, 0), (0, 5), (0, 0))).reshape(1, L * 8, 128)
    land_cw = lax.dynamic_update_slice(jnp.zeros((N_CHIPS, L * 8, 128), F32), cw_sh, (k_me, 0, 0))

    causal = jnp.tril(jnp.ones((CHUNK, CHUNK), dtype=bool))
    w_m = jnp.where(causal, w_s, 0.0)
    lw = dict(
        ln_g=ln_g.reshape(L, 1, WIDTH), ln_b=ln_b.reshape(L, 1, WIDTH),
        w2=w_m.reshape(L, 4, 256, CHUNK).astype(BF16),
        wt2=jnp.swapaxes(w_m, -1, -2).reshape(L, 4, 256, CHUNK).astype(BF16),
        bst=jnp.repeat(jnp.swapaxes(b_s, -1, -2), 64, axis=-1),
        cb=conv_b.reshape(L, 1, WIDTH), wpool=w_pool.astype(BF16), ps=pool_scale.reshape(L, 1, WIDTH),
        sel=(jnp.arange(WIDTH)[:, None] // 64 == jnp.arange(128)[None, :]).astype(F32))
    norm_g3 = norm_g.reshape(L, 1, D_MODEL)

    cm = _Comm()
    cm.gather_half(cm.through(land_win[0]))
    (p0, h0), (w_half0,), _ = _f1_own(xs, norm_g3, w_in, chip, cm)
    cm = _Comm()
    cm.forward_half(cm.through(w_half0))
    (w_all0,) = _comm_only("forward_w_in0", cm)
    first_rows = D_MODEL // 4
    cm = _Comm()
    for buf in (land_proj[0], land_wo[0], land_cw):
        cm.gather(cm.through(buf))
    cm.gather(cm.through(land_win[1]), 0, first_rows)
    (p0,), (g_proj0, g_wo0, g_cw, w_part1), _ = _f1_others(h0, p0, w_all0, chip, cm)
    lw["proj"] = [g_proj0, None]
    lw["wo"] = [g_wo0.reshape(D_MODEL, D_MODEL), None]
    lw["cw"] = g_cw.reshape(N_CHIPS, L, 8, 128).transpose(1, 2, 0, 3).reshape(L, 8, WIDTH)
    cm = _Comm()
    cm.gather(cm.through(w_part1), first_rows, D_MODEL - first_rows)
    (ya0, yb0, yc0, mm0, x1), (w_all1,), _ = _f2(xs, p0, lw, 0, comm=cm)
    cm = _Comm()
    for buf in (land_proj[1], land_wo[1]):
        cm.gather(cm.through(buf))
    (p1, h1), (g_proj1, g_wo1), _ = _f1(x1, norm_g3, w_all1, 1, comm=cm)
    lw["proj"][1] = g_proj1
    lw["wo"][1] = g_wo1.reshape(D_MODEL, D_MODEL)
    (ya1, yb1, yc1, mm1, dxl, loss_blk, g_final), _, _ = _f2(x1, p1, lw, 1,
                                                              loss=(tgt, final_g.reshape(1, D_MODEL)))
    loss = lax.psum(loss_blk[0, 0], ("x", "y", "c"))

    by_chip = lambda gwp, gwo: (gwp, gwo.reshape(N_CHIPS, 256, D_MODEL))
    recv_like = lambda a: ((3,) + a.shape[1:], a.dtype)
    part_rows = (D_MODEL, 3 * WIDTH, 256)
    gwin, recv_win, part_proj, part_wo, recv_proj, recv_wo = ([None] * L for _ in range(6))

    def sums(l, half):
        return [_sum4(f"sum_w_in{l}", gwin[l], chip, recv_win[l], l, L, half[0]),
                _sum4(f"sum_proj{l}", part_proj[l], chip, recv_proj[l], l, L, half[1]),
                _sum4(f"sum_w_o{l}", part_wo[l], chip, recv_wo[l], l, L, half[2])]

    gwo1 = _gwo(mm1, dxl, 1)
    (dp1, gwp1, *sm1), _, _ = _b1(p1, dxl, ya1, yb1, yc1, lw, 1)
    (gwin[1],), _, _ = _b2b(h1, dp1, "b2b_l1", chip)
    cm = _Comm()
    cm.scatter(cm.read(gwin[1]), cm.new((3, D_MODEL, SHARD_W), BF16), 0, half_rows)
    (dxl, gng1), _, (recv_half,) = _b2a(dp1, w_all1, x1, dxl, norm_g3, 1, comm=cm)
    part_proj[1], part_wo[1] = by_chip(gwp1, gwo1)
    gwo0 = _gwo(mm0, dxl, 0)
    cm = _Comm()
    cm.scatter(cm.read(gwin[1]), cm.through(recv_half), half_rows, half_rows)
    cm.scatter(cm.read(part_proj[1]), cm.new(*recv_like(part_proj[1])))
    cm.scatter(cm.read(part_wo[1]), cm.new(*recv_like(part_wo[1])))
    (dp0, gwp0, *sm0), (recv_win[1],), (recv_proj[1], recv_wo[1]) = _b1(p0, dxl, ya0, yb0, yc0, lw, 0, comm=cm)
    half = sums(1, [None, None, None])
    part_proj[0], part_wo[0] = by_chip(gwp0, gwo0)

    gpack = _pack_small("pack_small", [sm0, sm1], gng1, g_final)
    slice_rows = gpack.shape[0] // 8

    cm = _Comm()
    cm.scatter(cm.read(part_proj[0]), cm.new(*recv_like(part_proj[0])))
    cm.scatter(cm.read(part_wo[0]), cm.new(*recv_like(part_wo[0])))
    for a in range(3):
        cm.sibling(cm.read(half[a]), cm.new(half[a].shape, BF16), part_rows[a], part_rows[a])
    cm.slices_out(cm.read(gpack), cm.new((7, slice_rows, 128), F32), slice_rows)
    (g_others,), _, (recv_proj[0], recv_wo[0], *other, slices_in) = _b2b(h0, dp0, "b2b_l0_others", chip, "others",
                                                                        comm=cm)
    me = (4 * lax.axis_index("x") + 2 * lax.axis_index("y") + lax.axis_index("c")).astype(jnp.int32).reshape(1)
    gsum = _sum8("sum_small", gpack, me, slices_in)
    early_rows = D_MODEL // 8
    cm = _Comm()
    cm.scatter(cm.read(g_others), cm.new((3, D_MODEL, SHARD_W), BF16), 0, early_rows)
    cm.slices_back(cm.through(gsum), slice_rows)
    (gwin[0],), (gred,), (recv_early,) = _b2b(h0, dp0, "b2b_l0_own", chip, "own", comm=cm)
    cm = _Comm()
    cm.scatter(cm.read(g_others), cm.through(recv_early), early_rows, D_MODEL - early_rows)
    (dxl, gng0), (recv_win[0],), _ = _b2a(dp0, w_all0, xs, dxl, norm_g3, 0, comm=cm)
    grad_x = dxl.reshape(1, s, D_MODEL)

    half = sums(0, half)
    cm = _Comm()
    for a in range(3):
        cm.sibling(cm.read(half[a]), cm.through(other[a]), 0, part_rows[a])
    other = _comm_only("swap_halves0", cm)
    gng0 = _all_reduce_small("all_reduce_norm_g0", gng0.reshape(8, 128))
    gred = lax.dynamic_update_slice(gred, gng0, (0, 0))

    outs = {}
    shard2d = dict(w_in=(L * D_MODEL, SHARD_W), w_o=(L * 256, D_MODEL))
    for a, name in ((0, "w_in"), (2, "w_o")):
        r2 = shard2d[name]
        res = _adamw(f"adamw_{name}", W[name].reshape(r2), M[name].reshape(r2), Vv[name].reshape(r2),
                     [half[a], other[a]])
        outs[name] = [o.reshape(W[name].shape) for o in res]
    res = _adamw("adamw_proj", pcat(W), pcat(M), pcat(Vv), [half[1], other[1]])
    for i, name in enumerate(("w_pa", "w_pb", "w_pc")):
        outs[name] = [o.reshape(L, 3, WIDTH, 256)[:, i] for o in res]

    as_rows = lambda d: {nm: (d[nm].reshape(1, D_MODEL) if nm == "final_g" else d[nm]) for nm in _SMALL}
    outs.update(_adamw_small("adamw_small", gred, chip, as_rows(W), as_rows(M), as_rows(Vv)))
    outs["final_g"] = [o.reshape(D_MODEL) for o in outs["final_g"]]

    order = ("norm_g", "w_in", "ln_g", "ln_b", "w_s", "b_s", "conv_w", "conv_b", "w_pool", "pool_scale",
             "w_pa", "w_pb", "w_pc", "w_o", "final_g")
    return (loss, grad_x, *[outs[nm][0] for nm in order], *[outs[nm][1] for nm in order],
            *[outs[nm][2] for nm in order], *[outs[nm][3] for nm in order])
```

```python
import functools

import jax
import jax.numpy as jnp
from jax import lax
from jax.experimental import pallas as pl
from jax.experimental.pallas import tpu as pltpu

F32 = jnp.float32
BF16 = jnp.bfloat16

D_MODEL = 1024
DEPTH = 2
CHUNK = 128
WIDTH = 512
POOL_WINDOWS = (2, 4, 8, 16)
IN_TOTAL = 7680
N_CHIPS = 4
SHARD_W = IN_TOTAL // N_CHIPS
RMS_EPS = 1e-6
LN_EPS = 1e-5
HALO = 16

U, V, ZA, XB, BG, CG, ZB, XC, ZC, GA, GB, GC = (0, 512, 1024, 1536, 2048, 2560, 3072, 3584, 4096, 4608, 5632, 6656)

ADAM_LR = 0.001
ADAM_B1 = 0.9
ADAM_B2 = 0.999
ADAM_EPS = 1e-08
ADAM_WD = 0.01
ADAM_STEP = 10

VMEM_LIMIT = 56 * 1024 * 1024
MESH = pl.DeviceIdType.MESH
ANY = pl.BlockSpec(memory_space=pl.ANY)
NT = (((1,), (1,)), ((), ()))
TN = (((0,), (0,)), ((), ()))


def _params(sem=None):
    kw = dict(vmem_limit_bytes=VMEM_LIMIT)
    if sem is not None:
        kw["dimension_semantics"] = sem
    return pltpu.CompilerParams(**kw)


def _dot(a, b):
    return jnp.dot(a, b, preferred_element_type=F32)


def _dotg(a, b, dims):
    return lax.dot_general(a, b, dims, preferred_element_type=F32)


def _sigmoid(x):
    return 1.0 / (1.0 + jnp.exp(-x))


_GELU_K = 0.7978845608028654


def _gelu(x):
    th = jnp.tanh(_GELU_K * (x + 0.044715 * (x * x * x)))
    return 0.5 * x * (1.0 + th), th


def _gelu_grad(x, th):
    return 0.5 * (1.0 + th) + 0.5 * x * (1.0 - th * th) * (_GELU_K * (1.0 + 3.0 * 0.044715 * (x * x)))


def _colsum8(x):
    t, c = x.shape
    return jnp.sum(x.reshape(t // 8, 8, c), axis=0)


def _branches_fwd(p_ref, hxb_ref, hcg_ref, hxc_ref, first, tstart, w, sg_scr):
    t = p_ref.shape[0]
    nch = t // CHUNK

    def seg(o, width=WIDTH):
        return p_ref[:, o:o + width].astype(F32)

    lo = lax.broadcasted_iota(jnp.int32, (CHUNK, CHUNK), 1) < 64
    r = {}
    pu = seg(U)
    u_act, th_u = _gelu(pu)
    pv = seg(V)
    vg, th_v = _gelu(pv)
    mu = jnp.mean(vg, axis=-1, keepdims=True)
    xc = vg - mu
    var = jnp.mean(xc * xc, axis=-1, keepdims=True)
    rs = lax.rsqrt(var + LN_EPS)
    vhat = xc * rs
    vn = vhat * w["ln_g"][...] + w["ln_b"][...]
    vnb = vn.astype(BF16)
    for n in range(nch):
        for j in range(4):
            vb = vnb[n * CHUNK:(n + 1) * CHUNK, j * 128:(j + 1) * 128]
            z = _dot(w["w2"][j], vb)
            sg_scr[n * CHUNK:(n + 1) * CHUNK, j * 128:(j + 1) * 128] = (
                jnp.where(lo, z[:CHUNK], z[CHUNK:]) + w["bst"][:, j * 128:(j + 1) * 128])
    sg = sg_scr[...]
    a_br = u_act * sg
    za = seg(ZA)
    sa = _sigmoid(za)
    r.update(pu=pu, th_u=th_u, pv=pv, th_v=th_v, rs=rs, vhat=vhat, vnb=vnb, u_act=u_act, sg=sg,
             a_br=a_br, za=za, sa=sa, a_in=a_br * (za * sa))

    xb = seg(XB)
    cg = seg(CG)
    yb0 = cg * xb
    hal = hcg_ref[...].astype(F32) * hxb_ref[...].astype(F32)
    hal = jnp.where(first, 0.0, hal)
    ext = jnp.concatenate([hal, yb0], axis=0)
    y1 = pltpu.roll(ext, 1, 0)[HALO:]
    y2 = pltpu.roll(ext, 2, 0)[HALO:]
    cw = w["cw"]
    conv = cw[0:1, :] * y2 + cw[1:2, :] * y1 + cw[2:3, :] * yb0 + w["cb"][...]
    bg = seg(BG)
    b_br = bg * conv
    zb = seg(ZB)
    sb = _sigmoid(zb)
    r.update(xb=xb, cg=cg, yb0=yb0, y1=y1, y2=y2, conv=conv, bg=bg, b_br=b_br, zb=zb, sb=sb,
             b_in=b_br * (zb * sb))

    xcv = seg(XC)
    hxc = jnp.where(first, 0.0, hxc_ref[...].astype(F32))
    extc = jnp.concatenate([hxc, xcv], axis=0)
    tpos = tstart + lax.broadcasted_iota(jnp.int32, (t, 1), 0) + 1
    pooled, inv, q = [], [], []
    for gi, win in enumerate(POOL_WINDOWS):
        s = extc[:, gi * 128:(gi + 1) * 128]
        sh = 1
        while sh < win:
            s = s + pltpu.roll(s, sh, 0)
            sh *= 2
        inv_g = jnp.where(tpos >= win, 1.0 / win, 1.0 / jnp.minimum(tpos, win).astype(F32))
        pg = s[HALO:] * inv_g - xcv[:, gi * 128:(gi + 1) * 128]
        pooled.append(pg)
        inv.append(inv_g)
        q.append(_dot(pg.astype(BF16), w["wpool"][gi]))
    qv = jnp.concatenate(q, axis=1)
    c_br = qv * w["ps"][...]
    zc = seg(ZC)
    sc = _sigmoid(zc)
    r.update(pooled=pooled, inv=inv, q=qv, c_br=c_br, zc=zc, sc=sc, c_in=c_br * (zc * sc))
    return r


def _halo_specs(t, rev_n=None):
    def imap(col):
        def f(i):
            ti = i if rev_n is None else rev_n - 1 - i
            return (jnp.maximum(ti * (t // HALO) - 1, 0), col)
        return f
    return [pl.BlockSpec((HALO, WIDTH), imap(XB // WIDTH)),
            pl.BlockSpec((HALO, WIDTH), imap(CG // WIDTH)),
            pl.BlockSpec((HALO, WIDTH), imap(XC // WIDTH))]


def _const_spec(shape):
    nd = len(shape)
    return pl.BlockSpec(shape, lambda *_: (0,) * nd)


def _place():
    return lax.axis_index("x"), lax.axis_index("y"), lax.axis_index("c")


def _chip_peer(x, y, jm):
    px = (1 - x) if (jm & 2) else x
    py = (1 - y) if (jm & 1) else y
    return px, py


def _chip_part(buf, k, r0=0, nr=None):
    if len(buf.shape) == 3:
        return buf.at[k] if nr is None else buf.at[k, pl.ds(r0, nr)]
    cols = buf.shape[1] // N_CHIPS
    rows = pl.ds(0, buf.shape[0]) if nr is None else pl.ds(r0, nr)
    return buf.at[rows, pl.ds(pl.multiple_of(k * cols, 128), cols)]


def _row_range(ref, r0, nr):
    return ref if nr is None else ref.at[pl.ds(r0, nr)]


class _Comm:
    def __init__(self):
        self.ins, self.thru, self.fresh, self.moves, self.n = [], [], [], [], 0

    def read(self, arr):
        self.ins.append(arr)
        return ("ins", len(self.ins) - 1)

    def through(self, arr):
        self.thru.append(arr)
        return ("thru", len(self.thru) - 1)

    def new(self, shape, dtype):
        self.fresh.append(jax.ShapeDtypeStruct(tuple(shape), dtype))
        return ("fresh", len(self.fresh) - 1)

    def _add(self, n, *move):
        self.moves.append(move)
        self.n += n

    def gather(self, buf, r0=0, nr=None):
        self._add(3, "gather", buf, r0, nr)

    def scatter(self, src, dst, r0=0, nr=None):
        self._add(3, "scatter", src, dst, r0, nr)

    def sibling(self, src, dst, r0=0, nr=None):
        self._add(1, "sibling", src, dst, r0, nr)

    def gather_half(self, buf):
        self._add(3, "gather_half", buf)

    def forward_half(self, buf):
        self._add(3, "forward_half", buf)

    def slices_out(self, src, dst, rows):
        self._add(7, "slices_out", src, dst, rows)

    def slices_back(self, buf, rows):
        self._add(7, "slices_back", buf, rows)

    def copies(self, bufs, x, y, c):
        ref = lambda h: bufs[h[0]][h[1]]
        k_me = 2 * x + y
        me = 4 * x + 2 * y + c
        cps = []
        for move in self.moves:
            kind = move[0]
            if kind in ("gather", "scatter"):
                for jj, jm in enumerate((1, 2, 3)):
                    px, py = _chip_peer(x, y, jm)
                    k_peer = 2 * px + py
                    if kind == "gather":
                        _, buf, r0, nr = move
                        mine = _chip_part(ref(buf), k_me, r0, nr)
                        cps.append((mine, mine, _chip_part(ref(buf), k_peer, r0, nr), (px, py, c)))
                    else:
                        _, src, dst, r0, nr = move
                        slot = ref(dst).at[jj] if nr is None else ref(dst).at[jj, pl.ds(r0, nr)]
                        cps.append((_chip_part(ref(src), k_peer, r0, nr), slot, slot, (px, py, c)))
            elif kind in ("gather_half", "forward_half"):
                buf = ref(move[1])
                hr = buf.shape[-2] // 2
                mine_r0, other_r0 = pl.multiple_of(c * hr, 16), pl.multiple_of((1 - c) * hr, 16)
                for jm in (1, 2, 3):
                    px, py = _chip_peer(x, y, jm)
                    k_peer = 2 * px + py
                    if kind == "gather_half":
                        part = _chip_part(buf, k_me, mine_r0, hr)
                        cps.append((part, part, _chip_part(buf, k_peer, mine_r0, hr), (px, py, c)))
                    else:
                        part = _chip_part(buf, k_peer, mine_r0, hr)
                        cps.append((part, part, _chip_part(buf, k_peer, other_r0, hr), (x, y, 1 - c)))
            elif kind == "sibling":
                _, src, dst, r0, nr = move
                land = _row_range(ref(dst), r0, nr)
                cps.append((_row_range(ref(src), r0, nr), land, land, (x, y, 1 - c)))
            else:
                for j in range(1, 8):
                    px = (1 - x) if (j & 4) else x
                    py = (1 - y) if (j & 2) else y
                    pc = (1 - c) if (j & 1) else c
                    peer = 4 * px + 2 * py + pc
                    if kind == "slices_out":
                        _, src, dst, rows = move
                        slot = ref(dst).at[j - 1]
                        cps.append((ref(src).at[pl.ds(pl.multiple_of(peer * rows, 8), rows)], slot, slot,
                                    (px, py, pc)))
                    else:
                        _, buf, rows = move
                        mine = ref(buf).at[pl.ds(pl.multiple_of(me * rows, 8), rows)]
                        cps.append((mine, mine, ref(buf).at[pl.ds(pl.multiple_of(peer * rows, 8), rows)],
                                    (px, py, pc)))
        assert len(cps) == self.n
        return cps


def _pcall(body, *, name, grid, in_specs, out_specs, out_shape, scratch, sem, args, comm=None, prefetch=(),
           aliases=None):
    n_pre = len(prefetch)
    n_in, n_out, n_scr = len(in_specs), len(out_specs), len(scratch)
    io_alias = {n_pre + i: o for i, o in (aliases or {}).items()}
    if comm is None or comm.n == 0:
        gs = pltpu.PrefetchScalarGridSpec(num_scalar_prefetch=n_pre, grid=grid, in_specs=list(in_specs),
                                          out_specs=list(out_specs), scratch_shapes=list(scratch))
        outs = pl.pallas_call(body, name=name, grid_spec=gs, out_shape=list(out_shape), input_output_aliases=io_alias,
                              compiler_params=_params(sem))(*prefetch, *args)
        return list(outs), [], []
    n_ci, n_ct, n_cf, n_cp = len(comm.ins), len(comm.thru), len(comm.fresh), comm.n

    def wrapped(*refs):
        pos = 0
        def take(n):
            nonlocal pos
            got = refs[pos:pos + n]
            pos += n
            return got
        pre = take(n_pre)
        a, ci, _ = take(n_in), take(n_ci), take(n_ct)
        o, ct, cf = take(n_out), take(n_ct), take(n_cf)
        scr = take(n_scr)
        send_sems, recv_sems = take(2)
        first = functools.reduce(jnp.logical_and, [pl.program_id(d) == 0 for d in range(len(grid))])
        last = functools.reduce(jnp.logical_and, [pl.program_id(d) == grid[d] - 1 for d in range(len(grid))])
        x, y, c = _place()
        cps = comm.copies(dict(ins=ci, thru=ct, fresh=cf), x, y, c)

        def copy(i, src, dst, dev):
            return pltpu.make_async_remote_copy(src_ref=src, dst_ref=dst, send_sem=send_sems.at[i],
                                                recv_sem=recv_sems.at[i], device_id=dev, device_id_type=MESH)

        @pl.when(first)
        def _():
            for i, (src, dst, _, dev) in enumerate(cps):
                copy(i, src, dst, dev).start()

        body(*pre, *a, *o, *scr)

        @pl.when(last)
        def _():
            for i, (src, _, land, dev) in enumerate(cps):
                copy(i, src, land, dev).wait_recv()
            for i, (src, dst, _, dev) in enumerate(cps):
                copy(i, src, dst, dev).wait_send()

    thru_shapes = [jax.ShapeDtypeStruct(t.shape, t.dtype) for t in comm.thru]
    gs = pltpu.PrefetchScalarGridSpec(
        num_scalar_prefetch=n_pre, grid=grid,
        in_specs=list(in_specs) + [ANY] * (n_ci + n_ct),
        out_specs=list(out_specs) + [ANY] * (n_ct + n_cf),
        scratch_shapes=list(scratch) + [pltpu.SemaphoreType.DMA((n_cp,)), pltpu.SemaphoreType.DMA((n_cp,))])
    outs = pl.pallas_call(
        wrapped, name=name, grid_spec=gs,
        out_shape=list(out_shape) + thru_shapes + comm.fresh,
        input_output_aliases={**io_alias, **{n_pre + n_in + n_ci + t: n_out + t for t in range(n_ct)}},
        compiler_params=_params(sem),
    )(*prefetch, *args, *comm.ins, *comm.thru)
    outs = list(outs)
    return outs[:n_out], outs[n_out:n_out + n_ct], outs[n_out + n_ct:]


def _comm_only(name, comm):
    def body():
        pass
    _, thru, _ = _pcall(body, name=name, grid=(1,), in_specs=[], out_specs=[], out_shape=[], scratch=[],
                        sem=("arbitrary",), args=[], comm=comm)
    return thru


def _f1(x, norm_g3, w_all, l, comm=None, tm=256):
    s = x.shape[0]

    def body(x_ref, g_ref, w_hbm, p_ref, h_ref, w_vmem):
        @pl.when(pl.program_id(0) == 0)
        def _():
            pltpu.sync_copy(w_hbm, w_vmem)
        xv = x_ref[...]
        r = lax.rsqrt(jnp.mean(xv * xv, axis=-1, keepdims=True) + RMS_EPS)
        hb = ((xv * r) * g_ref[...]).astype(BF16)
        h_ref[...] = hb
        p_ref[...] = _dot(hb, w_vmem[...]).astype(BF16)

    return _pcall(
        body, name=f"f1_l{l}", grid=(s // tm,),
        in_specs=[pl.BlockSpec((tm, D_MODEL), lambda i: (i, 0)),
                  pl.BlockSpec((None, 1, D_MODEL), lambda i: (l, 0, 0)), ANY],
        out_specs=[pl.BlockSpec((tm, IN_TOTAL), lambda i: (i, 0)),
                   pl.BlockSpec((tm, D_MODEL), lambda i: (i, 0))],
        out_shape=[jax.ShapeDtypeStruct((s, IN_TOTAL), BF16), jax.ShapeDtypeStruct((s, D_MODEL), BF16)],
        scratch=[pltpu.VMEM((D_MODEL, IN_TOTAL), BF16)], sem=("arbitrary",), args=[x, norm_g3, w_all], comm=comm)


def _f1_own(x, norm_g3, w_in, chip, comm, tm=256):
    s = x.shape[0]

    def body(idx_ref, x_ref, g_ref, w_hbm, p_ref, h_ref, w_f32, w_vmem):
        @pl.when(pl.program_id(0) == 0)
        def _():
            pltpu.sync_copy(w_hbm.at[0], w_f32)
            w_vmem[...] = w_f32[...].astype(BF16)
        xv = x_ref[...]
        r = lax.rsqrt(jnp.mean(xv * xv, axis=-1, keepdims=True) + RMS_EPS)
        hb = ((xv * r) * g_ref[...]).astype(BF16)
        h_ref[...] = hb
        p_ref[...] = _dot(hb, w_vmem[...]).astype(BF16)

    return _pcall(
        body, name="f1_l0_own", grid=(s // tm,),
        in_specs=[pl.BlockSpec((tm, D_MODEL), lambda i, idx: (i, 0)),
                  pl.BlockSpec((None, 1, D_MODEL), lambda i, idx: (0, 0, 0)), ANY],
        out_specs=[pl.BlockSpec((tm, SHARD_W), lambda i, idx: (i, idx[0])),
                   pl.BlockSpec((tm, D_MODEL), lambda i, idx: (i, 0))],
        out_shape=[jax.ShapeDtypeStruct((s, IN_TOTAL), BF16), jax.ShapeDtypeStruct((s, D_MODEL), BF16)],
        scratch=[pltpu.VMEM((D_MODEL, SHARD_W), F32), pltpu.VMEM((D_MODEL, SHARD_W), BF16)],
        sem=("arbitrary",), args=[x, norm_g3, w_in], comm=comm, prefetch=[chip])


def _f1_others(h, p, w_all, chip, comm, tm=1024):
    s = h.shape[0]
    tm = min(tm, s)
    cols_of = lambda j, idx: (idx[0] + 1 + j) % N_CHIPS

    def body(idx_ref, h_ref, w_hbm, _, p_ref, w_vmem, sems):
        j = pl.program_id(1)

        @pl.when(jnp.logical_and(pl.program_id(0) == 0, j == 0))
        def _():
            loads = [pltpu.make_async_copy(
                w_hbm.at[:, pl.ds(pl.multiple_of(((idx_ref[0] + 1 + q) % N_CHIPS) * SHARD_W, 128), SHARD_W)],
                w_vmem.at[q], sems.at[q]) for q in range(N_CHIPS - 1)]
            for cp in loads:
                cp.start()
            for cp in loads:
                cp.wait()

        p_ref[...] = _dot(h_ref[...], w_vmem[j]).astype(BF16)

    return _pcall(
        body, name="f1_l0_others", grid=(s // tm, N_CHIPS - 1),
        in_specs=[pl.BlockSpec((tm, D_MODEL), lambda i, j, idx: (i, 0)), ANY, ANY],
        out_specs=[pl.BlockSpec((tm, SHARD_W), lambda i, j, idx: (i, cols_of(j, idx)))],
        out_shape=[jax.ShapeDtypeStruct((s, IN_TOTAL), BF16)],
        scratch=[pltpu.VMEM((N_CHIPS - 1, D_MODEL, SHARD_W), BF16), pltpu.SemaphoreType.DMA((N_CHIPS - 1,))],
        sem=("arbitrary", "arbitrary"), args=[h, w_all, p], comm=comm, prefetch=[chip], aliases={2: 0})


def _f2(x, p, lw, l, comm=None, loss=None, t=256):
    s = x.shape[0]
    n = s // t

    def body(p_ref, hxb_ref, hcg_ref, hxc_ref, x_ref, lng, lnb, w2, bst, cw, cb, wpool, ps,
             proj, wo, *rest):
        if loss is None:
            ya_ref, yb_ref, yc_ref, m_ref, xo_ref, sg_scr = rest
        else:
            t_ref, fg_ref, ya_ref, yb_ref, yc_ref, m_ref, xo_ref, loss_ref, gg_ref, sg_scr, lacc, gacc = rest
        i = pl.program_id(0)
        w = dict(ln_g=lng, ln_b=lnb, w2=w2, bst=bst, cw=cw, cb=cb, wpool=wpool, ps=ps)
        r = _branches_fwd(p_ref, hxb_ref, hcg_ref, hxc_ref, i == 0, i * t, w, sg_scr)

        def project(act, b):
            ab = act.astype(BF16)
            return jnp.concatenate([_dot(ab, proj[k, b * WIDTH:(b + 1) * WIDTH, :]) for k in range(N_CHIPS)], axis=1)

        ya = project(r["a_in"], 0)
        yb = project(r["b_in"], 1)
        yc = project(r["c_in"], 2)
        ya_ref[...] = ya.astype(BF16)
        yb_ref[...] = yb.astype(BF16)
        yc_ref[...] = yc.astype(BF16)
        m = (_sigmoid(p_ref[:, GA:GA + D_MODEL].astype(F32)) * ya
             + _sigmoid(p_ref[:, GB:GB + D_MODEL].astype(F32)) * yb
             + _sigmoid(p_ref[:, GC:GC + D_MODEL].astype(F32)) * yc)
        mb = m.astype(BF16)
        m_ref[...] = mb
        xo = x_ref[...] + _dot(mb, wo[...])
        if loss is None:
            xo_ref[...] = xo
        else:
            @pl.when(i == 0)
            def _():
                lacc[...] = jnp.zeros(lacc.shape, F32)
                gacc[...] = jnp.zeros(gacc.shape, F32)

            g = fg_ref[...]
            rr = lax.rsqrt(jnp.mean(xo * xo, axis=-1, keepdims=True) + RMS_EPS)
            err = (xo * rr) * g - t_ref[...]
            lacc[...] += _colsum8(err * err)
            dx, gx = _rms_bwd(xo, g, err * (1.0 / D_MODEL))
            xo_ref[...] = dx
            gacc[...] += _colsum8(gx)

            @pl.when(i == n - 1)
            def _():
                tot = jnp.sum(jnp.sum(lacc[...], axis=0, keepdims=True), axis=1, keepdims=True)
                loss_ref[...] = jnp.broadcast_to(tot * (0.5 / D_MODEL), loss_ref.shape)
                gg_ref[...] = jnp.sum(gacc[...], axis=0, keepdims=True)

    tile = lambda c: pl.BlockSpec((t, c), lambda i: (i, 0))
    lsel = lambda *blk: pl.BlockSpec((None,) + blk, lambda i: (l,) + (0,) * len(blk))
    act = jax.ShapeDtypeStruct((s, D_MODEL), BF16)
    f32s = lambda *shape: jax.ShapeDtypeStruct(shape, F32)
    with_loss = loss is not None
    return _pcall(
        body, name=f"f2_l{l}", grid=(n,),
        in_specs=[tile(IN_TOTAL)] + _halo_specs(t) + [
            tile(D_MODEL), lsel(1, WIDTH), lsel(1, WIDTH), lsel(4, 256, 128), lsel(CHUNK, WIDTH),
            lsel(8, WIDTH), lsel(1, WIDTH), lsel(4, 128, 128), lsel(1, WIDTH),
            _const_spec((N_CHIPS, 3 * WIDTH, 256)), _const_spec((D_MODEL, D_MODEL))]
        + ([tile(D_MODEL), _const_spec((1, D_MODEL))] if with_loss else []),
        out_specs=[tile(D_MODEL)] * 5 + ([_const_spec((8, 128)), _const_spec((1, D_MODEL))] if with_loss else []),
        out_shape=[act, act, act, act, f32s(s, D_MODEL)] + ([f32s(8, 128), f32s(1, D_MODEL)] if with_loss else []),
        scratch=[pltpu.VMEM((t, WIDTH), F32)]
        + ([pltpu.VMEM((8, D_MODEL), F32), pltpu.VMEM((8, D_MODEL), F32)] if with_loss else []),
        sem=("arbitrary",),
        args=[p, p, p, p, x, lw["ln_g"], lw["ln_b"], lw["w2"], lw["bst"], lw["cw"], lw["cb"], lw["wpool"], lw["ps"],
              lw["proj"][l], lw["wo"][l]] + (list(loss) if with_loss else []), comm=comm)


def _b1(p, dout, ya, yb, yc, lw, l, comm=None, t=256):
    s = p.shape[0]
    n = s // t
    nch = t // CHUNK

    def body(p_ref, hxb_ref, hcg_ref, hxc_ref, do_ref, ya_ref, yb_ref, yc_ref,
             lng, lnb, w2, wt2, bst, cw, cb, wpool, ps, proj_h, wo_h, sel_ref,
             dp_ref, gwp_h, gws_ref, gbs_ref, glng_ref, glnb_ref, gcw_ref, gcb_ref,
             gwpool_ref, gps_ref,
             wpa, wpb, wpc, wo, gwpa, gwpb, gwpc, gbs_acc, vec_acc, sg_scr, dvn_scr, car_dc, car_e, psem):
        i = pl.program_id(0)
        ti = n - 1 - i

        def by_chip_copies(vmem_bufs, hbm, to_hbm):
            cps = []
            for b, buf in enumerate(vmem_bufs):
                for k in range(N_CHIPS):
                    v = buf.at[:, pl.ds(k * 256, 256)]
                    h = hbm.at[k, pl.ds(b * WIDTH, WIDTH)]
                    cps.append(pltpu.make_async_copy(v, h, psem.at[b * N_CHIPS + k]) if to_hbm
                               else pltpu.make_async_copy(h, v, psem.at[b * N_CHIPS + k]))
            return cps

        @pl.when(i == 0)
        def _():
            loads = by_chip_copies((wpa, wpb, wpc), proj_h, False)
            for cp in loads:
                cp.start()
            pltpu.sync_copy(wo_h, wo)
            for cp in loads:
                cp.wait()
            for acc in (gwpa, gwpb, gwpc, gbs_acc, vec_acc, car_dc, car_e):
                acc[...] = jnp.zeros(acc.shape, acc.dtype)
            gws_ref[...] = jnp.zeros(gws_ref.shape, F32)
            gwpool_ref[...] = jnp.zeros(gwpool_ref.shape, F32)

        w = dict(ln_g=lng, ln_b=lnb, w2=w2, bst=bst, cw=cw, cb=cb, wpool=wpool, ps=ps)
        r = _branches_fwd(p_ref, hxb_ref, hcg_ref, hxc_ref, ti == 0, ti * t, w, sg_scr)

        def seg(o, width=WIDTH):
            return p_ref[:, o:o + width].astype(F32)

        def put(o, val):
            dp_ref[:, o:o + val.shape[1]] = val.astype(BF16)

        dob = do_ref[...].astype(BF16)
        dm = _dotg(dob, wo[...], NT)

        def merge_bwd(goff, y_ref, xin, wp, gwp):
            sx = _sigmoid(seg(goff, D_MODEL))
            dmy = dm * sx
            put(goff, dmy * y_ref[...].astype(F32) * (1.0 - sx))
            dyb = dmy.astype(BF16)
            gwp[...] += _dotg(xin.astype(BF16), dyb, TN)
            return _dotg(dyb, wp[...], NT)

        d_ain = merge_bwd(GA, ya_ref, r["a_in"], wpa, gwpa)
        d_bin = merge_bwd(GB, yb_ref, r["b_in"], wpb, gwpb)
        d_cin = merge_bwd(GC, yc_ref, r["c_in"], wpc, gwpc)

        def dsilu(z, sz):
            return sz * (1.0 + z * (1.0 - sz))

        za, sa = r["za"], r["sa"]
        d_abr = d_ain * (za * sa)
        put(ZA, d_ain * r["a_br"] * dsilu(za, sa))
        put(U, d_abr * r["sg"] * _gelu_grad(r["pu"], r["th_u"]))
        d_sg = d_abr * r["u_act"]
        dsgb = d_sg.astype(BF16)
        lo = lax.broadcasted_iota(jnp.int32, (CHUNK, CHUNK), 1) < 64
        zero = jnp.zeros((CHUNK, CHUNK), BF16)
        for c in range(nch):
            rows = slice(c * CHUNK, (c + 1) * CHUNK)
            gbs_acc[...] += d_sg[rows]
            for j in range(4):
                cols = slice(j * 128, (j + 1) * 128)
                dj = dsgb[rows, cols]
                zt = _dot(wt2[j], dj)
                dvn_scr[rows, cols] = jnp.where(lo, zt[:CHUNK], zt[CHUNK:])
                stacked = jnp.concatenate([jnp.where(lo, dj, zero), jnp.where(lo, zero, dj)], axis=0)
                gws_ref[j] += _dotg(stacked, r["vnb"][rows, cols], NT)
        d_vn = dvn_scr[...]
        vhat = r["vhat"]
        vec_acc[0] += _colsum8(d_vn * vhat)
        vec_acc[1] += _colsum8(d_vn)
        d_vhat = d_vn * lng[...]
        d_vg = r["rs"] * (d_vhat - jnp.mean(d_vhat, axis=-1, keepdims=True)
                          - vhat * jnp.mean(d_vhat * vhat, axis=-1, keepdims=True))
        put(V, d_vg * _gelu_grad(r["pv"], r["th_v"]))

        zb, sb = r["zb"], r["sb"]
        d_bbr = d_bin * (zb * sb)
        put(ZB, d_bin * r["b_br"] * dsilu(zb, sb))
        put(BG, d_bbr * r["conv"])
        dc = d_bbr * r["bg"]
        vec_acc[2] += _colsum8(dc)
        vec_acc[3] += _colsum8(dc * r["y2"])
        vec_acc[4] += _colsum8(dc * r["y1"])
        vec_acc[5] += _colsum8(dc * r["yb0"])
        ext = jnp.concatenate([dc, car_dc[...]], axis=0)
        ne = t + HALO
        d1 = pltpu.roll(ext, ne - 1, 0)[:t]
        d2 = pltpu.roll(ext, ne - 2, 0)[:t]
        d_yb0 = cw[2:3, :] * dc + cw[1:2, :] * d1 + cw[0:1, :] * d2
        put(CG, d_yb0 * r["xb"])
        put(XB, d_yb0 * r["cg"])
        car_dc[...] = dc[:HALO]

        zc, sc = r["zc"], r["sc"]
        d_cbr = d_cin * (zc * sc)
        put(ZC, d_cin * r["c_br"] * dsilu(zc, sc))
        vec_acc[6] += _colsum8(d_cbr * r["q"])
        d_q = d_cbr * ps[...]
        for gi, win in enumerate(POOL_WINDOWS):
            cols = slice(gi * 128, (gi + 1) * 128)
            dqb = d_q[:, cols].astype(BF16)
            d_pool = _dotg(dqb, wpool[gi], NT)
            gwpool_ref[gi] += _dotg(r["pooled"][gi].astype(BF16), dqb, TN)
            e = d_pool * r["inv"][gi]
            sx = jnp.concatenate([e, car_e[:, cols]], axis=0)
            sh = 1
            while sh < win:
                sx = sx + pltpu.roll(sx, ne - sh, 0)
                sh *= 2
            put(XC + gi * 128, sx[:t] - d_pool)
            car_e[:, cols] = e[:HALO]

        @pl.when(i == n - 1)
        def _():
            for acc, stage in ((gwpa, wpa), (gwpb, wpb), (gwpc, wpc)):
                stage[...] = acc[...].astype(BF16)
            stores = by_chip_copies((wpa, wpb, wpc), gwp_h, True)
            for cp in stores:
                cp.start()
            for cp in stores:
                cp.wait()
            gbs_ref[...] = jnp.dot(gbs_acc[...], sel_ref[...], preferred_element_type=F32,
                                   precision=lax.Precision.HIGHEST)
            red = lambda k: jnp.sum(vec_acc[k], axis=0, keepdims=True)
            glng_ref[...] = red(0)
            glnb_ref[...] = red(1)
            gcb_ref[...] = red(2)
            gcw_ref[...] = jnp.zeros(gcw_ref.shape, F32)
            for k in range(3):
                gcw_ref[k:k + 1, :] = red(3 + k)
            gps_ref[...] = red(6)
            tt = lax.broadcasted_iota(jnp.int32, (2 * CHUNK, CHUNK), 0) % CHUNK
            ss = lax.broadcasted_iota(jnp.int32, (2 * CHUNK, CHUNK), 1)
            for j in range(4):
                gws_ref[j] = jnp.where(tt >= ss, gws_ref[j], 0.0)

    rtile = lambda c: pl.BlockSpec((t, c), lambda i: (n - 1 - i, 0))
    lsel = lambda *blk: pl.BlockSpec((None,) + blk, lambda i: (l,) + (0,) * len(blk))
    f32s = lambda *shape: jax.ShapeDtypeStruct(shape, F32)
    return _pcall(
        body, name=f"b1_l{l}", grid=(n,),
        in_specs=[rtile(IN_TOTAL)] + _halo_specs(t, rev_n=n) + [rtile(D_MODEL)] * 4 + [
            lsel(1, WIDTH), lsel(1, WIDTH), lsel(4, 256, 128), lsel(4, 256, 128), lsel(CHUNK, WIDTH),
            lsel(8, WIDTH), lsel(1, WIDTH), lsel(4, 128, 128), lsel(1, WIDTH),
            ANY, ANY, _const_spec((WIDTH, 128))],
        out_specs=[rtile(IN_TOTAL), ANY,
                   _const_spec((4, 256, 128)), _const_spec((CHUNK, 128)), _const_spec((1, WIDTH)),
                   _const_spec((1, WIDTH)), _const_spec((8, WIDTH)), _const_spec((1, WIDTH)),
                   _const_spec((4, 128, 128)), _const_spec((1, WIDTH))],
        out_shape=[jax.ShapeDtypeStruct((s, IN_TOTAL), BF16), jax.ShapeDtypeStruct((N_CHIPS, 3 * WIDTH, 256), BF16),
                   f32s(4, 256, 128), f32s(CHUNK, 128),
                   f32s(1, WIDTH), f32s(1, WIDTH), f32s(8, WIDTH), f32s(1, WIDTH), f32s(4, 128, 128),
                   f32s(1, WIDTH)],
        scratch=[pltpu.VMEM((WIDTH, D_MODEL), BF16), pltpu.VMEM((WIDTH, D_MODEL), BF16),
                 pltpu.VMEM((WIDTH, D_MODEL), BF16), pltpu.VMEM((D_MODEL, D_MODEL), BF16),
                 pltpu.VMEM((WIDTH, D_MODEL), F32),
                 pltpu.VMEM((WIDTH, D_MODEL), F32), pltpu.VMEM((WIDTH, D_MODEL), F32),
                 pltpu.VMEM((CHUNK, WIDTH), F32), pltpu.VMEM((8, 8, WIDTH), F32),
                 pltpu.VMEM((t, WIDTH), F32), pltpu.VMEM((t, WIDTH), F32),
                 pltpu.VMEM((HALO, WIDTH), F32), pltpu.VMEM((HALO, WIDTH), F32),
                 pltpu.SemaphoreType.DMA((3 * N_CHIPS,))],
        sem=("arbitrary",),
        args=[p, p, p, p, dout, ya, yb, yc, lw["ln_g"], lw["ln_b"], lw["w2"], lw["wt2"], lw["bst"], lw["cw"],
              lw["cb"], lw["wpool"], lw["ps"], lw["proj"][l], lw["wo"][l], lw["sel"]], comm=comm)


def _rms_bwd(xv, g, dh):
    r = lax.rsqrt(jnp.mean(xv * xv, axis=-1, keepdims=True) + RMS_EPS)
    xhat = xv * r
    dxh = dh * g
    dx = r * (dxh - xhat * jnp.mean(dxh * xhat, axis=-1, keepdims=True))
    return dx, dh * xhat


def _b2a(dp, w_all, x, dout, norm_g3, l, comm=None, tm=256):
    s = x.shape[0]
    nm = s // tm

    def body(dp_ref, w_hbm, x_ref, do_ref, g_ref, dx_ref, gg_ref, w_vmem, gacc):
        i = pl.program_id(0)

        @pl.when(i == 0)
        def _():
            pltpu.sync_copy(w_hbm, w_vmem)
            gacc[...] = jnp.zeros(gacc.shape, F32)

        dh = _dotg(dp_ref[...], w_vmem[...], NT)
        dx, gx = _rms_bwd(x_ref[...], g_ref[...], dh)
        dx_ref[...] = do_ref[...] + dx
        gacc[...] += _colsum8(gx)

        @pl.when(i == nm - 1)
        def _():
            gg_ref[...] = jnp.sum(gacc[...], axis=0, keepdims=True)

    return _pcall(
        body, name=f"b2a_l{l}", grid=(nm,),
        in_specs=[pl.BlockSpec((tm, IN_TOTAL), lambda i: (i, 0)), ANY,
                  pl.BlockSpec((tm, D_MODEL), lambda i: (i, 0)),
                  pl.BlockSpec((tm, D_MODEL), lambda i: (i, 0)),
                  pl.BlockSpec((None, 1, D_MODEL), lambda i: (l, 0, 0))],
        out_specs=[pl.BlockSpec((tm, D_MODEL), lambda i: (i, 0)),
                   pl.BlockSpec((1, D_MODEL), lambda i: (0, 0))],
        out_shape=[jax.ShapeDtypeStruct((s, D_MODEL), F32), jax.ShapeDtypeStruct((1, D_MODEL), F32)],
        scratch=[pltpu.VMEM((D_MODEL, IN_TOTAL), BF16), pltpu.VMEM((8, D_MODEL), F32)],
        sem=("arbitrary",), args=[dp, w_all, x, dout, norm_g3], comm=comm)


def _b2b(h, dp, name, chip, part="all", comm=None, tk=2048):
    s = h.shape[0]
    tk = min(tk, s)
    nk = s // tk
    first, count = dict(all=(0, 4), others=(1, 3), own=(0, 1))[part]
    cols_of = lambda k, idx: (idx[0] + first + k) % N_CHIPS

    def body(idx_ref, h_ref, dp_ref, g_ref, acc):
        kk = pl.program_id(1)

        @pl.when(kk == 0)
        def _():
            acc[...] = jnp.zeros(acc.shape, F32)

        acc[...] += _dotg(h_ref[...], dp_ref[...], TN)

        @pl.when(kk == nk - 1)
        def _():
            g_ref[...] = acc[...].astype(BF16)

    own = part == "own"
    return _pcall(
        body, name=name, grid=(count, nk),
        in_specs=[pl.BlockSpec((tk, D_MODEL), lambda k, kk, idx: (kk, 0)),
                  pl.BlockSpec((tk, SHARD_W), lambda k, kk, idx: (kk, cols_of(k, idx)))],
        out_specs=[pl.BlockSpec((D_MODEL, SHARD_W), lambda k, kk, idx: (0, 0 if own else cols_of(k, idx)))],
        out_shape=[jax.ShapeDtypeStruct((D_MODEL, SHARD_W if own else IN_TOTAL), BF16)],
        scratch=[pltpu.VMEM((D_MODEL, SHARD_W), F32)],
        sem=("arbitrary", "arbitrary"), args=[h, dp], comm=comm, prefetch=[chip])


def _gwo(m, dout, l, tk=2048):
    s = m.shape[0]
    tk = min(tk, s)
    nk = s // tk

    def body(m_ref, do_ref, g_ref, acc):
        kk = pl.program_id(0)

        @pl.when(kk == 0)
        def _():
            acc[...] = jnp.zeros(acc.shape, F32)

        acc[...] += _dotg(m_ref[...], do_ref[...].astype(BF16), TN)

        @pl.when(kk == nk - 1)
        def _():
            g_ref[...] = acc[...].astype(BF16)

    return pl.pallas_call(
        body, name=f"gwo_l{l}", grid=(nk,),
        in_specs=[pl.BlockSpec((tk, D_MODEL), lambda kk: (kk, 0)), pl.BlockSpec((tk, D_MODEL), lambda kk: (kk, 0))],
        out_specs=_const_spec((D_MODEL, D_MODEL)),
        out_shape=jax.ShapeDtypeStruct((D_MODEL, D_MODEL), BF16),
        scratch_shapes=[pltpu.VMEM((D_MODEL, D_MODEL), F32)],
        compiler_params=_params(("arbitrary",)),
    )(m, dout)


def _row_block(rows, cols, n_arrays):
    budget = VMEM_LIMIT // 3 // (2 * 4 * n_arrays * cols)
    rb = rows
    while rb > budget and rb % 16 == 0:
        rb //= 2
    return rb


def _cast_slot(name, a, l, chip):
    _, rows, cols = a.shape
    rb = _row_block(rows, cols, 2)

    def body(idx_ref, a_ref, o_ref):
        o_ref[...] = a_ref[...].astype(BF16)

    gs = pltpu.PrefetchScalarGridSpec(
        num_scalar_prefetch=1, grid=(rows // rb,),
        in_specs=[pl.BlockSpec((None, rb, cols), lambda i, idx: (l, i, 0))],
        out_specs=pl.BlockSpec((None, rb, cols), lambda i, idx: (idx[0], i, 0)))
    return pl.pallas_call(body, name=name, grid_spec=gs, out_shape=jax.ShapeDtypeStruct((N_CHIPS, rows, cols), BF16),
                          compiler_params=_params(("arbitrary",)))(chip, a)


def _cast_cols(name, a, l, chip):
    _, rows, cols = a.shape
    rb = _row_block(rows, cols, 2)

    def body(idx_ref, a_ref, o_ref):
        o_ref[...] = a_ref[...].astype(BF16)

    gs = pltpu.PrefetchScalarGridSpec(
        num_scalar_prefetch=1, grid=(rows // rb,),
        in_specs=[pl.BlockSpec((None, rb, cols), lambda i, idx: (l, i, 0))],
        out_specs=pl.BlockSpec((rb, cols), lambda i, idx: (i, idx[0])))
    return pl.pallas_call(body, name=name, grid_spec=gs, out_shape=jax.ShapeDtypeStruct((rows, N_CHIPS * cols), BF16),
                          compiler_params=_params(("arbitrary",)))(chip, a)


def _sum4(name, own, chip, recv, l, n_layers, prev=None):
    _, rows, cols = recv.shape
    rb = _row_block(rows, cols, 5)
    nb = rows // rb
    if len(own.shape) == 3:
        own_spec = pl.BlockSpec((None, rb, cols), lambda i, idx: (idx[0], i, 0))
    elif own.shape[1] == cols:
        own_spec = pl.BlockSpec((rb, cols), lambda i, idx: (i, 0))
    else:
        own_spec = pl.BlockSpec((rb, cols), lambda i, idx: (i, idx[0]))

    def body(idx_ref, own_ref, r_ref, *rest):
        o_ref = rest[-1]
        o_ref[...] = (((own_ref[...].astype(F32) + r_ref[0].astype(F32)) + r_ref[1].astype(F32))
                      + r_ref[2].astype(F32)).astype(BF16)

    gs = pltpu.PrefetchScalarGridSpec(
        num_scalar_prefetch=1, grid=(nb,),
        in_specs=[own_spec,
                  pl.BlockSpec((3, rb, cols), lambda i, idx: (0, i, 0))] + ([ANY] if prev is not None else []),
        out_specs=pl.BlockSpec((rb, cols), lambda i, idx: (l * nb + i, 0)))
    args = (chip, own, recv) + ((prev,) if prev is not None else ())
    return pl.pallas_call(body, name=name, grid_spec=gs,
                          out_shape=jax.ShapeDtypeStruct((n_layers * rows, cols), BF16),
                          input_output_aliases=({3: 0} if prev is not None else {}),
                          compiler_params=_params(("arbitrary",)))(*args)


def _sum8(name, pack, me, recv):
    rows = recv.shape[1]

    def body(idx_ref, own_ref, r_ref, o_ref):
        acc = own_ref[...]
        for j in range(7):
            acc = acc + r_ref[j]
        o_ref[...] = acc

    gs = pltpu.PrefetchScalarGridSpec(
        num_scalar_prefetch=1, grid=(1,),
        in_specs=[pl.BlockSpec((rows, 128), lambda i, idx: (idx[0], 0)),
                  pl.BlockSpec((7, rows, 128), lambda i, idx: (0, 0, 0))],
        out_specs=pl.BlockSpec((rows, 128), lambda i, idx: (idx[0], 0)))
    return pl.pallas_call(body, name=name, grid_spec=gs, out_shape=jax.ShapeDtypeStruct(pack.shape, F32),
                          compiler_params=_params(("arbitrary",)))(me, pack, recv)


_SMALL = ("norm_g", "ln_g", "ln_b", "w_s", "b_s", "conv_b", "w_pool", "pool_scale", "final_g", "conv_w")
_SMALL_SHAPES = dict(norm_g=(DEPTH, D_MODEL), ln_g=(DEPTH, WIDTH), ln_b=(DEPTH, WIDTH), w_s=(DEPTH, 8, CHUNK, CHUNK),
                     b_s=(DEPTH, 8, CHUNK), conv_b=(DEPTH, WIDTH), w_pool=(DEPTH, 4, 128, 128),
                     pool_scale=(DEPTH, WIDTH), final_g=(1, D_MODEL), conv_w=(DEPTH, 3, WIDTH))


def _small_rows():
    base, r = {}, 0
    for nm in _SMALL:
        base[nm] = r
        size = 1
        for d in _SMALL_SHAPES[nm]:
            size *= d
        r += size // 128
    base["loss"] = r
    return base, -(-(r + 8) // 64) * 64


def _pack_small(name, raw, gng1, g_final, loss_blk):
    base, rows = _small_rows()
    n_l = len(raw[0])

    def body(*refs):
        o = refs[-1]
        per_layer = [refs[l * n_l:(l + 1) * n_l] for l in range(DEPTH)]
        gng1_ref, gfin_ref, loss_ref = refs[DEPTH * n_l:DEPTH * n_l + 3]
        o[...] = jnp.zeros(o.shape, F32)
        o[base["loss"]:base["loss"] + 8, :] = loss_ref[...]

        def put_row_vector(r0, ref, width):
            for j in range(width // 128):
                o[r0 + j:r0 + j + 1, :] = ref[0:1, j * 128:(j + 1) * 128]

        put_row_vector(base["norm_g"] + D_MODEL // 128, gng1_ref, D_MODEL)
        put_row_vector(base["final_g"], gfin_ref, D_MODEL)
        for l in range(DEPTH):
            gws, gbs, glng, glnb, gcw, gcb, gwpool, gps = per_layer[l]
            put_row_vector(base["ln_g"] + l * 4, glng, WIDTH)
            put_row_vector(base["ln_b"] + l * 4, glnb, WIDTH)
            put_row_vector(base["conv_b"] + l * 4, gcb, WIDTH)
            put_row_vector(base["pool_scale"] + l * 4, gps, WIDTH)
            o[base["w_s"] + l * 1024:base["w_s"] + (l + 1) * 1024, :] = gws[...].reshape(1024, 128)
            o[base["w_pool"] + l * 512:base["w_pool"] + (l + 1) * 512, :] = gwpool[...].reshape(512, 128)
            o[base["b_s"] + l * 8:base["b_s"] + (l + 1) * 8, :] = gbs[...].T[0:8, :]
            for k in range(3):
                for ch in range(N_CHIPS):
                    r = base["conv_w"] + (l * 3 + k) * N_CHIPS + ch
                    o[r:r + 1, :] = gcw[k:k + 1, ch * 128:(ch + 1) * 128]

    args = [a for l in range(DEPTH) for a in raw[l]] + [gng1, g_final, loss_blk]
    vm = pl.BlockSpec(memory_space=pltpu.VMEM)
    return pl.pallas_call(body, name=name, in_specs=[vm] * len(args), out_specs=vm,
                          out_shape=jax.ShapeDtypeStruct((rows, 128), F32), compiler_params=_params())(*args)


def _adamw_small(name, gred, chip, w, m, v):
    base, _ = _small_rows()

    def body(chip_ref, g_ref, *refs):
        n = len(_SMALL)
        w_r, m_r, v_r = refs[:n], refs[n:2 * n], refs[2 * n:3 * n]
        out = refs[3 * n:]

        def update(i, idx, g):
            d, mn, vn = _adamw_math(w_r[i][idx], g, m_r[i][idx], v_r[i][idx])
            for o, val in zip(out[4 * i:4 * i + 4], (g, d, mn, vn)):
                o[idx] = val

        for i, nm in enumerate(_SMALL):
            shape = _SMALL_SHAPES[nm]
            if nm == "conv_w":
                for l in range(DEPTH):
                    for k in range(3):
                        row = base[nm] + (l * 3 + k) * N_CHIPS + chip_ref[0]
                        update(i, (l, slice(k, k + 1), slice(None)), g_ref[pl.ds(row, 1), :])
            elif len(shape) == 2:
                per = shape[1] // 128
                for l in range(shape[0]):
                    for j in range(per):
                        r = base[nm] + l * per + j
                        update(i, (slice(l, l + 1), slice(j * 128, (j + 1) * 128)), g_ref[r:r + 1, :])
            else:
                rows = 1
                for dim in shape[:-1]:
                    rows *= dim
                update(i, (Ellipsis,), g_ref[base[nm]:base[nm] + rows, :].reshape(shape))

    arrs = [d[nm] for d in (w, m, v) for nm in _SMALL]
    vm = pl.BlockSpec(memory_space=pltpu.VMEM)
    gs = pltpu.PrefetchScalarGridSpec(num_scalar_prefetch=1, grid=(1,), in_specs=[vm] * (1 + len(arrs)),
                                      out_specs=[vm] * (4 * len(_SMALL)))
    outs = pl.pallas_call(
        body, name=name, grid_spec=gs,
        out_shape=[jax.ShapeDtypeStruct(w[nm].shape, F32) for nm in _SMALL for _ in range(4)],
        compiler_params=_params(("arbitrary",)))(chip, gred, *arrs)
    return {nm: list(outs[4 * i:4 * i + 4]) for i, nm in enumerate(_SMALL)}


def _adamw_math(w, g, m, v):
    m = ADAM_B1 * m + (1.0 - ADAM_B1) * g
    v = ADAM_B2 * v + (1.0 - ADAM_B2) * (g * g)
    m_hat = m / (1.0 - ADAM_B1 ** ADAM_STEP)
    v_hat = v / (1.0 - ADAM_B2 ** ADAM_STEP)
    delta = -ADAM_LR * (m_hat / (jnp.sqrt(v_hat) + ADAM_EPS) + ADAM_WD * w)
    return delta, m, v


def _adamw(name, w, m, v, g_parts):
    rows, cols = w.shape
    np_ = len(g_parts)
    rb = _row_block(rows, cols, 7 + np_)

    def body(*refs):
        w_ref, m_ref, v_ref = refs[:3]
        g_refs = refs[3:3 + np_]
        go_ref, d_ref, mo_ref, vo_ref = refs[3 + np_:]
        g = g_refs[0][...].astype(F32)
        for gr in g_refs[1:]:
            g = g + gr[...].astype(F32)
        d, mn, vn = _adamw_math(w_ref[...], g, m_ref[...], v_ref[...])
        go_ref[...] = g
        d_ref[...] = d
        mo_ref[...] = mn
        vo_ref[...] = vn

    spec = pl.BlockSpec((rb, cols), lambda i: (i, 0))
    shp = jax.ShapeDtypeStruct((rows, cols), F32)
    return pl.pallas_call(body, name=name, grid=(rows // rb,), in_specs=[spec] * (3 + np_),
                          out_specs=[spec] * 4, out_shape=[shp] * 4,
                          compiler_params=_params(("arbitrary",)))(w, m, v, *g_parts)


def _all_reduce_and_swap(name, a, srcs, dsts, rows):
    n = len(srcs)

    def body(a_ref, *refs):
        src, o_ref, dst = refs[:n], refs[2 * n], refs[2 * n + 1:3 * n + 1]
        buf, send_sems, recv_sems, swap_send, swap_recv = refs[3 * n + 1:]
        x, y, c = _place()
        swaps = [pltpu.make_async_remote_copy(
            src_ref=src[i].at[pl.ds(0, rows[i])], dst_ref=dst[i].at[pl.ds(0, rows[i])], send_sem=swap_send.at[i],
            recv_sem=swap_recv.at[i], device_id=(x, y, 1 - c), device_id_type=MESH) for i in range(n)]
        for cp in swaps:
            cp.start()
        o_ref[...] = a_ref[...]
        for rnd, peer in enumerate(((x, y, 1 - c), (x, 1 - y, c), (1 - x, y, c))):
            cp = pltpu.make_async_remote_copy(
                src_ref=o_ref, dst_ref=buf.at[rnd], send_sem=send_sems.at[rnd], recv_sem=recv_sems.at[rnd],
                device_id=peer, device_id_type=MESH)
            cp.start()
            cp.wait_recv()
            cp.wait_send()
            o_ref[...] = o_ref[...] + buf[rnd]
        for cp in swaps:
            cp.wait_recv()
        for cp in swaps:
            cp.wait_send()

    vm = pl.BlockSpec(memory_space=pltpu.VMEM)
    outs = pl.pallas_call(
        body, name=name, in_specs=[vm] + [ANY] * (2 * n), out_specs=[vm] + [ANY] * n,
        out_shape=[jax.ShapeDtypeStruct(a.shape, F32)] + [jax.ShapeDtypeStruct(d.shape, d.dtype) for d in dsts],
        input_output_aliases={1 + n + i: 1 + i for i in range(n)},
        scratch_shapes=[pltpu.VMEM((3,) + a.shape, F32), pltpu.SemaphoreType.DMA((3,)),
                        pltpu.SemaphoreType.DMA((3,)), pltpu.SemaphoreType.DMA((n,)), pltpu.SemaphoreType.DMA((n,))],
        compiler_params=_params(),
    )(a, *srcs, *dsts)
    return outs[0], list(outs[1:])


def kernel(x, norm_g, w_in, ln_g, ln_b, w_s, b_s, conv_w, conv_b, w_pool, pool_scale, w_pa, w_pb, w_pc, w_o, final_g, loss_target, m_norm_g, m_w_in, m_ln_g, m_ln_b, m_w_s, m_b_s, m_conv_w, m_conv_b, m_w_pool, m_pool_scale, m_w_pa, m_w_pb, m_w_pc, m_w_o, m_final_g, v_norm_g, v_w_in, v_ln_g, v_ln_b, v_w_s, v_b_s, v_conv_w, v_conv_b, v_w_pool, v_pool_scale, v_w_pa, v_w_pb, v_w_pc, v_w_o, v_final_g):
    W = dict(norm_g=norm_g, w_in=w_in, ln_g=ln_g, ln_b=ln_b, w_s=w_s, b_s=b_s, conv_w=conv_w, conv_b=conv_b,
             w_pool=w_pool, pool_scale=pool_scale, w_pa=w_pa, w_pb=w_pb, w_pc=w_pc, w_o=w_o, final_g=final_g)
    M = dict(norm_g=m_norm_g, w_in=m_w_in, ln_g=m_ln_g, ln_b=m_ln_b, w_s=m_w_s, b_s=m_b_s, conv_w=m_conv_w,
             conv_b=m_conv_b, w_pool=m_w_pool, pool_scale=m_pool_scale, w_pa=m_w_pa, w_pb=m_w_pb, w_pc=m_w_pc,
             w_o=m_w_o, final_g=m_final_g)
    Vv = dict(norm_g=v_norm_g, w_in=v_w_in, ln_g=v_ln_g, ln_b=v_ln_b, w_s=v_w_s, b_s=v_b_s, conv_w=v_conv_w,
              conv_b=v_conv_b, w_pool=v_w_pool, pool_scale=v_pool_scale, w_pa=v_w_pa, w_pb=v_w_pb, w_pc=v_w_pc,
              w_o=v_w_o, final_g=v_final_g)
    L = DEPTH
    s = x.shape[1]
    xs = x.reshape(s, D_MODEL)
    tgt = loss_target.reshape(s, D_MODEL)
    k_me = (2 * lax.axis_index("x") + lax.axis_index("y")).astype(jnp.int32)

    chip = k_me.reshape(1)
    assert L == 2
    half_rows = D_MODEL // 2

    land_win = [_cast_cols(f"cast_w_in{l}", w_in, l, chip) for l in range(L)]
    pcat = lambda d: jnp.concatenate([d[b][l] for l in range(L) for b in ("w_pa", "w_pb", "w_pc")], axis=0)
    pcat_w = pcat(W).reshape(L, 3 * WIDTH, 256)
    land_proj = [_cast_slot(f"cast_proj{l}", pcat_w, l, chip) for l in range(L)]
    land_wo = [_cast_slot(f"cast_w_o{l}", w_o, l, chip) for l in range(L)]
    cw_sh = jnp.pad(conv_w, ((0, 0), (0, 5), (0, 0))).reshape(1, L * 8, 128)
    land_cw = lax.dynamic_update_slice(jnp.zeros((N_CHIPS, L * 8, 128), F32), cw_sh, (k_me, 0, 0))

    causal = jnp.tril(jnp.ones((CHUNK, CHUNK), dtype=bool))
    w_m = jnp.where(causal, w_s, 0.0)
    lw = dict(
        ln_g=ln_g.reshape(L, 1, WIDTH), ln_b=ln_b.reshape(L, 1, WIDTH),
        w2=w_m.reshape(L, 4, 256, CHUNK).astype(BF16),
        wt2=jnp.swapaxes(w_m, -1, -2).reshape(L, 4, 256, CHUNK).astype(BF16),
        bst=jnp.repeat(jnp.swapaxes(b_s, -1, -2), 64, axis=-1),
        cb=conv_b.reshape(L, 1, WIDTH), wpool=w_pool.astype(BF16), ps=pool_scale.reshape(L, 1, WIDTH),
        sel=(jnp.arange(WIDTH)[:, None] // 64 == jnp.arange(128)[None, :]).astype(F32))
    norm_g3 = norm_g.reshape(L, 1, D_MODEL)

    cm = _Comm()
    cm.gather_half(cm.through(land_win[0]))
    (p0, h0), (w_half0,), _ = _f1_own(xs, norm_g3, w_in, chip, cm)
    cm = _Comm()
    cm.forward_half(cm.through(w_half0))
    (w_all0,) = _comm_only("forward_w_in0", cm)
    first_rows = D_MODEL // 4
    cm = _Comm()
    for buf in (land_proj[0], land_wo[0], land_cw):
        cm.gather(cm.through(buf))
    cm.gather(cm.through(land_win[1]), 0, first_rows)
    (p0,), (g_proj0, g_wo0, g_cw, w_part1), _ = _f1_others(h0, p0, w_all0, chip, cm)
    lw["proj"] = [g_proj0, None]
    lw["wo"] = [g_wo0.reshape(D_MODEL, D_MODEL), None]
    lw["cw"] = g_cw.reshape(N_CHIPS, L, 8, 128).transpose(1, 2, 0, 3).reshape(L, 8, WIDTH)
    cm = _Comm()
    cm.gather(cm.through(w_part1), first_rows, D_MODEL - first_rows)
    (ya0, yb0, yc0, mm0, x1), (w_all1,), _ = _f2(xs, p0, lw, 0, comm=cm)
    cm = _Comm()
    for buf in (land_proj[1], land_wo[1]):
        cm.gather(cm.through(buf))
    (p1, h1), (g_proj1, g_wo1), _ = _f1(x1, norm_g3, w_all1, 1, comm=cm)
    lw["proj"][1] = g_proj1
    lw["wo"][1] = g_wo1.reshape(D_MODEL, D_MODEL)
    (ya1, yb1, yc1, mm1, dxl, loss_blk, g_final), _, _ = _f2(x1, p1, lw, 1,
                                                              loss=(tgt, final_g.reshape(1, D_MODEL)))

    by_chip = lambda gwp, gwo: (gwp, gwo.reshape(N_CHIPS, 256, D_MODEL))
    recv_like = lambda a: ((3,) + a.shape[1:], a.dtype)
    part_rows = (D_MODEL, 3 * WIDTH, 256)
    gwin, recv_win, part_proj, part_wo, recv_proj, recv_wo = ([None] * L for _ in range(6))

    def sums(l, half):
        return [_sum4(f"sum_w_in{l}", gwin[l], chip, recv_win[l], l, L, half[0]),
                _sum4(f"sum_proj{l}", part_proj[l], chip, recv_proj[l], l, L, half[1]),
                _sum4(f"sum_w_o{l}", part_wo[l], chip, recv_wo[l], l, L, half[2])]

    gwo1 = _gwo(mm1, dxl, 1)
    (dp1, gwp1, *sm1), _, _ = _b1(p1, dxl, ya1, yb1, yc1, lw, 1)
    (gwin[1],), _, _ = _b2b(h1, dp1, "b2b_l1", chip)
    cm = _Comm()
    cm.scatter(cm.read(gwin[1]), cm.new((3, D_MODEL, SHARD_W), BF16), 0, half_rows)
    (dxl, gng1), _, (recv_half,) = _b2a(dp1, w_all1, x1, dxl, norm_g3, 1, comm=cm)
    part_proj[1], part_wo[1] = by_chip(gwp1, gwo1)
    gwo0 = _gwo(mm0, dxl, 0)
    cm = _Comm()
    cm.scatter(cm.read(gwin[1]), cm.through(recv_half), half_rows, half_rows)
    cm.scatter(cm.read(part_proj[1]), cm.new(*recv_like(part_proj[1])))
    cm.scatter(cm.read(part_wo[1]), cm.new(*recv_like(part_wo[1])))
    (dp0, gwp0, *sm0), (recv_win[1],), (recv_proj[1], recv_wo[1]) = _b1(p0, dxl, ya0, yb0, yc0, lw, 0, comm=cm)
    half = sums(1, [None, None, None])
    part_proj[0], part_wo[0] = by_chip(gwp0, gwo0)

    gpack = _pack_small("pack_small", [sm0, sm1], gng1, g_final, loss_blk)
    slice_rows = gpack.shape[0] // 8

    cm = _Comm()
    cm.scatter(cm.read(part_proj[0]), cm.new(*recv_like(part_proj[0])))
    cm.scatter(cm.read(part_wo[0]), cm.new(*recv_like(part_wo[0])))
    for a in range(3):
        cm.sibling(cm.read(half[a]), cm.new(half[a].shape, BF16), part_rows[a], part_rows[a])
    cm.slices_out(cm.read(gpack), cm.new((7, slice_rows, 128), F32), slice_rows)
    (g_others,), _, (recv_proj[0], recv_wo[0], *other, slices_in) = _b2b(h0, dp0, "b2b_l0_others", chip, "others",
                                                                        comm=cm)
    me = (4 * lax.axis_index("x") + 2 * lax.axis_index("y") + lax.axis_index("c")).astype(jnp.int32).reshape(1)
    gsum = _sum8("sum_small", gpack, me, slices_in)
    early_rows = D_MODEL // 8
    cm = _Comm()
    cm.scatter(cm.read(g_others), cm.new((3, D_MODEL, SHARD_W), BF16), 0, early_rows)
    cm.slices_back(cm.through(gsum), slice_rows)
    (gwin[0],), (gred,), (recv_early,) = _b2b(h0, dp0, "b2b_l0_own", chip, "own", comm=cm)
    cm = _Comm()
    cm.scatter(cm.read(g_others), cm.through(recv_early), early_rows, D_MODEL - early_rows)
    (dxl, gng0), (recv_win[0],), _ = _b2a(dp0, w_all0, xs, dxl, norm_g3, 0, comm=cm)
    grad_x = dxl.reshape(1, s, D_MODEL)

    half = sums(0, half)
    gng0, other = _all_reduce_and_swap("norm_g0_and_swap0", gng0.reshape(8, 128), half, other, part_rows)
    loss = gred[_small_rows()[0]["loss"], 0]
    gred = lax.dynamic_update_slice(gred, gng0, (0, 0))

    outs = {}
    shard2d = dict(w_in=(L * D_MODEL, SHARD_W), w_o=(L * 256, D_MODEL))
    for a, name in ((0, "w_in"), (2, "w_o")):
        r2 = shard2d[name]
        res = _adamw(f"adamw_{name}", W[name].reshape(r2), M[name].reshape(r2), Vv[name].reshape(r2),
                     [half[a], other[a]])
        outs[name] = [o.reshape(W[name].shape) for o in res]
    res = _adamw("adamw_proj", pcat(W), pcat(M), pcat(Vv), [half[1], other[1]])
    for i, name in enumerate(("w_pa", "w_pb", "w_pc")):
        outs[name] = [o.reshape(L, 3, WIDTH, 256)[:, i] for o in res]

    as_rows = lambda d: {nm: (d[nm].reshape(1, D_MODEL) if nm == "final_g" else d[nm]) for nm in _SMALL}
    outs.update(_adamw_small("adamw_small", gred, chip, as_rows(W), as_rows(M), as_rows(Vv)))
    outs["final_g"] = [o.reshape(D_MODEL) for o in outs["final_g"]]

    order = ("norm_g", "w_in", "ln_g", "ln_b", "w_s", "b_s", "conv_w", "conv_b", "w_pool", "pool_scale",
             "w_pa", "w_pb", "w_pc", "w_o", "final_g")
    return (loss, grad_x, *[outs[nm][0] for nm in order], *[outs[nm][1] for nm in order],
            *[outs[nm][2] for nm in order], *[outs[nm][3] for nm in order])
```

```python
import functools

import jax
import jax.numpy as jnp
from jax import lax
from jax.experimental import pallas as pl
from jax.experimental.pallas import tpu as pltpu

F32 = jnp.float32
BF16 = jnp.bfloat16

D_MODEL = 1024
DEPTH = 2
CHUNK = 128
WIDTH = 512
POOL_WINDOWS = (2, 4, 8, 16)
IN_TOTAL = 7680
N_CHIPS = 4
SHARD_W = IN_TOTAL // N_CHIPS
RMS_EPS = 1e-6
LN_EPS = 1e-5
HALO = 16

U, V, ZA, XB, BG, CG, ZB, XC, ZC, GA, GB, GC = (0, 512, 1024, 1536, 2048, 2560, 3072, 3584, 4096, 4608, 5632, 6656)

ADAM_LR = 0.001
ADAM_B1 = 0.9
ADAM_B2 = 0.999
ADAM_EPS = 1e-08
ADAM_WD = 0.01
ADAM_STEP = 10

VMEM_LIMIT = 56 * 1024 * 1024
MESH = pl.DeviceIdType.MESH
ANY = pl.BlockSpec(memory_space=pl.ANY)
NT = (((1,), (1,)), ((), ()))
TN = (((0,), (0,)), ((), ()))


def _params(sem=None):
    kw = dict(vmem_limit_bytes=VMEM_LIMIT)
    if sem is not None:
        kw["dimension_semantics"] = sem
    return pltpu.CompilerParams(**kw)


def _dot(a, b):
    return jnp.dot(a, b, preferred_element_type=F32)


def _dotg(a, b, dims):
    return lax.dot_general(a, b, dims, preferred_element_type=F32)


def _sigmoid(x):
    return 1.0 / (1.0 + jnp.exp(-x))


_GELU_K = 0.7978845608028654


def _gelu(x):
    th = jnp.tanh(_GELU_K * (x + 0.044715 * (x * x * x)))
    return 0.5 * x * (1.0 + th), th


def _gelu_grad(x, th):
    return 0.5 * (1.0 + th) + 0.5 * x * (1.0 - th * th) * (_GELU_K * (1.0 + 3.0 * 0.044715 * (x * x)))


def _colsum8(x):
    t, c = x.shape
    return jnp.sum(x.reshape(t // 8, 8, c), axis=0)


def _branches_fwd(p_ref, hxb_ref, hcg_ref, hxc_ref, first, tstart, w, sg_scr):
    t = p_ref.shape[0]
    nch = t // CHUNK

    def seg(o, width=WIDTH):
        return p_ref[:, o:o + width].astype(F32)

    lo = lax.broadcasted_iota(jnp.int32, (CHUNK, CHUNK), 1) < 64
    r = {}
    pu = seg(U)
    u_act, th_u = _gelu(pu)
    pv = seg(V)
    vg, th_v = _gelu(pv)
    mu = jnp.mean(vg, axis=-1, keepdims=True)
    xc = vg - mu
    var = jnp.mean(xc * xc, axis=-1, keepdims=True)
    rs = lax.rsqrt(var + LN_EPS)
    vhat = xc * rs
    vn = vhat * w["ln_g"][...] + w["ln_b"][...]
    vnb = vn.astype(BF16)
    for n in range(nch):
        for j in range(4):
            vb = vnb[n * CHUNK:(n + 1) * CHUNK, j * 128:(j + 1) * 128]
            z = _dot(w["w2"][j], vb)
            sg_scr[n * CHUNK:(n + 1) * CHUNK, j * 128:(j + 1) * 128] = (
                jnp.where(lo, z[:CHUNK], z[CHUNK:]) + w["bst"][:, j * 128:(j + 1) * 128])
    sg = sg_scr[...]
    a_br = u_act * sg
    za = seg(ZA)
    sa = _sigmoid(za)
    r.update(pu=pu, th_u=th_u, pv=pv, th_v=th_v, rs=rs, vhat=vhat, vnb=vnb, u_act=u_act, sg=sg,
             a_br=a_br, za=za, sa=sa, a_in=a_br * (za * sa))

    xb = seg(XB)
    cg = seg(CG)
    yb0 = cg * xb
    hal = hcg_ref[...].astype(F32) * hxb_ref[...].astype(F32)
    hal = jnp.where(first, 0.0, hal)
    ext = jnp.concatenate([hal, yb0], axis=0)
    y1 = pltpu.roll(ext, 1, 0)[HALO:]
    y2 = pltpu.roll(ext, 2, 0)[HALO:]
    cw = w["cw"]
    conv = cw[0:1, :] * y2 + cw[1:2, :] * y1 + cw[2:3, :] * yb0 + w["cb"][...]
    bg = seg(BG)
    b_br = bg * conv
    zb = seg(ZB)
    sb = _sigmoid(zb)
    r.update(xb=xb, cg=cg, yb0=yb0, y1=y1, y2=y2, conv=conv, bg=bg, b_br=b_br, zb=zb, sb=sb,
             b_in=b_br * (zb * sb))

    xcv = seg(XC)
    hxc = jnp.where(first, 0.0, hxc_ref[...].astype(F32))
    extc = jnp.concatenate([hxc, xcv], axis=0)
    tpos = tstart + lax.broadcasted_iota(jnp.int32, (t, 1), 0) + 1
    pooled, inv, q = [], [], []
    for gi, win in enumerate(POOL_WINDOWS):
        s = extc[:, gi * 128:(gi + 1) * 128]
        sh = 1
        while sh < win:
            s = s + pltpu.roll(s, sh, 0)
            sh *= 2
        inv_g = jnp.where(tpos >= win, 1.0 / win, 1.0 / jnp.minimum(tpos, win).astype(F32))
        pg = s[HALO:] * inv_g - xcv[:, gi * 128:(gi + 1) * 128]
        pooled.append(pg)
        inv.append(inv_g)
        q.append(_dot(pg.astype(BF16), w["wpool"][gi]))
    qv = jnp.concatenate(q, axis=1)
    c_br = qv * w["ps"][...]
    zc = seg(ZC)
    sc = _sigmoid(zc)
    r.update(pooled=pooled, inv=inv, q=qv, c_br=c_br, zc=zc, sc=sc, c_in=c_br * (zc * sc))
    return r


def _halo_specs(t, rev_n=None):
    def imap(col):
        def f(i):
            ti = i if rev_n is None else rev_n - 1 - i
            return (jnp.maximum(ti * (t // HALO) - 1, 0), col)
        return f
    return [pl.BlockSpec((HALO, WIDTH), imap(XB // WIDTH)),
            pl.BlockSpec((HALO, WIDTH), imap(CG // WIDTH)),
            pl.BlockSpec((HALO, WIDTH), imap(XC // WIDTH))]


def _const_spec(shape):
    nd = len(shape)
    return pl.BlockSpec(shape, lambda *_: (0,) * nd)


def _place():
    return lax.axis_index("x"), lax.axis_index("y"), lax.axis_index("c")


def _chip_peer(x, y, jm):
    px = (1 - x) if (jm & 2) else x
    py = (1 - y) if (jm & 1) else y
    return px, py


def _chip_part(buf, k, r0=0, nr=None):
    if len(buf.shape) == 3:
        return buf.at[k] if nr is None else buf.at[k, pl.ds(r0, nr)]
    cols = buf.shape[1] // N_CHIPS
    rows = pl.ds(0, buf.shape[0]) if nr is None else pl.ds(r0, nr)
    return buf.at[rows, pl.ds(pl.multiple_of(k * cols, 128), cols)]


def _row_range(ref, r0, nr):
    return ref if nr is None else ref.at[pl.ds(r0, nr)]


class _Comm:
    def __init__(self):
        self.ins, self.thru, self.fresh, self.moves, self.n = [], [], [], [], 0

    def read(self, arr):
        self.ins.append(arr)
        return ("ins", len(self.ins) - 1)

    def through(self, arr):
        self.thru.append(arr)
        return ("thru", len(self.thru) - 1)

    def new(self, shape, dtype):
        self.fresh.append(jax.ShapeDtypeStruct(tuple(shape), dtype))
        return ("fresh", len(self.fresh) - 1)

    def _add(self, n, *move):
        self.moves.append(move)
        self.n += n

    def gather(self, buf, r0=0, nr=None):
        self._add(3, "gather", buf, r0, nr)

    def scatter(self, src, dst, r0=0, nr=None):
        self._add(3, "scatter", src, dst, r0, nr)

    def sibling(self, src, dst, r0=0, nr=None):
        self._add(1, "sibling", src, dst, r0, nr)

    def gather_half(self, buf):
        self._add(3, "gather_half", buf)

    def forward_half(self, buf):
        self._add(3, "forward_half", buf)

    def slices_out(self, src, dst, rows):
        self._add(7, "slices_out", src, dst, rows)

    def slices_back(self, buf, rows):
        self._add(7, "slices_back", buf, rows)

    def copies(self, bufs, x, y, c):
        ref = lambda h: bufs[h[0]][h[1]]
        k_me = 2 * x + y
        me = 4 * x + 2 * y + c
        cps = []
        for move in self.moves:
            kind = move[0]
            if kind in ("gather", "scatter"):
                for jj, jm in enumerate((1, 2, 3)):
                    px, py = _chip_peer(x, y, jm)
                    k_peer = 2 * px + py
                    if kind == "gather":
                        _, buf, r0, nr = move
                        mine = _chip_part(ref(buf), k_me, r0, nr)
                        cps.append((mine, mine, _chip_part(ref(buf), k_peer, r0, nr), (px, py, c)))
                    else:
                        _, src, dst, r0, nr = move
                        slot = ref(dst).at[jj] if nr is None else ref(dst).at[jj, pl.ds(r0, nr)]
                        cps.append((_chip_part(ref(src), k_peer, r0, nr), slot, slot, (px, py, c)))
            elif kind in ("gather_half", "forward_half"):
                buf = ref(move[1])
                hr = buf.shape[-2] // 2
                mine_r0, other_r0 = pl.multiple_of(c * hr, 16), pl.multiple_of((1 - c) * hr, 16)
                for jm in (1, 2, 3):
                    px, py = _chip_peer(x, y, jm)
                    k_peer = 2 * px + py
                    if kind == "gather_half":
                        part = _chip_part(buf, k_me, mine_r0, hr)
                        cps.append((part, part, _chip_part(buf, k_peer, mine_r0, hr), (px, py, c)))
                    else:
                        part = _chip_part(buf, k_peer, mine_r0, hr)
                        cps.append((part, part, _chip_part(buf, k_peer, other_r0, hr), (x, y, 1 - c)))
            elif kind == "sibling":
                _, src, dst, r0, nr = move
                land = _row_range(ref(dst), r0, nr)
                cps.append((_row_range(ref(src), r0, nr), land, land, (x, y, 1 - c)))
            else:
                for j in range(1, 8):
                    px = (1 - x) if (j & 4) else x
                    py = (1 - y) if (j & 2) else y
                    pc = (1 - c) if (j & 1) else c
                    peer = 4 * px + 2 * py + pc
                    if kind == "slices_out":
                        _, src, dst, rows = move
                        slot = ref(dst).at[j - 1]
                        cps.append((ref(src).at[pl.ds(pl.multiple_of(peer * rows, 8), rows)], slot, slot,
                                    (px, py, pc)))
                    else:
                        _, buf, rows = move
                        mine = ref(buf).at[pl.ds(pl.multiple_of(me * rows, 8), rows)]
                        cps.append((mine, mine, ref(buf).at[pl.ds(pl.multiple_of(peer * rows, 8), rows)],
                                    (px, py, pc)))
        assert len(cps) == self.n
        return cps


def _pcall(body, *, name, grid, in_specs, out_specs, out_shape, scratch, sem, args, comm=None, prefetch=(),
           aliases=None):
    n_pre = len(prefetch)
    n_in, n_out, n_scr = len(in_specs), len(out_specs), len(scratch)
    io_alias = {n_pre + i: o for i, o in (aliases or {}).items()}
    if comm is None or comm.n == 0:
        gs = pltpu.PrefetchScalarGridSpec(num_scalar_prefetch=n_pre, grid=grid, in_specs=list(in_specs),
                                          out_specs=list(out_specs), scratch_shapes=list(scratch))
        outs = pl.pallas_call(body, name=name, grid_spec=gs, out_shape=list(out_shape), input_output_aliases=io_alias,
                              compiler_params=_params(sem))(*prefetch, *args)
        return list(outs), [], []
    n_ci, n_ct, n_cf, n_cp = len(comm.ins), len(comm.thru), len(comm.fresh), comm.n

    def wrapped(*refs):
        pos = 0
        def take(n):
            nonlocal pos
            got = refs[pos:pos + n]
            pos += n
            return got
        pre = take(n_pre)
        a, ci, _ = take(n_in), take(n_ci), take(n_ct)
        o, ct, cf = take(n_out), take(n_ct), take(n_cf)
        scr = take(n_scr)
        send_sems, recv_sems = take(2)
        first = functools.reduce(jnp.logical_and, [pl.program_id(d) == 0 for d in range(len(grid))])
        last = functools.reduce(jnp.logical_and, [pl.program_id(d) == grid[d] - 1 for d in range(len(grid))])
        x, y, c = _place()
        cps = comm.copies(dict(ins=ci, thru=ct, fresh=cf), x, y, c)

        def copy(i, src, dst, dev):
            return pltpu.make_async_remote_copy(src_ref=src, dst_ref=dst, send_sem=send_sems.at[i],
                                                recv_sem=recv_sems.at[i], device_id=dev, device_id_type=MESH)

        @pl.when(first)
        def _():
            for i, (src, dst, _, dev) in enumerate(cps):
                copy(i, src, dst, dev).start()

        body(*pre, *a, *o, *scr)

        @pl.when(last)
        def _():
            for i, (src, _, land, dev) in enumerate(cps):
                copy(i, src, land, dev).wait_recv()
            for i, (src, dst, _, dev) in enumerate(cps):
                copy(i, src, dst, dev).wait_send()

    thru_shapes = [jax.ShapeDtypeStruct(t.shape, t.dtype) for t in comm.thru]
    gs = pltpu.PrefetchScalarGridSpec(
        num_scalar_prefetch=n_pre, grid=grid,
        in_specs=list(in_specs) + [ANY] * (n_ci + n_ct),
        out_specs=list(out_specs) + [ANY] * (n_ct + n_cf),
        scratch_shapes=list(scratch) + [pltpu.SemaphoreType.DMA((n_cp,)), pltpu.SemaphoreType.DMA((n_cp,))])
    outs = pl.pallas_call(
        wrapped, name=name, grid_spec=gs,
        out_shape=list(out_shape) + thru_shapes + comm.fresh,
        input_output_aliases={**io_alias, **{n_pre + n_in + n_ci + t: n_out + t for t in range(n_ct)}},
        compiler_params=_params(sem),
    )(*prefetch, *args, *comm.ins, *comm.thru)
    outs = list(outs)
    return outs[:n_out], outs[n_out:n_out + n_ct], outs[n_out + n_ct:]


def _comm_only(name, comm):
    def body():
        pass
    _, thru, _ = _pcall(body, name=name, grid=(1,), in_specs=[], out_specs=[], out_shape=[], scratch=[],
                        sem=("arbitrary",), args=[], comm=comm)
    return thru


def _f1(x, norm_g3, w_all, l, comm=None, tm=512):
    s = x.shape[0]

    def body(x_ref, g_ref, w_hbm, p_ref, h_ref, w_vmem):
        @pl.when(pl.program_id(0) == 0)
        def _():
            pltpu.sync_copy(w_hbm, w_vmem)
        xv = x_ref[...]
        r = lax.rsqrt(jnp.mean(xv * xv, axis=-1, keepdims=True) + RMS_EPS)
        hb = ((xv * r) * g_ref[...]).astype(BF16)
        h_ref[...] = hb
        p_ref[...] = _dot(hb, w_vmem[...]).astype(BF16)

    return _pcall(
        body, name=f"f1_l{l}", grid=(s // tm,),
        in_specs=[pl.BlockSpec((tm, D_MODEL), lambda i: (i, 0)),
                  pl.BlockSpec((None, 1, D_MODEL), lambda i: (l, 0, 0)), ANY],
        out_specs=[pl.BlockSpec((tm, IN_TOTAL), lambda i: (i, 0)),
                   pl.BlockSpec((tm, D_MODEL), lambda i: (i, 0))],
        out_shape=[jax.ShapeDtypeStruct((s, IN_TOTAL), BF16), jax.ShapeDtypeStruct((s, D_MODEL), BF16)],
        scratch=[pltpu.VMEM((D_MODEL, IN_TOTAL), BF16)], sem=("arbitrary",), args=[x, norm_g3, w_all], comm=comm)


def _f1_own(x, norm_g3, w_in, chip, comm, tm=256):
    s = x.shape[0]

    def body(idx_ref, x_ref, g_ref, w_hbm, p_ref, h_ref, w_f32, w_vmem):
        @pl.when(pl.program_id(0) == 0)
        def _():
            pltpu.sync_copy(w_hbm.at[0], w_f32)
            w_vmem[...] = w_f32[...].astype(BF16)
        xv = x_ref[...]
        r = lax.rsqrt(jnp.mean(xv * xv, axis=-1, keepdims=True) + RMS_EPS)
        hb = ((xv * r) * g_ref[...]).astype(BF16)
        h_ref[...] = hb
        p_ref[...] = _dot(hb, w_vmem[...]).astype(BF16)

    return _pcall(
        body, name="f1_l0_own", grid=(s // tm,),
        in_specs=[pl.BlockSpec((tm, D_MODEL), lambda i, idx: (i, 0)),
                  pl.BlockSpec((None, 1, D_MODEL), lambda i, idx: (0, 0, 0)), ANY],
        out_specs=[pl.BlockSpec((tm, SHARD_W), lambda i, idx: (i, idx[0])),
                   pl.BlockSpec((tm, D_MODEL), lambda i, idx: (i, 0))],
        out_shape=[jax.ShapeDtypeStruct((s, IN_TOTAL), BF16), jax.ShapeDtypeStruct((s, D_MODEL), BF16)],
        scratch=[pltpu.VMEM((D_MODEL, SHARD_W), F32), pltpu.VMEM((D_MODEL, SHARD_W), BF16)],
        sem=("arbitrary",), args=[x, norm_g3, w_in], comm=comm, prefetch=[chip])


def _f1_others(h, p, w_all, chip, comm, tm=1024):
    s = h.shape[0]
    tm = min(tm, s)
    cols_of = lambda j, idx: (idx[0] + 1 + j) % N_CHIPS

    def body(idx_ref, h_ref, w_hbm, _, p_ref, w_vmem, sems):
        j = pl.program_id(1)

        @pl.when(jnp.logical_and(pl.program_id(0) == 0, j == 0))
        def _():
            loads = [pltpu.make_async_copy(
                w_hbm.at[:, pl.ds(pl.multiple_of(((idx_ref[0] + 1 + q) % N_CHIPS) * SHARD_W, 128), SHARD_W)],
                w_vmem.at[q], sems.at[q]) for q in range(N_CHIPS - 1)]
            for cp in loads:
                cp.start()
            for cp in loads:
                cp.wait()

        p_ref[...] = _dot(h_ref[...], w_vmem[j]).astype(BF16)

    return _pcall(
        body, name="f1_l0_others", grid=(s // tm, N_CHIPS - 1),
        in_specs=[pl.BlockSpec((tm, D_MODEL), lambda i, j, idx: (i, 0)), ANY, ANY],
        out_specs=[pl.BlockSpec((tm, SHARD_W), lambda i, j, idx: (i, cols_of(j, idx)))],
        out_shape=[jax.ShapeDtypeStruct((s, IN_TOTAL), BF16)],
        scratch=[pltpu.VMEM((N_CHIPS - 1, D_MODEL, SHARD_W), BF16), pltpu.SemaphoreType.DMA((N_CHIPS - 1,))],
        sem=("arbitrary", "arbitrary"), args=[h, w_all, p], comm=comm, prefetch=[chip], aliases={2: 0})


def _f2(x, p, lw, l, comm=None, loss=None, t=256):
    s = x.shape[0]
    n = s // t

    def body(p_ref, hxb_ref, hcg_ref, hxc_ref, x_ref, lng, lnb, w2, bst, cw, cb, wpool, ps,
             proj, wo, *rest):
        if loss is None:
            ya_ref, yb_ref, yc_ref, m_ref, xo_ref, sg_scr = rest
        else:
            t_ref, fg_ref, ya_ref, yb_ref, yc_ref, m_ref, xo_ref, loss_ref, gg_ref, sg_scr, lacc, gacc = rest
        i = pl.program_id(0)
        w = dict(ln_g=lng, ln_b=lnb, w2=w2, bst=bst, cw=cw, cb=cb, wpool=wpool, ps=ps)
        r = _branches_fwd(p_ref, hxb_ref, hcg_ref, hxc_ref, i == 0, i * t, w, sg_scr)

        def project(act, b):
            ab = act.astype(BF16)
            return jnp.concatenate([_dot(ab, proj[k, b * WIDTH:(b + 1) * WIDTH, :]) for k in range(N_CHIPS)], axis=1)

        ya = project(r["a_in"], 0)
        yb = project(r["b_in"], 1)
        yc = project(r["c_in"], 2)
        ya_ref[...] = ya.astype(BF16)
        yb_ref[...] = yb.astype(BF16)
        yc_ref[...] = yc.astype(BF16)
        m = (_sigmoid(p_ref[:, GA:GA + D_MODEL].astype(F32)) * ya
             + _sigmoid(p_ref[:, GB:GB + D_MODEL].astype(F32)) * yb
             + _sigmoid(p_ref[:, GC:GC + D_MODEL].astype(F32)) * yc)
        mb = m.astype(BF16)
        m_ref[...] = mb
        xo = x_ref[...] + _dot(mb, wo[...])
        if loss is None:
            xo_ref[...] = xo
        else:
            @pl.when(i == 0)
            def _():
                lacc[...] = jnp.zeros(lacc.shape, F32)
                gacc[...] = jnp.zeros(gacc.shape, F32)

            g = fg_ref[...]
            rr = lax.rsqrt(jnp.mean(xo * xo, axis=-1, keepdims=True) + RMS_EPS)
            err = (xo * rr) * g - t_ref[...]
            lacc[...] += _colsum8(err * err)
            dx, gx = _rms_bwd(xo, g, err * (1.0 / D_MODEL))
            xo_ref[...] = dx
            gacc[...] += _colsum8(gx)

            @pl.when(i == n - 1)
            def _():
                tot = jnp.sum(jnp.sum(lacc[...], axis=0, keepdims=True), axis=1, keepdims=True)
                loss_ref[...] = jnp.broadcast_to(tot * (0.5 / D_MODEL), loss_ref.shape)
                gg_ref[...] = jnp.sum(gacc[...], axis=0, keepdims=True)

    tile = lambda c: pl.BlockSpec((t, c), lambda i: (i, 0))
    lsel = lambda *blk: pl.BlockSpec((None,) + blk, lambda i: (l,) + (0,) * len(blk))
    act = jax.ShapeDtypeStruct((s, D_MODEL), BF16)
    f32s = lambda *shape: jax.ShapeDtypeStruct(shape, F32)
    with_loss = loss is not None
    return _pcall(
        body, name=f"f2_l{l}", grid=(n,),
        in_specs=[tile(IN_TOTAL)] + _halo_specs(t) + [
            tile(D_MODEL), lsel(1, WIDTH), lsel(1, WIDTH), lsel(4, 256, 128), lsel(CHUNK, WIDTH),
            lsel(8, WIDTH), lsel(1, WIDTH), lsel(4, 128, 128), lsel(1, WIDTH),
            _const_spec((N_CHIPS, 3 * WIDTH, 256)), _const_spec((D_MODEL, D_MODEL))]
        + ([tile(D_MODEL), _const_spec((1, D_MODEL))] if with_loss else []),
        out_specs=[tile(D_MODEL)] * 5 + ([_const_spec((8, 128)), _const_spec((1, D_MODEL))] if with_loss else []),
        out_shape=[act, act, act, act, f32s(s, D_MODEL)] + ([f32s(8, 128), f32s(1, D_MODEL)] if with_loss else []),
        scratch=[pltpu.VMEM((t, WIDTH), F32)]
        + ([pltpu.VMEM((8, D_MODEL), F32), pltpu.VMEM((8, D_MODEL), F32)] if with_loss else []),
        sem=("arbitrary",),
        args=[p, p, p, p, x, lw["ln_g"], lw["ln_b"], lw["w2"], lw["bst"], lw["cw"], lw["cb"], lw["wpool"], lw["ps"],
              lw["proj"][l], lw["wo"][l]] + (list(loss) if with_loss else []), comm=comm)


def _b1(p, dout, ya, yb, yc, lw, l, comm=None, t=256):
    s = p.shape[0]
    n = s // t
    nch = t // CHUNK

    def body(p_ref, hxb_ref, hcg_ref, hxc_ref, do_ref, ya_ref, yb_ref, yc_ref,
             lng, lnb, w2, wt2, bst, cw, cb, wpool, ps, proj_h, wo_h, sel_ref,
             dp_ref, gwp_h, gws_ref, gbs_ref, glng_ref, glnb_ref, gcw_ref, gcb_ref,
             gwpool_ref, gps_ref,
             wpa, wpb, wpc, wo, gwpa, gwpb, gwpc, gbs_acc, vec_acc, sg_scr, dvn_scr, car_dc, car_e, psem):
        i = pl.program_id(0)
        ti = n - 1 - i

        def by_chip_copies(vmem_bufs, hbm, to_hbm):
            cps = []
            for b, buf in enumerate(vmem_bufs):
                for k in range(N_CHIPS):
                    v = buf.at[:, pl.ds(k * 256, 256)]
                    h = hbm.at[k, pl.ds(b * WIDTH, WIDTH)]
                    cps.append(pltpu.make_async_copy(v, h, psem.at[b * N_CHIPS + k]) if to_hbm
                               else pltpu.make_async_copy(h, v, psem.at[b * N_CHIPS + k]))
            return cps

        @pl.when(i == 0)
        def _():
            loads = by_chip_copies((wpa, wpb, wpc), proj_h, False)
            for cp in loads:
                cp.start()
            pltpu.sync_copy(wo_h, wo)
            for cp in loads:
                cp.wait()
            for acc in (gwpa, gwpb, gwpc, gbs_acc, vec_acc, car_dc, car_e):
                acc[...] = jnp.zeros(acc.shape, acc.dtype)
            gws_ref[...] = jnp.zeros(gws_ref.shape, F32)
            gwpool_ref[...] = jnp.zeros(gwpool_ref.shape, F32)

        w = dict(ln_g=lng, ln_b=lnb, w2=w2, bst=bst, cw=cw, cb=cb, wpool=wpool, ps=ps)
        r = _branches_fwd(p_ref, hxb_ref, hcg_ref, hxc_ref, ti == 0, ti * t, w, sg_scr)

        def seg(o, width=WIDTH):
            return p_ref[:, o:o + width].astype(F32)

        def put(o, val):
            dp_ref[:, o:o + val.shape[1]] = val.astype(BF16)

        dob = do_ref[...].astype(BF16)
        dm = _dotg(dob, wo[...], NT)

        def merge_bwd(goff, y_ref, xin, wp, gwp):
            sx = _sigmoid(seg(goff, D_MODEL))
            dmy = dm * sx
            put(goff, dmy * y_ref[...].astype(F32) * (1.0 - sx))
            dyb = dmy.astype(BF16)
            gwp[...] += _dotg(xin.astype(BF16), dyb, TN)
            return _dotg(dyb, wp[...], NT)

        d_ain = merge_bwd(GA, ya_ref, r["a_in"], wpa, gwpa)
        d_bin = merge_bwd(GB, yb_ref, r["b_in"], wpb, gwpb)
        d_cin = merge_bwd(GC, yc_ref, r["c_in"], wpc, gwpc)

        def dsilu(z, sz):
            return sz * (1.0 + z * (1.0 - sz))

        za, sa = r["za"], r["sa"]
        d_abr = d_ain * (za * sa)
        put(ZA, d_ain * r["a_br"] * dsilu(za, sa))
        put(U, d_abr * r["sg"] * _gelu_grad(r["pu"], r["th_u"]))
        d_sg = d_abr * r["u_act"]
        dsgb = d_sg.astype(BF16)
        lo = lax.broadcasted_iota(jnp.int32, (CHUNK, CHUNK), 1) < 64
        zero = jnp.zeros((CHUNK, CHUNK), BF16)
        for c in range(nch):
            rows = slice(c * CHUNK, (c + 1) * CHUNK)
            gbs_acc[...] += d_sg[rows]
            for j in range(4):
                cols = slice(j * 128, (j + 1) * 128)
                dj = dsgb[rows, cols]
                zt = _dot(wt2[j], dj)
                dvn_scr[rows, cols] = jnp.where(lo, zt[:CHUNK], zt[CHUNK:])
                stacked = jnp.concatenate([jnp.where(lo, dj, zero), jnp.where(lo, zero, dj)], axis=0)
                gws_ref[j] += _dotg(stacked, r["vnb"][rows, cols], NT)
        d_vn = dvn_scr[...]
        vhat = r["vhat"]
        vec_acc[0] += _colsum8(d_vn * vhat)
        vec_acc[1] += _colsum8(d_vn)
        d_vhat = d_vn * lng[...]
        d_vg = r["rs"] * (d_vhat - jnp.mean(d_vhat, axis=-1, keepdims=True)
                          - vhat * jnp.mean(d_vhat * vhat, axis=-1, keepdims=True))
        put(V, d_vg * _gelu_grad(r["pv"], r["th_v"]))

        zb, sb = r["zb"], r["sb"]
        d_bbr = d_bin * (zb * sb)
        put(ZB, d_bin * r["b_br"] * dsilu(zb, sb))
        put(BG, d_bbr * r["conv"])
        dc = d_bbr * r["bg"]
        vec_acc[2] += _colsum8(dc)
        vec_acc[3] += _colsum8(dc * r["y2"])
        vec_acc[4] += _colsum8(dc * r["y1"])
        vec_acc[5] += _colsum8(dc * r["yb0"])
        ext = jnp.concatenate([dc, car_dc[...]], axis=0)
        ne = t + HALO
        d1 = pltpu.roll(ext, ne - 1, 0)[:t]
        d2 = pltpu.roll(ext, ne - 2, 0)[:t]
        d_yb0 = cw[2:3, :] * dc + cw[1:2, :] * d1 + cw[0:1, :] * d2
        put(CG, d_yb0 * r["xb"])
        put(XB, d_yb0 * r["cg"])
        car_dc[...] = dc[:HALO]

        zc, sc = r["zc"], r["sc"]
        d_cbr = d_cin * (zc * sc)
        put(ZC, d_cin * r["c_br"] * dsilu(zc, sc))
        vec_acc[6] += _colsum8(d_cbr * r["q"])
        d_q = d_cbr * ps[...]
        for gi, win in enumerate(POOL_WINDOWS):
            cols = slice(gi * 128, (gi + 1) * 128)
            dqb = d_q[:, cols].astype(BF16)
            d_pool = _dotg(dqb, wpool[gi], NT)
            gwpool_ref[gi] += _dotg(r["pooled"][gi].astype(BF16), dqb, TN)
            e = d_pool * r["inv"][gi]
            sx = jnp.concatenate([e, car_e[:, cols]], axis=0)
            sh = 1
            while sh < win:
                sx = sx + pltpu.roll(sx, ne - sh, 0)
                sh *= 2
            put(XC + gi * 128, sx[:t] - d_pool)
            car_e[:, cols] = e[:HALO]

        @pl.when(i == n - 1)
        def _():
            for acc, stage in ((gwpa, wpa), (gwpb, wpb), (gwpc, wpc)):
                stage[...] = acc[...].astype(BF16)
            stores = by_chip_copies((wpa, wpb, wpc), gwp_h, True)
            for cp in stores:
                cp.start()
            for cp in stores:
                cp.wait()
            gbs_ref[...] = jnp.dot(gbs_acc[...], sel_ref[...], preferred_element_type=F32,
                                   precision=lax.Precision.HIGHEST)
            red = lambda k: jnp.sum(vec_acc[k], axis=0, keepdims=True)
            glng_ref[...] = red(0)
            glnb_ref[...] = red(1)
            gcb_ref[...] = red(2)
            gcw_ref[...] = jnp.zeros(gcw_ref.shape, F32)
            for k in range(3):
                gcw_ref[k:k + 1, :] = red(3 + k)
            gps_ref[...] = red(6)
            tt = lax.broadcasted_iota(jnp.int32, (2 * CHUNK, CHUNK), 0) % CHUNK
            ss = lax.broadcasted_iota(jnp.int32, (2 * CHUNK, CHUNK), 1)
            for j in range(4):
                gws_ref[j] = jnp.where(tt >= ss, gws_ref[j], 0.0)

    rtile = lambda c: pl.BlockSpec((t, c), lambda i: (n - 1 - i, 0))
    lsel = lambda *blk: pl.BlockSpec((None,) + blk, lambda i: (l,) + (0,) * len(blk))
    f32s = lambda *shape: jax.ShapeDtypeStruct(shape, F32)
    return _pcall(
        body, name=f"b1_l{l}", grid=(n,),
        in_specs=[rtile(IN_TOTAL)] + _halo_specs(t, rev_n=n) + [rtile(D_MODEL)] * 4 + [
            lsel(1, WIDTH), lsel(1, WIDTH), lsel(4, 256, 128), lsel(4, 256, 128), lsel(CHUNK, WIDTH),
            lsel(8, WIDTH), lsel(1, WIDTH), lsel(4, 128, 128), lsel(1, WIDTH),
            ANY, ANY, _const_spec((WIDTH, 128))],
        out_specs=[rtile(IN_TOTAL), ANY,
                   _const_spec((4, 256, 128)), _const_spec((CHUNK, 128)), _const_spec((1, WIDTH)),
                   _const_spec((1, WIDTH)), _const_spec((8, WIDTH)), _const_spec((1, WIDTH)),
                   _const_spec((4, 128, 128)), _const_spec((1, WIDTH))],
        out_shape=[jax.ShapeDtypeStruct((s, IN_TOTAL), BF16), jax.ShapeDtypeStruct((N_CHIPS, 3 * WIDTH, 256), BF16),
                   f32s(4, 256, 128), f32s(CHUNK, 128),
                   f32s(1, WIDTH), f32s(1, WIDTH), f32s(8, WIDTH), f32s(1, WIDTH), f32s(4, 128, 128),
                   f32s(1, WIDTH)],
        scratch=[pltpu.VMEM((WIDTH, D_MODEL), BF16), pltpu.VMEM((WIDTH, D_MODEL), BF16),
                 pltpu.VMEM((WIDTH, D_MODEL), BF16), pltpu.VMEM((D_MODEL, D_MODEL), BF16),
                 pltpu.VMEM((WIDTH, D_MODEL), F32),
                 pltpu.VMEM((WIDTH, D_MODEL), F32), pltpu.VMEM((WIDTH, D_MODEL), F32),
                 pltpu.VMEM((CHUNK, WIDTH), F32), pltpu.VMEM((8, 8, WIDTH), F32),
                 pltpu.VMEM((t, WIDTH), F32), pltpu.VMEM((t, WIDTH), F32),
                 pltpu.VMEM((HALO, WIDTH), F32), pltpu.VMEM((HALO, WIDTH), F32),
                 pltpu.SemaphoreType.DMA((3 * N_CHIPS,))],
        sem=("arbitrary",),
        args=[p, p, p, p, dout, ya, yb, yc, lw["ln_g"], lw["ln_b"], lw["w2"], lw["wt2"], lw["bst"], lw["cw"],
              lw["cb"], lw["wpool"], lw["ps"], lw["proj"][l], lw["wo"][l], lw["sel"]], comm=comm)


def _rms_bwd(xv, g, dh):
    r = lax.rsqrt(jnp.mean(xv * xv, axis=-1, keepdims=True) + RMS_EPS)
    xhat = xv * r
    dxh = dh * g
    dx = r * (dxh - xhat * jnp.mean(dxh * xhat, axis=-1, keepdims=True))
    return dx, dh * xhat


def _b2a(dp, w_all, x, dout, norm_g3, l, comm=None, tm=512):
    s = x.shape[0]
    nm = s // tm

    def body(dp_ref, w_hbm, x_ref, do_ref, g_ref, dx_ref, gg_ref, w_vmem, gacc):
        i = pl.program_id(0)

        @pl.when(i == 0)
        def _():
            pltpu.sync_copy(w_hbm, w_vmem)
            gacc[...] = jnp.zeros(gacc.shape, F32)

        dh = _dotg(dp_ref[...], w_vmem[...], NT)
        dx, gx = _rms_bwd(x_ref[...], g_ref[...], dh)
        dx_ref[...] = do_ref[...] + dx
        gacc[...] += _colsum8(gx)

        @pl.when(i == nm - 1)
        def _():
            gg_ref[...] = jnp.sum(gacc[...], axis=0, keepdims=True)

    return _pcall(
        body, name=f"b2a_l{l}", grid=(nm,),
        in_specs=[pl.BlockSpec((tm, IN_TOTAL), lambda i: (i, 0)), ANY,
                  pl.BlockSpec((tm, D_MODEL), lambda i: (i, 0)),
                  pl.BlockSpec((tm, D_MODEL), lambda i: (i, 0)),
                  pl.BlockSpec((None, 1, D_MODEL), lambda i: (l, 0, 0))],
        out_specs=[pl.BlockSpec((tm, D_MODEL), lambda i: (i, 0)),
                   pl.BlockSpec((1, D_MODEL), lambda i: (0, 0))],
        out_shape=[jax.ShapeDtypeStruct((s, D_MODEL), F32), jax.ShapeDtypeStruct((1, D_MODEL), F32)],
        scratch=[pltpu.VMEM((D_MODEL, IN_TOTAL), BF16), pltpu.VMEM((8, D_MODEL), F32)],
        sem=("arbitrary",), args=[dp, w_all, x, dout, norm_g3], comm=comm)


def _b2b(h, dp, name, chip, part="all", comm=None, tk=2048):
    s = h.shape[0]
    tk = min(tk, s)
    nk = s // tk
    first, count = dict(all=(0, 4), others=(1, 3), own=(0, 1))[part]
    cols_of = lambda k, idx: (idx[0] + first + k) % N_CHIPS

    def body(idx_ref, h_ref, dp_ref, g_ref, acc):
        kk = pl.program_id(1)

        @pl.when(kk == 0)
        def _():
            acc[...] = jnp.zeros(acc.shape, F32)

        acc[...] += _dotg(h_ref[...], dp_ref[...], TN)

        @pl.when(kk == nk - 1)
        def _():
            g_ref[...] = acc[...].astype(BF16)

    own = part == "own"
    return _pcall(
        body, name=name, grid=(count, nk),
        in_specs=[pl.BlockSpec((tk, D_MODEL), lambda k, kk, idx: (kk, 0)),
                  pl.BlockSpec((tk, SHARD_W), lambda k, kk, idx: (kk, cols_of(k, idx)))],
        out_specs=[pl.BlockSpec((D_MODEL, SHARD_W), lambda k, kk, idx: (0, 0 if own else cols_of(k, idx)))],
        out_shape=[jax.ShapeDtypeStruct((D_MODEL, SHARD_W if own else IN_TOTAL), BF16)],
        scratch=[pltpu.VMEM((D_MODEL, SHARD_W), F32)],
        sem=("arbitrary", "arbitrary"), args=[h, dp], comm=comm, prefetch=[chip])


def _gwo(m, dout, l, tk=2048):
    s = m.shape[0]
    tk = min(tk, s)
    nk = s // tk

    def body(m_ref, do_ref, g_ref, acc):
        kk = pl.program_id(0)

        @pl.when(kk == 0)
        def _():
            acc[...] = jnp.zeros(acc.shape, F32)

        acc[...] += _dotg(m_ref[...], do_ref[...].astype(BF16), TN)

        @pl.when(kk == nk - 1)
        def _():
            g_ref[...] = acc[...].astype(BF16)

    return pl.pallas_call(
        body, name=f"gwo_l{l}", grid=(nk,),
        in_specs=[pl.BlockSpec((tk, D_MODEL), lambda kk: (kk, 0)), pl.BlockSpec((tk, D_MODEL), lambda kk: (kk, 0))],
        out_specs=_const_spec((D_MODEL, D_MODEL)),
        out_shape=jax.ShapeDtypeStruct((D_MODEL, D_MODEL), BF16),
        scratch_shapes=[pltpu.VMEM((D_MODEL, D_MODEL), F32)],
        compiler_params=_params(("arbitrary",)),
    )(m, dout)


def _row_block(rows, cols, n_arrays):
    budget = VMEM_LIMIT // 3 // (2 * 4 * n_arrays * cols)
    rb = rows
    while rb > budget and rb % 16 == 0:
        rb //= 2
    return rb


def _cast_slot(name, a, l, chip):
    _, rows, cols = a.shape
    rb = _row_block(rows, cols, 2)

    def body(idx_ref, a_ref, o_ref):
        o_ref[...] = a_ref[...].astype(BF16)

    gs = pltpu.PrefetchScalarGridSpec(
        num_scalar_prefetch=1, grid=(rows // rb,),
        in_specs=[pl.BlockSpec((None, rb, cols), lambda i, idx: (l, i, 0))],
        out_specs=pl.BlockSpec((None, rb, cols), lambda i, idx: (idx[0], i, 0)))
    return pl.pallas_call(body, name=name, grid_spec=gs, out_shape=jax.ShapeDtypeStruct((N_CHIPS, rows, cols), BF16),
                          compiler_params=_params(("arbitrary",)))(chip, a)


def _cast_cols(name, a, l, chip):
    _, rows, cols = a.shape
    rb = _row_block(rows, cols, 2)

    def body(idx_ref, a_ref, o_ref):
        o_ref[...] = a_ref[...].astype(BF16)

    gs = pltpu.PrefetchScalarGridSpec(
        num_scalar_prefetch=1, grid=(rows // rb,),
        in_specs=[pl.BlockSpec((None, rb, cols), lambda i, idx: (l, i, 0))],
        out_specs=pl.BlockSpec((rb, cols), lambda i, idx: (i, idx[0])))
    return pl.pallas_call(body, name=name, grid_spec=gs, out_shape=jax.ShapeDtypeStruct((rows, N_CHIPS * cols), BF16),
                          compiler_params=_params(("arbitrary",)))(chip, a)


def _sum4(name, own, chip, recv, l, n_layers, prev=None):
    _, rows, cols = recv.shape
    rb = _row_block(rows, cols, 5)
    nb = rows // rb
    if len(own.shape) == 3:
        own_spec = pl.BlockSpec((None, rb, cols), lambda i, idx: (idx[0], i, 0))
    elif own.shape[1] == cols:
        own_spec = pl.BlockSpec((rb, cols), lambda i, idx: (i, 0))
    else:
        own_spec = pl.BlockSpec((rb, cols), lambda i, idx: (i, idx[0]))

    def body(idx_ref, own_ref, r_ref, *rest):
        o_ref = rest[-1]
        o_ref[...] = (((own_ref[...].astype(F32) + r_ref[0].astype(F32)) + r_ref[1].astype(F32))
                      + r_ref[2].astype(F32)).astype(BF16)

    gs = pltpu.PrefetchScalarGridSpec(
        num_scalar_prefetch=1, grid=(nb,),
        in_specs=[own_spec,
                  pl.BlockSpec((3, rb, cols), lambda i, idx: (0, i, 0))] + ([ANY] if prev is not None else []),
        out_specs=pl.BlockSpec((rb, cols), lambda i, idx: (l * nb + i, 0)))
    args = (chip, own, recv) + ((prev,) if prev is not None else ())
    return pl.pallas_call(body, name=name, grid_spec=gs,
                          out_shape=jax.ShapeDtypeStruct((n_layers * rows, cols), BF16),
                          input_output_aliases=({3: 0} if prev is not None else {}),
                          compiler_params=_params(("arbitrary",)))(*args)


def _sum8(name, pack, me, recv):
    rows = recv.shape[1]

    def body(idx_ref, own_ref, r_ref, o_ref):
        acc = own_ref[...]
        for j in range(7):
            acc = acc + r_ref[j]
        o_ref[...] = acc

    gs = pltpu.PrefetchScalarGridSpec(
        num_scalar_prefetch=1, grid=(1,),
        in_specs=[pl.BlockSpec((rows, 128), lambda i, idx: (idx[0], 0)),
                  pl.BlockSpec((7, rows, 128), lambda i, idx: (0, 0, 0))],
        out_specs=pl.BlockSpec((rows, 128), lambda i, idx: (idx[0], 0)))
    return pl.pallas_call(body, name=name, grid_spec=gs, out_shape=jax.ShapeDtypeStruct(pack.shape, F32),
                          compiler_params=_params(("arbitrary",)))(me, pack, recv)


_SMALL = ("norm_g", "ln_g", "ln_b", "w_s", "b_s", "conv_b", "w_pool", "pool_scale", "final_g", "conv_w")
_SMALL_SHAPES = dict(norm_g=(DEPTH, D_MODEL), ln_g=(DEPTH, WIDTH), ln_b=(DEPTH, WIDTH), w_s=(DEPTH, 8, CHUNK, CHUNK),
                     b_s=(DEPTH, 8, CHUNK), conv_b=(DEPTH, WIDTH), w_pool=(DEPTH, 4, 128, 128),
                     pool_scale=(DEPTH, WIDTH), final_g=(1, D_MODEL), conv_w=(DEPTH, 3, WIDTH))


def _small_rows():
    base, r = {}, 0
    for nm in _SMALL:
        base[nm] = r
        size = 1
        for d in _SMALL_SHAPES[nm]:
            size *= d
        r += size // 128
    base["loss"] = r
    return base, -(-(r + 8) // 64) * 64


def _pack_small(name, raw, gng1, g_final, loss_blk):
    base, rows = _small_rows()
    n_l = len(raw[0])

    def body(*refs):
        o = refs[-1]
        per_layer = [refs[l * n_l:(l + 1) * n_l] for l in range(DEPTH)]
        gng1_ref, gfin_ref, loss_ref = refs[DEPTH * n_l:DEPTH * n_l + 3]
        o[...] = jnp.zeros(o.shape, F32)
        o[base["loss"]:base["loss"] + 8, :] = loss_ref[...]

        def put_row_vector(r0, ref, width):
            for j in range(width // 128):
                o[r0 + j:r0 + j + 1, :] = ref[0:1, j * 128:(j + 1) * 128]

        put_row_vector(base["norm_g"] + D_MODEL // 128, gng1_ref, D_MODEL)
        put_row_vector(base["final_g"], gfin_ref, D_MODEL)
        for l in range(DEPTH):
            gws, gbs, glng, glnb, gcw, gcb, gwpool, gps = per_layer[l]
            put_row_vector(base["ln_g"] + l * 4, glng, WIDTH)
            put_row_vector(base["ln_b"] + l * 4, glnb, WIDTH)
            put_row_vector(base["conv_b"] + l * 4, gcb, WIDTH)
            put_row_vector(base["pool_scale"] + l * 4, gps, WIDTH)
            o[base["w_s"] + l * 1024:base["w_s"] + (l + 1) * 1024, :] = gws[...].reshape(1024, 128)
            o[base["w_pool"] + l * 512:base["w_pool"] + (l + 1) * 512, :] = gwpool[...].reshape(512, 128)
            o[base["b_s"] + l * 8:base["b_s"] + (l + 1) * 8, :] = gbs[...].T[0:8, :]
            for k in range(3):
                for ch in range(N_CHIPS):
                    r = base["conv_w"] + (l * 3 + k) * N_CHIPS + ch
                    o[r:r + 1, :] = gcw[k:k + 1, ch * 128:(ch + 1) * 128]

    args = [a for l in range(DEPTH) for a in raw[l]] + [gng1, g_final, loss_blk]
    vm = pl.BlockSpec(memory_space=pltpu.VMEM)
    return pl.pallas_call(body, name=name, in_specs=[vm] * len(args), out_specs=vm,
                          out_shape=jax.ShapeDtypeStruct((rows, 128), F32), compiler_params=_params())(*args)


def _adamw_small(name, gred, chip, w, m, v):
    base, _ = _small_rows()

    def body(chip_ref, g_ref, *refs):
        n = len(_SMALL)
        w_r, m_r, v_r = refs[:n], refs[n:2 * n], refs[2 * n:3 * n]
        out = refs[3 * n:]

        def update(i, idx, g):
            d, mn, vn = _adamw_math(w_r[i][idx], g, m_r[i][idx], v_r[i][idx])
            for o, val in zip(out[4 * i:4 * i + 4], (g, d, mn, vn)):
                o[idx] = val

        for i, nm in enumerate(_SMALL):
            shape = _SMALL_SHAPES[nm]
            if nm == "conv_w":
                for l in range(DEPTH):
                    for k in range(3):
                        row = base[nm] + (l * 3 + k) * N_CHIPS + chip_ref[0]
                        update(i, (l, slice(k, k + 1), slice(None)), g_ref[pl.ds(row, 1), :])
            elif len(shape) == 2:
                per = shape[1] // 128
                for l in range(shape[0]):
                    for j in range(per):
                        r = base[nm] + l * per + j
                        update(i, (slice(l, l + 1), slice(j * 128, (j + 1) * 128)), g_ref[r:r + 1, :])
            else:
                rows = 1
                for dim in shape[:-1]:
                    rows *= dim
                update(i, (Ellipsis,), g_ref[base[nm]:base[nm] + rows, :].reshape(shape))

    arrs = [d[nm] for d in (w, m, v) for nm in _SMALL]
    vm = pl.BlockSpec(memory_space=pltpu.VMEM)
    gs = pltpu.PrefetchScalarGridSpec(num_scalar_prefetch=1, grid=(1,), in_specs=[vm] * (1 + len(arrs)),
                                      out_specs=[vm] * (4 * len(_SMALL)))
    outs = pl.pallas_call(
        body, name=name, grid_spec=gs,
        out_shape=[jax.ShapeDtypeStruct(w[nm].shape, F32) for nm in _SMALL for _ in range(4)],
        compiler_params=_params(("arbitrary",)))(chip, gred, *arrs)
    return {nm: list(outs[4 * i:4 * i + 4]) for i, nm in enumerate(_SMALL)}


def _adamw_math(w, g, m, v):
    m = ADAM_B1 * m + (1.0 - ADAM_B1) * g
    v = ADAM_B2 * v + (1.0 - ADAM_B2) * (g * g)
    m_hat = m / (1.0 - ADAM_B1 ** ADAM_STEP)
    v_hat = v / (1.0 - ADAM_B2 ** ADAM_STEP)
    delta = -ADAM_LR * (m_hat / (jnp.sqrt(v_hat) + ADAM_EPS) + ADAM_WD * w)
    return delta, m, v


def _adamw(name, w, m, v, g_parts):
    rows, cols = w.shape
    np_ = len(g_parts)
    rb = _row_block(rows, cols, 7 + np_)

    def body(*refs):
        w_ref, m_ref, v_ref = refs[:3]
        g_refs = refs[3:3 + np_]
        go_ref, d_ref, mo_ref, vo_ref = refs[3 + np_:]
        g = g_refs[0][...].astype(F32)
        for gr in g_refs[1:]:
            g = g + gr[...].astype(F32)
        d, mn, vn = _adamw_math(w_ref[...], g, m_ref[...], v_ref[...])
        go_ref[...] = g
        d_ref[...] = d
        mo_ref[...] = mn
        vo_ref[...] = vn

    spec = pl.BlockSpec((rb, cols), lambda i: (i, 0))
    shp = jax.ShapeDtypeStruct((rows, cols), F32)
    return pl.pallas_call(body, name=name, grid=(rows // rb,), in_specs=[spec] * (3 + np_),
                          out_specs=[spec] * 4, out_shape=[shp] * 4,
                          compiler_params=_params(("arbitrary",)))(w, m, v, *g_parts)


def _all_reduce_and_swap(name, a, srcs, dsts, rows):
    n = len(srcs)

    def body(a_ref, *refs):
        src, o_ref, dst = refs[:n], refs[2 * n], refs[2 * n + 1:3 * n + 1]
        buf, send_sems, recv_sems, swap_send, swap_recv = refs[3 * n + 1:]
        x, y, c = _place()
        swaps = [pltpu.make_async_remote_copy(
            src_ref=src[i].at[pl.ds(0, rows[i])], dst_ref=dst[i].at[pl.ds(0, rows[i])], send_sem=swap_send.at[i],
            recv_sem=swap_recv.at[i], device_id=(x, y, 1 - c), device_id_type=MESH) for i in range(n)]
        for cp in swaps:
            cp.start()
        o_ref[...] = a_ref[...]
        for rnd, peer in enumerate(((x, y, 1 - c), (x, 1 - y, c), (1 - x, y, c))):
            cp = pltpu.make_async_remote_copy(
                src_ref=o_ref, dst_ref=buf.at[rnd], send_sem=send_sems.at[rnd], recv_sem=recv_sems.at[rnd],
                device_id=peer, device_id_type=MESH)
            cp.start()
            cp.wait_recv()
            cp.wait_send()
            o_ref[...] = o_ref[...] + buf[rnd]
        for cp in swaps:
            cp.wait_recv()
        for cp in swaps:
            cp.wait_send()

    vm = pl.BlockSpec(memory_space=pltpu.VMEM)
    outs = pl.pallas_call(
        body, name=name, in_specs=[vm] + [ANY] * (2 * n), out_specs=[vm] + [ANY] * n,
        out_shape=[jax.ShapeDtypeStruct(a.shape, F32)] + [jax.ShapeDtypeStruct(d.shape, d.dtype) for d in dsts],
        input_output_aliases={1 + n + i: 1 + i for i in range(n)},
        scratch_shapes=[pltpu.VMEM((3,) + a.shape, F32), pltpu.SemaphoreType.DMA((3,)),
                        pltpu.SemaphoreType.DMA((3,)), pltpu.SemaphoreType.DMA((n,)), pltpu.SemaphoreType.DMA((n,))],
        compiler_params=_params(),
    )(a, *srcs, *dsts)
    return outs[0], list(outs[1:])


def kernel(x, norm_g, w_in, ln_g, ln_b, w_s, b_s, conv_w, conv_b, w_pool, pool_scale, w_pa, w_pb, w_pc, w_o, final_g, loss_target, m_norm_g, m_w_in, m_ln_g, m_ln_b, m_w_s, m_b_s, m_conv_w, m_conv_b, m_w_pool, m_pool_scale, m_w_pa, m_w_pb, m_w_pc, m_w_o, m_final_g, v_norm_g, v_w_in, v_ln_g, v_ln_b, v_w_s, v_b_s, v_conv_w, v_conv_b, v_w_pool, v_pool_scale, v_w_pa, v_w_pb, v_w_pc, v_w_o, v_final_g):
    W = dict(norm_g=norm_g, w_in=w_in, ln_g=ln_g, ln_b=ln_b, w_s=w_s, b_s=b_s, conv_w=conv_w, conv_b=conv_b,
             w_pool=w_pool, pool_scale=pool_scale, w_pa=w_pa, w_pb=w_pb, w_pc=w_pc, w_o=w_o, final_g=final_g)
    M = dict(norm_g=m_norm_g, w_in=m_w_in, ln_g=m_ln_g, ln_b=m_ln_b, w_s=m_w_s, b_s=m_b_s, conv_w=m_conv_w,
             conv_b=m_conv_b, w_pool=m_w_pool, pool_scale=m_pool_scale, w_pa=m_w_pa, w_pb=m_w_pb, w_pc=m_w_pc,
             w_o=m_w_o, final_g=m_final_g)
    Vv = dict(norm_g=v_norm_g, w_in=v_w_in, ln_g=v_ln_g, ln_b=v_ln_b, w_s=v_w_s, b_s=v_b_s, conv_w=v_conv_w,
              conv_b=v_conv_b, w_pool=v_w_pool, pool_scale=v_pool_scale, w_pa=v_w_pa, w_pb=v_w_pb, w_pc=v_w_pc,
              w_o=v_w_o, final_g=v_final_g)
    L = DEPTH
    s = x.shape[1]
    xs = x.reshape(s, D_MODEL)
    tgt = loss_target.reshape(s, D_MODEL)
    k_me = (2 * lax.axis_index("x") + lax.axis_index("y")).astype(jnp.int32)

    chip = k_me.reshape(1)
    assert L == 2
    half_rows = D_MODEL // 2

    land_win = [_cast_cols(f"cast_w_in{l}", w_in, l, chip) for l in range(L)]
    pcat = lambda d: jnp.concatenate([d[b][l] for l in range(L) for b in ("w_pa", "w_pb", "w_pc")], axis=0)
    pcat_w = pcat(W).reshape(L, 3 * WIDTH, 256)
    land_proj = [_cast_slot(f"cast_proj{l}", pcat_w, l, chip) for l in range(L)]
    land_wo = [_cast_slot(f"cast_w_o{l}", w_o, l, chip) for l in range(L)]
    cw_sh = jnp.pad(conv_w, ((0, 0), (0, 5), (0, 0))).reshape(1, L * 8, 128)
    land_cw = lax.dynamic_update_slice(jnp.zeros((N_CHIPS, L * 8, 128), F32), cw_sh, (k_me, 0, 0))

    causal = jnp.tril(jnp.ones((CHUNK, CHUNK), dtype=bool))
    w_m = jnp.where(causal, w_s, 0.0)
    lw = dict(
        ln_g=ln_g.reshape(L, 1, WIDTH), ln_b=ln_b.reshape(L, 1, WIDTH),
        w2=w_m.reshape(L, 4, 256, CHUNK).astype(BF16),
        wt2=jnp.swapaxes(w_m, -1, -2).reshape(L, 4, 256, CHUNK).astype(BF16),
        bst=jnp.repeat(jnp.swapaxes(b_s, -1, -2), 64, axis=-1),
        cb=conv_b.reshape(L, 1, WIDTH), wpool=w_pool.astype(BF16), ps=pool_scale.reshape(L, 1, WIDTH),
        sel=(jnp.arange(WIDTH)[:, None] // 64 == jnp.arange(128)[None, :]).astype(F32))
    norm_g3 = norm_g.reshape(L, 1, D_MODEL)

    cm = _Comm()
    cm.gather_half(cm.through(land_win[0]))
    (p0, h0), (w_half0,), _ = _f1_own(xs, norm_g3, w_in, chip, cm)
    cm = _Comm()
    cm.forward_half(cm.through(w_half0))
    (w_all0,) = _comm_only("forward_w_in0", cm)
    first_rows = D_MODEL // 4
    cm = _Comm()
    for buf in (land_proj[0], land_wo[0], land_cw):
        cm.gather(cm.through(buf))
    cm.gather(cm.through(land_win[1]), 0, first_rows)
    (p0,), (g_proj0, g_wo0, g_cw, w_part1), _ = _f1_others(h0, p0, w_all0, chip, cm)
    lw["proj"] = [g_proj0, None]
    lw["wo"] = [g_wo0.reshape(D_MODEL, D_MODEL), None]
    lw["cw"] = g_cw.reshape(N_CHIPS, L, 8, 128).transpose(1, 2, 0, 3).reshape(L, 8, WIDTH)
    cm = _Comm()
    cm.gather(cm.through(w_part1), first_rows, D_MODEL - first_rows)
    (ya0, yb0, yc0, mm0, x1), (w_all1,), _ = _f2(xs, p0, lw, 0, comm=cm)
    cm = _Comm()
    for buf in (land_proj[1], land_wo[1]):
        cm.gather(cm.through(buf))
    (p1, h1), (g_proj1, g_wo1), _ = _f1(x1, norm_g3, w_all1, 1, comm=cm)
    lw["proj"][1] = g_proj1
    lw["wo"][1] = g_wo1.reshape(D_MODEL, D_MODEL)
    (ya1, yb1, yc1, mm1, dxl, loss_blk, g_final), _, _ = _f2(x1, p1, lw, 1,
                                                              loss=(tgt, final_g.reshape(1, D_MODEL)))

    by_chip = lambda gwp, gwo: (gwp, gwo.reshape(N_CHIPS, 256, D_MODEL))
    recv_like = lambda a: ((3,) + a.shape[1:], a.dtype)
    part_rows = (D_MODEL, 3 * WIDTH, 256)
    gwin, recv_win, part_proj, part_wo, recv_proj, recv_wo = ([None] * L for _ in range(6))

    def sums(l, half):
        return [_sum4(f"sum_w_in{l}", gwin[l], chip, recv_win[l], l, L, half[0]),
                _sum4(f"sum_proj{l}", part_proj[l], chip, recv_proj[l], l, L, half[1]),
                _sum4(f"sum_w_o{l}", part_wo[l], chip, recv_wo[l], l, L, half[2])]

    gwo1 = _gwo(mm1, dxl, 1)
    (dp1, gwp1, *sm1), _, _ = _b1(p1, dxl, ya1, yb1, yc1, lw, 1)
    (gwin[1],), _, _ = _b2b(h1, dp1, "b2b_l1", chip)
    cm = _Comm()
    cm.scatter(cm.read(gwin[1]), cm.new((3, D_MODEL, SHARD_W), BF16), 0, half_rows)
    (dxl, gng1), _, (recv_half,) = _b2a(dp1, w_all1, x1, dxl, norm_g3, 1, comm=cm)
    part_proj[1], part_wo[1] = by_chip(gwp1, gwo1)
    gwo0 = _gwo(mm0, dxl, 0)
    cm = _Comm()
    cm.scatter(cm.read(gwin[1]), cm.through(recv_half), half_rows, half_rows)
    cm.scatter(cm.read(part_proj[1]), cm.new(*recv_like(part_proj[1])))
    cm.scatter(cm.read(part_wo[1]), cm.new(*recv_like(part_wo[1])))
    (dp0, gwp0, *sm0), (recv_win[1],), (recv_proj[1], recv_wo[1]) = _b1(p0, dxl, ya0, yb0, yc0, lw, 0, comm=cm)
    half = sums(1, [None, None, None])
    part_proj[0], part_wo[0] = by_chip(gwp0, gwo0)

    gpack = _pack_small("pack_small", [sm0, sm1], gng1, g_final, loss_blk)
    slice_rows = gpack.shape[0] // 8

    cm = _Comm()
    cm.scatter(cm.read(part_proj[0]), cm.new(*recv_like(part_proj[0])))
    cm.scatter(cm.read(part_wo[0]), cm.new(*recv_like(part_wo[0])))
    for a in range(3):
        cm.sibling(cm.read(half[a]), cm.new(half[a].shape, BF16), part_rows[a], part_rows[a])
    cm.slices_out(cm.read(gpack), cm.new((7, slice_rows, 128), F32), slice_rows)
    (g_others,), _, (recv_proj[0], recv_wo[0], *other, slices_in) = _b2b(h0, dp0, "b2b_l0_others", chip, "others",
                                                                        comm=cm)
    me = (4 * lax.axis_index("x") + 2 * lax.axis_index("y") + lax.axis_index("c")).astype(jnp.int32).reshape(1)
    gsum = _sum8("sum_small", gpack, me, slices_in)
    early_rows = D_MODEL // 8
    cm = _Comm()
    cm.scatter(cm.read(g_others), cm.new((3, D_MODEL, SHARD_W), BF16), 0, early_rows)
    cm.slices_back(cm.through(gsum), slice_rows)
    (gwin[0],), (gred,), (recv_early,) = _b2b(h0, dp0, "b2b_l0_own", chip, "own", comm=cm)
    cm = _Comm()
    cm.scatter(cm.read(g_others), cm.through(recv_early), early_rows, D_MODEL - early_rows)
    (dxl, gng0), (recv_win[0],), _ = _b2a(dp0, w_all0, xs, dxl, norm_g3, 0, comm=cm)
    grad_x = dxl.reshape(1, s, D_MODEL)

    half = sums(0, half)
    gng0, other = _all_reduce_and_swap("norm_g0_and_swap0", gng0.reshape(8, 128), half, other, part_rows)
    loss = gred[_small_rows()[0]["loss"], 0]
    gred = lax.dynamic_update_slice(gred, gng0, (0, 0))

    outs = {}
    shard2d = dict(w_in=(L * D_MODEL, SHARD_W), w_o=(L * 256, D_MODEL))
    for a, name in ((0, "w_in"), (2, "w_o")):
        r2 = shard2d[name]
        res = _adamw(f"adamw_{name}", W[name].reshape(r2), M[name].reshape(r2), Vv[name].reshape(r2),
                     [half[a], other[a]])
        outs[name] = [o.reshape(W[name].shape) for o in res]
    res = _adamw("adamw_proj", pcat(W), pcat(M), pcat(Vv), [half[1], other[1]])
    for i, name in enumerate(("w_pa", "w_pb", "w_pc")):
        outs[name] = [o.reshape(L, 3, WIDTH, 256)[:, i] for o in res]

    as_rows = lambda d: {nm: (d[nm].reshape(1, D_MODEL) if nm == "final_g" else d[nm]) for nm in _SMALL}
    outs.update(_adamw_small("adamw_small", gred, chip, as_rows(W), as_rows(M), as_rows(Vv)))
    outs["final_g"] = [o.reshape(D_MODEL) for o in outs["final_g"]]

    order = ("norm_g", "w_in", "ln_g", "ln_b", "w_s", "b_s", "conv_w", "conv_b", "w_pool", "pool_scale",
             "w_pa", "w_pb", "w_pc", "w_o", "final_g")
    return (loss, grad_x, *[outs[nm][0] for nm in order], *[outs[nm][1] for nm in order],
            *[outs[nm][2] for nm in order], *[outs[nm][3] for nm in order])
```

```python
import functools

import jax
import jax.numpy as jnp
from jax import lax
from jax.experimental import pallas as pl
from jax.experimental.pallas import tpu as pltpu

F32 = jnp.float32
BF16 = jnp.bfloat16

D_MODEL = 1024
DEPTH = 2
CHUNK = 128
WIDTH = 512
POOL_WINDOWS = (2, 4, 8, 16)
IN_TOTAL = 7680
N_CHIPS = 4
SHARD_W = IN_TOTAL // N_CHIPS
RMS_EPS = 1e-6
LN_EPS = 1e-5
HALO = 16

U, V, ZA, XB, BG, CG, ZB, XC, ZC, GA, GB, GC = (0, 512, 1024, 1536, 2048, 2560, 3072, 3584, 4096, 4608, 5632, 6656)

ADAM_LR = 0.001
ADAM_B1 = 0.9
ADAM_B2 = 0.999
ADAM_EPS = 1e-08
ADAM_WD = 0.01
ADAM_STEP = 10

VMEM_LIMIT = 56 * 1024 * 1024
MESH = pl.DeviceIdType.MESH
ANY = pl.BlockSpec(memory_space=pl.ANY)
NT = (((1,), (1,)), ((), ()))
TN = (((0,), (0,)), ((), ()))


def _params(sem=None):
    kw = dict(vmem_limit_bytes=VMEM_LIMIT)
    if sem is not None:
        kw["dimension_semantics"] = sem
    return pltpu.CompilerParams(**kw)


def _dot(a, b):
    return jnp.dot(a, b, preferred_element_type=F32)


def _dotg(a, b, dims):
    return lax.dot_general(a, b, dims, preferred_element_type=F32)


def _sigmoid(x):
    return 1.0 / (1.0 + jnp.exp(-x))


_GELU_K = 0.7978845608028654


def _gelu(x):
    th = jnp.tanh(_GELU_K * (x + 0.044715 * (x * x * x)))
    return 0.5 * x * (1.0 + th), th


def _gelu_grad(x, th):
    return 0.5 * (1.0 + th) + 0.5 * x * (1.0 - th * th) * (_GELU_K * (1.0 + 3.0 * 0.044715 * (x * x)))


def _colsum8(x):
    t, c = x.shape
    return jnp.sum(x.reshape(t // 8, 8, c), axis=0)


def _branches_fwd(p_ref, hxb_ref, hcg_ref, hxc_ref, first, tstart, w, sg_scr):
    t = p_ref.shape[0]
    nch = t // CHUNK

    def seg(o, width=WIDTH):
        return p_ref[:, o:o + width].astype(F32)

    lo = lax.broadcasted_iota(jnp.int32, (CHUNK, CHUNK), 1) < 64
    r = {}
    pu = seg(U)
    u_act, th_u = _gelu(pu)
    pv = seg(V)
    vg, th_v = _gelu(pv)
    mu = jnp.mean(vg, axis=-1, keepdims=True)
    xc = vg - mu
    var = jnp.mean(xc * xc, axis=-1, keepdims=True)
    rs = lax.rsqrt(var + LN_EPS)
    vhat = xc * rs
    vn = vhat * w["ln_g"][...] + w["ln_b"][...]
    vnb = vn.astype(BF16)
    for n in range(nch):
        for j in range(4):
            vb = vnb[n * CHUNK:(n + 1) * CHUNK, j * 128:(j + 1) * 128]
            z = _dot(w["w2"][j], vb)
            sg_scr[n * CHUNK:(n + 1) * CHUNK, j * 128:(j + 1) * 128] = (
                jnp.where(lo, z[:CHUNK], z[CHUNK:]) + w["bst"][:, j * 128:(j + 1) * 128])
    sg = sg_scr[...]
    a_br = u_act * sg
    za = seg(ZA)
    sa = _sigmoid(za)
    r.update(pu=pu, th_u=th_u, pv=pv, th_v=th_v, rs=rs, vhat=vhat, vnb=vnb, u_act=u_act, sg=sg,
             a_br=a_br, za=za, sa=sa, a_in=a_br * (za * sa))

    xb = seg(XB)
    cg = seg(CG)
    yb0 = cg * xb
    hal = hcg_ref[...].astype(F32) * hxb_ref[...].astype(F32)
    hal = jnp.where(first, 0.0, hal)
    ext = jnp.concatenate([hal, yb0], axis=0)
    y1 = pltpu.roll(ext, 1, 0)[HALO:]
    y2 = pltpu.roll(ext, 2, 0)[HALO:]
    cw = w["cw"]
    conv = cw[0:1, :] * y2 + cw[1:2, :] * y1 + cw[2:3, :] * yb0 + w["cb"][...]
    bg = seg(BG)
    b_br = bg * conv
    zb = seg(ZB)
    sb = _sigmoid(zb)
    r.update(xb=xb, cg=cg, yb0=yb0, y1=y1, y2=y2, conv=conv, bg=bg, b_br=b_br, zb=zb, sb=sb,
             b_in=b_br * (zb * sb))

    xcv = seg(XC)
    hxc = jnp.where(first, 0.0, hxc_ref[...].astype(F32))
    extc = jnp.concatenate([hxc, xcv], axis=0)
    tpos = tstart + lax.broadcasted_iota(jnp.int32, (t, 1), 0) + 1
    pooled, inv, q = [], [], []
    for gi, win in enumerate(POOL_WINDOWS):
        s = extc[:, gi * 128:(gi + 1) * 128]
        sh = 1
        while sh < win:
            s = s + pltpu.roll(s, sh, 0)
            sh *= 2
        inv_g = jnp.where(tpos >= win, 1.0 / win, 1.0 / jnp.minimum(tpos, win).astype(F32))
        pg = s[HALO:] * inv_g - xcv[:, gi * 128:(gi + 1) * 128]
        pooled.append(pg)
        inv.append(inv_g)
        q.append(_dot(pg.astype(BF16), w["wpool"][gi]))
    qv = jnp.concatenate(q, axis=1)
    c_br = qv * w["ps"][...]
    zc = seg(ZC)
    sc = _sigmoid(zc)
    r.update(pooled=pooled, inv=inv, q=qv, c_br=c_br, zc=zc, sc=sc, c_in=c_br * (zc * sc))
    return r


def _halo_specs(t, rev_n=None):
    def imap(col):
        def f(i):
            ti = i if rev_n is None else rev_n - 1 - i
            return (jnp.maximum(ti * (t // HALO) - 1, 0), col)
        return f
    return [pl.BlockSpec((HALO, WIDTH), imap(XB // WIDTH)),
            pl.BlockSpec((HALO, WIDTH), imap(CG // WIDTH)),
            pl.BlockSpec((HALO, WIDTH), imap(XC // WIDTH))]


def _const_spec(shape):
    nd = len(shape)
    return pl.BlockSpec(shape, lambda *_: (0,) * nd)


def _place():
    return lax.axis_index("x"), lax.axis_index("y"), lax.axis_index("c")


def _chip_peer(x, y, jm):
    px = (1 - x) if (jm & 2) else x
    py = (1 - y) if (jm & 1) else y
    return px, py


def _chip_part(buf, k, r0=0, nr=None):
    if len(buf.shape) == 3:
        return buf.at[k] if nr is None else buf.at[k, pl.ds(r0, nr)]
    cols = buf.shape[1] // N_CHIPS
    rows = pl.ds(0, buf.shape[0]) if nr is None else pl.ds(r0, nr)
    return buf.at[rows, pl.ds(pl.multiple_of(k * cols, 128), cols)]


def _row_range(ref, r0, nr):
    return ref if nr is None else ref.at[pl.ds(r0, nr)]


class _Comm:
    def __init__(self):
        self.ins, self.thru, self.fresh, self.moves, self.n = [], [], [], [], 0

    def read(self, arr):
        self.ins.append(arr)
        return ("ins", len(self.ins) - 1)

    def through(self, arr):
        self.thru.append(arr)
        return ("thru", len(self.thru) - 1)

    def new(self, shape, dtype):
        self.fresh.append(jax.ShapeDtypeStruct(tuple(shape), dtype))
        return ("fresh", len(self.fresh) - 1)

    def _add(self, n, *move):
        self.moves.append(move)
        self.n += n

    def gather(self, buf, r0=0, nr=None):
        self._add(3, "gather", buf, r0, nr)

    def scatter(self, src, dst, r0=0, nr=None):
        self._add(3, "scatter", src, dst, r0, nr)

    def sibling(self, src, dst, r0=0, nr=None):
        self._add(1, "sibling", src, dst, r0, nr)

    def gather_half(self, buf):
        self._add(3, "gather_half", buf)

    def forward_half(self, buf):
        self._add(3, "forward_half", buf)

    def slices_out(self, src, dst, rows):
        self._add(7, "slices_out", src, dst, rows)

    def slices_back(self, buf, rows):
        self._add(7, "slices_back", buf, rows)

    def copies(self, bufs, x, y, c):
        ref = lambda h: bufs[h[0]][h[1]]
        k_me = 2 * x + y
        me = 4 * x + 2 * y + c
        cps = []
        for move in self.moves:
            kind = move[0]
            if kind in ("gather", "scatter"):
                for jj, jm in enumerate((1, 2, 3)):
                    px, py = _chip_peer(x, y, jm)
                    k_peer = 2 * px + py
                    if kind == "gather":
                        _, buf, r0, nr = move
                        mine = _chip_part(ref(buf), k_me, r0, nr)
                        cps.append((mine, mine, _chip_part(ref(buf), k_peer, r0, nr), (px, py, c)))
                    else:
                        _, src, dst, r0, nr = move
                        slot = ref(dst).at[jj] if nr is None else ref(dst).at[jj, pl.ds(r0, nr)]
                        cps.append((_chip_part(ref(src), k_peer, r0, nr), slot, slot, (px, py, c)))
            elif kind in ("gather_half", "forward_half"):
                buf = ref(move[1])
                hr = buf.shape[-2] // 2
                mine_r0, other_r0 = pl.multiple_of(c * hr, 16), pl.multiple_of((1 - c) * hr, 16)
                for jm in (1, 2, 3):
                    px, py = _chip_peer(x, y, jm)
                    k_peer = 2 * px + py
                    if kind == "gather_half":
                        part = _chip_part(buf, k_me, mine_r0, hr)
                        cps.append((part, part, _chip_part(buf, k_peer, mine_r0, hr), (px, py, c)))
                    else:
                        part = _chip_part(buf, k_peer, mine_r0, hr)
                        cps.append((part, part, _chip_part(buf, k_peer, other_r0, hr), (x, y, 1 - c)))
            elif kind == "sibling":
                _, src, dst, r0, nr = move
                land = _row_range(ref(dst), r0, nr)
                cps.append((_row_range(ref(src), r0, nr), land, land, (x, y, 1 - c)))
            else:
                for j in range(1, 8):
                    px = (1 - x) if (j & 4) else x
                    py = (1 - y) if (j & 2) else y
                    pc = (1 - c) if (j & 1) else c
                    peer = 4 * px + 2 * py + pc
                    if kind == "slices_out":
                        _, src, dst, rows = move
                        slot = ref(dst).at[j - 1]
                        cps.append((ref(src).at[pl.ds(pl.multiple_of(peer * rows, 8), rows)], slot, slot,
                                    (px, py, pc)))
                    else:
                        _, buf, rows = move
                        mine = ref(buf).at[pl.ds(pl.multiple_of(me * rows, 8), rows)]
                        cps.append((mine, mine, ref(buf).at[pl.ds(pl.multiple_of(peer * rows, 8), rows)],
                                    (px, py, pc)))
        assert len(cps) == self.n
        return cps


def _pcall(body, *, name, grid, in_specs, out_specs, out_shape, scratch, sem, args, comm=None, prefetch=(),
           aliases=None):
    n_pre = len(prefetch)
    n_in, n_out, n_scr = len(in_specs), len(out_specs), len(scratch)
    io_alias = {n_pre + i: o for i, o in (aliases or {}).items()}
    if comm is None or comm.n == 0:
        gs = pltpu.PrefetchScalarGridSpec(num_scalar_prefetch=n_pre, grid=grid, in_specs=list(in_specs),
                                          out_specs=list(out_specs), scratch_shapes=list(scratch))
        outs = pl.pallas_call(body, name=name, grid_spec=gs, out_shape=list(out_shape), input_output_aliases=io_alias,
                              compiler_params=_params(sem))(*prefetch, *args)
        return list(outs), [], []
    n_ci, n_ct, n_cf, n_cp = len(comm.ins), len(comm.thru), len(comm.fresh), comm.n

    def wrapped(*refs):
        pos = 0
        def take(n):
            nonlocal pos
            got = refs[pos:pos + n]
            pos += n
            return got
        pre = take(n_pre)
        a, ci, _ = take(n_in), take(n_ci), take(n_ct)
        o, ct, cf = take(n_out), take(n_ct), take(n_cf)
        scr = take(n_scr)
        send_sems, recv_sems = take(2)
        first = functools.reduce(jnp.logical_and, [pl.program_id(d) == 0 for d in range(len(grid))])
        last = functools.reduce(jnp.logical_and, [pl.program_id(d) == grid[d] - 1 for d in range(len(grid))])
        x, y, c = _place()
        cps = comm.copies(dict(ins=ci, thru=ct, fresh=cf), x, y, c)

        def copy(i, src, dst, dev):
            return pltpu.make_async_remote_copy(src_ref=src, dst_ref=dst, send_sem=send_sems.at[i],
                                                recv_sem=recv_sems.at[i], device_id=dev, device_id_type=MESH)

        @pl.when(first)
        def _():
            for i, (src, dst, _, dev) in enumerate(cps):
                copy(i, src, dst, dev).start()

        body(*pre, *a, *o, *scr)

        @pl.when(last)
        def _():
            for i, (src, _, land, dev) in enumerate(cps):
                copy(i, src, land, dev).wait_recv()
            for i, (src, dst, _, dev) in enumerate(cps):
                copy(i, src, dst, dev).wait_send()

    thru_shapes = [jax.ShapeDtypeStruct(t.shape, t.dtype) for t in comm.thru]
    gs = pltpu.PrefetchScalarGridSpec(
        num_scalar_prefetch=n_pre, grid=grid,
        in_specs=list(in_specs) + [ANY] * (n_ci + n_ct),
        out_specs=list(out_specs) + [ANY] * (n_ct + n_cf),
        scratch_shapes=list(scratch) + [pltpu.SemaphoreType.DMA((n_cp,)), pltpu.SemaphoreType.DMA((n_cp,))])
    outs = pl.pallas_call(
        wrapped, name=name, grid_spec=gs,
        out_shape=list(out_shape) + thru_shapes + comm.fresh,
        input_output_aliases={**io_alias, **{n_pre + n_in + n_ci + t: n_out + t for t in range(n_ct)}},
        compiler_params=_params(sem),
    )(*prefetch, *args, *comm.ins, *comm.thru)
    outs = list(outs)
    return outs[:n_out], outs[n_out:n_out + n_ct], outs[n_out + n_ct:]


def _comm_only(name, comm):
    def body():
        pass
    _, thru, _ = _pcall(body, name=name, grid=(1,), in_specs=[], out_specs=[], out_shape=[], scratch=[],
                        sem=("arbitrary",), args=[], comm=comm)
    return thru


def _f1(x, norm_g3, w_all, l, comm=None, tm=512):
    s = x.shape[0]

    def body(x_ref, g_ref, w_hbm, p_ref, h_ref, w_vmem):
        @pl.when(pl.program_id(0) == 0)
        def _():
            pltpu.sync_copy(w_hbm, w_vmem)
        xv = x_ref[...]
        r = lax.rsqrt(jnp.mean(xv * xv, axis=-1, keepdims=True) + RMS_EPS)
        hb = ((xv * r) * g_ref[...]).astype(BF16)
        h_ref[...] = hb
        p_ref[...] = _dot(hb, w_vmem[...]).astype(BF16)

    return _pcall(
        body, name=f"f1_l{l}", grid=(s // tm,),
        in_specs=[pl.BlockSpec((tm, D_MODEL), lambda i: (i, 0)),
                  pl.BlockSpec((None, 1, D_MODEL), lambda i: (l, 0, 0)), ANY],
        out_specs=[pl.BlockSpec((tm, IN_TOTAL), lambda i: (i, 0)),
                   pl.BlockSpec((tm, D_MODEL), lambda i: (i, 0))],
        out_shape=[jax.ShapeDtypeStruct((s, IN_TOTAL), BF16), jax.ShapeDtypeStruct((s, D_MODEL), BF16)],
        scratch=[pltpu.VMEM((D_MODEL, IN_TOTAL), BF16)], sem=("arbitrary",), args=[x, norm_g3, w_all], comm=comm)


def _f1_own(x, norm_g3, w_in, chip, comm, tm=256):
    s = x.shape[0]

    def body(idx_ref, x_ref, g_ref, w_hbm, p_ref, h_ref, w_f32, w_vmem):
        @pl.when(pl.program_id(0) == 0)
        def _():
            pltpu.sync_copy(w_hbm.at[0], w_f32)
            w_vmem[...] = w_f32[...].astype(BF16)
        xv = x_ref[...]
        r = lax.rsqrt(jnp.mean(xv * xv, axis=-1, keepdims=True) + RMS_EPS)
        hb = ((xv * r) * g_ref[...]).astype(BF16)
        h_ref[...] = hb
        p_ref[...] = _dot(hb, w_vmem[...]).astype(BF16)

    return _pcall(
        body, name="f1_l0_own", grid=(s // tm,),
        in_specs=[pl.BlockSpec((tm, D_MODEL), lambda i, idx: (i, 0)),
                  pl.BlockSpec((None, 1, D_MODEL), lambda i, idx: (0, 0, 0)), ANY],
        out_specs=[pl.BlockSpec((tm, SHARD_W), lambda i, idx: (i, idx[0])),
                   pl.BlockSpec((tm, D_MODEL), lambda i, idx: (i, 0))],
        out_shape=[jax.ShapeDtypeStruct((s, IN_TOTAL), BF16), jax.ShapeDtypeStruct((s, D_MODEL), BF16)],
        scratch=[pltpu.VMEM((D_MODEL, SHARD_W), F32), pltpu.VMEM((D_MODEL, SHARD_W), BF16)],
        sem=("arbitrary",), args=[x, norm_g3, w_in], comm=comm, prefetch=[chip])


def _f1_others(h, p, w_all, chip, comm, tm=1024):
    s = h.shape[0]
    tm = min(tm, s)
    cols_of = lambda j, idx: (idx[0] + 1 + j) % N_CHIPS

    def body(idx_ref, h_ref, w_hbm, _, p_ref, w_vmem, sems):
        j = pl.program_id(1)

        @pl.when(jnp.logical_and(pl.program_id(0) == 0, j == 0))
        def _():
            loads = [pltpu.make_async_copy(
                w_hbm.at[:, pl.ds(pl.multiple_of(((idx_ref[0] + 1 + q) % N_CHIPS) * SHARD_W, 128), SHARD_W)],
                w_vmem.at[q], sems.at[q]) for q in range(N_CHIPS - 1)]
            for cp in loads:
                cp.start()
            for cp in loads:
                cp.wait()

        p_ref[...] = _dot(h_ref[...], w_vmem[j]).astype(BF16)

    return _pcall(
        body, name="f1_l0_others", grid=(s // tm, N_CHIPS - 1),
        in_specs=[pl.BlockSpec((tm, D_MODEL), lambda i, j, idx: (i, 0)), ANY, ANY],
        out_specs=[pl.BlockSpec((tm, SHARD_W), lambda i, j, idx: (i, cols_of(j, idx)))],
        out_shape=[jax.ShapeDtypeStruct((s, IN_TOTAL), BF16)],
        scratch=[pltpu.VMEM((N_CHIPS - 1, D_MODEL, SHARD_W), BF16), pltpu.SemaphoreType.DMA((N_CHIPS - 1,))],
        sem=("arbitrary", "arbitrary"), args=[h, w_all, p], comm=comm, prefetch=[chip], aliases={2: 0})


def _f2(x, p, lw, l, comm=None, loss=None, t=256):
    s = x.shape[0]
    n = s // t

    def body(p_ref, hxb_ref, hcg_ref, hxc_ref, x_ref, lng, lnb, w2, bst, cw, cb, wpool, ps,
             proj, wo, *rest):
        if loss is None:
            ya_ref, yb_ref, yc_ref, m_ref, xo_ref, sg_scr = rest
        else:
            t_ref, fg_ref, ya_ref, yb_ref, yc_ref, m_ref, xo_ref, loss_ref, gg_ref, sg_scr, lacc, gacc = rest
        i = pl.program_id(0)
        w = dict(ln_g=lng, ln_b=lnb, w2=w2, bst=bst, cw=cw, cb=cb, wpool=wpool, ps=ps)
        r = _branches_fwd(p_ref, hxb_ref, hcg_ref, hxc_ref, i == 0, i * t, w, sg_scr)

        def project(act, b):
            ab = act.astype(BF16)
            return jnp.concatenate([_dot(ab, proj[k, b * WIDTH:(b + 1) * WIDTH, :]) for k in range(N_CHIPS)], axis=1)

        ya = project(r["a_in"], 0)
        yb = project(r["b_in"], 1)
        yc = project(r["c_in"], 2)
        ya_ref[...] = ya.astype(BF16)
        yb_ref[...] = yb.astype(BF16)
        yc_ref[...] = yc.astype(BF16)
        m = (_sigmoid(p_ref[:, GA:GA + D_MODEL].astype(F32)) * ya
             + _sigmoid(p_ref[:, GB:GB + D_MODEL].astype(F32)) * yb
             + _sigmoid(p_ref[:, GC:GC + D_MODEL].astype(F32)) * yc)
        mb = m.astype(BF16)
        m_ref[...] = mb
        xo = x_ref[...] + _dot(mb, wo[...])
        if loss is None:
            xo_ref[...] = xo
        else:
            @pl.when(i == 0)
            def _():
                lacc[...] = jnp.zeros(lacc.shape, F32)
                gacc[...] = jnp.zeros(gacc.shape, F32)

            g = fg_ref[...]
            rr = lax.rsqrt(jnp.mean(xo * xo, axis=-1, keepdims=True) + RMS_EPS)
            err = (xo * rr) * g - t_ref[...]
            lacc[...] += _colsum8(err * err)
            dx, gx = _rms_bwd(xo, g, err * (1.0 / D_MODEL))
            xo_ref[...] = dx
            gacc[...] += _colsum8(gx)

            @pl.when(i == n - 1)
            def _():
                tot = jnp.sum(jnp.sum(lacc[...], axis=0, keepdims=True), axis=1, keepdims=True)
                loss_ref[...] = jnp.broadcast_to(tot * (0.5 / D_MODEL), loss_ref.shape)
                gg_ref[...] = jnp.sum(gacc[...], axis=0, keepdims=True)

    tile = lambda c: pl.BlockSpec((t, c), lambda i: (i, 0))
    lsel = lambda *blk: pl.BlockSpec((None,) + blk, lambda i: (l,) + (0,) * len(blk))
    act = jax.ShapeDtypeStruct((s, D_MODEL), BF16)
    f32s = lambda *shape: jax.ShapeDtypeStruct(shape, F32)
    with_loss = loss is not None
    return _pcall(
        body, name=f"f2_l{l}", grid=(n,),
        in_specs=[tile(IN_TOTAL)] + _halo_specs(t) + [
            tile(D_MODEL), lsel(1, WIDTH), lsel(1, WIDTH), lsel(4, 256, 128), lsel(CHUNK, WIDTH),
            lsel(8, WIDTH), lsel(1, WIDTH), lsel(4, 128, 128), lsel(1, WIDTH),
            _const_spec((N_CHIPS, 3 * WIDTH, 256)), _const_spec((D_MODEL, D_MODEL))]
        + ([tile(D_MODEL), _const_spec((1, D_MODEL))] if with_loss else []),
        out_specs=[tile(D_MODEL)] * 5 + ([_const_spec((8, 128)), _const_spec((1, D_MODEL))] if with_loss else []),
        out_shape=[act, act, act, act, f32s(s, D_MODEL)] + ([f32s(8, 128), f32s(1, D_MODEL)] if with_loss else []),
        scratch=[pltpu.VMEM((t, WIDTH), F32)]
        + ([pltpu.VMEM((8, D_MODEL), F32), pltpu.VMEM((8, D_MODEL), F32)] if with_loss else []),
        sem=("arbitrary",),
        args=[p, p, p, p, x, lw["ln_g"], lw["ln_b"], lw["w2"], lw["bst"], lw["cw"], lw["cb"], lw["wpool"], lw["ps"],
              lw["proj"][l], lw["wo"][l]] + (list(loss) if with_loss else []), comm=comm)


def _b1(p, dout, ya, yb, yc, lw, l, comm=None, t=256):
    s = p.shape[0]
    n = s // t
    nch = t // CHUNK

    def body(p_ref, hxb_ref, hcg_ref, hxc_ref, do_ref, ya_ref, yb_ref, yc_ref,
             lng, lnb, w2, wt2, bst, cw, cb, wpool, ps, proj_h, wo_h, sel_ref,
             dp_ref, gwp_h, gws_ref, gbs_ref, glng_ref, glnb_ref, gcw_ref, gcb_ref,
             gwpool_ref, gps_ref,
             wpa, wpb, wpc, wo, gwpa, gwpb, gwpc, gbs_acc, vec_acc, sg_scr, dvn_scr, car_dc, car_e, psem):
        i = pl.program_id(0)
        ti = n - 1 - i

        def by_chip_copies(vmem_bufs, hbm, to_hbm):
            cps = []
            for b, buf in enumerate(vmem_bufs):
                for k in range(N_CHIPS):
                    v = buf.at[:, pl.ds(k * 256, 256)]
                    h = hbm.at[k, pl.ds(b * WIDTH, WIDTH)]
                    cps.append(pltpu.make_async_copy(v, h, psem.at[b * N_CHIPS + k]) if to_hbm
                               else pltpu.make_async_copy(h, v, psem.at[b * N_CHIPS + k]))
            return cps

        @pl.when(i == 0)
        def _():
            loads = by_chip_copies((wpa, wpb, wpc), proj_h, False)
            for cp in loads:
                cp.start()
            pltpu.sync_copy(wo_h, wo)
            for cp in loads:
                cp.wait()
            for acc in (gwpa, gwpb, gwpc, gbs_acc, vec_acc, car_dc, car_e):
                acc[...] = jnp.zeros(acc.shape, acc.dtype)
            gws_ref[...] = jnp.zeros(gws_ref.shape, F32)
            gwpool_ref[...] = jnp.zeros(gwpool_ref.shape, F32)

        w = dict(ln_g=lng, ln_b=lnb, w2=w2, bst=bst, cw=cw, cb=cb, wpool=wpool, ps=ps)
        r = _branches_fwd(p_ref, hxb_ref, hcg_ref, hxc_ref, ti == 0, ti * t, w, sg_scr)

        def seg(o, width=WIDTH):
            return p_ref[:, o:o + width].astype(F32)

        def put(o, val):
            dp_ref[:, o:o + val.shape[1]] = val.astype(BF16)

        dob = do_ref[...].astype(BF16)
        dm = _dotg(dob, wo[...], NT)

        def merge_bwd(goff, y_ref, xin, wp, gwp):
            sx = _sigmoid(seg(goff, D_MODEL))
            dmy = dm * sx
            put(goff, dmy * y_ref[...].astype(F32) * (1.0 - sx))
            dyb = dmy.astype(BF16)
            gwp[...] += _dotg(xin.astype(BF16), dyb, TN)
            return _dotg(dyb, wp[...], NT)

        d_ain = merge_bwd(GA, ya_ref, r["a_in"], wpa, gwpa)
        d_bin = merge_bwd(GB, yb_ref, r["b_in"], wpb, gwpb)
        d_cin = merge_bwd(GC, yc_ref, r["c_in"], wpc, gwpc)

        def dsilu(z, sz):
            return sz * (1.0 + z * (1.0 - sz))

        za, sa = r["za"], r["sa"]
        d_abr = d_ain * (za * sa)
        put(ZA, d_ain * r["a_br"] * dsilu(za, sa))
        put(U, d_abr * r["sg"] * _gelu_grad(r["pu"], r["th_u"]))
        d_sg = d_abr * r["u_act"]
        dsgb = d_sg.astype(BF16)
        lo = lax.broadcasted_iota(jnp.int32, (CHUNK, CHUNK), 1) < 64
        zero = jnp.zeros((CHUNK, CHUNK), BF16)
        for c in range(nch):
            rows = slice(c * CHUNK, (c + 1) * CHUNK)
            gbs_acc[...] += d_sg[rows]
            for j in range(4):
                cols = slice(j * 128, (j + 1) * 128)
                dj = dsgb[rows, cols]
                zt = _dot(wt2[j], dj)
                dvn_scr[rows, cols] = jnp.where(lo, zt[:CHUNK], zt[CHUNK:])
                stacked = jnp.concatenate([jnp.where(lo, dj, zero), jnp.where(lo, zero, dj)], axis=0)
                gws_ref[j] += _dotg(stacked, r["vnb"][rows, cols], NT)
        d_vn = dvn_scr[...]
        vhat = r["vhat"]
        vec_acc[0] += _colsum8(d_vn * vhat)
        vec_acc[1] += _colsum8(d_vn)
        d_vhat = d_vn * lng[...]
        d_vg = r["rs"] * (d_vhat - jnp.mean(d_vhat, axis=-1, keepdims=True)
                          - vhat * jnp.mean(d_vhat * vhat, axis=-1, keepdims=True))
        put(V, d_vg * _gelu_grad(r["pv"], r["th_v"]))

        zb, sb = r["zb"], r["sb"]
        d_bbr = d_bin * (zb * sb)
        put(ZB, d_bin * r["b_br"] * dsilu(zb, sb))
        put(BG, d_bbr * r["conv"])
        dc = d_bbr * r["bg"]
        vec_acc[2] += _colsum8(dc)
        vec_acc[3] += _colsum8(dc * r["y2"])
        vec_acc[4] += _colsum8(dc * r["y1"])
        vec_acc[5] += _colsum8(dc * r["yb0"])
        ext = jnp.concatenate([dc, car_dc[...]], axis=0)
        ne = t + HALO
        d1 = pltpu.roll(ext, ne - 1, 0)[:t]
        d2 = pltpu.roll(ext, ne - 2, 0)[:t]
        d_yb0 = cw[2:3, :] * dc + cw[1:2, :] * d1 + cw[0:1, :] * d2
        put(CG, d_yb0 * r["xb"])
        put(XB, d_yb0 * r["cg"])
        car_dc[...] = dc[:HALO]

        zc, sc = r["zc"], r["sc"]
        d_cbr = d_cin * (zc * sc)
        put(ZC, d_cin * r["c_br"] * dsilu(zc, sc))
        vec_acc[6] += _colsum8(d_cbr * r["q"])
        d_q = d_cbr * ps[...]
        for gi, win in enumerate(POOL_WINDOWS):
            cols = slice(gi * 128, (gi + 1) * 128)
            dqb = d_q[:, cols].astype(BF16)
            d_pool = _dotg(dqb, wpool[gi], NT)
            gwpool_ref[gi] += _dotg(r["pooled"][gi].astype(BF16), dqb, TN)
            e = d_pool * r["inv"][gi]
            sx = jnp.concatenate([e, car_e[:, cols]], axis=0)
            sh = 1
            while sh < win:
                sx = sx + pltpu.roll(sx, ne - sh, 0)
                sh *= 2
            put(XC + gi * 128, sx[:t] - d_pool)
            car_e[:, cols] = e[:HALO]

        @pl.when(i == n - 1)
        def _():
            for acc, stage in ((gwpa, wpa), (gwpb, wpb), (gwpc, wpc)):
                stage[...] = acc[...].astype(BF16)
            stores = by_chip_copies((wpa, wpb, wpc), gwp_h, True)
            for cp in stores:
                cp.start()
            for cp in stores:
                cp.wait()
            gbs_ref[...] = jnp.dot(gbs_acc[...], sel_ref[...], preferred_element_type=F32,
                                   precision=lax.Precision.HIGHEST)
            red = lambda k: jnp.sum(vec_acc[k], axis=0, keepdims=True)
            glng_ref[...] = red(0)
            glnb_ref[...] = red(1)
            gcb_ref[...] = red(2)
            gcw_ref[...] = jnp.zeros(gcw_ref.shape, F32)
            for k in range(3):
                gcw_ref[k:k + 1, :] = red(3 + k)
            gps_ref[...] = red(6)
            tt = lax.broadcasted_iota(jnp.int32, (2 * CHUNK, CHUNK), 0) % CHUNK
            ss = lax.broadcasted_iota(jnp.int32, (2 * CHUNK, CHUNK), 1)
            for j in range(4):
                gws_ref[j] = jnp.where(tt >= ss, gws_ref[j], 0.0)

    rtile = lambda c: pl.BlockSpec((t, c), lambda i: (n - 1 - i, 0))
    lsel = lambda *blk: pl.BlockSpec((None,) + blk, lambda i: (l,) + (0,) * len(blk))
    f32s = lambda *shape: jax.ShapeDtypeStruct(shape, F32)
    return _pcall(
        body, name=f"b1_l{l}", grid=(n,),
        in_specs=[rtile(IN_TOTAL)] + _halo_specs(t, rev_n=n) + [rtile(D_MODEL)] * 4 + [
            lsel(1, WIDTH), lsel(1, WIDTH), lsel(4, 256, 128), lsel(4, 256, 128), lsel(CHUNK, WIDTH),
            lsel(8, WIDTH), lsel(1, WIDTH), lsel(4, 128, 128), lsel(1, WIDTH),
            ANY, ANY, _const_spec((WIDTH, 128))],
        out_specs=[rtile(IN_TOTAL), ANY,
                   _const_spec((4, 256, 128)), _const_spec((CHUNK, 128)), _const_spec((1, WIDTH)),
                   _const_spec((1, WIDTH)), _const_spec((8, WIDTH)), _const_spec((1, WIDTH)),
                   _const_spec((4, 128, 128)), _const_spec((1, WIDTH))],
        out_shape=[jax.ShapeDtypeStruct((s, IN_TOTAL), BF16), jax.ShapeDtypeStruct((N_CHIPS, 3 * WIDTH, 256), BF16),
                   f32s(4, 256, 128), f32s(CHUNK, 128),
                   f32s(1, WIDTH), f32s(1, WIDTH), f32s(8, WIDTH), f32s(1, WIDTH), f32s(4, 128, 128),
                   f32s(1, WIDTH)],
        scratch=[pltpu.VMEM((WIDTH, D_MODEL), BF16), pltpu.VMEM((WIDTH, D_MODEL), BF16),
                 pltpu.VMEM((WIDTH, D_MODEL), BF16), pltpu.VMEM((D_MODEL, D_MODEL), BF16),
                 pltpu.VMEM((WIDTH, D_MODEL), F32),
                 pltpu.VMEM((WIDTH, D_MODEL), F32), pltpu.VMEM((WIDTH, D_MODEL), F32),
                 pltpu.VMEM((CHUNK, WIDTH), F32), pltpu.VMEM((8, 8, WIDTH), F32),
                 pltpu.VMEM((t, WIDTH), F32), pltpu.VMEM((t, WIDTH), F32),
                 pltpu.VMEM((HALO, WIDTH), F32), pltpu.VMEM((HALO, WIDTH), F32),
                 pltpu.SemaphoreType.DMA((3 * N_CHIPS,))],
        sem=("arbitrary",),
        args=[p, p, p, p, dout, ya, yb, yc, lw["ln_g"], lw["ln_b"], lw["w2"], lw["wt2"], lw["bst"], lw["cw"],
              lw["cb"], lw["wpool"], lw["ps"], lw["proj"][l], lw["wo"][l], lw["sel"]], comm=comm)


def _rms_bwd(xv, g, dh):
    r = lax.rsqrt(jnp.mean(xv * xv, axis=-1, keepdims=True) + RMS_EPS)
    xhat = xv * r
    dxh = dh * g
    dx = r * (dxh - xhat * jnp.mean(dxh * xhat, axis=-1, keepdims=True))
    return dx, dh * xhat


def _b2a(dp, w_all, x, dout, norm_g3, l, comm=None, tm=512):
    s = x.shape[0]
    nm = s // tm

    def body(dp_ref, w_hbm, x_ref, do_ref, g_ref, dx_ref, gg_ref, w_vmem, gacc):
        i = pl.program_id(0)

        @pl.when(i == 0)
        def _():
            pltpu.sync_copy(w_hbm, w_vmem)
            gacc[...] = jnp.zeros(gacc.shape, F32)

        dh = _dotg(dp_ref[...], w_vmem[...], NT)
        dx, gx = _rms_bwd(x_ref[...], g_ref[...], dh)
        dx_ref[...] = do_ref[...] + dx
        gacc[...] += _colsum8(gx)

        @pl.when(i == nm - 1)
        def _():
            gg_ref[...] = jnp.sum(gacc[...], axis=0, keepdims=True)

    return _pcall(
        body, name=f"b2a_l{l}", grid=(nm,),
        in_specs=[pl.BlockSpec((tm, IN_TOTAL), lambda i: (i, 0)), ANY,
                  pl.BlockSpec((tm, D_MODEL), lambda i: (i, 0)),
                  pl.BlockSpec((tm, D_MODEL), lambda i: (i, 0)),
                  pl.BlockSpec((None, 1, D_MODEL), lambda i: (l, 0, 0))],
        out_specs=[pl.BlockSpec((tm, D_MODEL), lambda i: (i, 0)),
                   pl.BlockSpec((1, D_MODEL), lambda i: (0, 0))],
        out_shape=[jax.ShapeDtypeStruct((s, D_MODEL), F32), jax.ShapeDtypeStruct((1, D_MODEL), F32)],
        scratch=[pltpu.VMEM((D_MODEL, IN_TOTAL), BF16), pltpu.VMEM((8, D_MODEL), F32)],
        sem=("arbitrary",), args=[dp, w_all, x, dout, norm_g3], comm=comm)


def _b2b(h, dp, name, chip, part="all", comm=None, tk=2048):
    s = h.shape[0]
    tk = min(tk, s)
    nk = s // tk
    first, count = dict(all=(0, 4), others=(1, 3), own=(0, 1))[part]
    cols_of = lambda k, idx: (idx[0] + first + k) % N_CHIPS

    def body(idx_ref, h_ref, dp_ref, g_ref, acc):
        kk = pl.program_id(1)

        @pl.when(kk == 0)
        def _():
            acc[...] = jnp.zeros(acc.shape, F32)

        acc[...] += _dotg(h_ref[...], dp_ref[...], TN)

        @pl.when(kk == nk - 1)
        def _():
            g_ref[...] = acc[...].astype(BF16)

    own = part == "own"
    return _pcall(
        body, name=name, grid=(count, nk),
        in_specs=[pl.BlockSpec((tk, D_MODEL), lambda k, kk, idx: (kk, 0)),
                  pl.BlockSpec((tk, SHARD_W), lambda k, kk, idx: (kk, cols_of(k, idx)))],
        out_specs=[pl.BlockSpec((D_MODEL, SHARD_W), lambda k, kk, idx: (0, 0 if own else cols_of(k, idx)))],
        out_shape=[jax.ShapeDtypeStruct((D_MODEL, SHARD_W if own else IN_TOTAL), BF16)],
        scratch=[pltpu.VMEM((D_MODEL, SHARD_W), F32)],
        sem=("arbitrary", "arbitrary"), args=[h, dp], comm=comm, prefetch=[chip])


def _gwo(m, dout, l, tk=2048):
    s = m.shape[0]
    tk = min(tk, s)
    nk = s // tk

    def body(m_ref, do_ref, g_ref, acc):
        kk = pl.program_id(0)

        @pl.when(kk == 0)
        def _():
            acc[...] = jnp.zeros(acc.shape, F32)

        acc[...] += _dotg(m_ref[...], do_ref[...].astype(BF16), TN)

        @pl.when(kk == nk - 1)
        def _():
            g_ref[...] = acc[...].astype(BF16)

    return pl.pallas_call(
        body, name=f"gwo_l{l}", grid=(nk,),
        in_specs=[pl.BlockSpec((tk, D_MODEL), lambda kk: (kk, 0)), pl.BlockSpec((tk, D_MODEL), lambda kk: (kk, 0))],
        out_specs=_const_spec((D_MODEL, D_MODEL)),
        out_shape=jax.ShapeDtypeStruct((D_MODEL, D_MODEL), BF16),
        scratch_shapes=[pltpu.VMEM((D_MODEL, D_MODEL), F32)],
        compiler_params=_params(("arbitrary",)),
    )(m, dout)


def _row_block(rows, cols, n_arrays):
    budget = VMEM_LIMIT // 3 // (2 * 4 * n_arrays * cols)
    rb = rows
    while rb > budget and rb % 16 == 0:
        rb //= 2
    return rb


def _cast_slot(name, a, l, chip):
    _, rows, cols = a.shape
    rb = _row_block(rows, cols, 2)

    def body(idx_ref, a_ref, o_ref):
        o_ref[...] = a_ref[...].astype(BF16)

    gs = pltpu.PrefetchScalarGridSpec(
        num_scalar_prefetch=1, grid=(rows // rb,),
        in_specs=[pl.BlockSpec((None, rb, cols), lambda i, idx: (l, i, 0))],
        out_specs=pl.BlockSpec((None, rb, cols), lambda i, idx: (idx[0], i, 0)))
    return pl.pallas_call(body, name=name, grid_spec=gs, out_shape=jax.ShapeDtypeStruct((N_CHIPS, rows, cols), BF16),
                          compiler_params=_params(("arbitrary",)))(chip, a)


def _cast_cols(name, a, l, chip):
    _, rows, cols = a.shape
    rb = _row_block(rows, cols, 2)

    def body(idx_ref, a_ref, o_ref):
        o_ref[...] = a_ref[...].astype(BF16)

    gs = pltpu.PrefetchScalarGridSpec(
        num_scalar_prefetch=1, grid=(rows // rb,),
        in_specs=[pl.BlockSpec((None, rb, cols), lambda i, idx: (l, i, 0))],
        out_specs=pl.BlockSpec((rb, cols), lambda i, idx: (i, idx[0])))
    return pl.pallas_call(body, name=name, grid_spec=gs, out_shape=jax.ShapeDtypeStruct((rows, N_CHIPS * cols), BF16),
                          compiler_params=_params(("arbitrary",)))(chip, a)


def _sum4(name, own, chip, recv, l, n_layers, prev=None):
    _, rows, cols = recv.shape
    rb = _row_block(rows, cols, 5)
    nb = rows // rb
    if len(own.shape) == 3:
        own_spec = pl.BlockSpec((None, rb, cols), lambda i, idx: (idx[0], i, 0))
    elif own.shape[1] == cols:
        own_spec = pl.BlockSpec((rb, cols), lambda i, idx: (i, 0))
    else:
        own_spec = pl.BlockSpec((rb, cols), lambda i, idx: (i, idx[0]))

    def body(idx_ref, own_ref, r_ref, *rest):
        o_ref = rest[-1]
        o_ref[...] = (((own_ref[...].astype(F32) + r_ref[0].astype(F32)) + r_ref[1].astype(F32))
                      + r_ref[2].astype(F32)).astype(BF16)

    gs = pltpu.PrefetchScalarGridSpec(
        num_scalar_prefetch=1, grid=(nb,),
        in_specs=[own_spec,
                  pl.BlockSpec((3, rb, cols), lambda i, idx: (0, i, 0))] + ([ANY] if prev is not None else []),
        out_specs=pl.BlockSpec((rb, cols), lambda i, idx: (l * nb + i, 0)))
    args = (chip, own, recv) + ((prev,) if prev is not None else ())
    return pl.pallas_call(body, name=name, grid_spec=gs,
                          out_shape=jax.ShapeDtypeStruct((n_layers * rows, cols), BF16),
                          input_output_aliases=({3: 0} if prev is not None else {}),
                          compiler_params=_params(("arbitrary",)))(*args)


def _sum8(name, pack, me, recv):
    rows = recv.shape[1]

    def body(idx_ref, own_ref, r_ref, o_ref):
        acc = own_ref[...]
        for j in range(7):
            acc = acc + r_ref[j]
        o_ref[...] = acc

    gs = pltpu.PrefetchScalarGridSpec(
        num_scalar_prefetch=1, grid=(1,),
        in_specs=[pl.BlockSpec((rows, 128), lambda i, idx: (idx[0], 0)),
                  pl.BlockSpec((7, rows, 128), lambda i, idx: (0, 0, 0))],
        out_specs=pl.BlockSpec((rows, 128), lambda i, idx: (idx[0], 0)))
    return pl.pallas_call(body, name=name, grid_spec=gs, out_shape=jax.ShapeDtypeStruct(pack.shape, F32),
                          compiler_params=_params(("arbitrary",)))(me, pack, recv)


_SMALL = ("norm_g", "ln_g", "ln_b", "w_s", "b_s", "conv_b", "w_pool", "pool_scale", "final_g", "conv_w")
_SMALL_SHAPES = dict(norm_g=(DEPTH, D_MODEL), ln_g=(DEPTH, WIDTH), ln_b=(DEPTH, WIDTH), w_s=(DEPTH, 8, CHUNK, CHUNK),
                     b_s=(DEPTH, 8, CHUNK), conv_b=(DEPTH, WIDTH), w_pool=(DEPTH, 4, 128, 128),
                     pool_scale=(DEPTH, WIDTH), final_g=(1, D_MODEL), conv_w=(DEPTH, 3, WIDTH))


def _small_rows():
    base, r = {}, 0
    for nm in _SMALL:
        base[nm] = r
        size = 1
        for d in _SMALL_SHAPES[nm]:
            size *= d
        r += size // 128
    base["loss"] = r
    return base, -(-(r + 8) // 64) * 64


def _pack_small(name, raw, gng1, g_final, loss_blk):
    base, rows = _small_rows()
    n_l = len(raw[0])

    def body(*refs):
        o = refs[-1]
        per_layer = [refs[l * n_l:(l + 1) * n_l] for l in range(DEPTH)]
        gng1_ref, gfin_ref, loss_ref = refs[DEPTH * n_l:DEPTH * n_l + 3]
        o[...] = jnp.zeros(o.shape, F32)
        o[base["loss"]:base["loss"] + 8, :] = loss_ref[...]

        def put_row_vector(r0, ref, width):
            for j in range(width // 128):
                o[r0 + j:r0 + j + 1, :] = ref[0:1, j * 128:(j + 1) * 128]

        put_row_vector(base["norm_g"] + D_MODEL // 128, gng1_ref, D_MODEL)
        put_row_vector(base["final_g"], gfin_ref, D_MODEL)
        for l in range(DEPTH):
            gws, gbs, glng, glnb, gcw, gcb, gwpool, gps = per_layer[l]
            put_row_vector(base["ln_g"] + l * 4, glng, WIDTH)
            put_row_vector(base["ln_b"] + l * 4, glnb, WIDTH)
            put_row_vector(base["conv_b"] + l * 4, gcb, WIDTH)
            put_row_vector(base["pool_scale"] + l * 4, gps, WIDTH)
            o[base["w_s"] + l * 1024:base["w_s"] + (l + 1) * 1024, :] = gws[...].reshape(1024, 128)
            o[base["w_pool"] + l * 512:base["w_pool"] + (l + 1) * 512, :] = gwpool[...].reshape(512, 128)
            o[base["b_s"] + l * 8:base["b_s"] + (l + 1) * 8, :] = gbs[...].T[0:8, :]
            for k in range(3):
                for ch in range(N_CHIPS):
                    r = base["conv_w"] + (l * 3 + k) * N_CHIPS + ch
                    o[r:r + 1, :] = gcw[k:k + 1, ch * 128:(ch + 1) * 128]

    args = [a for l in range(DEPTH) for a in raw[l]] + [gng1, g_final, loss_blk]
    vm = pl.BlockSpec(memory_space=pltpu.VMEM)
    return pl.pallas_call(body, name=name, in_specs=[vm] * len(args), out_specs=vm,
                          out_shape=jax.ShapeDtypeStruct((rows, 128), F32), compiler_params=_params())(*args)


def _adamw_small(name, gred, chip, w, m, v):
    base, _ = _small_rows()

    def body(chip_ref, g_ref, *refs):
        n = len(_SMALL)
        w_r, m_r, v_r = refs[:n], refs[n:2 * n], refs[2 * n:3 * n]
        out = refs[3 * n:]

        def update(i, idx, g):
            d, mn, vn = _adamw_math(w_r[i][idx], g, m_r[i][idx], v_r[i][idx])
            for o, val in zip(out[4 * i:4 * i + 4], (g, d, mn, vn)):
                o[idx] = val

        for i, nm in enumerate(_SMALL):
            shape = _SMALL_SHAPES[nm]
            if nm == "conv_w":
                for l in range(DEPTH):
                    for k in range(3):
                        row = base[nm] + (l * 3 + k) * N_CHIPS + chip_ref[0]
                        update(i, (l, slice(k, k + 1), slice(None)), g_ref[pl.ds(row, 1), :])
            elif len(shape) == 2:
                per = shape[1] // 128
                for l in range(shape[0]):
                    for j in range(per):
                        r = base[nm] + l * per + j
                        update(i, (slice(l, l + 1), slice(j * 128, (j + 1) * 128)), g_ref[r:r + 1, :])
            else:
                rows = 1
                for dim in shape[:-1]:
                    rows *= dim
                update(i, (Ellipsis,), g_ref[base[nm]:base[nm] + rows, :].reshape(shape))

    arrs = [d[nm] for d in (w, m, v) for nm in _SMALL]
    vm = pl.BlockSpec(memory_space=pltpu.VMEM)
    gs = pltpu.PrefetchScalarGridSpec(num_scalar_prefetch=1, grid=(1,), in_specs=[vm] * (1 + len(arrs)),
                                      out_specs=[vm] * (4 * len(_SMALL)))
    outs = pl.pallas_call(
        body, name=name, grid_spec=gs,
        out_shape=[jax.ShapeDtypeStruct(w[nm].shape, F32) for nm in _SMALL for _ in range(4)],
        compiler_params=_params(("arbitrary",)))(chip, gred, *arrs)
    return {nm: list(outs[4 * i:4 * i + 4]) for i, nm in enumerate(_SMALL)}


def _adamw_math(w, g, m, v):
    m = ADAM_B1 * m + (1.0 - ADAM_B1) * g
    v = ADAM_B2 * v + (1.0 - ADAM_B2) * (g * g)
    m_hat = m / (1.0 - ADAM_B1 ** ADAM_STEP)
    v_hat = v / (1.0 - ADAM_B2 ** ADAM_STEP)
    delta = -ADAM_LR * (m_hat / (jnp.sqrt(v_hat) + ADAM_EPS) + ADAM_WD * w)
    return delta, m, v


def _adamw(name, w, m, v, g_parts):
    rows, cols = w.shape
    np_ = len(g_parts)
    rb = _row_block(rows, cols, 7 + np_)

    def body(*refs):
        w_ref, m_ref, v_ref = refs[:3]
        g_refs = refs[3:3 + np_]
        go_ref, d_ref, mo_ref, vo_ref = refs[3 + np_:]
        g = g_refs[0][...].astype(F32)
        for gr in g_refs[1:]:
            g = g + gr[...].astype(F32)
        d, mn, vn = _adamw_math(w_ref[...], g, m_ref[...], v_ref[...])
        go_ref[...] = g
        d_ref[...] = d
        mo_ref[...] = mn
        vo_ref[...] = vn

    spec = pl.BlockSpec((rb, cols), lambda i: (i, 0))
    shp = jax.ShapeDtypeStruct((rows, cols), F32)
    return pl.pallas_call(body, name=name, grid=(rows // rb,), in_specs=[spec] * (3 + np_),
                          out_specs=[spec] * 4, out_shape=[shp] * 4,
                          compiler_params=_params(("arbitrary",)))(w, m, v, *g_parts)


def _all_reduce_and_swap(name, a, srcs, dsts, rows):
    n = len(srcs)

    def body(a_ref, *refs):
        src, o_ref, dst = refs[:n], refs[2 * n], refs[2 * n + 1:3 * n + 1]
        buf, send_sems, recv_sems, swap_send, swap_recv = refs[3 * n + 1:]
        x, y, c = _place()
        swaps = [pltpu.make_async_remote_copy(
            src_ref=src[i].at[pl.ds(0, rows[i])], dst_ref=dst[i].at[pl.ds(0, rows[i])], send_sem=swap_send.at[i],
            recv_sem=swap_recv.at[i], device_id=(x, y, 1 - c), device_id_type=MESH) for i in range(n)]
        for cp in swaps:
            cp.start()
        o_ref[...] = a_ref[...]
        for rnd, peer in enumerate(((x, y, 1 - c), (x, 1 - y, c), (1 - x, y, c))):
            cp = pltpu.make_async_remote_copy(
                src_ref=o_ref, dst_ref=buf.at[rnd], send_sem=send_sems.at[rnd], recv_sem=recv_sems.at[rnd],
                device_id=peer, device_id_type=MESH)
            cp.start()
            cp.wait_recv()
            cp.wait_send()
            o_ref[...] = o_ref[...] + buf[rnd]
        for cp in swaps:
            cp.wait_recv()
        for cp in swaps:
            cp.wait_send()

    vm = pl.BlockSpec(memory_space=pltpu.VMEM)
    outs = pl.pallas_call(
        body, name=name, in_specs=[vm] + [ANY] * (2 * n), out_specs=[vm] + [ANY] * n,
        out_shape=[jax.ShapeDtypeStruct(a.shape, F32)] + [jax.ShapeDtypeStruct(d.shape, d.dtype) for d in dsts],
        input_output_aliases={1 + n + i: 1 + i for i in range(n)},
        scratch_shapes=[pltpu.VMEM((3,) + a.shape, F32), pltpu.SemaphoreType.DMA((3,)),
                        pltpu.SemaphoreType.DMA((3,)), pltpu.SemaphoreType.DMA((n,)), pltpu.SemaphoreType.DMA((n,))],
        compiler_params=_params(),
    )(a, *srcs, *dsts)
    return outs[0], list(outs[1:])


def kernel(x, norm_g, w_in, ln_g, ln_b, w_s, b_s, conv_w, conv_b, w_pool, pool_scale, w_pa, w_pb, w_pc, w_o, final_g, loss_target, m_norm_g, m_w_in, m_ln_g, m_ln_b, m_w_s, m_b_s, m_conv_w, m_conv_b, m_w_pool, m_pool_scale, m_w_pa, m_w_pb, m_w_pc, m_w_o, m_final_g, v_norm_g, v_w_in, v_ln_g, v_ln_b, v_w_s, v_b_s, v_conv_w, v_conv_b, v_w_pool, v_pool_scale, v_w_pa, v_w_pb, v_w_pc, v_w_o, v_final_g):
    W = dict(norm_g=norm_g, w_in=w_in, ln_g=ln_g, ln_b=ln_b, w_s=w_s, b_s=b_s, conv_w=conv_w, conv_b=conv_b,
             w_pool=w_pool, pool_scale=pool_scale, w_pa=w_pa, w_pb=w_pb, w_pc=w_pc, w_o=w_o, final_g=final_g)
    M = dict(norm_g=m_norm_g, w_in=m_w_in, ln_g=m_ln_g, ln_b=m_ln_b, w_s=m_w_s, b_s=m_b_s, conv_w=m_conv_w,
             conv_b=m_conv_b, w_pool=m_w_pool, pool_scale=m_pool_scale, w_pa=m_w_pa, w_pb=m_w_pb, w_pc=m_w_pc,
             w_o=m_w_o, final_g=m_final_g)
    Vv = dict(norm_g=v_norm_g, w_in=v_w_in, ln_g=v_ln_g, ln_b=v_ln_b, w_s=v_w_s, b_s=v_b_s, conv_w=v_conv_w,
              conv_b=v_conv_b, w_pool=v_w_pool, pool_scale=v_pool_scale, w_pa=v_w_pa, w_pb=v_w_pb, w_pc=v_w_pc,
              w_o=v_w_o, final_g=v_final_g)
    L = DEPTH
    s = x.shape[1]
    xs = x.reshape(s, D_MODEL)
    tgt = loss_target.reshape(s, D_MODEL)
    k_me = (2 * lax.axis_index("x") + lax.axis_index("y")).astype(jnp.int32)

    chip = k_me.reshape(1)
    assert L == 2
    half_rows = D_MODEL // 2

    land_win = [_cast_cols(f"cast_w_in{l}", w_in, l, chip) for l in range(L)]
    pcat = lambda d: jnp.concatenate([d[b][l] for l in range(L) for b in ("w_pa", "w_pb", "w_pc")], axis=0)
    pcat_w = pcat(W).reshape(L, 3 * WIDTH, 256)
    land_proj = [_cast_slot(f"cast_proj{l}", pcat_w, l, chip) for l in range(L)]
    land_wo = [_cast_slot(f"cast_w_o{l}", w_o, l, chip) for l in range(L)]
    cw_sh = jnp.pad(conv_w, ((0, 0), (0, 5), (0, 0))).reshape(1, L * 8, 128)
    land_cw = lax.dynamic_update_slice(jnp.zeros((N_CHIPS, L * 8, 128), F32), cw_sh, (k_me, 0, 0))

    causal = jnp.tril(jnp.ones((CHUNK, CHUNK), dtype=bool))
    w_m = jnp.where(causal, w_s, 0.0)
    lw = dict(
        ln_g=ln_g.reshape(L, 1, WIDTH), ln_b=ln_b.reshape(L, 1, WIDTH),
        w2=w_m.reshape(L, 4, 256, CHUNK).astype(BF16),
        wt2=jnp.swapaxes(w_m, -1, -2).reshape(L, 4, 256, CHUNK).astype(BF16),
        bst=jnp.repeat(jnp.swapaxes(b_s, -1, -2), 64, axis=-1),
        cb=conv_b.reshape(L, 1, WIDTH), wpool=w_pool.astype(BF16), ps=pool_scale.reshape(L, 1, WIDTH),
        sel=(jnp.arange(WIDTH)[:, None] // 64 == jnp.arange(128)[None, :]).astype(F32))
    norm_g3 = norm_g.reshape(L, 1, D_MODEL)

    cm = _Comm()
    cm.gather_half(cm.through(land_win[0]))
    (p0, h0), (w_half0,), _ = _f1_own(xs, norm_g3, w_in, chip, cm)
    cm = _Comm()
    cm.forward_half(cm.through(w_half0))
    (w_all0,) = _comm_only("forward_w_in0", cm)
    first_rows = D_MODEL // 4
    cm = _Comm()
    for buf in (land_proj[0], land_wo[0], land_cw):
        cm.gather(cm.through(buf))
    cm.gather(cm.through(land_win[1]), 0, first_rows)
    (p0,), (g_proj0, g_wo0, g_cw, w_part1), _ = _f1_others(h0, p0, w_all0, chip, cm)
    lw["proj"] = [g_proj0, None]
    lw["wo"] = [g_wo0.reshape(D_MODEL, D_MODEL), None]
    lw["cw"] = g_cw.reshape(N_CHIPS, L, 8, 128).transpose(1, 2, 0, 3).reshape(L, 8, WIDTH)
    cm = _Comm()
    cm.gather(cm.through(w_part1), first_rows, D_MODEL - first_rows)
    (ya0, yb0, yc0, mm0, x1), (w_all1,), _ = _f2(xs, p0, lw, 0, comm=cm)
    cm = _Comm()
    for buf in (land_proj[1], land_wo[1]):
        cm.gather(cm.through(buf))
    (p1, h1), (g_proj1, g_wo1), _ = _f1(x1, norm_g3, w_all1, 1, comm=cm)
    lw["proj"][1] = g_proj1
    lw["wo"][1] = g_wo1.reshape(D_MODEL, D_MODEL)
    (ya1, yb1, yc1, mm1, dxl, loss_blk, g_final), _, _ = _f2(x1, p1, lw, 1,
                                                              loss=(tgt, final_g.reshape(1, D_MODEL)))

    by_chip = lambda gwp, gwo: (gwp, gwo.reshape(N_CHIPS, 256, D_MODEL))
    recv_like = lambda a: ((3,) + a.shape[1:], a.dtype)
    part_rows = (D_MODEL, 3 * WIDTH, 256)
    gwin, recv_win, part_proj, part_wo, recv_proj, recv_wo = ([None] * L for _ in range(6))

    def sums(l, half):
        return [_sum4(f"sum_w_in{l}", gwin[l], chip, recv_win[l], l, L, half[0]),
                _sum4(f"sum_proj{l}", part_proj[l], chip, recv_proj[l], l, L, half[1]),
                _sum4(f"sum_w_o{l}", part_wo[l], chip, recv_wo[l], l, L, half[2])]

    gwo1 = _gwo(mm1, dxl, 1)
    (dp1, gwp1, *sm1), _, _ = _b1(p1, dxl, ya1, yb1, yc1, lw, 1)
    (gwin[1],), _, _ = _b2b(h1, dp1, "b2b_l1", chip)
    cm = _Comm()
    cm.scatter(cm.read(gwin[1]), cm.new((3, D_MODEL, SHARD_W), BF16), 0, half_rows)
    (dxl, gng1), _, (recv_half,) = _b2a(dp1, w_all1, x1, dxl, norm_g3, 1, comm=cm)
    part_proj[1], part_wo[1] = by_chip(gwp1, gwo1)
    gwo0 = _gwo(mm0, dxl, 0)
    cm = _Comm()
    cm.scatter(cm.read(gwin[1]), cm.through(recv_half), half_rows, half_rows)
    cm.scatter(cm.read(part_proj[1]), cm.new(*recv_like(part_proj[1])))
    cm.scatter(cm.read(part_wo[1]), cm.new(*recv_like(part_wo[1])))
    (dp0, gwp0, *sm0), (recv_win[1],), (recv_proj[1], recv_wo[1]) = _b1(p0, dxl, ya0, yb0, yc0, lw, 0, comm=cm)
    half = sums(1, [None, None, None])
    part_proj[0], part_wo[0] = by_chip(gwp0, gwo0)

    gpack = _pack_small("pack_small", [sm0, sm1], gng1, g_final, loss_blk)
    slice_rows = gpack.shape[0] // 8

    cm = _Comm()
    cm.scatter(cm.read(part_proj[0]), cm.new(*recv_like(part_proj[0])))
    cm.scatter(cm.read(part_wo[0]), cm.new(*recv_like(part_wo[0])))
    for a in range(3):
        cm.sibling(cm.read(half[a]), cm.new(half[a].shape, BF16), part_rows[a], part_rows[a])
    cm.slices_out(cm.read(gpack), cm.new((7, slice_rows, 128), F32), slice_rows)
    (g_others,), _, (recv_proj[0], recv_wo[0], *other, slices_in) = _b2b(h0, dp0, "b2b_l0_others", chip, "others",
                                                                        comm=cm)
    me = (4 * lax.axis_index("x") + 2 * lax.axis_index("y") + lax.axis_index("c")).astype(jnp.int32).reshape(1)
    gsum = _sum8("sum_small", gpack, me, slices_in)
    early_rows = D_MODEL // 8
    cm = _Comm()
    cm.scatter(cm.read(g_others), cm.new((3, D_MODEL, SHARD_W), BF16), 0, early_rows)
    (gwin[0],), _, (recv_early,) = _b2b(h0, dp0, "b2b_l0_own", chip, "own", comm=cm)
    cm = _Comm()
    cm.scatter(cm.read(g_others), cm.through(recv_early), early_rows, D_MODEL - early_rows)
    cm.slices_back(cm.through(gsum), slice_rows)
    (dxl, gng0), (recv_win[0], gred), _ = _b2a(dp0, w_all0, xs, dxl, norm_g3, 0, comm=cm)
    grad_x = dxl.reshape(1, s, D_MODEL)

    half = sums(0, half)
    gng0, other = _all_reduce_and_swap("norm_g0_and_swap0", gng0.reshape(8, 128), half, other, part_rows)
    loss = gred[_small_rows()[0]["loss"], 0]
    gred = lax.dynamic_update_slice(gred, gng0, (0, 0))

    outs = {}
    shard2d = dict(w_in=(L * D_MODEL, SHARD_W), w_o=(L * 256, D_MODEL))
    for a, name in ((0, "w_in"), (2, "w_o")):
        r2 = shard2d[name]
        res = _adamw(f"adamw_{name}", W[name].reshape(r2), M[name].reshape(r2), Vv[name].reshape(r2),
                     [half[a], other[a]])
        outs[name] = [o.reshape(W[name].shape) for o in res]
    res = _adamw("adamw_proj", pcat(W), pcat(M), pcat(Vv), [half[1], other[1]])
    for i, name in enumerate(("w_pa", "w_pb", "w_pc")):
        outs[name] = [o.reshape(L, 3, WIDTH, 256)[:, i] for o in res]

    as_rows = lambda d: {nm: (d[nm].reshape(1, D_MODEL) if nm == "final_g" else d[nm]) for nm in _SMALL}
    outs.update(_adamw_small("adamw_small", gred, chip, as_rows(W), as_rows(M), as_rows(Vv)))
    outs["final_g"] = [o.reshape(D_MODEL) for o in outs["final_g"]]

    order = ("norm_g", "w_in", "ln_g", "ln_b", "w_s", "b_s", "conv_w", "conv_b", "w_pool", "pool_scale",
             "w_pa", "w_pb", "w_pc", "w_o", "final_g")
    return (loss, grad_x, *[outs[nm][0] for nm in order], *[outs[nm][1] for nm in order],
            *[outs[nm][2] for nm in order], *[outs[nm][3] for nm in order])
```
